```python
import jax
import jax.numpy as jnp
from jax import lax
import numpy as np

D_MODEL = 2048
BATCH = 4
SEQ = 4096
DEPTH = 2

GRID_W = 64
CTX_LEN = 256
N_MOD = 9
D_FF = 5632
RMS_EPS = 1e-6
ROPE_BASE = 10000.0

GLA_HEADS = 4
GLA_DK = D_MODEL // 16
GLA_DV = D_MODEL // 8
GLA_LOWRANK = 16
GLA_TAU = 16.0
GLA_CHUNK = 64
FNET_GROUPS = 4
FNET_GROUP_DIM = D_MODEL // 8
GLA_QK_W = GLA_HEADS * GLA_DK
GLA_V_W = GLA_HEADS * GLA_DV
FNET_W = FNET_GROUPS * FNET_GROUP_DIM
AB_SPLITS = (GLA_QK_W, 2 * GLA_QK_W, 2 * GLA_QK_W + GLA_V_W,
             2 * GLA_QK_W + GLA_V_W + GLA_LOWRANK, 2 * GLA_QK_W + GLA_V_W + 2 * GLA_LOWRANK,
             2 * GLA_QK_W + 2 * GLA_V_W + 2 * GLA_LOWRANK)
AB_IN_W = AB_SPLITS[-1] + FNET_W
AB_MIX_W = GLA_V_W + FNET_W

NA_HEADS = D_MODEL // 64
NA_HEAD_DIM = D_MODEL // NA_HEADS
NA_KH_MAX = 8
NA_KW = 16

kernel_name = 'hybrid_gla_fnet_natten_macaron_dit'

F32 = jnp.float32


def rms_norm(x, g):
    x32 = x.astype(F32)
    y = x32 * lax.rsqrt(jnp.mean(x32 * x32, axis=-1, keepdims=True) + RMS_EPS)
    return (y * g.astype(F32)).astype(x.dtype)


def modulate(h, shift, scale):
    return h * (1 + scale) + shift


def swiglu(h, w_gu, w_down):
    a, u = jnp.split(h @ w_gu, 2, axis=-1)
    return (jax.nn.silu(a) * u) @ w_down


def split_heads(z, n_heads):
    b, t, hd = z.shape
    return z.reshape(b, t, n_heads, hd // n_heads).transpose(0, 2, 1, 3)


def merge_heads(z):
    b, h, t, d = z.shape
    return z.transpose(0, 2, 1, 3).reshape(b, t, h * d)


def _rope_axis(x, pos):
    half = x.shape[-1] // 2
    inv_freq = ROPE_BASE ** (-jnp.arange(half, dtype=F32) / half)
    ang = pos.astype(F32)[:, None] * inv_freq[None, :]
    cos, sin = jnp.cos(ang), jnp.sin(ang)
    x1 = x[..., :half].astype(F32)
    x2 = x[..., half:].astype(F32)
    return jnp.concatenate([x1 * cos - x2 * sin, x1 * sin + x2 * cos], axis=-1).astype(x.dtype)


def axial_rope_2d(x):
    t = jnp.arange(x.shape[-2])
    row, col = t // GRID_W, t % GRID_W
    h = x.shape[-1] // 2
    return jnp.concatenate([_rope_axis(x[..., :h], row), _rope_axis(x[..., h:], col)], axis=-1)


def gla_chunked(q, k, v, g, s0):
    b, h, t, dk = q.shape
    dv = v.shape[-1]
    n = t // GLA_CHUNK

    def to_chunks(a):
        return a.reshape(b, h, n, GLA_CHUNK, a.shape[-1]).transpose(2, 0, 1, 3, 4).astype(F32)

    causal = jnp.tril(jnp.ones((GLA_CHUNK, GLA_CHUNK), dtype=bool))

    def step(s, inp):
        qi, ki, vi, gi = inp
        bc = jnp.cumsum(gi, axis=2)
        o_inter = jnp.einsum('bhtk,bhkv->bhtv', qi * jnp.exp(bc), s)
        diff = bc[:, :, :, None, :] - bc[:, :, None, :, :]
        decay = jnp.exp(jnp.where(causal[:, :, None], diff, -jnp.inf))
        att = jnp.einsum('bhtk,bhsk,bhtsk->bhts', qi, ki, decay)
        o = o_inter + jnp.einsum('bhts,bhsv->bhtv', att, vi)
        b_last = bc[:, :, -1:, :]
        s_new = (jnp.exp(b_last[:, :, 0, :])[..., None] * s
                 + jnp.einsum('bhsk,bhsv->bhkv', ki * jnp.exp(b_last - bc), vi))
        return s_new, o

    s_fin, oc = lax.scan(step, s0.astype(F32), (to_chunks(q), to_chunks(k), to_chunks(v), to_chunks(g)))
    return oc.transpose(1, 2, 0, 3, 4).reshape(b, h, t, dv), s_fin


def gla_bidirectional(q, k, v, g_f, g_b, s_f0, s_b0):
    rev = lambda a: jnp.flip(a, axis=2)
    o_f, s_f = gla_chunked(q, k, v, g_f, s_f0)
    o_b, s_b = gla_chunked(rev(q), rev(k), rev(v), rev(g_b), s_b0)
    return o_f + rev(o_b), s_f, s_b


def _gla_fnet_project(u, w_in, gw_f, gb_f, gw_b, gb_b, rotary):
    q, k, v, lr_f, lr_b, r, f = jnp.split(u @ w_in, AB_SPLITS, axis=-1)
    q = split_heads(q, GLA_HEADS) * (GLA_DK ** -0.5)
    k = split_heads(k, GLA_HEADS)
    if rotary:
        q, k = axial_rope_2d(q), axial_rope_2d(k)
    v = split_heads(v, GLA_HEADS)
    g_f = split_heads(jax.nn.log_sigmoid((lr_f @ gw_f + gb_f).astype(F32)) / GLA_TAU, GLA_HEADS)
    g_b = split_heads(jax.nn.log_sigmoid((lr_b @ gw_b + gb_b).astype(F32)) / GLA_TAU, GLA_HEADS)
    b, t, _ = f.shape
    fg = f.reshape(b, t, FNET_GROUPS, FNET_GROUP_DIM).astype(F32)
    f_mix = jnp.fft.fft2(fg, axes=(1, 3), norm='ortho').real.reshape(b, t, FNET_W).astype(u.dtype)
    return q, k, v, g_f, g_b, r, f_mix


def _gla_fnet_out(o, r, f_mix, g_norm, w_out):
    o = o * lax.rsqrt(jnp.mean(o * o, axis=-1, keepdims=True) + RMS_EPS)
    o = merge_heads(o) * g_norm.astype(F32) * jax.nn.silu(r.astype(F32))
    return jnp.concatenate([o.astype(f_mix.dtype), f_mix], axis=-1) @ w_out


def mix_gla_fnet(u, uc, need_ctx, w_in, gw_f, gb_f, gw_b, gb_b, g_norm, w_out):
    qc, kc, vc, gfc, gbc, rc, fc = _gla_fnet_project(uc, w_in, gw_f, gb_f, gw_b, gb_b, rotary=False)
    zeros = jnp.zeros((uc.shape[0], GLA_HEADS, GLA_DK, GLA_DV), F32)
    oc, s_f, s_b = gla_bidirectional(qc, kc, vc, gfc, gbc, zeros, zeros)
    q, k, v, gf, gb, r, f = _gla_fnet_project(u, w_in, gw_f, gb_f, gw_b, gb_b, rotary=True)
    o, _, _ = gla_bidirectional(q, k, v, gf, gb, s_f, s_b)
    y = _gla_fnet_out(o, r, f, g_norm, w_out)
    yc = _gla_fnet_out(oc, rc, fc, g_norm, w_out) if need_ctx else None
    return y, yc


def mix_neighbourhood(u, uc, need_ctx, w_qkv, rpb, w_out):
    b, t, _ = u.shape
    rows = t // GRID_W
    kh = min(NA_KH_MAX, rows)
    scale = NA_HEAD_DIM ** -0.5
    q, k, v = (split_heads(z, NA_HEADS) for z in jnp.split(u @ w_qkv, 3, axis=-1))
    q = q * scale
    kc, vc = (split_heads(z, NA_HEADS) for z in jnp.split(uc @ w_qkv[:, D_MODEL:], 2, axis=-1))
    grid = lambda z: z.reshape(b, NA_HEADS, rows, GRID_W, NA_HEAD_DIM)
    qg, kg, vg = grid(q), grid(k), grid(v)
    col = jnp.arange(GRID_W)
    col_start = jnp.clip(col - NA_KW // 2, 0, GRID_W - NA_KW)
    col_mask = (col[None, :] >= col_start[:, None]) & (col[None, :] < col_start[:, None] + NA_KW)
    dc_idx = jnp.clip(col[None, :] - col[:, None] + NA_KW - 1, 0, 2 * NA_KW - 2)
    rpb32 = rpb.astype(F32)
    n_loc = kh * GRID_W

    def row_block(r):
        r0 = jnp.clip(r - kh // 2, 0, rows - kh)
        kb = lax.dynamic_slice_in_dim(kg, r0, kh, axis=2)
        vb = lax.dynamic_slice_in_dim(vg, r0, kh, axis=2)
        qr = lax.dynamic_index_in_dim(qg, r, axis=2, keepdims=False)
        dr_idx = r0 + jnp.arange(kh) - r + NA_KH_MAX - 1
        bias = rpb32[:, dr_idx[:, None, None], dc_idx[None, :, :]].transpose(0, 2, 1, 3)
        s_loc = jnp.einsum('bhqd,bhikd->bhqik', qr, kb).astype(F32) + bias
        s_loc = jnp.where(col_mask[:, None, :], s_loc, -jnp.inf).reshape(b, NA_HEADS, GRID_W, n_loc)
        s_ctx = jnp.einsum('bhqd,bhcd->bhqc', qr, kc).astype(F32)
        p = jax.nn.softmax(jnp.concatenate([s_loc, s_ctx], axis=-1), axis=-1).astype(v.dtype)
        p_loc = p[..., :n_loc].reshape(b, NA_HEADS, GRID_W, kh, GRID_W)
        return (jnp.einsum('bhqik,bhikd->bhqd', p_loc, vb)
                + jnp.einsum('bhqc,bhcd->bhqd', p[..., n_loc:], vc))

    o = lax.map(row_block, jnp.arange(rows))
    o = o.transpose(1, 2, 0, 3, 4).reshape(b, NA_HEADS, t, NA_HEAD_DIM)
    y = merge_heads(o) @ w_out
    yc = None
    if need_ctx:
        qc = split_heads(uc @ w_qkv[:, :D_MODEL], NA_HEADS) * scale
        pc = jax.nn.softmax(jnp.einsum('bhqd,bhcd->bhqc', qc, kc).astype(F32), axis=-1).astype(vc.dtype)
        yc = merge_heads(jnp.einsum('bhqc,bhcd->bhqd', pc, vc)) @ w_out
    return y, yc


def trunk_layer(x, xc, c, c_ctx, common, mixer, last):
    ada_w, ada_b, n_ffn1, n_mix, n_ffn2, f1_gu, f1_down, f2_gu, f2_down = common
    m = jnp.split((jax.nn.silu(c) @ ada_w + ada_b)[:, None, :], N_MOD, axis=-1)
    mc = jnp.split((jax.nn.silu(c_ctx) @ ada_w + ada_b)[None, None, :], N_MOD, axis=-1)
    sub_in = lambda h, g, mm, j: modulate(rms_norm(h, g), mm[3 * j], mm[3 * j + 1])
    x = x + 0.5 * m[2] * swiglu(sub_in(x, n_ffn1, m, 0), f1_gu, f1_down)
    xc = xc + 0.5 * mc[2] * swiglu(sub_in(xc, n_ffn1, mc, 0), f1_gu, f1_down)
    y, yc = mixer(sub_in(x, n_mix, m, 1), sub_in(xc, n_mix, mc, 1), not last)
    x = x + m[5] * y
    x = x + 0.5 * m[8] * swiglu(sub_in(x, n_ffn2, m, 2), f2_gu, f2_down)
    if not last:
        xc = xc + mc[5] * yc
        xc = xc + 0.5 * mc[8] * swiglu(sub_in(xc, n_ffn2, mc, 2), f2_gu, f2_down)
    return x, xc


def setup_inputs(seed: int = 0) -> dict:
    key = jax.random.key(seed)
    keys = iter(jax.random.split(key, 48))
    D = D_MODEL

    def normal(shape, scale):
        return jax.random.normal(next(keys), shape, F32) * scale

    def gain(n):
        return 1.0 + 0.02 * jax.random.normal(next(keys), (n,), F32)

    def common(p):
        return {p + 'ada_w': normal((D, N_MOD * D), 0.5 * D ** -0.5),
                p + 'ada_b': normal((N_MOD * D,), 0.02),
                p + 'norm_ffn1': gain(D), p + 'norm_mix': gain(D), p + 'norm_ffn2': gain(D),
                p + 'ffn1_w_gu': normal((D, 2 * D_FF), D ** -0.5),
                p + 'ffn1_w_down': normal((D_FF, D), D_FF ** -0.5),
                p + 'ffn2_w_gu': normal((D, 2 * D_FF), D ** -0.5),
                p + 'ffn2_w_down': normal((D_FF, D), D_FF ** -0.5)}

    inputs = {'x': normal((BATCH, SEQ, D), 1.0),
              'c': normal((BATCH, D), 1.0),
              'ctx': normal((BATCH, CTX_LEN, D), 1.0),
              'c_ctx': normal((D,), 1.0)}
    inputs.update(common('l0_'))
    inputs.update({'l0_w_in': normal((D, AB_IN_W), D ** -0.5),
                   'l0_gla_gate_w_fwd': normal((GLA_LOWRANK, GLA_QK_W), GLA_LOWRANK ** -0.5),
                   'l0_gla_gate_b_fwd': normal((GLA_QK_W,), 0.02),
                   'l0_gla_gate_w_bwd': normal((GLA_LOWRANK, GLA_QK_W), GLA_LOWRANK ** -0.5),
                   'l0_gla_gate_b_bwd': normal((GLA_QK_W,), 0.02),
                   'l0_gla_norm': gain(GLA_V_W),
                   'l0_w_out': normal((AB_MIX_W, D), AB_MIX_W ** -0.5)})
    inputs.update(common('l1_'))
    inputs.update({'l1_w_qkv': normal((D, 3 * D), D ** -0.5),
                   'l1_rpb': normal((NA_HEADS, 2 * NA_KH_MAX - 1, 2 * NA_KW - 1), 0.1),
                   'l1_w_out': normal((D, D), D ** -0.5)})
    inputs['norm_out'] = gain(D)
    return inputs


def reference(x, c, ctx, c_ctx,
              l0_ada_w, l0_ada_b, l0_norm_ffn1, l0_norm_mix, l0_norm_ffn2,
              l0_ffn1_w_gu, l0_ffn1_w_down, l0_ffn2_w_gu, l0_ffn2_w_down,
              l0_w_in, l0_gla_gate_w_fwd, l0_gla_gate_b_fwd, l0_gla_gate_w_bwd, l0_gla_gate_b_bwd,
              l0_gla_norm, l0_w_out,
              l1_ada_w, l1_ada_b, l1_norm_ffn1, l1_norm_mix, l1_norm_ffn2,
              l1_ffn1_w_gu, l1_ffn1_w_down, l1_ffn2_w_gu, l1_ffn2_w_down,
              l1_w_qkv, l1_rpb, l1_w_out,
              norm_out):
    layers = [
        ((l0_ada_w, l0_ada_b, l0_norm_ffn1, l0_norm_mix, l0_norm_ffn2,
          l0_ffn1_w_gu, l0_ffn1_w_down, l0_ffn2_w_gu, l0_ffn2_w_down),
         lambda u, uc, need: mix_gla_fnet(u, uc, need, l0_w_in, l0_gla_gate_w_fwd, l0_gla_gate_b_fwd,
                                          l0_gla_gate_w_bwd, l0_gla_gate_b_bwd, l0_gla_norm, l0_w_out)),
        ((l1_ada_w, l1_ada_b, l1_norm_ffn1, l1_norm_mix, l1_norm_ffn2,
          l1_ffn1_w_gu, l1_ffn1_w_down, l1_ffn2_w_gu, l1_ffn2_w_down),
         lambda u, uc, need: mix_neighbourhood(u, uc, need, l1_w_qkv, l1_rpb, l1_w_out)),
    ]
    xc = ctx
    for i in range(DEPTH):
        common, mixer = layers[i]
        x, xc = trunk_layer(x, xc, c, c_ctx, common, mixer, last=(i == DEPTH - 1))
    return rms_norm(x, norm_out)
```

```python
import functools

import numpy as np
import jax
import jax.numpy as jnp
from jax import lax
from jax.experimental import pallas as pl
from jax.experimental.pallas import tpu as pltpu

D_MODEL = 2048
BATCH = 4
SEQ = 4096
GRID_W = 64
CTX_LEN = 256
N_MOD = 9
D_FF = 5632
RMS_EPS = 1e-6
ROPE_BASE = 10000.0

GLA_HEADS = 4
GLA_DK = 128
GLA_DV = 256
GLA_LOWRANK = 16
GLA_TAU = 16.0
GLA_CHUNK = 64
FNET_GROUPS = 4
FNET_GROUP_DIM = 256
GLA_QK_W = GLA_HEADS * GLA_DK
GLA_V_W = GLA_HEADS * GLA_DV
FNET_W = FNET_GROUPS * FNET_GROUP_DIM

NA_HEADS = 32
NA_HEAD_DIM = 64
NA_KH = 8
NA_KW = 16
NA_QROWS = 8
NA_KROWS = 16
NEG_BIG = -1e30

N_LAT = BATCH * SEQ
N_CTX = BATCH * CTX_LEN
N_TOK = N_LAT + N_CTX

F32 = jnp.float32
BF16 = jnp.bfloat16
MIB = 1024 * 1024


def _cparams(sem, vmem_mib):
    return pltpu.CompilerParams(dimension_semantics=sem, vmem_limit_bytes=vmem_mib * MIB)


def _dot(a, b):
    return jnp.dot(a, b, preferred_element_type=F32)


def _dot_nt(a, b):
    return lax.dot_general(a, b, (((1,), (1,)), ((), ())), preferred_element_type=F32)


def _dot_tn(a, b):
    return lax.dot_general(a, b, (((0,), (0,)), ((), ())), preferred_element_type=F32)


def _split2(x):
    hi = x.astype(BF16)
    lo = (x - hi.astype(F32)).astype(BF16)
    return hi, lo


def _dot3(a, b):
    ah, al = _split2(a)
    bh, bl = _split2(b)
    return _dot(ah, bh) + (_dot(al, bh) + _dot(ah, bl))


def _dot_exact_lhs(l_bf16, x):
    x1 = x.astype(BF16)
    r1 = x - x1.astype(F32)
    x2 = r1.astype(BF16)
    x3 = (r1 - x2.astype(F32)).astype(BF16)
    return _dot(l_bf16, x1) + (_dot(l_bf16, x2) + _dot(l_bf16, x3))


def _sigmoid(x):
    return 1.0 / (1.0 + jnp.exp(-x))


def _rms(x, g):
    ms = jnp.mean(x * x, axis=-1, keepdims=True)
    return x * lax.rsqrt(ms + RMS_EPS) * g


def _mod_row(i, tm):
    return jnp.minimum(i // (SEQ // tm), BATCH)


def _mod_spec(j, tm, width=D_MODEL, col=None):
    if col is None:
        return pl.BlockSpec((1, 1, width), lambda i, *_: (_mod_row(i, tm) * N_MOD + j, 0, 0))
    return pl.BlockSpec((1, 1, width), lambda i, n, *_: (_mod_row(i, tm) * N_MOD + j, 0, n))


ADA_TN = 1024


def _ada_kernel(c_ref, w_ref, b_ref, o_ref):
    c = c_ref[...]
    s = (c * _sigmoid(c)).astype(BF16)
    o_ref[...] = _dot(s, w_ref[...].astype(BF16)) + b_ref[...]


def _ada(cc, w, b):
    n = w.shape[1]
    out = pl.pallas_call(
        _ada_kernel,
        grid=(n // ADA_TN,),
        in_specs=[pl.BlockSpec((8, D_MODEL), lambda j: (0, 0)),
                  pl.BlockSpec((D_MODEL, ADA_TN), lambda j: (0, j)),
                  pl.BlockSpec((1, ADA_TN), lambda j: (0, j))],
        out_specs=pl.BlockSpec((8, ADA_TN), lambda j: (0, j)),
        out_shape=jax.ShapeDtypeStruct((8, n), F32),
        compiler_params=_cparams(("arbitrary",), 40),
        name="ada_mod",
    )(cc, w, b.reshape(1, n))
    return out.reshape(8 * N_MOD, 1, D_MODEL)


FFN_TM = 512
FFN_TF = 512


def _ffn_kernel(x_ref, g_ref, sh_ref, sc_ref, gt_ref, wg_ref, wu_ref, wd_ref, *rest, final_norm):
    if final_norm:
        go_ref, o_ref, h_ref, acc_ref = rest
    else:
        o_ref, h_ref, acc_ref = rest
    f = pl.program_id(1)

    @pl.when(f == 0)
    def _():
        y = _rms(x_ref[...], g_ref[...])
        h_ref[...] = (y * (1.0 + sc_ref[0]) + sh_ref[0]).astype(BF16)
        acc_ref[...] = jnp.zeros_like(acc_ref)

    h = h_ref[...]
    a = _dot(h, wg_ref[...])
    u = _dot(h, wu_ref[...])
    act = (a * _sigmoid(a) * u).astype(BF16)
    acc_ref[...] += _dot(act, wd_ref[...])

    @pl.when(f == pl.num_programs(1) - 1)
    def _():
        y = x_ref[...] + (0.5 * gt_ref[0]) * acc_ref[...]
        if final_norm:
            y = _rms(y, go_ref[...])
        o_ref[...] = y


def _ffn(xs, n_rows, mod, sub, norm_g, w_gu, w_down, final_g=None):
    tm, tf = FFN_TM, FFN_TF
    nf = D_FF // tf
    in_specs = [pl.BlockSpec((tm, D_MODEL), lambda i, f: (i, 0)),
                pl.BlockSpec((1, D_MODEL), lambda i, f: (0, 0)),
                _mod_spec(3 * sub, tm), _mod_spec(3 * sub + 1, tm), _mod_spec(3 * sub + 2, tm),
                pl.BlockSpec((D_MODEL, tf), lambda i, f: (0, f)),
                pl.BlockSpec((D_MODEL, tf), lambda i, f: (0, f + nf)),
                pl.BlockSpec((tf, D_MODEL), lambda i, f: (f, 0))]
    args = [xs, norm_g.reshape(1, D_MODEL), mod, mod, mod, w_gu, w_gu, w_down]
    if final_g is not None:
        in_specs.append(pl.BlockSpec((1, D_MODEL), lambda i, f: (0, 0)))
        args.append(final_g.reshape(1, D_MODEL))
    return pl.pallas_call(
        functools.partial(_ffn_kernel, final_norm=final_g is not None),
        grid=(n_rows // tm, nf),
        in_specs=in_specs,
        out_specs=pl.BlockSpec((tm, D_MODEL), lambda i, f: (i, 0)),
        out_shape=jax.ShapeDtypeStruct((n_rows, D_MODEL), F32),
        scratch_shapes=[pltpu.VMEM((tm, D_MODEL), BF16), pltpu.VMEM((tm, D_MODEL), F32)],
        compiler_params=_cparams(("parallel", "arbitrary"), 48),
        name="ffn_half",
    )(*args)


NMM_TM = 1024
NMM_TN = 1024


def _nmm_kernel(x_ref, g_ref, sh_ref, sc_ref, w_ref, *rest, narrow):
    if narrow:
        wn_ref, o_ref, on_ref, h_ref = rest
    else:
        o_ref, h_ref = rest

    @pl.when(pl.program_id(1) == 0)
    def _():
        y = _rms(x_ref[...], g_ref[...])
        h_ref[...] = (y * (1.0 + sc_ref[0]) + sh_ref[0]).astype(BF16)
        if narrow:
            on_ref[...] = _dot(h_ref[...], wn_ref[...])

    o_ref[...] = _dot(h_ref[...], w_ref[...]).astype(o_ref.dtype)


def _nmm(xs, mod, sub, norm_g, w, out_dtype, w_narrow=None):
    tm, tn = NMM_TM, NMM_TN
    n = w.shape[1]
    narrow = w_narrow is not None
    in_specs = [pl.BlockSpec((tm, D_MODEL), lambda i, j: (i, 0)),
                pl.BlockSpec((1, D_MODEL), lambda i, j: (0, 0)),
                _mod_spec(3 * sub, tm), _mod_spec(3 * sub + 1, tm),
                pl.BlockSpec((D_MODEL, tn), lambda i, j: (0, j))]
    args = [xs, norm_g.reshape(1, D_MODEL), mod, mod, w]
    out_specs = pl.BlockSpec((tm, tn), lambda i, j: (i, j))
    out_shape = jax.ShapeDtypeStruct((N_TOK, n), out_dtype)
    if narrow:
        nn = w_narrow.shape[1]
        in_specs.append(pl.BlockSpec((D_MODEL, nn), lambda i, j: (0, 0)))
        args.append(w_narrow)
        out_specs = [out_specs, pl.BlockSpec((tm, nn), lambda i, j: (i, 0))]
        out_shape = [out_shape, jax.ShapeDtypeStruct((N_TOK, nn), F32)]
    return pl.pallas_call(
        functools.partial(_nmm_kernel, narrow=narrow),
        grid=(N_TOK // tm, n // tn),
        in_specs=in_specs,
        out_specs=out_specs,
        out_shape=out_shape,
        scratch_shapes=[pltpu.VMEM((tm, D_MODEL), BF16)],
        compiler_params=_cparams(("parallel", "arbitrary"), 48),
        name="norm_mod_matmul",
    )(*args)


GLA_CTX_CHUNKS = CTX_LEN // GLA_CHUNK
GLA_LAT_CHUNKS = SEQ // GLA_CHUNK
GLA_STEPS = GLA_CTX_CHUNKS + GLA_LAT_CHUNKS
GLA_LR_PAD = 128


def _rope(x, cos, sin_signed, first_half):
    partner = jnp.where(first_half, pltpu.roll(x, 96, 1), pltpu.roll(x, 32, 1))
    return x * cos + partner * sin_signed


def _gla_kernel(qf, kf, vf, lf, cf, sf, qb, kb, vb, lb, cb, sb, gwf, gbf, gwb, gbb,
                of_ref, ob_ref, st_ref):
    @pl.when(pl.program_id(1) == 0)
    def _():
        st_ref[...] = jnp.zeros_like(st_ref)

    row = lax.broadcasted_iota(jnp.int32, (GLA_CHUNK, GLA_CHUNK), 0)
    col = lax.broadcasted_iota(jnp.int32, (GLA_CHUNK, GLA_CHUNK), 1)
    lane = lax.broadcasted_iota(jnp.int32, (GLA_CHUNK, GLA_DK), 1)
    first_half = (lane % 64) < 32
    scale = GLA_DK ** -0.5
    mid_row = GLA_CHUNK // 2

    dirs = ((qf, kf, vf, lf, cf, sf, gwf, gbf, of_ref), (qb, kb, vb, lb, cb, sb, gwb, gbb, ob_ref))
    for d, (q_ref, k_ref, v_ref, l_ref, c_ref, s_ref, gw_ref, gb_ref, o_ref) in enumerate(dirs):
        tri = (row >= col) if d == 0 else (row <= col)
        tri_b = jnp.where(tri, 1.0, 0.0).astype(BF16)
        x = _dot3(l_ref[...], gw_ref[...]) + gb_ref[...]
        g = (jnp.minimum(x, 0.0) - jnp.log1p(jnp.exp(-jnp.abs(x)))) * (1.0 / GLA_TAU)
        bc = _dot_exact_lhs(tri_b, g)
        mid = bc[mid_row:mid_row + 1, :]
        last = bc[GLA_CHUNK - 1:GLA_CHUNK, :] if d == 0 else bc[0:1, :]
        e_q = jnp.exp(bc)
        e_qm = jnp.exp(bc - mid)
        e_km = jnp.exp(mid - bc)
        e_kl = jnp.exp(last - bc)
        e_l = jnp.exp(last)
        cos = c_ref[...]
        sin = s_ref[...]
        for h in range(GLA_HEADS):
            sk = slice(h * GLA_DK, (h + 1) * GLA_DK)
            sv = slice(h * GLA_DV, (h + 1) * GLA_DV)
            qh = _rope(q_ref[:, sk] * scale, cos, sin, first_half)
            kh = _rope(k_ref[:, sk], cos, sin, first_half)
            vh = v_ref[:, sv].astype(BF16)
            att = _dot_nt((qh * e_qm[:, sk]).astype(BF16), (kh * e_km[:, sk]).astype(BF16))
            att = jnp.where(tri, att, 0.0)
            st = st_ref[d, h]
            o = _dot_nt((qh * e_q[:, sk]).astype(BF16), st.astype(BF16))
            o = o + _dot(att.astype(BF16), vh)
            o_ref[:, sv] = o
            kd = (kh * e_kl[:, sk]).astype(BF16)
            st_ref[d, h] = st * e_l[:, sk] + _dot_tn(vh, kd)


def _gla(proj, lr, cos_t, sin_t, gwf, gbf, gwb, gbb):
    c = GLA_CHUNK
    ctx0 = N_LAT // c

    def fwd_blk(b, s):
        return jnp.where(s < GLA_CTX_CHUNKS, ctx0 + GLA_CTX_CHUNKS * b + s,
                         GLA_LAT_CHUNKS * b + s - GLA_CTX_CHUNKS)

    def bwd_blk(b, s):
        return jnp.where(s < GLA_CTX_CHUNKS, ctx0 + GLA_CTX_CHUNKS * b + (GLA_CTX_CHUNKS - 1 - s),
                         GLA_LAT_CHUNKS * b + (GLA_STEPS - 1 - s))

    def fwd_rope(b, s):
        return jnp.where(s < GLA_CTX_CHUNKS, GLA_LAT_CHUNKS, s - GLA_CTX_CHUNKS)

    def bwd_rope(b, s):
        return jnp.where(s < GLA_CTX_CHUNKS, GLA_LAT_CHUNKS, GLA_STEPS - 1 - s)

    def dir_specs(blk, rope):
        return [pl.BlockSpec((c, GLA_QK_W), lambda b, s: (blk(b, s), 0)),
                pl.BlockSpec((c, GLA_QK_W), lambda b, s: (blk(b, s), 1)),
                pl.BlockSpec((c, GLA_V_W), lambda b, s: (blk(b, s), 1)),
                pl.BlockSpec((c, GLA_LR_PAD), lambda b, s: (blk(b, s), 0)),
                pl.BlockSpec((c, GLA_DK), lambda b, s: (rope(b, s), 0)),
                pl.BlockSpec((c, GLA_DK), lambda b, s: (rope(b, s), 0))]

    const = lambda shape: pl.BlockSpec(shape, lambda b, s: (0, 0))
    in_specs = (dir_specs(fwd_blk, fwd_rope) + dir_specs(bwd_blk, bwd_rope)
                + [const((GLA_LR_PAD, GLA_QK_W)), const((1, GLA_QK_W)),
                   const((GLA_LR_PAD, GLA_QK_W)), const((1, GLA_QK_W))])
    return pl.pallas_call(
        _gla_kernel,
        grid=(BATCH, GLA_STEPS),
        in_specs=in_specs,
        out_specs=[pl.BlockSpec((c, GLA_V_W), lambda b, s: (fwd_blk(b, s), 0)),
                   pl.BlockSpec((c, GLA_V_W), lambda b, s: (bwd_blk(b, s), 0))],
        out_shape=[jax.ShapeDtypeStruct((N_TOK, GLA_V_W), F32)] * 2,
        scratch_shapes=[pltpu.VMEM((2, GLA_HEADS, GLA_DV, GLA_DK), F32)],
        compiler_params=_cparams(("parallel", "arbitrary"), 32),
        name="gla_scan",
    )(proj, proj, proj, lr, cos_t, sin_t, proj, proj, proj, lr, cos_t, sin_t, gwf, gbf, gwb, gbb)


CDFT_TM = 1024
PROJ_F_COL0 = 3072


def _cdft_kernel(x_ref, w_ref, zc_ref, zs_ref):
    z = _dot(x_ref[...].astype(BF16), w_ref[...])
    zc_ref[...] = z[:, :FNET_GROUP_DIM].astype(BF16)
    zs_ref[...] = z[:, FNET_GROUP_DIM:].astype(BF16)


def _chan_dft(proj, wc):
    tm, gd = CDFT_TM, FNET_GROUP_DIM
    col0 = PROJ_F_COL0 // gd
    return pl.pallas_call(
        _cdft_kernel,
        grid=(N_TOK // tm, FNET_GROUPS),
        in_specs=[pl.BlockSpec((tm, gd), lambda i, g: (i, col0 + g)),
                  pl.BlockSpec((gd, 2 * gd), lambda i, g: (0, 0))],
        out_specs=[pl.BlockSpec((tm, gd), lambda i, g: (i, g))] * 2,
        out_shape=[jax.ShapeDtypeStruct((N_TOK, FNET_W), BF16)] * 2,
        compiler_params=_cparams(("parallel", "parallel"), 32),
        name="fnet_channel_dft",
    )(proj, wc)


def _tdft_kernel(c_ref, s_ref, zc_ref, zs_ref, *rest):
    o_ref = rest[-1]
    o_ref[...] = _dot(c_ref[...], zc_ref[...]) - _dot(s_ref[...], zs_ref[...])


def _time_dft(cmat, smat, zc, zs, t_len, tm, tn, row0, prev=None):
    zb0 = row0 // t_len
    ob0 = row0 // tm
    mt = t_len // tm
    in_specs = [pl.BlockSpec((tm, t_len), lambda b, n, m: (m, 0)),
                pl.BlockSpec((tm, t_len), lambda b, n, m: (m, 0)),
                pl.BlockSpec((t_len, tn), lambda b, n, m: (zb0 + b, n)),
                pl.BlockSpec((t_len, tn), lambda b, n, m: (zb0 + b, n))]
    args = [cmat, smat, zc, zs]
    aliases = {}
    if prev is not None:
        in_specs.append(pl.BlockSpec(memory_space=pl.ANY))
        args.append(prev)
        aliases = {4: 0}
    return pl.pallas_call(
        _tdft_kernel,
        grid=(BATCH, FNET_W // tn, mt),
        in_specs=in_specs,
        out_specs=pl.BlockSpec((tm, tn), lambda b, n, m: (ob0 + b * mt + m, n)),
        out_shape=jax.ShapeDtypeStruct((N_TOK, FNET_W), F32),
        input_output_aliases=aliases,
        compiler_params=_cparams(("parallel", "parallel", "arbitrary"), 44),
        name="fnet_time_dft",
    )(*args)


def _dft_tables(n):
    k = np.arange(n)
    ang = 2.0 * np.pi * ((k[:, None] * k[None, :]) % n) / n
    return np.cos(ang) / np.sqrt(n), np.sin(ang) / np.sqrt(n)


def _big_dft_tables():
    r = GRID_W
    k = np.arange(r)
    a = 2.0 * np.pi * ((k[:, None] * k[None, :]) % r) / r
    c = 2.0 * np.pi * (k[:, None] * k[None, :]) / SEQ
    ca, sa = jnp.asarray(np.cos(a), F32), jnp.asarray(np.sin(a), F32)
    cc, sc = jnp.asarray(np.cos(c), F32), jnp.asarray(np.sin(c), F32)
    ca1, sa1 = ca[:, None, None, :], sa[:, None, None, :]
    ca2, sa2 = ca[None, :, :, None], sa[None, :, :, None]
    cc3, sc3 = cc[None, :, None, :], sc[None, :, None, :]
    re = ca1 * ca2 - sa1 * sa2
    im = ca1 * sa2 + sa1 * ca2
    cos_t = (re * cc3 - im * sc3) * (SEQ ** -0.5)
    sin_t = (re * sc3 + im * cc3) * (SEQ ** -0.5)
    return cos_t.reshape(SEQ, SEQ).astype(BF16), sin_t.reshape(SEQ, SEQ).astype(BF16)


GO_TM = 512
GO_TN = 1024


def _glaout_kernel(of_ref, ob_ref, r_ref, fm_ref, gn_ref, x_ref, gt_ref, w_ref, o_ref, h_ref):
    @pl.when(pl.program_id(1) == 0)
    def _():
        for h in range(GLA_HEADS):
            sv = slice(h * GLA_DV, (h + 1) * GLA_DV)
            o = of_ref[:, sv] + ob_ref[:, sv]
            o = o * lax.rsqrt(jnp.mean(o * o, axis=-1, keepdims=True) + RMS_EPS)
            r = r_ref[:, sv]
            h_ref[:, sv] = (o * gn_ref[:, sv] * (r * _sigmoid(r))).astype(BF16)
        h_ref[:, GLA_V_W:] = fm_ref[...].astype(BF16)

    o_ref[...] = x_ref[...] + gt_ref[0] * _dot(h_ref[...], w_ref[...])


def _glaout(of, ob, proj, fm, g_norm, xs, mod, w_out):
    tm, tn = GO_TM, GO_TN
    half = lambda col: pl.BlockSpec((tm, GLA_V_W), lambda i, j: (i, col))
    return pl.pallas_call(
        _glaout_kernel,
        grid=(N_TOK // tm, D_MODEL // tn),
        in_specs=[half(0), half(0), half(2), half(0),
                  pl.BlockSpec((1, GLA_V_W), lambda i, j: (0, 0)),
                  pl.BlockSpec((tm, tn), lambda i, j: (i, j)),
                  _mod_spec(5, tm, tn, col=True),
                  pl.BlockSpec((D_MODEL, tn), lambda i, j: (0, j))],
        out_specs=pl.BlockSpec((tm, tn), lambda i, j: (i, j)),
        out_shape=jax.ShapeDtypeStruct((N_TOK, D_MODEL), F32),
        scratch_shapes=[pltpu.VMEM((tm, D_MODEL), BF16)],
        compiler_params=_cparams(("parallel", "arbitrary"), 56),
        name="gla_fnet_out",
    )(of, ob, proj, fm, g_norm.reshape(1, GLA_V_W), xs, mod, w_out)


NA_TQ = NA_QROWS * GRID_W
NA_TK = NA_KROWS * GRID_W
NA_KBLK = 256


def _na_kernel(q_ref, k0, k1, k2, k3, v0, v1, v2, v3, kc_ref, vc_ref, b_ref, o_ref):
    q = q_ref[...] * (NA_HEAD_DIM ** -0.5)
    k = jnp.concatenate([k0[...], k1[...], k2[...], k3[...]], axis=0)
    v = jnp.concatenate([v0[...], v1[...], v2[...], v3[...]], axis=0)
    kc = kc_ref[...]
    vc = vc_ref[...]
    lane = lax.broadcasted_iota(jnp.int32, (NA_TQ, 2 * NA_HEAD_DIM), 1)
    out = jnp.zeros((NA_TQ, 2 * NA_HEAD_DIM), F32)
    for hh in range(2):
        mine = (lane // NA_HEAD_DIM) == hh
        qm = jnp.where(mine, q, jnp.zeros_like(q))
        s_loc = _dot_nt(qm, k) + b_ref[hh, 0]
        s_ctx = _dot_nt(qm, kc)
        m = jnp.maximum(jnp.max(s_loc, axis=-1, keepdims=True), jnp.max(s_ctx, axis=-1, keepdims=True))
        p_loc = jnp.exp(s_loc - m)
        p_ctx = jnp.exp(s_ctx - m)
        denom = jnp.sum(p_loc, axis=-1, keepdims=True) + jnp.sum(p_ctx, axis=-1, keepdims=True)
        o = (_dot(p_loc.astype(BF16), v) + _dot(p_ctx.astype(BF16), vc)) / denom
        out = jnp.where(mine, o, out)
    o_ref[...] = out.astype(o_ref.dtype)


def _na(qkv, bias):
    hp_n = NA_HEADS // 2
    jn = (SEQ // GRID_W) // NA_QROWS
    kb_per_img = SEQ // NA_KBLK
    last_start = kb_per_img - NA_TK // NA_KBLK

    def kstart(j):
        return jnp.clip(2 * j - 1, 0, last_start)

    def case(j):
        return jnp.where(j == 0, 0, jnp.where(j == jn - 1, 2, 1))

    def kv_specs(col0):
        return [pl.BlockSpec((NA_KBLK, 128), functools.partial(
            lambda hp, j, b, t: (b * kb_per_img + kstart(j) + t, col0 + hp), t=t)) for t in range(4)]

    ctx0 = N_LAT // CTX_LEN
    in_specs = ([pl.BlockSpec((NA_TQ, 128), lambda hp, j, b: (b * jn + j, hp))]
                + kv_specs(hp_n) + kv_specs(2 * hp_n)
                + [pl.BlockSpec((CTX_LEN, 128), lambda hp, j, b: (ctx0 + b, hp_n + hp)),
                   pl.BlockSpec((CTX_LEN, 128), lambda hp, j, b: (ctx0 + b, 2 * hp_n + hp)),
                   pl.BlockSpec((2, 1, NA_TQ, NA_TK), lambda hp, j, b: (hp, case(j), 0, 0))])
    return pl.pallas_call(
        _na_kernel,
        grid=(hp_n, jn, BATCH),
        in_specs=in_specs,
        out_specs=pl.BlockSpec((NA_TQ, 128), lambda hp, j, b: (b * jn + j, hp)),
        out_shape=jax.ShapeDtypeStruct((N_LAT, D_MODEL), BF16),
        compiler_params=_cparams(("parallel", "arbitrary", "arbitrary"), 48),
        name="neighbourhood_attention",
    )(qkv, *([qkv] * 8), qkv, qkv, bias)


def _na_bias(rpb):
    w = GRID_W
    rows = SEQ // GRID_W
    col = np.arange(w)
    cs = np.clip(col - NA_KW // 2, 0, w - NA_KW)
    col_mask = (col[None, :] >= cs[:, None]) & (col[None, :] < cs[:, None] + NA_KW)
    dc_idx = np.clip(col[None, :] - col[:, None] + NA_KW - 1, 0, 2 * NA_KW - 2)
    t1 = rpb.astype(F32)[:, :, dc_idx]
    t1 = jnp.where(col_mask[None, None], t1, NEG_BIG)
    t1 = jnp.concatenate([t1, jnp.full((NA_HEADS, 1, w, w), NEG_BIG, F32)], axis=1)
    idx = np.full((3, NA_QROWS, NA_KROWS), 2 * NA_KH - 1, np.int32)
    jn = rows // NA_QROWS
    for cse, j in enumerate((0, 1, jn - 1)):
        start = int(np.clip(NA_QROWS * j - NA_KH // 2, 0, rows - NA_KROWS))
        for i in range(NA_QROWS):
            qr = NA_QROWS * j + i
            r0 = int(np.clip(qr - NA_KH // 2, 0, rows - NA_KH))
            for l in range(NA_KROWS):
                kr = start + l
                if r0 <= kr < r0 + NA_KH:
                    idx[cse, i, l] = kr - qr + NA_KH - 1
    b = t1[:, idx]
    return b.transpose(0, 1, 2, 4, 3, 5).reshape(NA_HEADS, 3, NA_TQ, NA_TK)


MR_TM = 1024
MR_TN = 1024


def _mmres_kernel(a_ref, w_ref, x_ref, gt_ref, o_ref):
    o_ref[...] = x_ref[...] + gt_ref[0] * _dot(a_ref[...], w_ref[...])


def _mm_res(a, w, xs, mod, n_rows):
    tm, tn = MR_TM, MR_TN
    return pl.pallas_call(
        _mmres_kernel,
        grid=(n_rows // tm, D_MODEL // tn),
        in_specs=[pl.BlockSpec((tm, D_MODEL), lambda i, j: (i, 0)),
                  pl.BlockSpec((D_MODEL, tn), lambda i, j: (0, j)),
                  pl.BlockSpec((tm, tn), lambda i, j: (i, j)),
                  _mod_spec(5, tm, tn, col=True)],
        out_specs=pl.BlockSpec((tm, tn), lambda i, j: (i, j)),
        out_shape=jax.ShapeDtypeStruct((n_rows, D_MODEL), F32),
        compiler_params=_cparams(("parallel", "arbitrary"), 40),
        name="matmul_gated_residual",
    )(a, w, xs, mod)


def _rope_tables():
    half = GLA_DK // 4
    inv_freq = ROPE_BASE ** (-jnp.arange(half, dtype=F32) / half)
    t = jnp.arange(SEQ)
    ang_r = (t // GRID_W).astype(F32)[:, None] * inv_freq[None, :]
    ang_c = (t % GRID_W).astype(F32)[:, None] * inv_freq[None, :]
    cr, sr, cc, sc = jnp.cos(ang_r), jnp.sin(ang_r), jnp.cos(ang_c), jnp.sin(ang_c)
    cos_t = jnp.concatenate([cr, cr, cc, cc], axis=-1)
    sin_t = jnp.concatenate([-sr, sr, -sc, sc], axis=-1)
    cos_t = jnp.concatenate([cos_t, jnp.ones((GLA_CHUNK, GLA_DK), F32)], axis=0)
    sin_t = jnp.concatenate([sin_t, jnp.zeros((GLA_CHUNK, GLA_DK), F32)], axis=0)
    return cos_t, sin_t


def kernel(x, c, ctx, c_ctx, l0_ada_w, l0_ada_b, l0_norm_ffn1, l0_norm_mix, l0_norm_ffn2, l0_ffn1_w_gu, l0_ffn1_w_down, l0_ffn2_w_gu, l0_ffn2_w_down, l0_w_in, l0_gla_gate_w_fwd, l0_gla_gate_b_fwd, l0_gla_gate_w_bwd, l0_gla_gate_b_bwd, l0_gla_norm, l0_w_out, l1_ada_w, l1_ada_b, l1_norm_ffn1, l1_norm_mix, l1_norm_ffn2, l1_ffn1_w_gu, l1_ffn1_w_down, l1_ffn2_w_gu, l1_ffn2_w_down, l1_w_qkv, l1_rpb, l1_w_out, norm_out):
    bf = lambda w: w.astype(BF16)
    xs = jnp.concatenate([x.reshape(N_LAT, D_MODEL), ctx.reshape(N_CTX, D_MODEL)], axis=0)
    cc = jnp.concatenate([c, c_ctx[None, :], jnp.zeros((8 - BATCH - 1, D_MODEL), F32)], axis=0)

    mod0 = _ada(cc, l0_ada_w, l0_ada_b)
    xs = _ffn(xs, N_TOK, mod0, 0, l0_norm_ffn1, bf(l0_ffn1_w_gu), bf(l0_ffn1_w_down))

    a0 = 2 * GLA_QK_W + GLA_V_W
    a1 = a0 + 2 * GLA_LOWRANK
    w_main = bf(jnp.concatenate([l0_w_in[:, :a0], l0_w_in[:, a1:]], axis=1))
    w_lr = bf(jnp.pad(l0_w_in[:, a0:a1], ((0, 0), (0, GLA_LR_PAD - 2 * GLA_LOWRANK))))
    proj, lr = _nmm(xs, mod0, 1, l0_norm_mix, w_main, F32, w_narrow=w_lr)

    cos_t, sin_t = _rope_tables()
    gwf = jnp.pad(l0_gla_gate_w_fwd, ((0, GLA_LR_PAD - GLA_LOWRANK), (0, 0)))
    gwb = jnp.pad(l0_gla_gate_w_bwd, ((GLA_LOWRANK, GLA_LR_PAD - 2 * GLA_LOWRANK), (0, 0)))
    of, ob = _gla(proj, lr, cos_t, sin_t, gwf, l0_gla_gate_b_fwd.reshape(1, -1),
                  gwb, l0_gla_gate_b_bwd.reshape(1, -1))

    cc_np, sc_np = _dft_tables(FNET_GROUP_DIM)
    wc = jnp.asarray(np.concatenate([cc_np, sc_np], axis=1), F32).astype(BF16)
    zc, zs = _chan_dft(proj, wc)
    cos_big, sin_big = _big_dft_tables()
    fm = _time_dft(cos_big, sin_big, zc, zs, SEQ, 512, 512, 0)
    ct_np, st_np = _dft_tables(CTX_LEN)
    fm = _time_dft(jnp.asarray(ct_np, F32).astype(BF16), jnp.asarray(st_np, F32).astype(BF16),
                   zc, zs, CTX_LEN, CTX_LEN, 512, N_LAT, prev=fm)

    xs = _glaout(of, ob, proj, fm, l0_gla_norm, xs, mod0, bf(l0_w_out))
    xs = _ffn(xs, N_TOK, mod0, 2, l0_norm_ffn2, bf(l0_ffn2_w_gu), bf(l0_ffn2_w_down))

    mod1 = _ada(cc, l1_ada_w, l1_ada_b)
    xs = _ffn(xs, N_TOK, mod1, 0, l1_norm_ffn1, bf(l1_ffn1_w_gu), bf(l1_ffn1_w_down))
    qkv = _nmm(xs, mod1, 1, l1_norm_mix, bf(l1_w_qkv), BF16)
    ao = _na(qkv, _na_bias(l1_rpb))
    xl = _mm_res(ao, bf(l1_w_out), xs, mod1, N_LAT)
    out = _ffn(xl, N_LAT, mod1, 2, l1_norm_ffn2, bf(l1_ffn2_w_gu), bf(l1_ffn2_w_down), final_g=norm_out)
    return out.reshape(BATCH, SEQ, D_MODEL)
```

```python
import functools

import numpy as np
import jax
import jax.numpy as jnp
from jax import lax
from jax.experimental import pallas as pl
from jax.experimental.pallas import tpu as pltpu

D_MODEL = 2048
BATCH = 4
SEQ = 4096
GRID_W = 64
CTX_LEN = 256
N_MOD = 9
D_FF = 5632
RMS_EPS = 1e-6
ROPE_BASE = 10000.0

GLA_HEADS = 4
GLA_DK = 128
GLA_DV = 256
GLA_LOWRANK = 16
GLA_TAU = 16.0
GLA_CHUNK = 64
FNET_GROUPS = 4
FNET_GROUP_DIM = 256
GLA_QK_W = GLA_HEADS * GLA_DK
GLA_V_W = GLA_HEADS * GLA_DV
FNET_W = FNET_GROUPS * FNET_GROUP_DIM

NA_HEADS = 32
NA_HEAD_DIM = 64
NA_KH = 8
NA_KW = 16
NA_QROWS = 8
NEG_BIG = -1e30

N_LAT = BATCH * SEQ
N_CTX = BATCH * CTX_LEN
N_TOK = N_LAT + N_CTX

F32 = jnp.float32
BF16 = jnp.bfloat16
MIB = 1024 * 1024


def _cparams(sem, vmem_mib):
    return pltpu.CompilerParams(dimension_semantics=sem, vmem_limit_bytes=vmem_mib * MIB)


def _dot(a, b):
    return jnp.dot(a, b, preferred_element_type=F32)


def _dot_nt(a, b):
    return lax.dot_general(a, b, (((1,), (1,)), ((), ())), preferred_element_type=F32)


def _dot_tn(a, b):
    return lax.dot_general(a, b, (((0,), (0,)), ((), ())), preferred_element_type=F32)


def _split2(x):
    hi = x.astype(BF16)
    lo = (x - hi.astype(F32)).astype(BF16)
    return hi, lo


def _dot3(a, b):
    ah, al = _split2(a)
    bh, bl = _split2(b)
    return _dot(ah, bh) + (_dot(al, bh) + _dot(ah, bl))


def _dot_exact_lhs(l_bf16, x):
    x1 = x.astype(BF16)
    r1 = x - x1.astype(F32)
    x2 = r1.astype(BF16)
    x3 = (r1 - x2.astype(F32)).astype(BF16)
    return _dot(l_bf16, x1) + (_dot(l_bf16, x2) + _dot(l_bf16, x3))


def _sigmoid(x):
    return 1.0 / (1.0 + jnp.exp(-x))


def _rms(x, g):
    ms = jnp.mean(x * x, axis=-1, keepdims=True)
    return x * lax.rsqrt(ms + RMS_EPS) * g


ROW_CHUNK = 16
ROW_GROUP = 8


def _row_groups(n_rows, body, group=ROW_GROUP):
    span = ROW_CHUNK * group

    def step(t, carry):
        base = t * span
        body([pl.ds(pl.multiple_of(base + c * ROW_CHUNK, ROW_CHUNK), ROW_CHUNK) for c in range(group)])
        return carry
    lax.fori_loop(0, n_rows // span, step, 0)


def _inv_rms(x):
    return lax.rsqrt(jnp.mean(x * x, axis=-1, keepdims=True) + RMS_EPS)


def _norm_mod_rows(x_ref, g_ref, sh_ref, sc_ref, h_ref, gs_ref, zero_ref=None):
    d = x_ref.shape[1]
    gs_ref[0] = jnp.broadcast_to(g_ref[...] * (1.0 + sc_ref[0]), (ROW_CHUNK, d))
    gs_ref[1] = jnp.broadcast_to(sh_ref[0], (ROW_CHUNK, d))

    def body(chunks):
        inv = [_inv_rms(x_ref[rows, :]) for rows in chunks]
        for rows, r in zip(chunks, inv):
            h_ref[rows, :] = (x_ref[rows, :] * r * gs_ref[0] + gs_ref[1]).astype(BF16)
            if zero_ref is not None:
                zero_ref[rows, :] = jnp.zeros((ROW_CHUNK, zero_ref.shape[1]), zero_ref.dtype)
    _row_groups(x_ref.shape[0], body)


def _mod_row(i, tm):
    return jnp.minimum(i // (SEQ // tm), BATCH)


def _mod_spec(j, tm, width=D_MODEL, col=None):
    if col is None:
        return pl.BlockSpec((1, 1, width), lambda i, *_: (_mod_row(i, tm) * N_MOD + j, 0, 0))
    return pl.BlockSpec((1, 1, width), lambda i, n, *_: (_mod_row(i, tm) * N_MOD + j, 0, n))


ADA_TN = 1024


def _ada_kernel(c_ref, w_ref, b_ref, o_ref):
    c = c_ref[...]
    s = (c * _sigmoid(c)).astype(BF16)
    o_ref[...] = _dot(s, w_ref[...].astype(BF16)) + b_ref[...]


def _ada(cc, w, b):
    n = w.shape[1]
    out = pl.pallas_call(
        _ada_kernel,
        grid=(n // ADA_TN,),
        in_specs=[pl.BlockSpec((8, D_MODEL), lambda j: (0, 0)),
                  pl.BlockSpec((D_MODEL, ADA_TN), lambda j: (0, j)),
                  pl.BlockSpec((1, ADA_TN), lambda j: (0, j))],
        out_specs=pl.BlockSpec((8, ADA_TN), lambda j: (0, j)),
        out_shape=jax.ShapeDtypeStruct((8, n), F32),
        compiler_params=_cparams(("arbitrary",), 40),
        name="ada_mod",
    )(cc, w, b.reshape(1, n))
    return out.reshape(8 * N_MOD, 1, D_MODEL)


FFN_TM = 512
FFN_TF = 512


def _ffn_kernel(x_ref, g_ref, sh_ref, sc_ref, gt_ref, wg_ref, wu_ref, wd_ref, *rest, final_norm):
    if final_norm:
        go_ref, o_ref, h_ref, acc_ref, gs_ref = rest
    else:
        o_ref, h_ref, acc_ref, gs_ref = rest
    f = pl.program_id(1)

    @pl.when(f == 0)
    def _():
        _norm_mod_rows(x_ref, g_ref, sh_ref, sc_ref, h_ref, gs_ref, zero_ref=acc_ref)

    h = h_ref[...]
    a = _dot(h, wg_ref[...])
    u = _dot(h, wu_ref[...])
    act = (a * _sigmoid(a) * u).astype(BF16)
    acc_ref[...] += _dot(act, wd_ref[...])

    @pl.when(f == pl.num_programs(1) - 1)
    def _():
        if not final_norm:
            o_ref[...] = x_ref[...] + (0.5 * gt_ref[0]) * acc_ref[...]
        else:
            d = x_ref.shape[1]
            gs_ref[0] = jnp.broadcast_to(0.5 * gt_ref[0], (ROW_CHUNK, d))
            gs_ref[1] = jnp.broadcast_to(go_ref[...], (ROW_CHUNK, d))

            def body(chunks):
                resid = lambda rows: x_ref[rows, :] + gs_ref[0] * acc_ref[rows, :]
                inv = [_inv_rms(resid(rows)) for rows in chunks]
                for rows, r in zip(chunks, inv):
                    o_ref[rows, :] = resid(rows) * r * gs_ref[1]
            _row_groups(x_ref.shape[0], body)


def _ffn(xs, n_rows, mod, sub, norm_g, w_gu, w_down, final_g=None):
    tm, tf = FFN_TM, FFN_TF
    nf = D_FF // tf
    in_specs = [pl.BlockSpec((tm, D_MODEL), lambda i, f: (i, 0)),
                pl.BlockSpec((1, D_MODEL), lambda i, f: (0, 0)),
                _mod_spec(3 * sub, tm), _mod_spec(3 * sub + 1, tm), _mod_spec(3 * sub + 2, tm),
                pl.BlockSpec((D_MODEL, tf), lambda i, f: (0, f)),
                pl.BlockSpec((D_MODEL, tf), lambda i, f: (0, f + nf)),
                pl.BlockSpec((tf, D_MODEL), lambda i, f: (f, 0))]
    args = [xs, norm_g.reshape(1, D_MODEL), mod, mod, mod, w_gu, w_gu, w_down]
    if final_g is not None:
        in_specs.append(pl.BlockSpec((1, D_MODEL), lambda i, f: (0, 0)))
        args.append(final_g.reshape(1, D_MODEL))
    return pl.pallas_call(
        functools.partial(_ffn_kernel, final_norm=final_g is not None),
        grid=(n_rows // tm, nf),
        in_specs=in_specs,
        out_specs=pl.BlockSpec((tm, D_MODEL), lambda i, f: (i, 0)),
        out_shape=jax.ShapeDtypeStruct((n_rows, D_MODEL), F32),
        scratch_shapes=[pltpu.VMEM((tm, D_MODEL), BF16), pltpu.VMEM((tm, D_MODEL), F32),
                        pltpu.VMEM((2, ROW_CHUNK, D_MODEL), F32)],
        compiler_params=_cparams(("parallel", "arbitrary"), 48),
        name="ffn_half",
    )(*args)


NMM_TM = 1024
NMM_TN = 1024


def _nmm_kernel(x_ref, g_ref, sh_ref, sc_ref, w_ref, *rest, narrow):
    if narrow:
        wn_ref, o_ref, on_ref, h_ref, gs_ref = rest
    else:
        o_ref, h_ref, gs_ref = rest

    @pl.when(pl.program_id(1) == 0)
    def _():
        _norm_mod_rows(x_ref, g_ref, sh_ref, sc_ref, h_ref, gs_ref)
        if narrow:
            on_ref[...] = _dot(h_ref[...], wn_ref[...])

    o_ref[...] = _dot(h_ref[...], w_ref[...]).astype(o_ref.dtype)


def _nmm(xs, mod, sub, norm_g, w, out_dtype, w_narrow=None):
    tm, tn = NMM_TM, NMM_TN
    n = w.shape[1]
    narrow = w_narrow is not None
    in_specs = [pl.BlockSpec((tm, D_MODEL), lambda i, j: (i, 0)),
                pl.BlockSpec((1, D_MODEL), lambda i, j: (0, 0)),
                _mod_spec(3 * sub, tm), _mod_spec(3 * sub + 1, tm),
                pl.BlockSpec((D_MODEL, tn), lambda i, j: (0, j))]
    args = [xs, norm_g.reshape(1, D_MODEL), mod, mod, w]
    out_specs = pl.BlockSpec((tm, tn), lambda i, j: (i, j))
    out_shape = jax.ShapeDtypeStruct((N_TOK, n), out_dtype)
    if narrow:
        nn = w_narrow.shape[1]
        in_specs.append(pl.BlockSpec((D_MODEL, nn), lambda i, j: (0, 0)))
        args.append(w_narrow)
        out_specs = [out_specs, pl.BlockSpec((tm, nn), lambda i, j: (i, 0))]
        out_shape = [out_shape, jax.ShapeDtypeStruct((N_TOK, nn), F32)]
    return pl.pallas_call(
        functools.partial(_nmm_kernel, narrow=narrow),
        grid=(N_TOK // tm, n // tn),
        in_specs=in_specs,
        out_specs=out_specs,
        out_shape=out_shape,
        scratch_shapes=[pltpu.VMEM((tm, D_MODEL), BF16), pltpu.VMEM((2, ROW_CHUNK, D_MODEL), F32)],
        compiler_params=_cparams(("parallel", "arbitrary"), 48),
        name="norm_mod_matmul",
    )(*args)


GLA_CTX_CHUNKS = CTX_LEN // GLA_CHUNK
GLA_LAT_CHUNKS = SEQ // GLA_CHUNK
GLA_STEPS = GLA_CTX_CHUNKS + GLA_LAT_CHUNKS
GLA_LR_PAD = 128


def _rope(x, cos, sin_signed, first_half):
    partner = jnp.where(first_half, pltpu.roll(x, 96, 1), pltpu.roll(x, 32, 1))
    return x * cos + partner * sin_signed


def _gla_kernel(qf, kf, vf, lf, cf, sf, qb, kb, vb, lb, cb, sb, gwf, gbf, gwb, gbb,
                of_ref, ob_ref, st_ref):
    @pl.when(pl.program_id(1) == 0)
    def _():
        st_ref[...] = jnp.zeros_like(st_ref)

    row = lax.broadcasted_iota(jnp.int32, (GLA_CHUNK, GLA_CHUNK), 0)
    col = lax.broadcasted_iota(jnp.int32, (GLA_CHUNK, GLA_CHUNK), 1)
    lane = lax.broadcasted_iota(jnp.int32, (GLA_CHUNK, GLA_DK), 1)
    first_half = (lane % 64) < 32
    scale = GLA_DK ** -0.5
    mid_row = GLA_CHUNK // 2

    dirs = ((qf, kf, vf, lf, cf, sf, gwf, gbf, of_ref), (qb, kb, vb, lb, cb, sb, gwb, gbb, ob_ref))
    for d, (q_ref, k_ref, v_ref, l_ref, c_ref, s_ref, gw_ref, gb_ref, o_ref) in enumerate(dirs):
        tri = (row >= col) if d == 0 else (row <= col)
        tri_b = jnp.where(tri, 1.0, 0.0).astype(BF16)
        x = _dot3(l_ref[...], gw_ref[...]) + gb_ref[...]
        g = (jnp.minimum(x, 0.0) - jnp.log1p(jnp.exp(-jnp.abs(x)))) * (1.0 / GLA_TAU)
        bc = _dot_exact_lhs(tri_b, g)
        mid = bc[mid_row:mid_row + 1, :]
        last = bc[GLA_CHUNK - 1:GLA_CHUNK, :] if d == 0 else bc[0:1, :]
        e_q = jnp.exp(bc)
        e_qm = jnp.exp(bc - mid)
        e_km = jnp.exp(mid - bc)
        e_kl = jnp.exp(last - bc)
        e_l = jnp.exp(last)
        cos = c_ref[...]
        sin = s_ref[...]
        for h in range(GLA_HEADS):
            sk = slice(h * GLA_DK, (h + 1) * GLA_DK)
            sv = slice(h * GLA_DV, (h + 1) * GLA_DV)
            qh = _rope(q_ref[:, sk] * scale, cos, sin, first_half)
            kh = _rope(k_ref[:, sk], cos, sin, first_half)
            vh = v_ref[:, sv].astype(BF16)
            att = _dot_nt((qh * e_qm[:, sk]).astype(BF16), (kh * e_km[:, sk]).astype(BF16))
            att = jnp.where(tri, att, 0.0)
            st = st_ref[d, h]
            o = _dot_nt((qh * e_q[:, sk]).astype(BF16), st.astype(BF16))
            o = o + _dot(att.astype(BF16), vh)
            o_ref[:, sv] = o
            kd = (kh * e_kl[:, sk]).astype(BF16)
            st_ref[d, h] = st * e_l[:, sk] + _dot_tn(vh, kd)


def _gla(proj, lr, cos_t, sin_t, gwf, gbf, gwb, gbb):
    c = GLA_CHUNK
    ctx0 = N_LAT // c

    def fwd_blk(b, s):
        return jnp.where(s < GLA_CTX_CHUNKS, ctx0 + GLA_CTX_CHUNKS * b + s,
                         GLA_LAT_CHUNKS * b + s - GLA_CTX_CHUNKS)

    def bwd_blk(b, s):
        return jnp.where(s < GLA_CTX_CHUNKS, ctx0 + GLA_CTX_CHUNKS * b + (GLA_CTX_CHUNKS - 1 - s),
                         GLA_LAT_CHUNKS * b + (GLA_STEPS - 1 - s))

    def fwd_rope(b, s):
        return jnp.where(s < GLA_CTX_CHUNKS, GLA_LAT_CHUNKS, s - GLA_CTX_CHUNKS)

    def bwd_rope(b, s):
        return jnp.where(s < GLA_CTX_CHUNKS, GLA_LAT_CHUNKS, GLA_STEPS - 1 - s)

    def dir_specs(blk, rope):
        return [pl.BlockSpec((c, GLA_QK_W), lambda b, s: (blk(b, s), 0)),
                pl.BlockSpec((c, GLA_QK_W), lambda b, s: (blk(b, s), 1)),
                pl.BlockSpec((c, GLA_V_W), lambda b, s: (blk(b, s), 1)),
                pl.BlockSpec((c, GLA_LR_PAD), lambda b, s: (blk(b, s), 0)),
                pl.BlockSpec((c, GLA_DK), lambda b, s: (rope(b, s), 0)),
                pl.BlockSpec((c, GLA_DK), lambda b, s: (rope(b, s), 0))]

    const = lambda shape: pl.BlockSpec(shape, lambda b, s: (0, 0))
    in_specs = (dir_specs(fwd_blk, fwd_rope) + dir_specs(bwd_blk, bwd_rope)
                + [const((GLA_LR_PAD, GLA_QK_W)), const((1, GLA_QK_W)),
                   const((GLA_LR_PAD, GLA_QK_W)), const((1, GLA_QK_W))])
    return pl.pallas_call(
        _gla_kernel,
        grid=(BATCH, GLA_STEPS),
        in_specs=in_specs,
        out_specs=[pl.BlockSpec((c, GLA_V_W), lambda b, s: (fwd_blk(b, s), 0)),
                   pl.BlockSpec((c, GLA_V_W), lambda b, s: (bwd_blk(b, s), 0))],
        out_shape=[jax.ShapeDtypeStruct((N_TOK, GLA_V_W), F32)] * 2,
        scratch_shapes=[pltpu.VMEM((2, GLA_HEADS, GLA_DV, GLA_DK), F32)],
        compiler_params=_cparams(("parallel", "arbitrary"), 32),
        name="gla_scan",
    )(proj, proj, proj, lr, cos_t, sin_t, proj, proj, proj, lr, cos_t, sin_t, gwf, gbf, gwb, gbb)


CDFT_TM = 1024
PROJ_F_COL0 = 3072


def _cdft_kernel(x_ref, w_ref, zc_ref, zs_ref):
    z = _dot(x_ref[...].astype(BF16), w_ref[...])
    zc_ref[...] = z[:, :FNET_GROUP_DIM].astype(BF16)
    zs_ref[...] = z[:, FNET_GROUP_DIM:].astype(BF16)


def _chan_dft(proj, wc):
    tm, gd = CDFT_TM, FNET_GROUP_DIM
    col0 = PROJ_F_COL0 // gd
    return pl.pallas_call(
        _cdft_kernel,
        grid=(N_TOK // tm, FNET_GROUPS),
        in_specs=[pl.BlockSpec((tm, gd), lambda i, g: (i, col0 + g)),
                  pl.BlockSpec((gd, 2 * gd), lambda i, g: (0, 0))],
        out_specs=[pl.BlockSpec((tm, gd), lambda i, g: (i, g))] * 2,
        out_shape=[jax.ShapeDtypeStruct((N_TOK, FNET_W), BF16)] * 2,
        compiler_params=_cparams(("parallel", "parallel"), 32),
        name="fnet_channel_dft",
    )(proj, wc)


def _tdft_kernel(c_ref, s_ref, zc_ref, zs_ref, *rest):
    o_ref = rest[-1]
    o_ref[...] = _dot(c_ref[...], zc_ref[...]) - _dot(s_ref[...], zs_ref[...])


def _time_dft(cmat, smat, zc, zs, t_len, tm, tn, row0, prev=None):
    zb0 = row0 // t_len
    ob0 = row0 // tm
    mt = t_len // tm
    in_specs = [pl.BlockSpec((tm, t_len), lambda b, n, m: (m, 0)),
                pl.BlockSpec((tm, t_len), lambda b, n, m: (m, 0)),
                pl.BlockSpec((t_len, tn), lambda b, n, m: (zb0 + b, n)),
                pl.BlockSpec((t_len, tn), lambda b, n, m: (zb0 + b, n))]
    args = [cmat, smat, zc, zs]
    aliases = {}
    if prev is not None:
        in_specs.append(pl.BlockSpec(memory_space=pl.ANY))
        args.append(prev)
        aliases = {4: 0}
    return pl.pallas_call(
        _tdft_kernel,
        grid=(BATCH, FNET_W // tn, mt),
        in_specs=in_specs,
        out_specs=pl.BlockSpec((tm, tn), lambda b, n, m: (ob0 + b * mt + m, n)),
        out_shape=jax.ShapeDtypeStruct((N_TOK, FNET_W), F32),
        input_output_aliases=aliases,
        compiler_params=_cparams(("parallel", "parallel", "arbitrary"), 44),
        name="fnet_time_dft",
    )(*args)


def _dft_tables(n):
    k = np.arange(n)
    ang = 2.0 * np.pi * ((k[:, None] * k[None, :]) % n) / n
    return np.cos(ang) / np.sqrt(n), np.sin(ang) / np.sqrt(n)


def _dft_table_kernel(ac_ref, as_ref, bc_ref, bs_ref, co_ref, so_ref):
    ac, as_ = ac_ref[0], as_ref[0]
    bc, bs = bc_ref[...], bs_ref[...]
    co_ref[...] = (ac * bc - as_ * bs).astype(BF16)
    so_ref[...] = (ac * bs + as_ * bc).astype(BF16)


def _big_dft_tables():
    r = GRID_W
    k = np.arange(SEQ)
    j = np.arange(r)
    pa = 2.0 * np.pi * ((j[:, None] * r * k[None, :]) % SEQ) / SEQ
    pb = 2.0 * np.pi * ((j[:, None] * k[None, :]) % SEQ) / SEQ
    ac = jnp.asarray(np.cos(pa).reshape(r, 1, SEQ), F32)
    as_ = jnp.asarray(np.sin(pa).reshape(r, 1, SEQ), F32)
    bc = jnp.asarray(np.cos(pb) * SEQ ** -0.5, F32)
    bs = jnp.asarray(np.sin(pb) * SEQ ** -0.5, F32)
    row = pl.BlockSpec((1, 1, SEQ), lambda i: (i, 0, 0))
    full = pl.BlockSpec((r, SEQ), lambda i: (0, 0))
    return pl.pallas_call(
        _dft_table_kernel,
        grid=(r,),
        in_specs=[row, row, full, full],
        out_specs=[pl.BlockSpec((r, SEQ), lambda i: (i, 0))] * 2,
        out_shape=[jax.ShapeDtypeStruct((SEQ, SEQ), BF16)] * 2,
        compiler_params=_cparams(("parallel",), 16),
        name="dft_tables",
    )(ac, as_, bc, bs)


GO_TM = 512
GO_TN = 1024


def _glaout_kernel(of_ref, ob_ref, r_ref, fm_ref, gn_ref, x_ref, gt_ref, w_ref, o_ref, h_ref):
    @pl.when(pl.program_id(1) == 0)
    def _():
        heads = [slice(h * GLA_DV, (h + 1) * GLA_DV) for h in range(GLA_HEADS)]

        def body(chunks):
            inv = [[_inv_rms(of_ref[rows, sv] + ob_ref[rows, sv]) for sv in heads] for rows in chunks]
            for rows, inv_c in zip(chunks, inv):
                for sv, r in zip(heads, inv_c):
                    o = (of_ref[rows, sv] + ob_ref[rows, sv]) * r
                    gate = r_ref[rows, sv]
                    h_ref[rows, sv] = (o * gn_ref[:, sv] * (gate * _sigmoid(gate))).astype(BF16)
                h_ref[rows, GLA_V_W:] = fm_ref[rows, :].astype(BF16)
        _row_groups(of_ref.shape[0], body, group=4)

    o_ref[...] = x_ref[...] + gt_ref[0] * _dot(h_ref[...], w_ref[...])


def _glaout(of, ob, proj, fm, g_norm, xs, mod, w_out):
    tm, tn = GO_TM, GO_TN
    half = lambda col: pl.BlockSpec((tm, GLA_V_W), lambda i, j: (i, col))
    return pl.pallas_call(
        _glaout_kernel,
        grid=(N_TOK // tm, D_MODEL // tn),
        in_specs=[half(0), half(0), half(2), half(0),
                  pl.BlockSpec((1, GLA_V_W), lambda i, j: (0, 0)),
                  pl.BlockSpec((tm, tn), lambda i, j: (i, j)),
                  _mod_spec(5, tm, tn, col=True),
                  pl.BlockSpec((D_MODEL, tn), lambda i, j: (0, j))],
        out_specs=pl.BlockSpec((tm, tn), lambda i, j: (i, j)),
        out_shape=jax.ShapeDtypeStruct((N_TOK, D_MODEL), F32),
        scratch_shapes=[pltpu.VMEM((tm, D_MODEL), BF16)],
        compiler_params=_cparams(("parallel", "arbitrary"), 56),
        name="gla_fnet_out",
    )(of, ob, proj, fm, g_norm.reshape(1, GLA_V_W), xs, mod, w_out)


NA_HROWS = NA_QROWS // 2
NA_WROWS = NA_HROWS + NA_KH
NA_TQ = NA_QROWS * GRID_W
NA_HQ = NA_HROWS * GRID_W
NA_WK = NA_WROWS * GRID_W
NA_KBLK = NA_HROWS * GRID_W
NA_WBLKS = NA_WK // NA_KBLK
NA_MASKED = 2 * NA_KH - 1
NA_ROW_BLOCKS = (SEQ // GRID_W) // NA_QROWS
NA_CASES = (0, 1, NA_ROW_BLOCKS - 1)


def _na_window_row0(j, half):
    q0 = NA_QROWS * j + NA_HROWS * half
    return int(np.clip(q0 - NA_KH // 2, 0, SEQ // GRID_W - NA_WROWS))


def _na_bias_slots():
    rows = SEQ // GRID_W
    slots = np.full((len(NA_CASES), 2, NA_HROWS, NA_WROWS), NA_MASKED, np.int32)
    for case, j in enumerate(NA_CASES):
        for half in range(2):
            w0 = _na_window_row0(j, half)
            for i in range(NA_HROWS):
                qr = NA_QROWS * j + NA_HROWS * half + i
                r0 = int(np.clip(qr - NA_KH // 2, 0, rows - NA_KH))
                for l in range(NA_WROWS):
                    kr = w0 + l
                    if r0 <= kr < r0 + NA_KH:
                        slots[case, half, i, l] = kr - qr + NA_KH - 1
    return slots


def _na_kernel(q_ref, *refs):
    nb = NA_WBLKS
    ka, kb, va, vb = refs[0:nb], refs[nb:2 * nb], refs[2 * nb:3 * nb], refs[3 * nb:4 * nb]
    kc_ref, vc_ref, td_ref, o_ref, bias_ref = refs[4 * nb:]
    j = pl.program_id(1)

    @pl.when((j == 0) & (pl.program_id(2) == 0))
    def _():
        slots = _na_bias_slots()
        left = lax.broadcasted_iota(jnp.int32, (GRID_W, 2 * GRID_W), 1) < GRID_W
        for case in range(len(NA_CASES)):
            for half in range(2):
                for hh in range(2):
                    dst = (case * 2 + half) * 2 + hh
                    for i in range(NA_HROWS):
                        for p in range(NA_WROWS // 2):
                            sl = int(slots[case, half, i, 2 * p])
                            sr = int(slots[case, half, i, 2 * p + 1])
                            tile = td_ref[hh, sl] if sl == sr else jnp.where(left, td_ref[hh, sl], td_ref[hh, sr])
                            bias_ref[dst, i * GRID_W:(i + 1) * GRID_W, p * 2 * GRID_W:(p + 1) * 2 * GRID_W] = tile

    case = jnp.where(j == 0, 0, jnp.where(j == NA_ROW_BLOCKS - 1, 2, 1))
    kc = kc_ref[...]
    vc = vc_ref[...]
    lane = lax.broadcasted_iota(jnp.int32, (NA_HQ, 2 * NA_HEAD_DIM), 1)
    for half, (k_refs, v_refs) in enumerate(((ka, va), (kb, vb))):
        q = q_ref[half * NA_HQ:(half + 1) * NA_HQ, :] * (NA_HEAD_DIM ** -0.5)
        k = jnp.concatenate([r[...] for r in k_refs], axis=0)
        v = jnp.concatenate([r[...] for r in v_refs], axis=0)
        out = None
        for hh in range(2):
            mine = (lane // NA_HEAD_DIM) == hh
            qm = jnp.where(mine, q, jnp.zeros_like(q))
            s_loc = _dot_nt(qm, k) + bias_ref[(case * 2 + half) * 2 + hh]
            s_ctx = _dot_nt(qm, kc)
            m = jnp.maximum(jnp.max(s_loc, axis=-1, keepdims=True), jnp.max(s_ctx, axis=-1, keepdims=True))
            p_loc = jnp.exp(s_loc - m)
            p_ctx = jnp.exp(s_ctx - m)
            denom = jnp.sum(p_loc, axis=-1, keepdims=True) + jnp.sum(p_ctx, axis=-1, keepdims=True)
            o = (_dot(p_loc.astype(BF16), v) + _dot(p_ctx.astype(BF16), vc)) / denom
            out = o if hh == 0 else jnp.where(mine, o, out)
        o_ref[half * NA_HQ:(half + 1) * NA_HQ, :] = out.astype(o_ref.dtype)


def _na(qkv, tiles):
    hp_n = NA_HEADS // 2
    jn = NA_ROW_BLOCKS
    kb_per_img = SEQ // NA_KBLK
    last_blk = kb_per_img - NA_WBLKS

    def kv_spec(col0, half, t):
        return pl.BlockSpec((NA_KBLK, 128), lambda hp, j, b: (
            b * kb_per_img + jnp.clip(2 * j + half - 1, 0, last_blk) + t, col0 + hp))

    def kv_specs(col0):
        return [kv_spec(col0, half, t) for half in range(2) for t in range(NA_WBLKS)]

    ctx0 = N_LAT // CTX_LEN
    in_specs = ([pl.BlockSpec((NA_TQ, 128), lambda hp, j, b: (b * jn + j, hp))]
                + kv_specs(hp_n) + kv_specs(2 * hp_n)
                + [pl.BlockSpec((CTX_LEN, 128), lambda hp, j, b: (ctx0 + b, hp_n + hp)),
                   pl.BlockSpec((CTX_LEN, 128), lambda hp, j, b: (ctx0 + b, 2 * hp_n + hp)),
                   pl.BlockSpec((2, 2 * NA_KH, GRID_W, 2 * GRID_W), lambda hp, j, b: (hp, 0, 0, 0))])
    return pl.pallas_call(
        _na_kernel,
        grid=(hp_n, jn, BATCH),
        in_specs=in_specs,
        out_specs=pl.BlockSpec((NA_TQ, 128), lambda hp, j, b: (b * jn + j, hp)),
        out_shape=jax.ShapeDtypeStruct((N_LAT, D_MODEL), BF16),
        scratch_shapes=[pltpu.VMEM((len(NA_CASES) * 4, NA_HQ, NA_WK), F32)],
        compiler_params=_cparams(("arbitrary", "arbitrary", "arbitrary"), 40),
        name="neighbourhood_attention",
    )(qkv, *([qkv] * (4 * NA_WBLKS)), qkv, qkv, tiles)


def _na_bias_tiles(rpb):
    w = GRID_W
    col = np.arange(w)
    cs = np.clip(col - NA_KW // 2, 0, w - NA_KW)
    col_mask = (col[None, :] >= cs[:, None]) & (col[None, :] < cs[:, None] + NA_KW)
    dc_idx = np.clip(col[None, :] - col[:, None] + NA_KW - 1, 0, 2 * NA_KW - 2)
    t1 = rpb.astype(F32)[:, :, dc_idx]
    t1 = jnp.where(col_mask[None, None], t1, NEG_BIG)
    t1 = jnp.concatenate([t1, jnp.full((NA_HEADS, 1, w, w), NEG_BIG, F32)], axis=1)
    return jnp.concatenate([t1, t1], axis=-1)


MR_TM = 1024
MR_TN = 1024


def _mmres_kernel(a_ref, w_ref, x_ref, gt_ref, o_ref):
    o_ref[...] = x_ref[...] + gt_ref[0] * _dot(a_ref[...], w_ref[...])


def _mm_res(a, w, xs, mod, n_rows):
    tm, tn = MR_TM, MR_TN
    return pl.pallas_call(
        _mmres_kernel,
        grid=(n_rows // tm, D_MODEL // tn),
        in_specs=[pl.BlockSpec((tm, D_MODEL), lambda i, j: (i, 0)),
                  pl.BlockSpec((D_MODEL, tn), lambda i, j: (0, j)),
                  pl.BlockSpec((tm, tn), lambda i, j: (i, j)),
                  _mod_spec(5, tm, tn, col=True)],
        out_specs=pl.BlockSpec((tm, tn), lambda i, j: (i, j)),
        out_shape=jax.ShapeDtypeStruct((n_rows, D_MODEL), F32),
        compiler_params=_cparams(("parallel", "arbitrary"), 40),
        name="matmul_gated_residual",
    )(a, w, xs, mod)


def _rope_tables():
    half = GLA_DK // 4
    inv_freq = ROPE_BASE ** (-jnp.arange(half, dtype=F32) / half)
    t = jnp.arange(SEQ)
    ang_r = (t // GRID_W).astype(F32)[:, None] * inv_freq[None, :]
    ang_c = (t % GRID_W).astype(F32)[:, None] * inv_freq[None, :]
    cr, sr, cc, sc = jnp.cos(ang_r), jnp.sin(ang_r), jnp.cos(ang_c), jnp.sin(ang_c)
    cos_t = jnp.concatenate([cr, cr, cc, cc], axis=-1)
    sin_t = jnp.concatenate([-sr, sr, -sc, sc], axis=-1)
    cos_t = jnp.concatenate([cos_t, jnp.ones((GLA_CHUNK, GLA_DK), F32)], axis=0)
    sin_t = jnp.concatenate([sin_t, jnp.zeros((GLA_CHUNK, GLA_DK), F32)], axis=0)
    return cos_t, sin_t


def kernel(x, c, ctx, c_ctx, l0_ada_w, l0_ada_b, l0_norm_ffn1, l0_norm_mix, l0_norm_ffn2, l0_ffn1_w_gu, l0_ffn1_w_down, l0_ffn2_w_gu, l0_ffn2_w_down, l0_w_in, l0_gla_gate_w_fwd, l0_gla_gate_b_fwd, l0_gla_gate_w_bwd, l0_gla_gate_b_bwd, l0_gla_norm, l0_w_out, l1_ada_w, l1_ada_b, l1_norm_ffn1, l1_norm_mix, l1_norm_ffn2, l1_ffn1_w_gu, l1_ffn1_w_down, l1_ffn2_w_gu, l1_ffn2_w_down, l1_w_qkv, l1_rpb, l1_w_out, norm_out):
    bf = lambda w: w.astype(BF16)
    xs = jnp.concatenate([x.reshape(N_LAT, D_MODEL), ctx.reshape(N_CTX, D_MODEL)], axis=0)
    cc = jnp.concatenate([c, c_ctx[None, :], jnp.zeros((8 - BATCH - 1, D_MODEL), F32)], axis=0)

    mod0 = _ada(cc, l0_ada_w, l0_ada_b)
    xs = _ffn(xs, N_TOK, mod0, 0, l0_norm_ffn1, bf(l0_ffn1_w_gu), bf(l0_ffn1_w_down))

    a0 = 2 * GLA_QK_W + GLA_V_W
    a1 = a0 + 2 * GLA_LOWRANK
    w_main = bf(jnp.concatenate([l0_w_in[:, :a0], l0_w_in[:, a1:]], axis=1))
    w_lr = bf(jnp.pad(l0_w_in[:, a0:a1], ((0, 0), (0, GLA_LR_PAD - 2 * GLA_LOWRANK))))
    proj, lr = _nmm(xs, mod0, 1, l0_norm_mix, w_main, F32, w_narrow=w_lr)

    cos_t, sin_t = _rope_tables()
    gwf = jnp.pad(l0_gla_gate_w_fwd, ((0, GLA_LR_PAD - GLA_LOWRANK), (0, 0)))
    gwb = jnp.pad(l0_gla_gate_w_bwd, ((GLA_LOWRANK, GLA_LR_PAD - 2 * GLA_LOWRANK), (0, 0)))
    of, ob = _gla(proj, lr, cos_t, sin_t, gwf, l0_gla_gate_b_fwd.reshape(1, -1),
                  gwb, l0_gla_gate_b_bwd.reshape(1, -1))

    cc_np, sc_np = _dft_tables(FNET_GROUP_DIM)
    wc = jnp.asarray(np.concatenate([cc_np, sc_np], axis=1), F32).astype(BF16)
    zc, zs = _chan_dft(proj, wc)
    cos_big, sin_big = _big_dft_tables()
    fm = _time_dft(cos_big, sin_big, zc, zs, SEQ, 512, 512, 0)
    ct_np, st_np = _dft_tables(CTX_LEN)
    fm = _time_dft(jnp.asarray(ct_np, F32).astype(BF16), jnp.asarray(st_np, F32).astype(BF16),
                   zc, zs, CTX_LEN, CTX_LEN, 512, N_LAT, prev=fm)

    xs = _glaout(of, ob, proj, fm, l0_gla_norm, xs, mod0, bf(l0_w_out))
    xs = _ffn(xs, N_TOK, mod0, 2, l0_norm_ffn2, bf(l0_ffn2_w_gu), bf(l0_ffn2_w_down))

    mod1 = _ada(cc, l1_ada_w, l1_ada_b)
    xs = _ffn(xs, N_TOK, mod1, 0, l1_norm_ffn1, bf(l1_ffn1_w_gu), bf(l1_ffn1_w_down))
    qkv = _nmm(xs, mod1, 1, l1_norm_mix, bf(l1_w_qkv), BF16)
    ao = _na(qkv, _na_bias_tiles(l1_rpb))
    xl = _mm_res(ao, bf(l1_w_out), xs, mod1, N_LAT)
    out = _ffn(xl, N_LAT, mod1, 2, l1_norm_ffn2, bf(l1_ffn2_w_gu), bf(l1_ffn2_w_down), final_g=norm_out)
    return out.reshape(BATCH, SEQ, D_MODEL)
```

```python
import functools

import numpy as np
import jax
import jax.numpy as jnp
from jax import lax
from jax.experimental import pallas as pl
from jax.experimental.pallas import tpu as pltpu

D_MODEL = 2048
BATCH = 4
SEQ = 4096
GRID_W = 64
CTX_LEN = 256
N_MOD = 9
D_FF = 5632
RMS_EPS = 1e-6
ROPE_BASE = 10000.0

GLA_HEADS = 4
GLA_DK = 128
GLA_DV = 256
GLA_LOWRANK = 16
GLA_TAU = 16.0
GLA_CHUNK = 64
FNET_GROUPS = 4
FNET_GROUP_DIM = 256
GLA_QK_W = GLA_HEADS * GLA_DK
GLA_V_W = GLA_HEADS * GLA_DV
FNET_W = FNET_GROUPS * FNET_GROUP_DIM

NA_HEADS = 32
NA_HEAD_DIM = 64
NA_KH = 8
NA_KW = 16
NA_QROWS = 8
NEG_BIG = -1e30
LOG2_E = 1.4426950408889634
NA_Q_SCALE = NA_HEAD_DIM ** -0.5 * LOG2_E

N_LAT = BATCH * SEQ
N_CTX = BATCH * CTX_LEN
N_TOK = N_LAT + N_CTX

F32 = jnp.float32
BF16 = jnp.bfloat16
MIB = 1024 * 1024


def _cparams(sem, vmem_mib):
    return pltpu.CompilerParams(dimension_semantics=sem, vmem_limit_bytes=vmem_mib * MIB)


def _dot(a, b):
    return jnp.dot(a, b, preferred_element_type=F32)


def _dot_nt(a, b):
    return lax.dot_general(a, b, (((1,), (1,)), ((), ())), preferred_element_type=F32)


def _dot_tn(a, b):
    return lax.dot_general(a, b, (((0,), (0,)), ((), ())), preferred_element_type=F32)


def _split2(x):
    hi = x.astype(BF16)
    lo = (x - hi.astype(F32)).astype(BF16)
    return hi, lo


def _dot3(a, b):
    ah, al = _split2(a)
    bh, bl = _split2(b)
    return _dot(ah, bh) + (_dot(al, bh) + _dot(ah, bl))


def _dot_exact_lhs(l_bf16, x):
    x1 = x.astype(BF16)
    r1 = x - x1.astype(F32)
    x2 = r1.astype(BF16)
    x3 = (r1 - x2.astype(F32)).astype(BF16)
    return _dot(l_bf16, x1) + (_dot(l_bf16, x2) + _dot(l_bf16, x3))


def _sigmoid(x):
    return 1.0 / (1.0 + jnp.exp(-x))


def _rms(x, g):
    ms = jnp.mean(x * x, axis=-1, keepdims=True)
    return x * lax.rsqrt(ms + RMS_EPS) * g


ROW_CHUNK = 16
ROW_GROUP = 8


def _row_groups(n_rows, body, group=ROW_GROUP):
    span = ROW_CHUNK * group

    def step(t, carry):
        base = t * span
        body([pl.ds(pl.multiple_of(base + c * ROW_CHUNK, ROW_CHUNK), ROW_CHUNK) for c in range(group)])
        return carry
    lax.fori_loop(0, n_rows // span, step, 0)


def _inv_rms(x):
    return lax.rsqrt(jnp.mean(x * x, axis=-1, keepdims=True) + RMS_EPS)


def _norm_mod_rows(x_ref, g_ref, sh_ref, sc_ref, h_ref, gs_ref, zero_ref=None):
    d = x_ref.shape[1]
    gs_ref[0] = jnp.broadcast_to(g_ref[...] * (1.0 + sc_ref[0]), (ROW_CHUNK, d))
    gs_ref[1] = jnp.broadcast_to(sh_ref[0], (ROW_CHUNK, d))

    def body(chunks):
        inv = [_inv_rms(x_ref[rows, :]) for rows in chunks]
        for rows, r in zip(chunks, inv):
            h_ref[rows, :] = (x_ref[rows, :] * r * gs_ref[0] + gs_ref[1]).astype(BF16)
            if zero_ref is not None:
                zero_ref[rows, :] = jnp.zeros((ROW_CHUNK, zero_ref.shape[1]), zero_ref.dtype)
    _row_groups(x_ref.shape[0], body)


def _mod_row(i, tm):
    return jnp.minimum(i // (SEQ // tm), BATCH)


def _mod_spec(j, tm, width=D_MODEL, col=None, tile0=0):
    if col is None:
        return pl.BlockSpec((1, 1, width), lambda i, *_: (_mod_row(i + tile0, tm) * N_MOD + j, 0, 0))
    return pl.BlockSpec((1, 1, width), lambda i, n, *_: (_mod_row(i + tile0, tm) * N_MOD + j, 0, n))


ADA_TN = 1024


def _ada_kernel(c_ref, w_ref, b_ref, o_ref):
    c = c_ref[...]
    s = (c * _sigmoid(c)).astype(BF16)
    o_ref[...] = _dot(s, w_ref[...].astype(BF16)) + b_ref[...]


def _ada(cc, w, b):
    n = w.shape[1]
    out = pl.pallas_call(
        _ada_kernel,
        grid=(n // ADA_TN,),
        in_specs=[pl.BlockSpec((8, D_MODEL), lambda j: (0, 0)),
                  pl.BlockSpec((D_MODEL, ADA_TN), lambda j: (0, j)),
                  pl.BlockSpec((1, ADA_TN), lambda j: (0, j))],
        out_specs=pl.BlockSpec((8, ADA_TN), lambda j: (0, j)),
        out_shape=jax.ShapeDtypeStruct((8, n), F32),
        compiler_params=_cparams(("arbitrary",), 40),
        name="ada_mod",
    )(cc, w, b.reshape(1, n))
    return out.reshape(8 * N_MOD, 1, D_MODEL)


FFN_TM = 1024
FFN_TF = 512


def _ffn_kernel(x_ref, g_ref, sh_ref, sc_ref, gt_ref, wg_ref, wu_ref, wd_ref, *rest, final_norm):
    go_ref = rest[0] if final_norm else None
    o_ref, h_ref, acc_ref, gs_ref = rest[-4:]
    f = pl.program_id(1)

    @pl.when(f == 0)
    def _():
        _norm_mod_rows(x_ref, g_ref, sh_ref, sc_ref, h_ref, gs_ref, zero_ref=acc_ref)

    h = h_ref[...]
    a = _dot(h, wg_ref[...])
    u = _dot(h, wu_ref[...])
    act = (a * _sigmoid(a) * u).astype(BF16)
    acc_ref[...] += _dot(act, wd_ref[...])

    @pl.when(f == pl.num_programs(1) - 1)
    def _():
        if not final_norm:
            o_ref[...] = x_ref[...] + (0.5 * gt_ref[0]) * acc_ref[...]
        else:
            d = x_ref.shape[1]
            gs_ref[0] = jnp.broadcast_to(0.5 * gt_ref[0], (ROW_CHUNK, d))
            gs_ref[1] = jnp.broadcast_to(go_ref[...], (ROW_CHUNK, d))

            def body(chunks):
                resid = lambda rows: x_ref[rows, :] + gs_ref[0] * acc_ref[rows, :]
                inv = [_inv_rms(resid(rows)) for rows in chunks]
                for rows, r in zip(chunks, inv):
                    o_ref[rows, :] = resid(rows) * r * gs_ref[1]
            _row_groups(x_ref.shape[0], body)


def _ffn(x_in, mod, sub, norm_g, w_gu, w_down, *, out_rows, tile0=0, prev=None, final_g=None):
    tm, tf = FFN_TM, FFN_TF
    nf = D_FF // tf
    in_specs = [pl.BlockSpec((tm, D_MODEL), lambda i, f: (i, 0)),
                pl.BlockSpec((1, D_MODEL), lambda i, f: (0, 0)),
                _mod_spec(3 * sub, tm, tile0=tile0), _mod_spec(3 * sub + 1, tm, tile0=tile0),
                _mod_spec(3 * sub + 2, tm, tile0=tile0),
                pl.BlockSpec((D_MODEL, tf), lambda i, f: (0, f)),
                pl.BlockSpec((D_MODEL, tf), lambda i, f: (0, f + nf)),
                pl.BlockSpec((tf, D_MODEL), lambda i, f: (f, 0))]
    args = [x_in, norm_g.reshape(1, D_MODEL), mod, mod, mod, w_gu, w_gu, w_down]
    if final_g is not None:
        in_specs.append(pl.BlockSpec((1, D_MODEL), lambda i, f: (0, 0)))
        args.append(final_g.reshape(1, D_MODEL))
    aliases = {}
    if prev is not None:
        aliases = {len(args): 0}
        in_specs.append(pl.BlockSpec(memory_space=pl.ANY))
        args.append(prev)
    return pl.pallas_call(
        functools.partial(_ffn_kernel, final_norm=final_g is not None),
        grid=(x_in.shape[0] // tm, nf),
        in_specs=in_specs,
        out_specs=pl.BlockSpec((tm, D_MODEL), lambda i, f: (i + tile0, 0), pipeline_mode=pl.Buffered(1)),
        out_shape=jax.ShapeDtypeStruct((out_rows, D_MODEL), F32),
        scratch_shapes=[pltpu.VMEM((tm, D_MODEL), BF16), pltpu.VMEM((tm, D_MODEL), F32),
                        pltpu.VMEM((2, ROW_CHUNK, D_MODEL), F32)],
        input_output_aliases=aliases,
        compiler_params=_cparams(("parallel", "arbitrary"), 60),
        name="ffn_half",
    )(*args)


NMM_TM = 1024
NMM_TN = 1024


def _nmm_kernel(x_ref, g_ref, sh_ref, sc_ref, w_ref, *rest, narrow):
    if narrow:
        wn_ref, o_ref, on_ref, h_ref, gs_ref = rest
    else:
        o_ref, h_ref, gs_ref = rest

    @pl.when(pl.program_id(1) == 0)
    def _():
        _norm_mod_rows(x_ref, g_ref, sh_ref, sc_ref, h_ref, gs_ref)
        if narrow:
            on_ref[...] = _dot(h_ref[...], wn_ref[...])

    o_ref[...] = _dot(h_ref[...], w_ref[...]).astype(o_ref.dtype)


def _nmm(xs, mod, sub, norm_g, w, out_dtype, w_narrow=None):
    tm, tn = NMM_TM, NMM_TN
    n = w.shape[1]
    narrow = w_narrow is not None
    in_specs = [pl.BlockSpec((tm, D_MODEL), lambda i, j: (i, 0)),
                pl.BlockSpec((1, D_MODEL), lambda i, j: (0, 0)),
                _mod_spec(3 * sub, tm), _mod_spec(3 * sub + 1, tm),
                pl.BlockSpec((D_MODEL, tn), lambda i, j: (0, j))]
    args = [xs, norm_g.reshape(1, D_MODEL), mod, mod, w]
    out_specs = pl.BlockSpec((tm, tn), lambda i, j: (i, j))
    out_shape = jax.ShapeDtypeStruct((N_TOK, n), out_dtype)
    if narrow:
        nn = w_narrow.shape[1]
        in_specs.append(pl.BlockSpec((D_MODEL, nn), lambda i, j: (0, 0)))
        args.append(w_narrow)
        out_specs = [out_specs, pl.BlockSpec((tm, nn), lambda i, j: (i, 0))]
        out_shape = [out_shape, jax.ShapeDtypeStruct((N_TOK, nn), F32)]
    return pl.pallas_call(
        functools.partial(_nmm_kernel, narrow=narrow),
        grid=(N_TOK // tm, n // tn),
        in_specs=in_specs,
        out_specs=out_specs,
        out_shape=out_shape,
        scratch_shapes=[pltpu.VMEM((tm, D_MODEL), BF16), pltpu.VMEM((2, ROW_CHUNK, D_MODEL), F32)],
        compiler_params=_cparams(("parallel", "arbitrary"), 48),
        name="norm_mod_matmul",
    )(*args)


GLA_BLOCK = CTX_LEN
GLA_LAT_BLOCKS = SEQ // GLA_BLOCK
GLA_STEPS = 1 + GLA_LAT_BLOCKS
GLA_LR_PAD = 128


def _rope(x, cos, sin_signed):
    return x * cos + pltpu.roll(x, GLA_DK // 2, 1) * sin_signed


def _gla_kernel(qf, kf, vf, lf, cf, sf, qb, kb, vb, lb, cb, sb, gwf, gbf, gwb, gbb,
                of_ref, ob_ref, st_ref):
    @pl.when(pl.program_id(1) == 0)
    def _():
        st_ref[...] = jnp.zeros_like(st_ref)

    c = GLA_CHUNK
    n = GLA_BLOCK
    row = lax.broadcasted_iota(jnp.int32, (n, n), 0)
    col = lax.broadcasted_iota(jnp.int32, (n, n), 1)
    same_chunk = (row // c) == (col // c)
    row_c = lax.broadcasted_iota(jnp.int32, (c, c), 0)
    col_c = lax.broadcasted_iota(jnp.int32, (c, c), 1)
    scale = GLA_DK ** -0.5

    dirs = ((qf, kf, vf, lf, cf, sf, gwf, gbf, of_ref), (qb, kb, vb, lb, cb, sb, gwb, gbb, ob_ref))
    for d, (q_ref, k_ref, v_ref, l_ref, c_ref, s_ref, gw_ref, gb_ref, o_ref) in enumerate(dirs):
        causal = (row >= col) if d == 0 else (row <= col)
        tri_b = jnp.where(same_chunk & causal, 1.0, 0.0).astype(BF16)
        tri = (row_c >= col_c) if d == 0 else (row_c <= col_c)
        x = _dot3(l_ref[...], gw_ref[...]) + gb_ref[...]
        g = (jnp.minimum(x, 0.0) - jnp.log1p(jnp.exp(-jnp.abs(x)))) * (1.0 / GLA_TAU)
        bc_all = _dot_exact_lhs(tri_b, g)
        cos = c_ref[...]
        sin = s_ref[...]
        chunk_order = range(n // c) if d == 0 else range(n // c - 1, -1, -1)
        for ci in chunk_order:
            rows = slice(ci * c, (ci + 1) * c)
            bc = bc_all[rows, :]
            mid = bc[c // 2:c // 2 + 1, :]
            last = bc[c - 1:c, :] if d == 0 else bc[0:1, :]
            e_q = jnp.exp(bc)
            e_qm = jnp.exp(bc - mid)
            e_km = jnp.exp(mid - bc)
            e_kl = jnp.exp(last - bc)
            e_l = jnp.exp(last)
            for h in range(GLA_HEADS):
                sk = slice(h * GLA_DK, (h + 1) * GLA_DK)
                sv = slice(h * GLA_DV, (h + 1) * GLA_DV)
                qh = _rope(q_ref[rows, sk] * scale, cos[rows, :], sin[rows, :])
                kh = _rope(k_ref[rows, sk], cos[rows, :], sin[rows, :])
                vh = v_ref[rows, sv].astype(BF16)
                att = _dot_nt((qh * e_qm[:, sk]).astype(BF16), (kh * e_km[:, sk]).astype(BF16))
                att = jnp.where(tri, att, 0.0)
                st = st_ref[d, h]
                o = _dot_nt((qh * e_q[:, sk]).astype(BF16), st.astype(BF16))
                o = o + _dot(att.astype(BF16), vh)
                o_ref[rows, sv] = o
                kd = (kh * e_kl[:, sk]).astype(BF16)
                st_ref[d, h] = st * e_l[:, sk] + _dot_tn(vh, kd)


def _gla(proj, lr, cos_t, sin_t, gwf, gbf, gwb, gbb):
    c = GLA_BLOCK
    ctx0 = N_LAT // c

    def fwd_blk(b, s):
        return jnp.where(s == 0, ctx0 + b, GLA_LAT_BLOCKS * b + s - 1)

    def bwd_blk(b, s):
        return jnp.where(s == 0, ctx0 + b, GLA_LAT_BLOCKS * b + (GLA_STEPS - 1 - s))

    def fwd_rope(b, s):
        return jnp.where(s == 0, GLA_LAT_BLOCKS, s - 1)

    def bwd_rope(b, s):
        return jnp.where(s == 0, GLA_LAT_BLOCKS, GLA_STEPS - 1 - s)

    def dir_specs(blk, rope):
        return [pl.BlockSpec((c, GLA_QK_W), lambda b, s: (blk(b, s), 0)),
                pl.BlockSpec((c, GLA_QK_W), lambda b, s: (blk(b, s), 1)),
                pl.BlockSpec((c, GLA_V_W), lambda b, s: (blk(b, s), 1)),
                pl.BlockSpec((c, GLA_LR_PAD), lambda b, s: (blk(b, s), 0)),
                pl.BlockSpec((c, GLA_DK), lambda b, s: (rope(b, s), 0)),
                pl.BlockSpec((c, GLA_DK), lambda b, s: (rope(b, s), 0))]

    const = lambda shape: pl.BlockSpec(shape, lambda b, s: (0, 0))
    in_specs = (dir_specs(fwd_blk, fwd_rope) + dir_specs(bwd_blk, bwd_rope)
                + [const((GLA_LR_PAD, GLA_QK_W)), const((1, GLA_QK_W)),
                   const((GLA_LR_PAD, GLA_QK_W)), const((1, GLA_QK_W))])
    return pl.pallas_call(
        _gla_kernel,
        grid=(BATCH, GLA_STEPS),
        in_specs=in_specs,
        out_specs=[pl.BlockSpec((c, GLA_V_W), lambda b, s: (fwd_blk(b, s), 0)),
                   pl.BlockSpec((c, GLA_V_W), lambda b, s: (bwd_blk(b, s), 0))],
        out_shape=[jax.ShapeDtypeStruct((N_TOK, GLA_V_W), F32)] * 2,
        scratch_shapes=[pltpu.VMEM((2, GLA_HEADS, GLA_DV, GLA_DK), F32)],
        compiler_params=_cparams(("parallel", "arbitrary"), 32),
        name="gla_scan",
    )(proj, proj, proj, lr, cos_t, sin_t, proj, proj, proj, lr, cos_t, sin_t, gwf, gbf, gwb, gbb)


CDFT_TM = 1024
PROJ_F_COL0 = 3072


def _cdft_kernel(x_ref, w_ref, zc_ref, zs_ref):
    z = _dot(x_ref[...].astype(BF16), w_ref[...])
    zc_ref[...] = z[:, :FNET_GROUP_DIM].astype(BF16)
    zs_ref[...] = z[:, FNET_GROUP_DIM:].astype(BF16)


def _chan_dft(proj, wc):
    tm, gd = CDFT_TM, FNET_GROUP_DIM
    col0 = PROJ_F_COL0 // gd
    return pl.pallas_call(
        _cdft_kernel,
        grid=(N_TOK // tm, FNET_GROUPS),
        in_specs=[pl.BlockSpec((tm, gd), lambda i, g: (i, col0 + g)),
                  pl.BlockSpec((gd, 2 * gd), lambda i, g: (0, 0))],
        out_specs=[pl.BlockSpec((tm, gd), lambda i, g: (i, g))] * 2,
        out_shape=[jax.ShapeDtypeStruct((N_TOK, FNET_W), BF16)] * 2,
        compiler_params=_cparams(("parallel", "parallel"), 32),
        name="fnet_channel_dft",
    )(proj, wc)


def _tdft_kernel(c_ref, s_ref, zc_ref, zs_ref, *rest):
    o_ref = rest[-1]
    o_ref[...] = _dot(c_ref[...], zc_ref[...]) - _dot(s_ref[...], zs_ref[...])


def _time_dft(cmat, smat, zc, zs, t_len, tm, tn, row0, prev=None):
    zb0 = row0 // t_len
    ob0 = row0 // tm
    mt = t_len // tm
    in_specs = [pl.BlockSpec((tm, t_len), lambda b, n, m: (m, 0)),
                pl.BlockSpec((tm, t_len), lambda b, n, m: (m, 0)),
                pl.BlockSpec((t_len, tn), lambda b, n, m: (zb0 + b, n)),
                pl.BlockSpec((t_len, tn), lambda b, n, m: (zb0 + b, n))]
    args = [cmat, smat, zc, zs]
    aliases = {}
    if prev is not None:
        in_specs.append(pl.BlockSpec(memory_space=pl.ANY))
        args.append(prev)
        aliases = {4: 0}
    return pl.pallas_call(
        _tdft_kernel,
        grid=(BATCH, FNET_W // tn, mt),
        in_specs=in_specs,
        out_specs=pl.BlockSpec((tm, tn), lambda b, n, m: (ob0 + b * mt + m, n)),
        out_shape=jax.ShapeDtypeStruct((N_TOK, FNET_W), F32),
        input_output_aliases=aliases,
        compiler_params=_cparams(("parallel", "parallel", "arbitrary"), 44),
        name="fnet_time_dft",
    )(*args)


def _dft_tables(n):
    k = np.arange(n)
    ang = 2.0 * np.pi * ((k[:, None] * k[None, :]) % n) / n
    return np.cos(ang) / np.sqrt(n), np.sin(ang) / np.sqrt(n)


def _dft_table_kernel(ac_ref, as_ref, bc_ref, bs_ref, co_ref, so_ref):
    ac, as_ = ac_ref[0], as_ref[0]
    bc, bs = bc_ref[...], bs_ref[...]
    co_ref[...] = (ac * bc - as_ * bs).astype(BF16)
    so_ref[...] = (ac * bs + as_ * bc).astype(BF16)


def _big_dft_tables():
    r = GRID_W
    k = np.arange(SEQ)
    j = np.arange(r)
    pa = 2.0 * np.pi * ((j[:, None] * r * k[None, :]) % SEQ) / SEQ
    pb = 2.0 * np.pi * ((j[:, None] * k[None, :]) % SEQ) / SEQ
    ac = jnp.asarray(np.cos(pa).reshape(r, 1, SEQ), F32)
    as_ = jnp.asarray(np.sin(pa).reshape(r, 1, SEQ), F32)
    bc = jnp.asarray(np.cos(pb) * SEQ ** -0.5, F32)
    bs = jnp.asarray(np.sin(pb) * SEQ ** -0.5, F32)
    row = pl.BlockSpec((1, 1, SEQ), lambda i: (i, 0, 0))
    full = pl.BlockSpec((r, SEQ), lambda i: (0, 0))
    return pl.pallas_call(
        _dft_table_kernel,
        grid=(r,),
        in_specs=[row, row, full, full],
        out_specs=[pl.BlockSpec((r, SEQ), lambda i: (i, 0))] * 2,
        out_shape=[jax.ShapeDtypeStruct((SEQ, SEQ), BF16)] * 2,
        compiler_params=_cparams(("parallel",), 16),
        name="dft_tables",
    )(ac, as_, bc, bs)


GO_TM = 512
GO_TN = 1024


def _glaout_kernel(of_ref, ob_ref, r_ref, fm_ref, gn_ref, x_ref, gt_ref, w_ref, o_ref, h_ref):
    @pl.when(pl.program_id(1) == 0)
    def _():
        heads = [slice(h * GLA_DV, (h + 1) * GLA_DV) for h in range(GLA_HEADS)]

        def body(chunks):
            inv = [[_inv_rms(of_ref[rows, sv] + ob_ref[rows, sv]) for sv in heads] for rows in chunks]
            for rows, inv_c in zip(chunks, inv):
                for sv, r in zip(heads, inv_c):
                    o = (of_ref[rows, sv] + ob_ref[rows, sv]) * r
                    gate = r_ref[rows, sv]
                    h_ref[rows, sv] = (o * gn_ref[:, sv] * (gate * _sigmoid(gate))).astype(BF16)
                h_ref[rows, GLA_V_W:] = fm_ref[rows, :].astype(BF16)
        _row_groups(of_ref.shape[0], body, group=4)

    o_ref[...] = x_ref[...] + gt_ref[0] * _dot(h_ref[...], w_ref[...])


def _glaout(of, ob, proj, fm, g_norm, xs, mod, w_out):
    tm, tn = GO_TM, GO_TN
    half = lambda col: pl.BlockSpec((tm, GLA_V_W), lambda i, j: (i, col))
    return pl.pallas_call(
        _glaout_kernel,
        grid=(N_TOK // tm, D_MODEL // tn),
        in_specs=[half(0), half(0), half(2), half(0),
                  pl.BlockSpec((1, GLA_V_W), lambda i, j: (0, 0)),
                  pl.BlockSpec((tm, tn), lambda i, j: (i, j)),
                  _mod_spec(5, tm, tn, col=True),
                  pl.BlockSpec((D_MODEL, tn), lambda i, j: (0, j))],
        out_specs=pl.BlockSpec((tm, tn), lambda i, j: (i, j)),
        out_shape=jax.ShapeDtypeStruct((N_TOK, D_MODEL), F32),
        scratch_shapes=[pltpu.VMEM((tm, D_MODEL), BF16)],
        compiler_params=_cparams(("parallel", "arbitrary"), 56),
        name="gla_fnet_out",
    )(of, ob, proj, fm, g_norm.reshape(1, GLA_V_W), xs, mod, w_out)


NA_HROWS = NA_QROWS // 2
NA_WROWS = NA_HROWS + NA_KH
NA_TQ = NA_QROWS * GRID_W
NA_HQ = NA_HROWS * GRID_W
NA_WK = NA_WROWS * GRID_W
NA_KBLK = NA_HROWS * GRID_W
NA_WBLKS = NA_WK // NA_KBLK
NA_MASKED = 2 * NA_KH - 1
NA_ROW_BLOCKS = (SEQ // GRID_W) // NA_QROWS
NA_CASES = (0, 1, NA_ROW_BLOCKS - 1)


def _na_window_row0(j, half):
    q0 = NA_QROWS * j + NA_HROWS * half
    return int(np.clip(q0 - NA_KH // 2, 0, SEQ // GRID_W - NA_WROWS))


def _na_bias_slots():
    rows = SEQ // GRID_W
    slots = np.full((len(NA_CASES), 2, NA_HROWS, NA_WROWS), NA_MASKED, np.int32)
    for case, j in enumerate(NA_CASES):
        for half in range(2):
            w0 = _na_window_row0(j, half)
            for i in range(NA_HROWS):
                qr = NA_QROWS * j + NA_HROWS * half + i
                r0 = int(np.clip(qr - NA_KH // 2, 0, rows - NA_KH))
                for l in range(NA_WROWS):
                    kr = w0 + l
                    if r0 <= kr < r0 + NA_KH:
                        slots[case, half, i, l] = kr - qr + NA_KH - 1
    return slots


def _na_kernel(q_ref, *refs):
    nb = NA_WBLKS
    ka, kb, va, vb = refs[0:nb], refs[nb:2 * nb], refs[2 * nb:3 * nb], refs[3 * nb:4 * nb]
    kc_ref, vc_ref, td_ref, o_ref, bias_ref = refs[4 * nb:]
    j = pl.program_id(1)

    @pl.when((j == 0) & (pl.program_id(2) == 0))
    def _():
        slots = _na_bias_slots()
        left = lax.broadcasted_iota(jnp.int32, (GRID_W, 2 * GRID_W), 1) < GRID_W
        for case in range(len(NA_CASES)):
            for half in range(2):
                for hh in range(2):
                    dst = (case * 2 + half) * 2 + hh
                    for i in range(NA_HROWS):
                        for p in range(NA_WROWS // 2):
                            sl = int(slots[case, half, i, 2 * p])
                            sr = int(slots[case, half, i, 2 * p + 1])
                            tile = td_ref[hh, sl] if sl == sr else jnp.where(left, td_ref[hh, sl], td_ref[hh, sr])
                            bias_ref[dst, i * GRID_W:(i + 1) * GRID_W, p * 2 * GRID_W:(p + 1) * 2 * GRID_W] = tile

    case = jnp.where(j == 0, 0, jnp.where(j == NA_ROW_BLOCKS - 1, 2, 1))
    kc = kc_ref[...]
    vc = vc_ref[...]
    lane = lax.broadcasted_iota(jnp.int32, (NA_HQ, 2 * NA_HEAD_DIM), 1)
    for half, (k_refs, v_refs) in enumerate(((ka, va), (kb, vb))):
        q = q_ref[half * NA_HQ:(half + 1) * NA_HQ, :]
        k = jnp.concatenate([r[...] for r in k_refs], axis=0)
        v = jnp.concatenate([r[...] for r in v_refs], axis=0)
        out = None
        for hh in range(2):
            mine = (lane // NA_HEAD_DIM) == hh
            qm = jnp.where(mine, q, jnp.zeros_like(q))
            s_loc = _dot_nt(qm, k) + bias_ref[(case * 2 + half) * 2 + hh]
            s_ctx = _dot_nt(qm, kc)
            m = jnp.maximum(jnp.max(s_loc, axis=-1, keepdims=True), jnp.max(s_ctx, axis=-1, keepdims=True))
            p_loc = jnp.exp2(s_loc - m)
            p_ctx = jnp.exp2(s_ctx - m)
            denom = jnp.sum(p_loc, axis=-1, keepdims=True) + jnp.sum(p_ctx, axis=-1, keepdims=True)
            o = (_dot(p_loc.astype(BF16), v) + _dot(p_ctx.astype(BF16), vc)) / denom
            out = o if hh == 0 else jnp.where(mine, o, out)
        o_ref[half * NA_HQ:(half + 1) * NA_HQ, :] = out.astype(o_ref.dtype)


def _na(qkv, tiles):
    hp_n = NA_HEADS // 2
    jn = NA_ROW_BLOCKS
    kb_per_img = SEQ // NA_KBLK
    last_blk = kb_per_img - NA_WBLKS

    def kv_spec(col0, half, t):
        return pl.BlockSpec((NA_KBLK, 128), lambda hp, j, b: (
            b * kb_per_img + jnp.clip(2 * j + half - 1, 0, last_blk) + t, col0 + hp))

    def kv_specs(col0):
        return [kv_spec(col0, half, t) for half in range(2) for t in range(NA_WBLKS)]

    ctx0 = N_LAT // CTX_LEN
    in_specs = ([pl.BlockSpec((NA_TQ, 128), lambda hp, j, b: (b * jn + j, hp))]
                + kv_specs(hp_n) + kv_specs(2 * hp_n)
                + [pl.BlockSpec((CTX_LEN, 128), lambda hp, j, b: (ctx0 + b, hp_n + hp)),
                   pl.BlockSpec((CTX_LEN, 128), lambda hp, j, b: (ctx0 + b, 2 * hp_n + hp)),
                   pl.BlockSpec((2, 2 * NA_KH, GRID_W, 2 * GRID_W), lambda hp, j, b: (hp, 0, 0, 0))])
    return pl.pallas_call(
        _na_kernel,
        grid=(hp_n, jn, BATCH),
        in_specs=in_specs,
        out_specs=pl.BlockSpec((NA_TQ, 128), lambda hp, j, b: (b * jn + j, hp)),
        out_shape=jax.ShapeDtypeStruct((N_LAT, D_MODEL), BF16),
        scratch_shapes=[pltpu.VMEM((len(NA_CASES) * 4, NA_HQ, NA_WK), F32)],
        compiler_params=_cparams(("arbitrary", "arbitrary", "arbitrary"), 40),
        name="neighbourhood_attention",
    )(qkv, *([qkv] * (4 * NA_WBLKS)), qkv, qkv, tiles)


def _na_bias_tiles(rpb):
    w = GRID_W
    col = np.arange(w)
    cs = np.clip(col - NA_KW // 2, 0, w - NA_KW)
    col_mask = (col[None, :] >= cs[:, None]) & (col[None, :] < cs[:, None] + NA_KW)
    dc_idx = np.clip(col[None, :] - col[:, None] + NA_KW - 1, 0, 2 * NA_KW - 2)
    t1 = (rpb.astype(F32) * LOG2_E)[:, :, dc_idx]
    t1 = jnp.where(col_mask[None, None], t1, NEG_BIG)
    t1 = jnp.concatenate([t1, jnp.full((NA_HEADS, 1, w, w), NEG_BIG, F32)], axis=1)
    return jnp.concatenate([t1, t1], axis=-1)


MR_TM = 1024
MR_TN = 1024


def _mmres_kernel(a_ref, w_ref, x_ref, gt_ref, o_ref):
    o_ref[...] = x_ref[...] + gt_ref[0] * _dot(a_ref[...], w_ref[...])


def _mm_res(a, w, xs, mod, n_rows):
    tm, tn = MR_TM, MR_TN
    return pl.pallas_call(
        _mmres_kernel,
        grid=(n_rows // tm, D_MODEL // tn),
        in_specs=[pl.BlockSpec((tm, D_MODEL), lambda i, j: (i, 0)),
                  pl.BlockSpec((D_MODEL, tn), lambda i, j: (0, j)),
                  pl.BlockSpec((tm, tn), lambda i, j: (i, j)),
                  _mod_spec(5, tm, tn, col=True)],
        out_specs=pl.BlockSpec((tm, tn), lambda i, j: (i, j)),
        out_shape=jax.ShapeDtypeStruct((n_rows, D_MODEL), F32),
        compiler_params=_cparams(("parallel", "arbitrary"), 40),
        name="matmul_gated_residual",
    )(a, w, xs, mod)


def _rope_tables():
    half = GLA_DK // 4
    inv_freq = ROPE_BASE ** (-jnp.arange(half, dtype=F32) / half)
    t = jnp.arange(SEQ)
    ang_r = (t // GRID_W).astype(F32)[:, None] * inv_freq[None, :]
    ang_c = (t % GRID_W).astype(F32)[:, None] * inv_freq[None, :]
    cr, sr, cc, sc = jnp.cos(ang_r), jnp.sin(ang_r), jnp.cos(ang_c), jnp.sin(ang_c)
    cos_t = jnp.concatenate([cr, cc, cr, cc], axis=-1)
    sin_t = jnp.concatenate([-sr, -sc, sr, sc], axis=-1)
    cos_t = jnp.concatenate([cos_t, jnp.ones((GLA_BLOCK, GLA_DK), F32)], axis=0)
    sin_t = jnp.concatenate([sin_t, jnp.zeros((GLA_BLOCK, GLA_DK), F32)], axis=0)
    return cos_t, sin_t


def _pair_layout(w):
    lead = w.shape[:-1]
    quarter = GLA_DK // 4
    w = w.reshape(*lead, GLA_HEADS, 2, 2, quarter)
    return jnp.swapaxes(w, -2, -3).reshape(*lead, GLA_QK_W)


def kernel(x, c, ctx, c_ctx, l0_ada_w, l0_ada_b, l0_norm_ffn1, l0_norm_mix, l0_norm_ffn2, l0_ffn1_w_gu, l0_ffn1_w_down, l0_ffn2_w_gu, l0_ffn2_w_down, l0_w_in, l0_gla_gate_w_fwd, l0_gla_gate_b_fwd, l0_gla_gate_w_bwd, l0_gla_gate_b_bwd, l0_gla_norm, l0_w_out, l1_ada_w, l1_ada_b, l1_norm_ffn1, l1_norm_mix, l1_norm_ffn2, l1_ffn1_w_gu, l1_ffn1_w_down, l1_ffn2_w_gu, l1_ffn2_w_down, l1_w_qkv, l1_rpb, l1_w_out, norm_out):
    bf = lambda w: w.astype(BF16)
    cc = jnp.concatenate([c, c_ctx[None, :], jnp.zeros((8 - BATCH - 1, D_MODEL), F32)], axis=0)

    mod0 = _ada(cc, l0_ada_w, l0_ada_b)
    w_gu, w_down = bf(l0_ffn1_w_gu), bf(l0_ffn1_w_down)
    xs = _ffn(x.reshape(N_LAT, D_MODEL), mod0, 0, l0_norm_ffn1, w_gu, w_down, out_rows=N_TOK)
    xs = _ffn(ctx.reshape(N_CTX, D_MODEL), mod0, 0, l0_norm_ffn1, w_gu, w_down, out_rows=N_TOK,
              tile0=N_LAT // FFN_TM, prev=xs)

    a0 = 2 * GLA_QK_W + GLA_V_W
    a1 = a0 + 2 * GLA_LOWRANK
    w_main = bf(jnp.concatenate([_pair_layout(l0_w_in[:, :GLA_QK_W]), _pair_layout(l0_w_in[:, GLA_QK_W:2 * GLA_QK_W]),
                                 l0_w_in[:, 2 * GLA_QK_W:a0], l0_w_in[:, a1:]], axis=1))
    w_lr = bf(jnp.pad(l0_w_in[:, a0:a1], ((0, 0), (0, GLA_LR_PAD - 2 * GLA_LOWRANK))))
    proj, lr = _nmm(xs, mod0, 1, l0_norm_mix, w_main, F32, w_narrow=w_lr)

    cos_t, sin_t = _rope_tables()
    gwf = jnp.pad(_pair_layout(l0_gla_gate_w_fwd), ((0, GLA_LR_PAD - GLA_LOWRANK), (0, 0)))
    gwb = jnp.pad(_pair_layout(l0_gla_gate_w_bwd), ((GLA_LOWRANK, GLA_LR_PAD - 2 * GLA_LOWRANK), (0, 0)))
    of, ob = _gla(proj, lr, cos_t, sin_t, gwf, _pair_layout(l0_gla_gate_b_fwd).reshape(1, -1),
                  gwb, _pair_layout(l0_gla_gate_b_bwd).reshape(1, -1))

    cc_np, sc_np = _dft_tables(FNET_GROUP_DIM)
    wc = jnp.asarray(np.concatenate([cc_np, sc_np], axis=1), F32).astype(BF16)
    zc, zs = _chan_dft(proj, wc)
    cos_big, sin_big = _big_dft_tables()
    fm = _time_dft(cos_big, sin_big, zc, zs, SEQ, 512, 512, 0)
    ct_np, st_np = _dft_tables(CTX_LEN)
    fm = _time_dft(jnp.asarray(ct_np, F32).astype(BF16), jnp.asarray(st_np, F32).astype(BF16),
                   zc, zs, CTX_LEN, CTX_LEN, 512, N_LAT, prev=fm)

    xs = _glaout(of, ob, proj, fm, l0_gla_norm, xs, mod0, bf(l0_w_out))
    xs = _ffn(xs, mod0, 2, l0_norm_ffn2, bf(l0_ffn2_w_gu), bf(l0_ffn2_w_down), out_rows=N_TOK)

    mod1 = _ada(cc, l1_ada_w, l1_ada_b)
    xs = _ffn(xs, mod1, 0, l1_norm_ffn1, bf(l1_ffn1_w_gu), bf(l1_ffn1_w_down), out_rows=N_TOK)
    w_qkv = bf(jnp.concatenate([l1_w_qkv[:, :D_MODEL] * NA_Q_SCALE, l1_w_qkv[:, D_MODEL:]], axis=1))
    qkv = _nmm(xs, mod1, 1, l1_norm_mix, w_qkv, BF16)
    ao = _na(qkv, _na_bias_tiles(l1_rpb))
    xl = _mm_res(ao, bf(l1_w_out), xs, mod1, N_LAT)
    out = _ffn(xl, mod1, 2, l1_norm_ffn2, bf(l1_ffn2_w_gu), bf(l1_ffn2_w_down), out_rows=N_LAT, final_g=norm_out)
    return out.reshape(BATCH, SEQ, D_MODEL)
```

```python
import functools

import numpy as np
import jax
import jax.numpy as jnp
from jax import lax
from jax.experimental import pallas as pl
from jax.experimental.pallas import tpu as pltpu

D_MODEL = 2048
BATCH = 4
SEQ = 4096
GRID_W = 64
CTX_LEN = 256
N_MOD = 9
D_FF = 5632
RMS_EPS = 1e-6
ROPE_BASE = 10000.0

GLA_HEADS = 4
GLA_DK = 128
GLA_DV = 256
GLA_LOWRANK = 16
GLA_TAU = 16.0
GLA_CHUNK = 64
FNET_GROUPS = 4
FNET_GROUP_DIM = 256
GLA_QK_W = GLA_HEADS * GLA_DK
GLA_V_W = GLA_HEADS * GLA_DV
FNET_W = FNET_GROUPS * FNET_GROUP_DIM

NA_HEADS = 32
NA_HEAD_DIM = 64
NA_KH = 8
NA_KW = 16
NA_QROWS = 8
NEG_BIG = -1e30
LOG2_E = 1.4426950408889634
NA_Q_SCALE = NA_HEAD_DIM ** -0.5 * LOG2_E

N_LAT = BATCH * SEQ
N_CTX = BATCH * CTX_LEN
N_TOK = N_LAT + N_CTX

F32 = jnp.float32
BF16 = jnp.bfloat16
MIB = 1024 * 1024


def _cparams(sem, vmem_mib):
    return pltpu.CompilerParams(dimension_semantics=sem, vmem_limit_bytes=vmem_mib * MIB)


def _dot(a, b):
    return jnp.dot(a, b, preferred_element_type=F32)


def _dot_nt(a, b):
    return lax.dot_general(a, b, (((1,), (1,)), ((), ())), preferred_element_type=F32)


def _dot_tn(a, b):
    return lax.dot_general(a, b, (((0,), (0,)), ((), ())), preferred_element_type=F32)


def _split2(x):
    hi = x.astype(BF16)
    lo = (x - hi.astype(F32)).astype(BF16)
    return hi, lo


def _dot3(a, b):
    ah, al = _split2(a)
    bh, bl = _split2(b)
    return _dot(ah, bh) + (_dot(al, bh) + _dot(ah, bl))


def _dot_exact_lhs(l_bf16, x):
    x1 = x.astype(BF16)
    r1 = x - x1.astype(F32)
    x2 = r1.astype(BF16)
    x3 = (r1 - x2.astype(F32)).astype(BF16)
    return _dot(l_bf16, x1) + (_dot(l_bf16, x2) + _dot(l_bf16, x3))


def _sigmoid(x):
    return 1.0 / (1.0 + jnp.exp(-x))


def _rms(x, g):
    ms = jnp.mean(x * x, axis=-1, keepdims=True)
    return x * lax.rsqrt(ms + RMS_EPS) * g


ROW_CHUNK = 16
ROW_GROUP = 8


def _row_groups(n_rows, body, group=ROW_GROUP):
    span = ROW_CHUNK * group

    def step(t, carry):
        base = t * span
        body([pl.ds(pl.multiple_of(base + c * ROW_CHUNK, ROW_CHUNK), ROW_CHUNK) for c in range(group)])
        return carry
    lax.fori_loop(0, n_rows // span, step, 0)


def _inv_rms(x):
    return lax.rsqrt(jnp.mean(x * x, axis=-1, keepdims=True) + RMS_EPS)


def _norm_mod_rows(x_ref, g_ref, sh_ref, sc_ref, h_ref, gs_ref, zero_ref=None):
    d = x_ref.shape[1]
    gs_ref[0] = jnp.broadcast_to(g_ref[...] * (1.0 + sc_ref[0]), (ROW_CHUNK, d))
    gs_ref[1] = jnp.broadcast_to(sh_ref[0], (ROW_CHUNK, d))

    def body(chunks):
        inv = [_inv_rms(x_ref[rows, :]) for rows in chunks]
        for rows, r in zip(chunks, inv):
            h_ref[rows, :] = (x_ref[rows, :] * r * gs_ref[0] + gs_ref[1]).astype(BF16)
            if zero_ref is not None:
                zero_ref[rows, :] = jnp.zeros((ROW_CHUNK, zero_ref.shape[1]), zero_ref.dtype)
    _row_groups(x_ref.shape[0], body)


def _mod_row(i, tm):
    return jnp.minimum(i // (SEQ // tm), BATCH)


def _mod_spec(j, tm, width=D_MODEL, col=None, tile0=0):
    if col is None:
        return pl.BlockSpec((1, 1, width), lambda i, *_: (_mod_row(i + tile0, tm) * N_MOD + j, 0, 0))
    return pl.BlockSpec((1, 1, width), lambda i, n, *_: (_mod_row(i + tile0, tm) * N_MOD + j, 0, n))


ADA_TN = 1024


def _ada_kernel(c_ref, w_ref, b_ref, o_ref):
    c = c_ref[...]
    s = (c * _sigmoid(c)).astype(BF16)
    o_ref[...] = _dot(s, w_ref[...].astype(BF16)) + b_ref[...]


def _ada(cc, w, b):
    n = w.shape[1]
    out = pl.pallas_call(
        _ada_kernel,
        grid=(n // ADA_TN,),
        in_specs=[pl.BlockSpec((8, D_MODEL), lambda j: (0, 0)),
                  pl.BlockSpec((D_MODEL, ADA_TN), lambda j: (0, j)),
                  pl.BlockSpec((1, ADA_TN), lambda j: (0, j))],
        out_specs=pl.BlockSpec((8, ADA_TN), lambda j: (0, j)),
        out_shape=jax.ShapeDtypeStruct((8, n), F32),
        compiler_params=_cparams(("arbitrary",), 40),
        name="ada_mod",
    )(cc, w, b.reshape(1, n))
    return out.reshape(8 * N_MOD, 1, D_MODEL)


FFN_TM = 1024
FFN_TF = 512


def _ffn_kernel(x_ref, g_ref, sh_ref, sc_ref, gt_ref, wg_ref, wu_ref, wd_ref, *rest, final_norm):
    go_ref = rest[0] if final_norm else None
    o_ref, h_ref, gs_ref = rest[-3:]
    f = pl.program_id(1)

    @pl.when(f == 0)
    def _():
        _norm_mod_rows(x_ref, g_ref, sh_ref, sc_ref, h_ref, gs_ref, zero_ref=o_ref)

    h = h_ref[...]
    a = _dot(h, wg_ref[...])
    u = _dot(h, wu_ref[...])
    act = (a * _sigmoid(a) * u).astype(BF16)
    o_ref[...] += _dot(act, wd_ref[...])

    @pl.when(f == pl.num_programs(1) - 1)
    def _():
        if not final_norm:
            o_ref[...] = x_ref[...] + (0.5 * gt_ref[0]) * o_ref[...]
        else:
            d = x_ref.shape[1]
            gs_ref[0] = jnp.broadcast_to(0.5 * gt_ref[0], (ROW_CHUNK, d))
            gs_ref[1] = jnp.broadcast_to(go_ref[...], (ROW_CHUNK, d))

            def body(chunks):
                resid = lambda rows: x_ref[rows, :] + gs_ref[0] * o_ref[rows, :]
                inv = [_inv_rms(resid(rows)) for rows in chunks]
                for rows, r in zip(chunks, inv):
                    o_ref[rows, :] = resid(rows) * r * gs_ref[1]
            _row_groups(x_ref.shape[0], body)


def _ffn(x_in, mod, sub, norm_g, w_gu, w_down, *, out_rows, tile0=0, prev=None, final_g=None):
    tm, tf = FFN_TM, FFN_TF
    nf = D_FF // tf
    in_specs = [pl.BlockSpec((tm, D_MODEL), lambda i, f: (i, 0)),
                pl.BlockSpec((1, D_MODEL), lambda i, f: (0, 0)),
                _mod_spec(3 * sub, tm, tile0=tile0), _mod_spec(3 * sub + 1, tm, tile0=tile0),
                _mod_spec(3 * sub + 2, tm, tile0=tile0),
                pl.BlockSpec((D_MODEL, tf), lambda i, f: (0, f)),
                pl.BlockSpec((D_MODEL, tf), lambda i, f: (0, f + nf)),
                pl.BlockSpec((tf, D_MODEL), lambda i, f: (f, 0))]
    args = [x_in, norm_g.reshape(1, D_MODEL), mod, mod, mod, w_gu, w_gu, w_down]
    if final_g is not None:
        in_specs.append(pl.BlockSpec((1, D_MODEL), lambda i, f: (0, 0)))
        args.append(final_g.reshape(1, D_MODEL))
    aliases = {}
    if prev is not None:
        aliases = {len(args): 0}
        in_specs.append(pl.BlockSpec(memory_space=pl.ANY))
        args.append(prev)
    return pl.pallas_call(
        functools.partial(_ffn_kernel, final_norm=final_g is not None),
        grid=(x_in.shape[0] // tm, nf),
        in_specs=in_specs,
        out_specs=pl.BlockSpec((tm, D_MODEL), lambda i, f: (i + tile0, 0)),
        out_shape=jax.ShapeDtypeStruct((out_rows, D_MODEL), F32),
        scratch_shapes=[pltpu.VMEM((tm, D_MODEL), BF16), pltpu.VMEM((2, ROW_CHUNK, D_MODEL), F32)],
        input_output_aliases=aliases,
        compiler_params=_cparams(("parallel", "arbitrary"), 60),
        name="ffn_half",
    )(*args)


NMM_TM = 1024
NMM_TN = 1024


def _nmm_kernel(x_ref, g_ref, sh_ref, sc_ref, w_ref, *rest, narrow):
    if narrow:
        wn_ref, o_ref, on_ref, h_ref, gs_ref = rest
    else:
        o_ref, h_ref, gs_ref = rest

    @pl.when(pl.program_id(1) == 0)
    def _():
        _norm_mod_rows(x_ref, g_ref, sh_ref, sc_ref, h_ref, gs_ref)
        if narrow:
            on_ref[...] = _dot(h_ref[...], wn_ref[...])

    o_ref[...] = _dot(h_ref[...], w_ref[...]).astype(o_ref.dtype)


def _nmm(xs, mod, sub, norm_g, w, out_dtype, w_narrow=None):
    tm, tn = NMM_TM, NMM_TN
    n = w.shape[1]
    narrow = w_narrow is not None
    in_specs = [pl.BlockSpec((tm, D_MODEL), lambda i, j: (i, 0)),
                pl.BlockSpec((1, D_MODEL), lambda i, j: (0, 0)),
                _mod_spec(3 * sub, tm), _mod_spec(3 * sub + 1, tm),
                pl.BlockSpec((D_MODEL, tn), lambda i, j: (0, j))]
    args = [xs, norm_g.reshape(1, D_MODEL), mod, mod, w]
    out_specs = pl.BlockSpec((tm, tn), lambda i, j: (i, j))
    out_shape = jax.ShapeDtypeStruct((N_TOK, n), out_dtype)
    if narrow:
        nn = w_narrow.shape[1]
        in_specs.append(pl.BlockSpec((D_MODEL, nn), lambda i, j: (0, 0)))
        args.append(w_narrow)
        out_specs = [out_specs, pl.BlockSpec((tm, nn), lambda i, j: (i, 0))]
        out_shape = [out_shape, jax.ShapeDtypeStruct((N_TOK, nn), F32)]
    return pl.pallas_call(
        functools.partial(_nmm_kernel, narrow=narrow),
        grid=(N_TOK // tm, n // tn),
        in_specs=in_specs,
        out_specs=out_specs,
        out_shape=out_shape,
        scratch_shapes=[pltpu.VMEM((tm, D_MODEL), BF16), pltpu.VMEM((2, ROW_CHUNK, D_MODEL), F32)],
        compiler_params=_cparams(("parallel", "arbitrary"), 48),
        name="norm_mod_matmul",
    )(*args)


GLA_BLOCK = CTX_LEN
GLA_LAT_BLOCKS = SEQ // GLA_BLOCK
GLA_STEPS = 1 + GLA_LAT_BLOCKS
GLA_LR_PAD = 128


def _rope(x, cos, sin_signed):
    return x * cos + pltpu.roll(x, GLA_DK // 2, 1) * sin_signed


def _gla_kernel(qf, kf, vf, lf, cf, sf, qb, kb, vb, lb, cb, sb, gwf, gbf, gwb, gbb,
                of_ref, ob_ref, st_ref):
    @pl.when(pl.program_id(1) == 0)
    def _():
        st_ref[...] = jnp.zeros_like(st_ref)

    c = GLA_CHUNK
    n = GLA_BLOCK
    row = lax.broadcasted_iota(jnp.int32, (n, n), 0)
    col = lax.broadcasted_iota(jnp.int32, (n, n), 1)
    same_chunk = (row // c) == (col // c)
    row_c = lax.broadcasted_iota(jnp.int32, (c, c), 0)
    col_c = lax.broadcasted_iota(jnp.int32, (c, c), 1)
    scale = GLA_DK ** -0.5

    dirs = ((qf, kf, vf, lf, cf, sf, gwf, gbf, of_ref), (qb, kb, vb, lb, cb, sb, gwb, gbb, ob_ref))
    for d, (q_ref, k_ref, v_ref, l_ref, c_ref, s_ref, gw_ref, gb_ref, o_ref) in enumerate(dirs):
        causal = (row >= col) if d == 0 else (row <= col)
        tri_b = jnp.where(same_chunk & causal, 1.0, 0.0).astype(BF16)
        tri = (row_c >= col_c) if d == 0 else (row_c <= col_c)
        x = _dot3(l_ref[...], gw_ref[...]) + gb_ref[...]
        g = (jnp.minimum(x, 0.0) - jnp.log1p(jnp.exp(-jnp.abs(x)))) * (1.0 / GLA_TAU)
        bc_all = _dot_exact_lhs(tri_b, g)
        cos = c_ref[...]
        sin = s_ref[...]
        chunk_order = range(n // c) if d == 0 else range(n // c - 1, -1, -1)
        for ci in chunk_order:
            rows = slice(ci * c, (ci + 1) * c)
            bc = bc_all[rows, :]
            mid = bc[c // 2:c // 2 + 1, :]
            last = bc[c - 1:c, :] if d == 0 else bc[0:1, :]
            e_q = jnp.exp(bc)
            e_qm = jnp.exp(bc - mid)
            e_km = jnp.exp(mid - bc)
            e_kl = jnp.exp(last - bc)
            e_l = jnp.exp(last)
            for h in range(GLA_HEADS):
                sk = slice(h * GLA_DK, (h + 1) * GLA_DK)
                sv = slice(h * GLA_DV, (h + 1) * GLA_DV)
                qh = _rope(q_ref[rows, sk] * scale, cos[rows, :], sin[rows, :])
                kh = _rope(k_ref[rows, sk], cos[rows, :], sin[rows, :])
                vh = v_ref[rows, sv].astype(BF16)
                att = _dot_nt((qh * e_qm[:, sk]).astype(BF16), (kh * e_km[:, sk]).astype(BF16))
                att = jnp.where(tri, att, 0.0)
                st = st_ref[d, h]
                o = _dot_nt((qh * e_q[:, sk]).astype(BF16), st.astype(BF16))
                o = o + _dot(att.astype(BF16), vh)
                o_ref[rows, sv] = o
                kd = (kh * e_kl[:, sk]).astype(BF16)
                st_ref[d, h] = st * e_l[:, sk] + _dot_tn(vh, kd)


def _gla(proj, lr, cos_t, sin_t, gwf, gbf, gwb, gbb):
    c = GLA_BLOCK
    ctx0 = N_LAT // c

    def fwd_blk(b, s):
        return jnp.where(s == 0, ctx0 + b, GLA_LAT_BLOCKS * b + s - 1)

    def bwd_blk(b, s):
        return jnp.where(s == 0, ctx0 + b, GLA_LAT_BLOCKS * b + (GLA_STEPS - 1 - s))

    def fwd_rope(b, s):
        return jnp.where(s == 0, GLA_LAT_BLOCKS, s - 1)

    def bwd_rope(b, s):
        return jnp.where(s == 0, GLA_LAT_BLOCKS, GLA_STEPS - 1 - s)

    def dir_specs(blk, rope):
        return [pl.BlockSpec((c, GLA_QK_W), lambda b, s: (blk(b, s), 0)),
                pl.BlockSpec((c, GLA_QK_W), lambda b, s: (blk(b, s), 1)),
                pl.BlockSpec((c, GLA_V_W), lambda b, s: (blk(b, s), 1)),
                pl.BlockSpec((c, GLA_LR_PAD), lambda b, s: (blk(b, s), 0)),
                pl.BlockSpec((c, GLA_DK), lambda b, s: (rope(b, s), 0)),
                pl.BlockSpec((c, GLA_DK), lambda b, s: (rope(b, s), 0))]

    const = lambda shape: pl.BlockSpec(shape, lambda b, s: (0, 0))
    in_specs = (dir_specs(fwd_blk, fwd_rope) + dir_specs(bwd_blk, bwd_rope)
                + [const((GLA_LR_PAD, GLA_QK_W)), const((1, GLA_QK_W)),
                   const((GLA_LR_PAD, GLA_QK_W)), const((1, GLA_QK_W))])
    return pl.pallas_call(
        _gla_kernel,
        grid=(BATCH, GLA_STEPS),
        in_specs=in_specs,
        out_specs=[pl.BlockSpec((c, GLA_V_W), lambda b, s: (fwd_blk(b, s), 0)),
                   pl.BlockSpec((c, GLA_V_W), lambda b, s: (bwd_blk(b, s), 0))],
        out_shape=[jax.ShapeDtypeStruct((N_TOK, GLA_V_W), F32)] * 2,
        scratch_shapes=[pltpu.VMEM((2, GLA_HEADS, GLA_DV, GLA_DK), F32)],
        compiler_params=_cparams(("parallel", "arbitrary"), 32),
        name="gla_scan",
    )(proj, proj, proj, lr, cos_t, sin_t, proj, proj, proj, lr, cos_t, sin_t, gwf, gbf, gwb, gbb)


CDFT_TM = 1024
PROJ_F_COL0 = 3072


def _cdft_kernel(x_ref, w_ref, zc_ref, zs_ref):
    z = _dot(x_ref[...].astype(BF16), w_ref[...])
    zc_ref[...] = z[:, :FNET_GROUP_DIM].astype(BF16)
    zs_ref[...] = z[:, FNET_GROUP_DIM:].astype(BF16)


def _chan_dft(proj, wc):
    tm, gd = CDFT_TM, FNET_GROUP_DIM
    col0 = PROJ_F_COL0 // gd
    return pl.pallas_call(
        _cdft_kernel,
        grid=(N_TOK // tm, FNET_GROUPS),
        in_specs=[pl.BlockSpec((tm, gd), lambda i, g: (i, col0 + g)),
                  pl.BlockSpec((gd, 2 * gd), lambda i, g: (0, 0))],
        out_specs=[pl.BlockSpec((tm, gd), lambda i, g: (i, g))] * 2,
        out_shape=[jax.ShapeDtypeStruct((N_TOK, FNET_W), BF16)] * 2,
        compiler_params=_cparams(("parallel", "parallel"), 32),
        name="fnet_channel_dft",
    )(proj, wc)


def _tdft_kernel(c_ref, s_ref, zc_ref, zs_ref, *rest):
    o_ref = rest[-1]
    o_ref[...] = _dot(c_ref[...], zc_ref[...]) - _dot(s_ref[...], zs_ref[...])


def _time_dft(cmat, smat, zc, zs, t_len, tm, tn, row0, prev=None):
    zb0 = row0 // t_len
    ob0 = row0 // tm
    mt = t_len // tm
    in_specs = [pl.BlockSpec((tm, t_len), lambda b, n, m: (m, 0)),
                pl.BlockSpec((tm, t_len), lambda b, n, m: (m, 0)),
                pl.BlockSpec((t_len, tn), lambda b, n, m: (zb0 + b, n)),
                pl.BlockSpec((t_len, tn), lambda b, n, m: (zb0 + b, n))]
    args = [cmat, smat, zc, zs]
    aliases = {}
    if prev is not None:
        in_specs.append(pl.BlockSpec(memory_space=pl.ANY))
        args.append(prev)
        aliases = {4: 0}
    return pl.pallas_call(
        _tdft_kernel,
        grid=(BATCH, FNET_W // tn, mt),
        in_specs=in_specs,
        out_specs=pl.BlockSpec((tm, tn), lambda b, n, m: (ob0 + b * mt + m, n)),
        out_shape=jax.ShapeDtypeStruct((N_TOK, FNET_W), F32),
        input_output_aliases=aliases,
        compiler_params=_cparams(("parallel", "parallel", "arbitrary"), 44),
        name="fnet_time_dft",
    )(*args)


def _dft_tables(n):
    k = np.arange(n)
    ang = 2.0 * np.pi * ((k[:, None] * k[None, :]) % n) / n
    return np.cos(ang) / np.sqrt(n), np.sin(ang) / np.sqrt(n)


def _dft_table_kernel(ac_ref, as_ref, bc_ref, bs_ref, co_ref, so_ref):
    ac, as_ = ac_ref[0], as_ref[0]
    bc, bs = bc_ref[...], bs_ref[...]
    co_ref[...] = (ac * bc - as_ * bs).astype(BF16)
    so_ref[...] = (ac * bs + as_ * bc).astype(BF16)


def _big_dft_tables():
    r = GRID_W
    k = np.arange(SEQ)
    j = np.arange(r)
    pa = 2.0 * np.pi * ((j[:, None] * r * k[None, :]) % SEQ) / SEQ
    pb = 2.0 * np.pi * ((j[:, None] * k[None, :]) % SEQ) / SEQ
    ac = jnp.asarray(np.cos(pa).reshape(r, 1, SEQ), F32)
    as_ = jnp.asarray(np.sin(pa).reshape(r, 1, SEQ), F32)
    bc = jnp.asarray(np.cos(pb) * SEQ ** -0.5, F32)
    bs = jnp.asarray(np.sin(pb) * SEQ ** -0.5, F32)
    row = pl.BlockSpec((1, 1, SEQ), lambda i: (i, 0, 0))
    full = pl.BlockSpec((r, SEQ), lambda i: (0, 0))
    return pl.pallas_call(
        _dft_table_kernel,
        grid=(r,),
        in_specs=[row, row, full, full],
        out_specs=[pl.BlockSpec((r, SEQ), lambda i: (i, 0))] * 2,
        out_shape=[jax.ShapeDtypeStruct((SEQ, SEQ), BF16)] * 2,
        compiler_params=_cparams(("parallel",), 16),
        name="dft_tables",
    )(ac, as_, bc, bs)


GO_TM = 512
GO_TN = 2048


def _glaout_kernel(of_ref, ob_ref, r_ref, fm_ref, gn_ref, x_ref, gt_ref, w_ref, o_ref, h_ref):
    @pl.when(pl.program_id(1) == 0)
    def _():
        heads = [slice(h * GLA_DV, (h + 1) * GLA_DV) for h in range(GLA_HEADS)]

        def body(chunks):
            inv = [[_inv_rms(of_ref[rows, sv] + ob_ref[rows, sv]) for sv in heads] for rows in chunks]
            for rows, inv_c in zip(chunks, inv):
                for sv, r in zip(heads, inv_c):
                    o = (of_ref[rows, sv] + ob_ref[rows, sv]) * r
                    gate = r_ref[rows, sv]
                    h_ref[rows, sv] = (o * gn_ref[:, sv] * (gate * _sigmoid(gate))).astype(BF16)
                h_ref[rows, GLA_V_W:] = fm_ref[rows, :].astype(BF16)
        _row_groups(of_ref.shape[0], body, group=4)

    o_ref[...] = x_ref[...] + gt_ref[0] * _dot(h_ref[...], w_ref[...])


def _glaout(of, ob, proj, fm, g_norm, xs, mod, w_out):
    tm, tn = GO_TM, GO_TN
    half = lambda col: pl.BlockSpec((tm, GLA_V_W), lambda i, j: (i, col))
    return pl.pallas_call(
        _glaout_kernel,
        grid=(N_TOK // tm, D_MODEL // tn),
        in_specs=[half(0), half(0), half(2), half(0),
                  pl.BlockSpec((1, GLA_V_W), lambda i, j: (0, 0)),
                  pl.BlockSpec((tm, tn), lambda i, j: (i, j)),
                  _mod_spec(5, tm, tn, col=True),
                  pl.BlockSpec((D_MODEL, tn), lambda i, j: (0, j))],
        out_specs=pl.BlockSpec((tm, tn), lambda i, j: (i, j)),
        out_shape=jax.ShapeDtypeStruct((N_TOK, D_MODEL), F32),
        scratch_shapes=[pltpu.VMEM((tm, D_MODEL), BF16)],
        compiler_params=_cparams(("parallel", "arbitrary"), 56),
        name="gla_fnet_out",
    )(of, ob, proj, fm, g_norm.reshape(1, GLA_V_W), xs, mod, w_out)


NA_HROWS = NA_QROWS // 2
NA_WROWS = NA_HROWS + NA_KH
NA_TQ = NA_QROWS * GRID_W
NA_HQ = NA_HROWS * GRID_W
NA_WK = NA_WROWS * GRID_W
NA_KBLK = NA_HROWS * GRID_W
NA_WBLKS = NA_WK // NA_KBLK
NA_MASKED = 2 * NA_KH - 1
NA_ROW_BLOCKS = (SEQ // GRID_W) // NA_QROWS
NA_CASES = (0, 1, NA_ROW_BLOCKS - 1)


def _na_window_row0(j, half):
    q0 = NA_QROWS * j + NA_HROWS * half
    return int(np.clip(q0 - NA_KH // 2, 0, SEQ // GRID_W - NA_WROWS))


def _na_bias_slots():
    rows = SEQ // GRID_W
    slots = np.full((len(NA_CASES), 2, NA_HROWS, NA_WROWS), NA_MASKED, np.int32)
    for case, j in enumerate(NA_CASES):
        for half in range(2):
            w0 = _na_window_row0(j, half)
            for i in range(NA_HROWS):
                qr = NA_QROWS * j + NA_HROWS * half + i
                r0 = int(np.clip(qr - NA_KH // 2, 0, rows - NA_KH))
                for l in range(NA_WROWS):
                    kr = w0 + l
                    if r0 <= kr < r0 + NA_KH:
                        slots[case, half, i, l] = kr - qr + NA_KH - 1
    return slots


def _na_kernel(q_ref, *refs):
    nb = NA_WBLKS
    ka, kb, va, vb = refs[0:nb], refs[nb:2 * nb], refs[2 * nb:3 * nb], refs[3 * nb:4 * nb]
    kc_ref, vc_ref, td_ref, o_ref, bias_ref = refs[4 * nb:]
    j = pl.program_id(1)

    @pl.when((j == 0) & (pl.program_id(2) == 0))
    def _():
        slots = _na_bias_slots()
        left = lax.broadcasted_iota(jnp.int32, (GRID_W, 2 * GRID_W), 1) < GRID_W
        for case in range(len(NA_CASES)):
            for half in range(2):
                for hh in range(2):
                    dst = (case * 2 + half) * 2 + hh
                    for i in range(NA_HROWS):
                        for p in range(NA_WROWS // 2):
                            sl = int(slots[case, half, i, 2 * p])
                            sr = int(slots[case, half, i, 2 * p + 1])
                            tile = td_ref[hh, sl] if sl == sr else jnp.where(left, td_ref[hh, sl], td_ref[hh, sr])
                            bias_ref[dst, i * GRID_W:(i + 1) * GRID_W, p * 2 * GRID_W:(p + 1) * 2 * GRID_W] = tile

    case = jnp.where(j == 0, 0, jnp.where(j == NA_ROW_BLOCKS - 1, 2, 1))
    kc = kc_ref[...]
    vc = vc_ref[...]
    lane = lax.broadcasted_iota(jnp.int32, (NA_HQ, 2 * NA_HEAD_DIM), 1)
    kv_refs = ((ka, va), (kb, vb))
    items = [(half, hh) for half in range(2) for hh in range(2)]

    def scores(half, hh):
        q = q_ref[half * NA_HQ:(half + 1) * NA_HQ, :]
        qm = jnp.where((lane // NA_HEAD_DIM) == hh, q, jnp.zeros_like(q))
        k = jnp.concatenate([r[...] for r in kv_refs[half][0]], axis=0)
        return _dot_nt(qm, k) + bias_ref[(case * 2 + half) * 2 + hh], _dot_nt(qm, kc)

    def softmax(s_loc, s_ctx):
        m = jnp.maximum(jnp.max(s_loc, axis=-1, keepdims=True), jnp.max(s_ctx, axis=-1, keepdims=True))
        p_loc = jnp.exp2(s_loc - m)
        p_ctx = jnp.exp2(s_ctx - m)
        denom = jnp.sum(p_loc, axis=-1, keepdims=True) + jnp.sum(p_ctx, axis=-1, keepdims=True)
        return p_loc.astype(BF16), p_ctx.astype(BF16), denom

    def weighted(half, p_loc, p_ctx, denom):
        v = jnp.concatenate([r[...] for r in kv_refs[half][1]], axis=0)
        return (_dot(p_loc, v) + _dot(p_ctx, vc)) / denom

    s, p, o = {}, {}, {}
    for t in range(len(items) + 2):
        if t < len(items):
            s[t] = scores(*items[t])
        if 0 <= t - 1 < len(items):
            p[t - 1] = softmax(*s.pop(t - 1))
        if 0 <= t - 2 < len(items):
            o[t - 2] = weighted(items[t - 2][0], *p.pop(t - 2))
    for half in range(2):
        first = (lane // NA_HEAD_DIM) == 0
        out = jnp.where(first, o[2 * half], o[2 * half + 1])
        o_ref[half * NA_HQ:(half + 1) * NA_HQ, :] = out.astype(o_ref.dtype)


def _na(qkv, tiles):
    hp_n = NA_HEADS // 2
    jn = NA_ROW_BLOCKS
    kb_per_img = SEQ // NA_KBLK
    last_blk = kb_per_img - NA_WBLKS

    def kv_spec(col0, half, t):
        return pl.BlockSpec((NA_KBLK, 128), lambda hp, j, b: (
            b * kb_per_img + jnp.clip(2 * j + half - 1, 0, last_blk) + t, col0 + hp))

    def kv_specs(col0):
        return [kv_spec(col0, half, t) for half in range(2) for t in range(NA_WBLKS)]

    ctx0 = N_LAT // CTX_LEN
    in_specs = ([pl.BlockSpec((NA_TQ, 128), lambda hp, j, b: (b * jn + j, hp))]
                + kv_specs(hp_n) + kv_specs(2 * hp_n)
                + [pl.BlockSpec((CTX_LEN, 128), lambda hp, j, b: (ctx0 + b, hp_n + hp)),
                   pl.BlockSpec((CTX_LEN, 128), lambda hp, j, b: (ctx0 + b, 2 * hp_n + hp)),
                   pl.BlockSpec((2, 2 * NA_KH, GRID_W, 2 * GRID_W), lambda hp, j, b: (hp, 0, 0, 0))])
    return pl.pallas_call(
        _na_kernel,
        grid=(hp_n, jn, BATCH),
        in_specs=in_specs,
        out_specs=pl.BlockSpec((NA_TQ, 128), lambda hp, j, b: (b * jn + j, hp)),
        out_shape=jax.ShapeDtypeStruct((N_LAT, D_MODEL), BF16),
        scratch_shapes=[pltpu.VMEM((len(NA_CASES) * 4, NA_HQ, NA_WK), F32)],
        compiler_params=_cparams(("arbitrary", "arbitrary", "arbitrary"), 40),
        name="neighbourhood_attention",
    )(qkv, *([qkv] * (4 * NA_WBLKS)), qkv, qkv, tiles)


def _na_bias_tiles(rpb):
    w = GRID_W
    col = np.arange(w)
    cs = np.clip(col - NA_KW // 2, 0, w - NA_KW)
    col_mask = (col[None, :] >= cs[:, None]) & (col[None, :] < cs[:, None] + NA_KW)
    dc_idx = np.clip(col[None, :] - col[:, None] + NA_KW - 1, 0, 2 * NA_KW - 2)
    t1 = (rpb.astype(F32) * LOG2_E)[:, :, dc_idx]
    t1 = jnp.where(col_mask[None, None], t1, NEG_BIG)
    t1 = jnp.concatenate([t1, jnp.full((NA_HEADS, 1, w, w), NEG_BIG, F32)], axis=1)
    return jnp.concatenate([t1, t1], axis=-1)


MR_TM = 1024
MR_TN = 1024


def _mmres_kernel(a_ref, w_ref, x_ref, gt_ref, o_ref):
    o_ref[...] = x_ref[...] + gt_ref[0] * _dot(a_ref[...], w_ref[...])


def _mm_res(a, w, xs, mod, n_rows):
    tm, tn = MR_TM, MR_TN
    return pl.pallas_call(
        _mmres_kernel,
        grid=(n_rows // tm, D_MODEL // tn),
        in_specs=[pl.BlockSpec((tm, D_MODEL), lambda i, j: (i, 0)),
                  pl.BlockSpec((D_MODEL, tn), lambda i, j: (0, j)),
                  pl.BlockSpec((tm, tn), lambda i, j: (i, j)),
                  _mod_spec(5, tm, tn, col=True)],
        out_specs=pl.BlockSpec((tm, tn), lambda i, j: (i, j)),
        out_shape=jax.ShapeDtypeStruct((n_rows, D_MODEL), F32),
        compiler_params=_cparams(("parallel", "arbitrary"), 40),
        name="matmul_gated_residual",
    )(a, w, xs, mod)


def _rope_tables():
    half = GLA_DK // 4
    inv_freq = ROPE_BASE ** (-jnp.arange(half, dtype=F32) / half)
    t = jnp.arange(SEQ)
    ang_r = (t // GRID_W).astype(F32)[:, None] * inv_freq[None, :]
    ang_c = (t % GRID_W).astype(F32)[:, None] * inv_freq[None, :]
    cr, sr, cc, sc = jnp.cos(ang_r), jnp.sin(ang_r), jnp.cos(ang_c), jnp.sin(ang_c)
    cos_t = jnp.concatenate([cr, cc, cr, cc], axis=-1)
    sin_t = jnp.concatenate([-sr, -sc, sr, sc], axis=-1)
    cos_t = jnp.concatenate([cos_t, jnp.ones((GLA_BLOCK, GLA_DK), F32)], axis=0)
    sin_t = jnp.concatenate([sin_t, jnp.zeros((GLA_BLOCK, GLA_DK), F32)], axis=0)
    return cos_t, sin_t


def _pair_layout(w):
    lead = w.shape[:-1]
    quarter = GLA_DK // 4
    w = w.reshape(*lead, GLA_HEADS, 2, 2, quarter)
    return jnp.swapaxes(w, -2, -3).reshape(*lead, GLA_QK_W)


def kernel(x, c, ctx, c_ctx, l0_ada_w, l0_ada_b, l0_norm_ffn1, l0_norm_mix, l0_norm_ffn2, l0_ffn1_w_gu, l0_ffn1_w_down, l0_ffn2_w_gu, l0_ffn2_w_down, l0_w_in, l0_gla_gate_w_fwd, l0_gla_gate_b_fwd, l0_gla_gate_w_bwd, l0_gla_gate_b_bwd, l0_gla_norm, l0_w_out, l1_ada_w, l1_ada_b, l1_norm_ffn1, l1_norm_mix, l1_norm_ffn2, l1_ffn1_w_gu, l1_ffn1_w_down, l1_ffn2_w_gu, l1_ffn2_w_down, l1_w_qkv, l1_rpb, l1_w_out, norm_out):
    bf = lambda w: w.astype(BF16)
    cc = jnp.concatenate([c, c_ctx[None, :], jnp.zeros((8 - BATCH - 1, D_MODEL), F32)], axis=0)

    mod0 = _ada(cc, l0_ada_w, l0_ada_b)
    w_gu, w_down = bf(l0_ffn1_w_gu), bf(l0_ffn1_w_down)
    xs = _ffn(x.reshape(N_LAT, D_MODEL), mod0, 0, l0_norm_ffn1, w_gu, w_down, out_rows=N_TOK)
    xs = _ffn(ctx.reshape(N_CTX, D_MODEL), mod0, 0, l0_norm_ffn1, w_gu, w_down, out_rows=N_TOK,
              tile0=N_LAT // FFN_TM, prev=xs)

    a0 = 2 * GLA_QK_W + GLA_V_W
    a1 = a0 + 2 * GLA_LOWRANK
    w_main = bf(jnp.concatenate([_pair_layout(l0_w_in[:, :GLA_QK_W]), _pair_layout(l0_w_in[:, GLA_QK_W:2 * GLA_QK_W]),
                                 l0_w_in[:, 2 * GLA_QK_W:a0], l0_w_in[:, a1:]], axis=1))
    w_lr = bf(jnp.pad(l0_w_in[:, a0:a1], ((0, 0), (0, GLA_LR_PAD - 2 * GLA_LOWRANK))))
    proj, lr = _nmm(xs, mod0, 1, l0_norm_mix, w_main, F32, w_narrow=w_lr)

    cos_t, sin_t = _rope_tables()
    gwf = jnp.pad(_pair_layout(l0_gla_gate_w_fwd), ((0, GLA_LR_PAD - GLA_LOWRANK), (0, 0)))
    gwb = jnp.pad(_pair_layout(l0_gla_gate_w_bwd), ((GLA_LOWRANK, GLA_LR_PAD - 2 * GLA_LOWRANK), (0, 0)))
    of, ob = _gla(proj, lr, cos_t, sin_t, gwf, _pair_layout(l0_gla_gate_b_fwd).reshape(1, -1),
                  gwb, _pair_layout(l0_gla_gate_b_bwd).reshape(1, -1))

    cc_np, sc_np = _dft_tables(FNET_GROUP_DIM)
    wc = jnp.asarray(np.concatenate([cc_np, sc_np], axis=1), F32).astype(BF16)
    zc, zs = _chan_dft(proj, wc)
    cos_big, sin_big = _big_dft_tables()
    fm = _time_dft(cos_big, sin_big, zc, zs, SEQ, 512, 512, 0)
    ct_np, st_np = _dft_tables(CTX_LEN)
    fm = _time_dft(jnp.asarray(ct_np, F32).astype(BF16), jnp.asarray(st_np, F32).astype(BF16),
                   zc, zs, CTX_LEN, CTX_LEN, 512, N_LAT, prev=fm)

    xs = _glaout(of, ob, proj, fm, l0_gla_norm, xs, mod0, bf(l0_w_out))
    xs = _ffn(xs, mod0, 2, l0_norm_ffn2, bf(l0_ffn2_w_gu), bf(l0_ffn2_w_down), out_rows=N_TOK)

    mod1 = _ada(cc, l1_ada_w, l1_ada_b)
    xs = _ffn(xs, mod1, 0, l1_norm_ffn1, bf(l1_ffn1_w_gu), bf(l1_ffn1_w_down), out_rows=N_TOK)
    w_qkv = bf(jnp.concatenate([l1_w_qkv[:, :D_MODEL] * NA_Q_SCALE, l1_w_qkv[:, D_MODEL:]], axis=1))
    qkv = _nmm(xs, mod1, 1, l1_norm_mix, w_qkv, BF16)
    ao = _na(qkv, _na_bias_tiles(l1_rpb))
    xl = _mm_res(ao, bf(l1_w_out), xs, mod1, N_LAT)
    out = _ffn(xl, mod1, 2, l1_norm_ffn2, bf(l1_ffn2_w_gu), bf(l1_ffn2_w_down), out_rows=N_LAT, final_g=norm_out)
    return out.reshape(BATCH, SEQ, D_MODEL)
```

```python
import functools

import numpy as np
import jax
import jax.numpy as jnp
from jax import lax
from jax.experimental import pallas as pl
from jax.experimental.pallas import tpu as pltpu

D_MODEL = 2048
BATCH = 4
SEQ = 4096
GRID_W = 64
CTX_LEN = 256
N_MOD = 9
D_FF = 5632
RMS_EPS = 1e-6
ROPE_BASE = 10000.0

GLA_HEADS = 4
GLA_DK = 128
GLA_DV = 256
GLA_LOWRANK = 16
GLA_TAU = 16.0
GLA_CHUNK = 64
FNET_GROUPS = 4
FNET_GROUP_DIM = 256
GLA_QK_W = GLA_HEADS * GLA_DK
GLA_V_W = GLA_HEADS * GLA_DV
FNET_W = FNET_GROUPS * FNET_GROUP_DIM

NA_HEADS = 32
NA_HEAD_DIM = 64
NA_KH = 8
NA_KW = 16
NA_QROWS = 8
NEG_BIG = -1e30
LOG2_E = 1.4426950408889634
NA_Q_SCALE = NA_HEAD_DIM ** -0.5 * LOG2_E

N_LAT = BATCH * SEQ
N_CTX = BATCH * CTX_LEN
N_TOK = N_LAT + N_CTX

F32 = jnp.float32
BF16 = jnp.bfloat16
MIB = 1024 * 1024


def _cparams(sem, vmem_mib):
    return pltpu.CompilerParams(dimension_semantics=sem, vmem_limit_bytes=vmem_mib * MIB)


def _dot(a, b):
    return jnp.dot(a, b, preferred_element_type=F32)


def _dot_nt(a, b):
    return lax.dot_general(a, b, (((1,), (1,)), ((), ())), preferred_element_type=F32)


def _dot_tn(a, b):
    return lax.dot_general(a, b, (((0,), (0,)), ((), ())), preferred_element_type=F32)


def _split2(x):
    hi = x.astype(BF16)
    lo = (x - hi.astype(F32)).astype(BF16)
    return hi, lo


def _dot3(a, b):
    ah, al = _split2(a)
    bh, bl = _split2(b)
    return _dot(ah, bh) + (_dot(al, bh) + _dot(ah, bl))


def _dot_exact_lhs(l_bf16, x):
    x1 = x.astype(BF16)
    r1 = x - x1.astype(F32)
    x2 = r1.astype(BF16)
    x3 = (r1 - x2.astype(F32)).astype(BF16)
    return _dot(l_bf16, x1) + (_dot(l_bf16, x2) + _dot(l_bf16, x3))


def _sigmoid(x):
    return 1.0 / (1.0 + jnp.exp(-x))


def _rms(x, g):
    ms = jnp.mean(x * x, axis=-1, keepdims=True)
    return x * lax.rsqrt(ms + RMS_EPS) * g


ROW_CHUNK = 16
ROW_GROUP = 8


def _row_groups(n_rows, body, group=ROW_GROUP):
    span = ROW_CHUNK * group

    def step(t, carry):
        base = t * span
        body([pl.ds(pl.multiple_of(base + c * ROW_CHUNK, ROW_CHUNK), ROW_CHUNK) for c in range(group)])
        return carry
    lax.fori_loop(0, n_rows // span, step, 0)


def _inv_rms(x):
    return lax.rsqrt(jnp.mean(x * x, axis=-1, keepdims=True) + RMS_EPS)


def _norm_mod_rows(x_ref, g_ref, sh_ref, sc_ref, h_ref, gs_ref, zero_ref=None):
    d = x_ref.shape[1]
    gs_ref[0] = jnp.broadcast_to(g_ref[...] * (1.0 + sc_ref[0]), (ROW_CHUNK, d))
    gs_ref[1] = jnp.broadcast_to(sh_ref[0], (ROW_CHUNK, d))

    def body(chunks):
        inv = [_inv_rms(x_ref[rows, :]) for rows in chunks]
        for rows, r in zip(chunks, inv):
            h_ref[rows, :] = (x_ref[rows, :] * r * gs_ref[0] + gs_ref[1]).astype(BF16)
            if zero_ref is not None:
                zero_ref[rows, :] = jnp.zeros((ROW_CHUNK, zero_ref.shape[1]), zero_ref.dtype)
    _row_groups(x_ref.shape[0], body)


def _mod_row(i, tm):
    return jnp.minimum(i // (SEQ // tm), BATCH)


def _mod_spec(j, tm, width=D_MODEL, col=None, tile0=0):
    if col is None:
        return pl.BlockSpec((1, 1, width), lambda i, *_: (_mod_row(i + tile0, tm) * N_MOD + j, 0, 0))
    return pl.BlockSpec((1, 1, width), lambda i, n, *_: (_mod_row(i + tile0, tm) * N_MOD + j, 0, n))


CAST_ROW_ALIGN = 16


def _cast_plan(rows, steps):
    per = pl.cdiv(pl.cdiv(rows, steps), CAST_ROW_ALIGN) * CAST_ROW_ALIGN
    while rows % per:
        per += CAST_ROW_ALIGN
    return per, rows // per


def _cast_specs(w, steps, flat_index):
    per, used = _cast_plan(w.shape[0], steps)
    spec = pl.BlockSpec((per, w.shape[1]), lambda *g: (jnp.minimum(flat_index(*g), used - 1), 0))
    return spec, jax.ShapeDtypeStruct(w.shape, BF16), used


def _hosting_casts(kernel_fn, n_in, n_out, used, flat_index, grid_rank):
    nc = len(used)

    def wrapped(*refs):
        a, b, c = n_in + nc, n_in + nc + n_out, n_in + nc + n_out + nc
        kernel_fn(*refs[:n_in], *refs[a:b], *refs[c:])
        t = flat_index(*[pl.program_id(d) for d in range(grid_rank)])
        for n_used, src_ref, dst_ref in zip(used, refs[n_in:a], refs[b:c]):
            @pl.when(t < n_used)
            def _(src_ref=src_ref, dst_ref=dst_ref):
                dst_ref[...] = src_ref[...].astype(BF16)
    return wrapped


ADA_TN = 1024


def _ada_kernel(c_ref, w_ref, b_ref, o_ref):
    c = c_ref[...]
    s = (c * _sigmoid(c)).astype(BF16)
    o_ref[...] = _dot(s, w_ref[...].astype(BF16)) + b_ref[...]


def _ada(cc, w, b):
    n = w.shape[1]
    out = pl.pallas_call(
        _ada_kernel,
        grid=(n // ADA_TN,),
        in_specs=[pl.BlockSpec((8, D_MODEL), lambda j: (0, 0)),
                  pl.BlockSpec((D_MODEL, ADA_TN), lambda j: (0, j)),
                  pl.BlockSpec((1, ADA_TN), lambda j: (0, j))],
        out_specs=pl.BlockSpec((8, ADA_TN), lambda j: (0, j)),
        out_shape=jax.ShapeDtypeStruct((8, n), F32),
        compiler_params=_cparams(("arbitrary",), 40),
        name="ada_mod",
    )(cc, w, b.reshape(1, n))
    return out.reshape(8 * N_MOD, 1, D_MODEL)


FFN_TM = 1024
FFN_TF = 512


def _ffn_kernel(x_ref, g_ref, sh_ref, sc_ref, gt_ref, wg_ref, wu_ref, wd_ref, *rest, final_norm):
    go_ref = rest[0] if final_norm else None
    o_ref, h_ref, gs_ref = rest[-3:]
    f = pl.program_id(1)

    @pl.when(f == 0)
    def _():
        _norm_mod_rows(x_ref, g_ref, sh_ref, sc_ref, h_ref, gs_ref, zero_ref=o_ref)

    h = h_ref[...]
    a = _dot(h, wg_ref[...])
    u = _dot(h, wu_ref[...])
    act = (a * _sigmoid(a) * u).astype(BF16)
    o_ref[...] += _dot(act, wd_ref[...])

    @pl.when(f == pl.num_programs(1) - 1)
    def _():
        if not final_norm:
            o_ref[...] = x_ref[...] + (0.5 * gt_ref[0]) * o_ref[...]
        else:
            d = x_ref.shape[1]
            gs_ref[0] = jnp.broadcast_to(0.5 * gt_ref[0], (ROW_CHUNK, d))
            gs_ref[1] = jnp.broadcast_to(go_ref[...], (ROW_CHUNK, d))

            def body(chunks):
                resid = lambda rows: x_ref[rows, :] + gs_ref[0] * o_ref[rows, :]
                inv = [_inv_rms(resid(rows)) for rows in chunks]
                for rows, r in zip(chunks, inv):
                    o_ref[rows, :] = resid(rows) * r * gs_ref[1]
            _row_groups(x_ref.shape[0], body)


def _ffn(x_in, mod, sub, norm_g, w_gu, w_down, *, out_rows, tile0=0, prev=None, final_g=None):
    tm, tf = FFN_TM, FFN_TF
    nf = D_FF // tf
    in_specs = [pl.BlockSpec((tm, D_MODEL), lambda i, f: (i, 0)),
                pl.BlockSpec((1, D_MODEL), lambda i, f: (0, 0)),
                _mod_spec(3 * sub, tm, tile0=tile0), _mod_spec(3 * sub + 1, tm, tile0=tile0),
                _mod_spec(3 * sub + 2, tm, tile0=tile0),
                pl.BlockSpec((D_MODEL, tf), lambda i, f: (0, f)),
                pl.BlockSpec((D_MODEL, tf), lambda i, f: (0, f + nf)),
                pl.BlockSpec((tf, D_MODEL), lambda i, f: (f, 0))]
    args = [x_in, norm_g.reshape(1, D_MODEL), mod, mod, mod, w_gu, w_gu, w_down]
    if final_g is not None:
        in_specs.append(pl.BlockSpec((1, D_MODEL), lambda i, f: (0, 0)))
        args.append(final_g.reshape(1, D_MODEL))
    aliases = {}
    if prev is not None:
        aliases = {len(args): 0}
        in_specs.append(pl.BlockSpec(memory_space=pl.ANY))
        args.append(prev)
    return pl.pallas_call(
        functools.partial(_ffn_kernel, final_norm=final_g is not None),
        grid=(x_in.shape[0] // tm, nf),
        in_specs=in_specs,
        out_specs=pl.BlockSpec((tm, D_MODEL), lambda i, f: (i + tile0, 0)),
        out_shape=jax.ShapeDtypeStruct((out_rows, D_MODEL), F32),
        scratch_shapes=[pltpu.VMEM((tm, D_MODEL), BF16), pltpu.VMEM((2, ROW_CHUNK, D_MODEL), F32)],
        input_output_aliases=aliases,
        compiler_params=_cparams(("parallel", "arbitrary"), 60),
        name="ffn_half",
    )(*args)


NMM_TM = 1024
NMM_TN = 1024


def _nmm_kernel(x_ref, g_ref, sh_ref, sc_ref, w_ref, *rest, gla):
    if gla:
        wn_ref, o_ref, ob_ref, on_ref, h_ref, gs_ref = rest
    else:
        o_ref, h_ref, gs_ref = rest
    j = pl.program_id(1)

    @pl.when(j == 0)
    def _():
        _norm_mod_rows(x_ref, g_ref, sh_ref, sc_ref, h_ref, gs_ref)
        if gla:
            on_ref[...] = _dot(h_ref[...], wn_ref[...])

    if not gla:
        o_ref[...] = _dot(h_ref[...], w_ref[...]).astype(o_ref.dtype)
    else:
        @pl.when(j % 2 == 0)
        def _():
            o_ref[...] = _dot(h_ref[...], w_ref[...])

        @pl.when(j % 2 == 1)
        def _():
            ob_ref[...] = _dot(h_ref[...], w_ref[...]).astype(BF16)


def _nmm(xs, mod, sub, norm_g, w, w_narrow=None, cast_ws=()):
    tm, tn = NMM_TM, NMM_TN
    n = w.shape[1]
    gla = w_narrow is not None
    in_specs = [pl.BlockSpec((tm, D_MODEL), lambda i, j: (i, 0)),
                pl.BlockSpec((1, D_MODEL), lambda i, j: (0, 0)),
                _mod_spec(3 * sub, tm), _mod_spec(3 * sub + 1, tm),
                pl.BlockSpec((D_MODEL, tn), lambda i, j: (0, j))]
    args = [xs, norm_g.reshape(1, D_MODEL), mod, mod, w]
    if gla:
        nn = w_narrow.shape[1]
        in_specs.append(pl.BlockSpec((D_MODEL, nn), lambda i, j: (0, 0)))
        args.append(w_narrow)
        out_specs = [pl.BlockSpec((tm, tn), lambda i, j: (i, j // 2)),
                     pl.BlockSpec((tm, tn), lambda i, j: (i, j // 2)),
                     pl.BlockSpec((tm, nn), lambda i, j: (i, 0))]
        out_shape = [jax.ShapeDtypeStruct((N_TOK, n // 2), F32), jax.ShapeDtypeStruct((N_TOK, n // 2), BF16),
                     jax.ShapeDtypeStruct((N_TOK, nn), F32)]
    else:
        out_specs = [pl.BlockSpec((tm, tn), lambda i, j: (i, j))]
        out_shape = [jax.ShapeDtypeStruct((N_TOK, n), BF16)]
    nj = n // tn
    flat = lambda i, j: i * nj + j
    casts = [_cast_specs(cw, (N_TOK // tm) * nj, flat) for cw in cast_ws]
    return pl.pallas_call(
        _hosting_casts(functools.partial(_nmm_kernel, gla=gla), len(args), len(out_specs),
                       [u for _, _, u in casts], flat, 2),
        grid=(N_TOK // tm, nj),
        in_specs=in_specs + [s for s, _, _ in casts],
        out_specs=out_specs + [s for s, _, _ in casts],
        out_shape=out_shape + [o for _, o, _ in casts],
        scratch_shapes=[pltpu.VMEM((tm, D_MODEL), BF16), pltpu.VMEM((2, ROW_CHUNK, D_MODEL), F32)],
        compiler_params=_cparams(("arbitrary", "arbitrary"), 56),
        name="norm_mod_matmul",
    )(*args, *cast_ws)


GLA_BLOCK = CTX_LEN
GLA_LAT_BLOCKS = SEQ // GLA_BLOCK
GLA_STEPS = 1 + GLA_LAT_BLOCKS
GLA_LR_PAD = 128


def _rope(x, cos, sin_signed):
    return x * cos + pltpu.roll(x, GLA_DK // 2, 1) * sin_signed


def _gla_kernel(qf, kf, vf, lf, cf, sf, qb, kb, vb, lb, cb, sb, gwf, gbf, gwb, gbb,
                of_ref, ob_ref, st_ref):
    @pl.when(pl.program_id(1) == 0)
    def _():
        st_ref[...] = jnp.zeros_like(st_ref)

    c = GLA_CHUNK
    n = GLA_BLOCK
    row = lax.broadcasted_iota(jnp.int32, (n, n), 0)
    col = lax.broadcasted_iota(jnp.int32, (n, n), 1)
    same_chunk = (row // c) == (col // c)
    row_c = lax.broadcasted_iota(jnp.int32, (c, c), 0)
    col_c = lax.broadcasted_iota(jnp.int32, (c, c), 1)
    scale = GLA_DK ** -0.5

    dirs = ((qf, kf, vf, lf, cf, sf, gwf, gbf, of_ref), (qb, kb, vb, lb, cb, sb, gwb, gbb, ob_ref))
    for d, (q_ref, k_ref, v_ref, l_ref, c_ref, s_ref, gw_ref, gb_ref, o_ref) in enumerate(dirs):
        causal = (row >= col) if d == 0 else (row <= col)
        tri_b = jnp.where(same_chunk & causal, 1.0, 0.0).astype(BF16)
        tri = (row_c >= col_c) if d == 0 else (row_c <= col_c)
        x = _dot3(l_ref[...], gw_ref[...]) + gb_ref[...]
        g = (jnp.minimum(x, 0.0) - jnp.log1p(jnp.exp(-jnp.abs(x)))) * (1.0 / GLA_TAU)
        bc_all = _dot_exact_lhs(tri_b, g)
        cos = c_ref[...]
        sin = s_ref[...]
        chunk_order = range(n // c) if d == 0 else range(n // c - 1, -1, -1)
        for ci in chunk_order:
            rows = slice(ci * c, (ci + 1) * c)
            bc = bc_all[rows, :]
            mid = bc[c // 2:c // 2 + 1, :]
            last = bc[c - 1:c, :] if d == 0 else bc[0:1, :]
            e_q = jnp.exp(bc)
            e_qm = jnp.exp(bc - mid)
            e_km = jnp.exp(mid - bc)
            e_kl = jnp.exp(last - bc)
            e_l = jnp.exp(last)
            for h in range(GLA_HEADS):
                sk = slice(h * GLA_DK, (h + 1) * GLA_DK)
                sv = slice(h * GLA_DV, (h + 1) * GLA_DV)
                qh = _rope(q_ref[rows, sk] * scale, cos[rows, :], sin[rows, :])
                kh = _rope(k_ref[rows, sk], cos[rows, :], sin[rows, :])
                vh = v_ref[rows, sv].astype(BF16)
                att = _dot_nt((qh * e_qm[:, sk]).astype(BF16), (kh * e_km[:, sk]).astype(BF16))
                att = jnp.where(tri, att, 0.0)
                st = st_ref[d, h]
                o = _dot_nt((qh * e_q[:, sk]).astype(BF16), st.astype(BF16))
                o = o + _dot(att.astype(BF16), vh)
                o_ref[rows, sv] = o
                kd = (kh * e_kl[:, sk]).astype(BF16)
                st_ref[d, h] = st * e_l[:, sk] + _dot_tn(vh, kd)


def _gla(pf, pb, lr, cos_t, sin_t, gwf, gbf, gwb, gbb, cast_ws):
    c = GLA_BLOCK
    ctx0 = N_LAT // c

    def fwd_blk(b, s):
        return jnp.where(s == 0, ctx0 + b, GLA_LAT_BLOCKS * b + s - 1)

    def bwd_blk(b, s):
        return jnp.where(s == 0, ctx0 + b, GLA_LAT_BLOCKS * b + (GLA_STEPS - 1 - s))

    def fwd_rope(b, s):
        return jnp.where(s == 0, GLA_LAT_BLOCKS, s - 1)

    def bwd_rope(b, s):
        return jnp.where(s == 0, GLA_LAT_BLOCKS, GLA_STEPS - 1 - s)

    def dir_specs(blk, rope):
        return [pl.BlockSpec((c, GLA_QK_W), lambda b, s: (blk(b, s), 0)),
                pl.BlockSpec((c, GLA_QK_W), lambda b, s: (blk(b, s), 1)),
                pl.BlockSpec((c, GLA_V_W), lambda b, s: (blk(b, s), 0)),
                pl.BlockSpec((c, GLA_LR_PAD), lambda b, s: (blk(b, s), 0)),
                pl.BlockSpec((c, GLA_DK), lambda b, s: (rope(b, s), 0)),
                pl.BlockSpec((c, GLA_DK), lambda b, s: (rope(b, s), 0))]

    const = lambda shape: pl.BlockSpec(shape, lambda b, s: (0, 0))
    in_specs = (dir_specs(fwd_blk, fwd_rope) + dir_specs(bwd_blk, bwd_rope)
                + [const((GLA_LR_PAD, GLA_QK_W)), const((1, GLA_QK_W)),
                   const((GLA_LR_PAD, GLA_QK_W)), const((1, GLA_QK_W))])
    flat = lambda b, s: b * GLA_STEPS + s
    casts = [_cast_specs(w, BATCH * GLA_STEPS, flat) for w in cast_ws]
    args = [pf, pf, pb, lr, cos_t, sin_t, pf, pf, pb, lr, cos_t, sin_t, gwf, gbf, gwb, gbb]
    return pl.pallas_call(
        _hosting_casts(_gla_kernel, len(args), 2, [u for _, _, u in casts], flat, 2),
        grid=(BATCH, GLA_STEPS),
        in_specs=in_specs + [s for s, _, _ in casts],
        out_specs=[pl.BlockSpec((c, GLA_V_W), lambda b, s: (fwd_blk(b, s), 0)),
                   pl.BlockSpec((c, GLA_V_W), lambda b, s: (bwd_blk(b, s), 0))] + [s for s, _, _ in casts],
        out_shape=[jax.ShapeDtypeStruct((N_TOK, GLA_V_W), F32)] * 2 + [o for _, o, _ in casts],
        scratch_shapes=[pltpu.VMEM((2, GLA_HEADS, GLA_DV, GLA_DK), F32)],
        compiler_params=_cparams(("arbitrary", "arbitrary"), 40),
        name="gla_scan",
    )(*args, *cast_ws)


CDFT_TM = 1024
PROJ_F_COL0 = GLA_V_W


def _cdft_kernel(x_ref, w_ref, zc_ref, zs_ref):
    z = _dot(x_ref[...].astype(BF16), w_ref[...])
    zc_ref[...] = z[:, :FNET_GROUP_DIM].astype(BF16)
    zs_ref[...] = z[:, FNET_GROUP_DIM:].astype(BF16)


def _chan_dft(proj, wc):
    tm, gd = CDFT_TM, FNET_GROUP_DIM
    col0 = PROJ_F_COL0 // gd
    return pl.pallas_call(
        _cdft_kernel,
        grid=(N_TOK // tm, FNET_GROUPS),
        in_specs=[pl.BlockSpec((tm, gd), lambda i, g: (i, col0 + g)),
                  pl.BlockSpec((gd, 2 * gd), lambda i, g: (0, 0))],
        out_specs=[pl.BlockSpec((tm, gd), lambda i, g: (i, g))] * 2,
        out_shape=[jax.ShapeDtypeStruct((N_TOK, FNET_W), BF16)] * 2,
        compiler_params=_cparams(("parallel", "parallel"), 32),
        name="fnet_channel_dft",
    )(proj, wc)


def _tdft_kernel(c_ref, s_ref, zc_ref, zs_ref, *rest):
    o_ref = rest[-1]
    o_ref[...] = _dot(c_ref[...], zc_ref[...]) - _dot(s_ref[...], zs_ref[...])


def _time_dft(cmat, smat, zc, zs, t_len, tm, tn, row0, prev=None, cast_ws=()):
    zb0 = row0 // t_len
    ob0 = row0 // tm
    mt = t_len // tm
    in_specs = [pl.BlockSpec((tm, t_len), lambda b, n, m: (m, 0)),
                pl.BlockSpec((tm, t_len), lambda b, n, m: (m, 0)),
                pl.BlockSpec((t_len, tn), lambda b, n, m: (zb0 + b, n)),
                pl.BlockSpec((t_len, tn), lambda b, n, m: (zb0 + b, n))]
    args = [cmat, smat, zc, zs]
    aliases = {}
    if prev is not None:
        in_specs.append(pl.BlockSpec(memory_space=pl.ANY))
        args.append(prev)
        aliases = {4: 0}
    nt = FNET_W // tn
    flat = lambda b, n, m: (b * nt + n) * mt + m
    casts = [_cast_specs(w, BATCH * nt * mt, flat) for w in cast_ws]
    return pl.pallas_call(
        _hosting_casts(_tdft_kernel, len(args), 1, [u for _, _, u in casts], flat, 3),
        grid=(BATCH, nt, mt),
        in_specs=in_specs + [s for s, _, _ in casts],
        out_specs=[pl.BlockSpec((tm, tn), lambda b, n, m: (ob0 + b * mt + m, n))] + [s for s, _, _ in casts],
        out_shape=[jax.ShapeDtypeStruct((N_TOK, FNET_W), F32)] + [o for _, o, _ in casts],
        input_output_aliases=aliases,
        compiler_params=_cparams(("arbitrary", "arbitrary", "arbitrary"), 48),
        name="fnet_time_dft",
    )(*args, *cast_ws)


def _dft_tables(n):
    k = np.arange(n)
    ang = 2.0 * np.pi * ((k[:, None] * k[None, :]) % n) / n
    return np.cos(ang) / np.sqrt(n), np.sin(ang) / np.sqrt(n)


def _dft_table_kernel(ac_ref, as_ref, bc_ref, bs_ref, co_ref, so_ref):
    ac, as_ = ac_ref[0], as_ref[0]
    bc, bs = bc_ref[...], bs_ref[...]
    co_ref[...] = (ac * bc - as_ * bs).astype(BF16)
    so_ref[...] = (ac * bs + as_ * bc).astype(BF16)


def _big_dft_tables():
    r = GRID_W
    k = np.arange(SEQ)
    j = np.arange(r)
    pa = 2.0 * np.pi * ((j[:, None] * r * k[None, :]) % SEQ) / SEQ
    pb = 2.0 * np.pi * ((j[:, None] * k[None, :]) % SEQ) / SEQ
    ac = jnp.asarray(np.cos(pa).reshape(r, 1, SEQ), F32)
    as_ = jnp.asarray(np.sin(pa).reshape(r, 1, SEQ), F32)
    bc = jnp.asarray(np.cos(pb) * SEQ ** -0.5, F32)
    bs = jnp.asarray(np.sin(pb) * SEQ ** -0.5, F32)
    row = pl.BlockSpec((1, 1, SEQ), lambda i: (i, 0, 0))
    full = pl.BlockSpec((r, SEQ), lambda i: (0, 0))
    return pl.pallas_call(
        _dft_table_kernel,
        grid=(r,),
        in_specs=[row, row, full, full],
        out_specs=[pl.BlockSpec((r, SEQ), lambda i: (i, 0))] * 2,
        out_shape=[jax.ShapeDtypeStruct((SEQ, SEQ), BF16)] * 2,
        compiler_params=_cparams(("parallel",), 16),
        name="dft_tables",
    )(ac, as_, bc, bs)


GO_TM = 512
GO_TN = 2048


def _glaout_kernel(of_ref, ob_ref, r_ref, fm_ref, gn_ref, x_ref, gt_ref, w_ref, o_ref, h_ref):
    @pl.when(pl.program_id(1) == 0)
    def _():
        heads = [slice(h * GLA_DV, (h + 1) * GLA_DV) for h in range(GLA_HEADS)]

        def body(chunks):
            inv = [[_inv_rms(of_ref[rows, sv] + ob_ref[rows, sv]) for sv in heads] for rows in chunks]
            for rows, inv_c in zip(chunks, inv):
                for sv, r in zip(heads, inv_c):
                    o = (of_ref[rows, sv] + ob_ref[rows, sv]) * r
                    gate = r_ref[rows, sv]
                    h_ref[rows, sv] = (o * gn_ref[:, sv] * (gate * _sigmoid(gate))).astype(BF16)
                h_ref[rows, GLA_V_W:] = fm_ref[rows, :].astype(BF16)
        _row_groups(of_ref.shape[0], body, group=4)

    o_ref[...] = x_ref[...] + gt_ref[0] * _dot(h_ref[...], w_ref[...])


def _glaout(of, ob, pf, fm, g_norm, xs, mod, w_out):
    tm, tn = GO_TM, GO_TN
    half = lambda col: pl.BlockSpec((tm, GLA_V_W), lambda i, j: (i, col))
    return pl.pallas_call(
        _glaout_kernel,
        grid=(N_TOK // tm, D_MODEL // tn),
        in_specs=[half(0), half(0), half(1), half(0),
                  pl.BlockSpec((1, GLA_V_W), lambda i, j: (0, 0)),
                  pl.BlockSpec((tm, tn), lambda i, j: (i, j)),
                  _mod_spec(5, tm, tn, col=True),
                  pl.BlockSpec((D_MODEL, tn), lambda i, j: (0, j))],
        out_specs=pl.BlockSpec((tm, tn), lambda i, j: (i, j)),
        out_shape=jax.ShapeDtypeStruct((N_TOK, D_MODEL), F32),
        scratch_shapes=[pltpu.VMEM((tm, D_MODEL), BF16)],
        compiler_params=_cparams(("parallel", "arbitrary"), 56),
        name="gla_fnet_out",
    )(of, ob, pf, fm, g_norm.reshape(1, GLA_V_W), xs, mod, w_out)


NA_HROWS = NA_QROWS // 2
NA_WROWS = NA_HROWS + NA_KH
NA_TQ = NA_QROWS * GRID_W
NA_HQ = NA_HROWS * GRID_W
NA_WK = NA_WROWS * GRID_W
NA_KBLK = NA_HROWS * GRID_W
NA_WBLKS = NA_WK // NA_KBLK
NA_MASKED = 2 * NA_KH - 1
NA_ROW_BLOCKS = (SEQ // GRID_W) // NA_QROWS
NA_CASES = (0, 1, NA_ROW_BLOCKS - 1)


def _na_window_row0(j, half):
    q0 = NA_QROWS * j + NA_HROWS * half
    return int(np.clip(q0 - NA_KH // 2, 0, SEQ // GRID_W - NA_WROWS))


def _na_bias_slots():
    rows = SEQ // GRID_W
    slots = np.full((len(NA_CASES), 2, NA_HROWS, NA_WROWS), NA_MASKED, np.int32)
    for case, j in enumerate(NA_CASES):
        for half in range(2):
            w0 = _na_window_row0(j, half)
            for i in range(NA_HROWS):
                qr = NA_QROWS * j + NA_HROWS * half + i
                r0 = int(np.clip(qr - NA_KH // 2, 0, rows - NA_KH))
                for l in range(NA_WROWS):
                    kr = w0 + l
                    if r0 <= kr < r0 + NA_KH:
                        slots[case, half, i, l] = kr - qr + NA_KH - 1
    return slots


def _na_kernel(q_ref, *refs):
    nb = NA_WBLKS
    ka, kb, va, vb = refs[0:nb], refs[nb:2 * nb], refs[2 * nb:3 * nb], refs[3 * nb:4 * nb]
    kc_ref, vc_ref, td_ref, o_ref, bias_ref = refs[4 * nb:]
    j = pl.program_id(1)

    @pl.when((j == 0) & (pl.program_id(2) == 0))
    def _():
        slots = _na_bias_slots()
        left = lax.broadcasted_iota(jnp.int32, (GRID_W, 2 * GRID_W), 1) < GRID_W
        for case in range(len(NA_CASES)):
            for half in range(2):
                for hh in range(2):
                    dst = (case * 2 + half) * 2 + hh
                    for i in range(NA_HROWS):
                        for p in range(NA_WROWS // 2):
                            sl = int(slots[case, half, i, 2 * p])
                            sr = int(slots[case, half, i, 2 * p + 1])
                            tile = td_ref[hh, sl] if sl == sr else jnp.where(left, td_ref[hh, sl], td_ref[hh, sr])
                            bias_ref[dst, i * GRID_W:(i + 1) * GRID_W, p * 2 * GRID_W:(p + 1) * 2 * GRID_W] = tile

    case = jnp.where(j == 0, 0, jnp.where(j == NA_ROW_BLOCKS - 1, 2, 1))
    kc = kc_ref[...]
    vc = vc_ref[...]
    lane = lax.broadcasted_iota(jnp.int32, (NA_HQ, 2 * NA_HEAD_DIM), 1)
    kv_refs = ((ka, va), (kb, vb))
    items = [(half, hh) for half in range(2) for hh in range(2)]

    def scores(half, hh):
        q = q_ref[half * NA_HQ:(half + 1) * NA_HQ, :]
        qm = jnp.where((lane // NA_HEAD_DIM) == hh, q, jnp.zeros_like(q))
        k = jnp.concatenate([r[...] for r in kv_refs[half][0]], axis=0)
        return _dot_nt(qm, k) + bias_ref[(case * 2 + half) * 2 + hh], _dot_nt(qm, kc)

    def softmax(s_loc, s_ctx):
        m = jnp.maximum(jnp.max(s_loc, axis=-1, keepdims=True), jnp.max(s_ctx, axis=-1, keepdims=True))
        p_loc = jnp.exp2(s_loc - m)
        p_ctx = jnp.exp2(s_ctx - m)
        denom = jnp.sum(p_loc, axis=-1, keepdims=True) + jnp.sum(p_ctx, axis=-1, keepdims=True)
        return p_loc.astype(BF16), p_ctx.astype(BF16), denom

    def weighted(half, p_loc, p_ctx, denom):
        v = jnp.concatenate([r[...] for r in kv_refs[half][1]], axis=0)
        return (_dot(p_loc, v) + _dot(p_ctx, vc)) / denom

    s, p, o = {}, {}, {}
    for t in range(len(items) + 2):
        if t < len(items):
            s[t] = scores(*items[t])
        if 0 <= t - 1 < len(items):
            p[t - 1] = softmax(*s.pop(t - 1))
        if 0 <= t - 2 < len(items):
            o[t - 2] = weighted(items[t - 2][0], *p.pop(t - 2))
    for half in range(2):
        first = (lane // NA_HEAD_DIM) == 0
        out = jnp.where(first, o[2 * half], o[2 * half + 1])
        o_ref[half * NA_HQ:(half + 1) * NA_HQ, :] = out.astype(o_ref.dtype)


def _na(qkv, tiles, cast_ws=()):
    hp_n = NA_HEADS // 2
    jn = NA_ROW_BLOCKS
    kb_per_img = SEQ // NA_KBLK
    last_blk = kb_per_img - NA_WBLKS

    def kv_spec(col0, half, t):
        return pl.BlockSpec((NA_KBLK, 128), lambda hp, j, b: (
            b * kb_per_img + jnp.clip(2 * j + half - 1, 0, last_blk) + t, col0 + hp))

    def kv_specs(col0):
        return [kv_spec(col0, half, t) for half in range(2) for t in range(NA_WBLKS)]

    ctx0 = N_LAT // CTX_LEN
    in_specs = ([pl.BlockSpec((NA_TQ, 128), lambda hp, j, b: (b * jn + j, hp))]
                + kv_specs(hp_n) + kv_specs(2 * hp_n)
                + [pl.BlockSpec((CTX_LEN, 128), lambda hp, j, b: (ctx0 + b, hp_n + hp)),
                   pl.BlockSpec((CTX_LEN, 128), lambda hp, j, b: (ctx0 + b, 2 * hp_n + hp)),
                   pl.BlockSpec((2, 2 * NA_KH, GRID_W, 2 * GRID_W), lambda hp, j, b: (hp, 0, 0, 0))])
    flat = lambda hp, j, b: (hp * jn + j) * BATCH + b
    casts = [_cast_specs(w, hp_n * jn * BATCH, flat) for w in cast_ws]
    return pl.pallas_call(
        _hosting_casts(_na_kernel, len(in_specs), 1, [u for _, _, u in casts], flat, 3),
        grid=(hp_n, jn, BATCH),
        in_specs=in_specs + [s for s, _, _ in casts],
        out_specs=[pl.BlockSpec((NA_TQ, 128), lambda hp, j, b: (b * jn + j, hp))] + [s for s, _, _ in casts],
        out_shape=[jax.ShapeDtypeStruct((N_LAT, D_MODEL), BF16)] + [o for _, o, _ in casts],
        scratch_shapes=[pltpu.VMEM((len(NA_CASES) * 4, NA_HQ, NA_WK), F32)],
        compiler_params=_cparams(("arbitrary", "arbitrary", "arbitrary"), 48),
        name="neighbourhood_attention",
    )(qkv, *([qkv] * (4 * NA_WBLKS)), qkv, qkv, tiles, *cast_ws)


def _na_bias_tiles(rpb):
    w = GRID_W
    col = np.arange(w)
    cs = np.clip(col - NA_KW // 2, 0, w - NA_KW)
    col_mask = (col[None, :] >= cs[:, None]) & (col[None, :] < cs[:, None] + NA_KW)
    dc_idx = np.clip(col[None, :] - col[:, None] + NA_KW - 1, 0, 2 * NA_KW - 2)
    t1 = (rpb.astype(F32) * LOG2_E)[:, :, dc_idx]
    t1 = jnp.where(col_mask[None, None], t1, NEG_BIG)
    t1 = jnp.concatenate([t1, jnp.full((NA_HEADS, 1, w, w), NEG_BIG, F32)], axis=1)
    return jnp.concatenate([t1, t1], axis=-1)


MR_TM = 1024
MR_TN = 1024


def _mmres_kernel(a_ref, w_ref, x_ref, gt_ref, o_ref):
    o_ref[...] = x_ref[...] + gt_ref[0] * _dot(a_ref[...], w_ref[...])


def _mm_res(a, w, xs, mod, n_rows):
    tm, tn = MR_TM, MR_TN
    return pl.pallas_call(
        _mmres_kernel,
        grid=(n_rows // tm, D_MODEL // tn),
        in_specs=[pl.BlockSpec((tm, D_MODEL), lambda i, j: (i, 0)),
                  pl.BlockSpec((D_MODEL, tn), lambda i, j: (0, j)),
                  pl.BlockSpec((tm, tn), lambda i, j: (i, j)),
                  _mod_spec(5, tm, tn, col=True)],
        out_specs=pl.BlockSpec((tm, tn), lambda i, j: (i, j)),
        out_shape=jax.ShapeDtypeStruct((n_rows, D_MODEL), F32),
        compiler_params=_cparams(("parallel", "arbitrary"), 40),
        name="matmul_gated_residual",
    )(a, w, xs, mod)


def _rope_tables():
    half = GLA_DK // 4
    inv_freq = ROPE_BASE ** (-jnp.arange(half, dtype=F32) / half)
    t = jnp.arange(SEQ)
    ang_r = (t // GRID_W).astype(F32)[:, None] * inv_freq[None, :]
    ang_c = (t % GRID_W).astype(F32)[:, None] * inv_freq[None, :]
    cr, sr, cc, sc = jnp.cos(ang_r), jnp.sin(ang_r), jnp.cos(ang_c), jnp.sin(ang_c)
    cos_t = jnp.concatenate([cr, cc, cr, cc], axis=-1)
    sin_t = jnp.concatenate([-sr, -sc, sr, sc], axis=-1)
    cos_t = jnp.concatenate([cos_t, jnp.ones((GLA_BLOCK, GLA_DK), F32)], axis=0)
    sin_t = jnp.concatenate([sin_t, jnp.zeros((GLA_BLOCK, GLA_DK), F32)], axis=0)
    return cos_t, sin_t


def _pair_layout(w):
    lead = w.shape[:-1]
    quarter = GLA_DK // 4
    w = w.reshape(*lead, GLA_HEADS, 2, 2, quarter)
    return jnp.swapaxes(w, -2, -3).reshape(*lead, GLA_QK_W)


def kernel(x, c, ctx, c_ctx, l0_ada_w, l0_ada_b, l0_norm_ffn1, l0_norm_mix, l0_norm_ffn2, l0_ffn1_w_gu, l0_ffn1_w_down, l0_ffn2_w_gu, l0_ffn2_w_down, l0_w_in, l0_gla_gate_w_fwd, l0_gla_gate_b_fwd, l0_gla_gate_w_bwd, l0_gla_gate_b_bwd, l0_gla_norm, l0_w_out, l1_ada_w, l1_ada_b, l1_norm_ffn1, l1_norm_mix, l1_norm_ffn2, l1_ffn1_w_gu, l1_ffn1_w_down, l1_ffn2_w_gu, l1_ffn2_w_down, l1_w_qkv, l1_rpb, l1_w_out, norm_out):
    bf = lambda w: w.astype(BF16)
    cc = jnp.concatenate([c, c_ctx[None, :], jnp.zeros((8 - BATCH - 1, D_MODEL), F32)], axis=0)

    mod0 = _ada(cc, l0_ada_w, l0_ada_b)
    w_gu, w_down = bf(l0_ffn1_w_gu), bf(l0_ffn1_w_down)
    xs = _ffn(x.reshape(N_LAT, D_MODEL), mod0, 0, l0_norm_ffn1, w_gu, w_down, out_rows=N_TOK)
    xs = _ffn(ctx.reshape(N_CTX, D_MODEL), mod0, 0, l0_norm_ffn1, w_gu, w_down, out_rows=N_TOK,
              tile0=N_LAT // FFN_TM, prev=xs)

    a0 = 2 * GLA_QK_W + GLA_V_W
    a1 = a0 + 2 * GLA_LOWRANK
    w_main = bf(jnp.concatenate([_pair_layout(l0_w_in[:, :GLA_QK_W]), _pair_layout(l0_w_in[:, GLA_QK_W:2 * GLA_QK_W]),
                                 l0_w_in[:, 2 * GLA_QK_W:a0], l0_w_in[:, a1:]], axis=1))
    w_lr = bf(jnp.pad(l0_w_in[:, a0:a1], ((0, 0), (0, GLA_LR_PAD - 2 * GLA_LOWRANK))))
    pf, pb, lr = _nmm(xs, mod0, 1, l0_norm_mix, w_main, w_narrow=w_lr)

    cos_t, sin_t = _rope_tables()
    gwf = jnp.pad(_pair_layout(l0_gla_gate_w_fwd), ((0, GLA_LR_PAD - GLA_LOWRANK), (0, 0)))
    gwb = jnp.pad(_pair_layout(l0_gla_gate_w_bwd), ((GLA_LOWRANK, GLA_LR_PAD - 2 * GLA_LOWRANK), (0, 0)))
    of, ob, w_gu2, w_down2 = _gla(pf, pb, lr, cos_t, sin_t, gwf, _pair_layout(l0_gla_gate_b_fwd).reshape(1, -1),
                                  gwb, _pair_layout(l0_gla_gate_b_bwd).reshape(1, -1),
                                  cast_ws=(l0_ffn2_w_gu, l0_ffn2_w_down))

    cc_np, sc_np = _dft_tables(FNET_GROUP_DIM)
    wc = jnp.asarray(np.concatenate([cc_np, sc_np], axis=1), F32).astype(BF16)
    zc, zs = _chan_dft(pb, wc)
    cos_big, sin_big = _big_dft_tables()
    fm, w_gu3, w_down3 = _time_dft(cos_big, sin_big, zc, zs, SEQ, 512, 512, 0, cast_ws=(l1_ffn1_w_gu, l1_ffn1_w_down))
    ct_np, st_np = _dft_tables(CTX_LEN)
    fm = _time_dft(jnp.asarray(ct_np, F32).astype(BF16), jnp.asarray(st_np, F32).astype(BF16),
                   zc, zs, CTX_LEN, CTX_LEN, 512, N_LAT, prev=fm)[0]

    xs = _glaout(of, ob, pf, fm, l0_gla_norm, xs, mod0, bf(l0_w_out))
    xs = _ffn(xs, mod0, 2, l0_norm_ffn2, w_gu2, w_down2, out_rows=N_TOK)

    mod1 = _ada(cc, l1_ada_w, l1_ada_b)
    xs = _ffn(xs, mod1, 0, l1_norm_ffn1, w_gu3, w_down3, out_rows=N_TOK)
    w_qkv = bf(jnp.concatenate([l1_w_qkv[:, :D_MODEL] * NA_Q_SCALE, l1_w_qkv[:, D_MODEL:]], axis=1))
    qkv, w_gu4, w_down4 = _nmm(xs, mod1, 1, l1_norm_mix, w_qkv, cast_ws=(l1_ffn2_w_gu, l1_ffn2_w_down))
    ao = _na(qkv, _na_bias_tiles(l1_rpb))[0]
    xl = _mm_res(ao, bf(l1_w_out), xs, mod1, N_LAT)
    out = _ffn(xl, mod1, 2, l1_norm_ffn2, w_gu4, w_down4, out_rows=N_LAT, final_g=norm_out)
    return out.reshape(BATCH, SEQ, D_MODEL)
```

```python
import functools

import numpy as np
import jax
import jax.numpy as jnp
from jax import lax
from jax.experimental import pallas as pl
from jax.experimental.pallas import tpu as pltpu

D_MODEL = 2048
BATCH = 4
SEQ = 4096
GRID_W = 64
CTX_LEN = 256
N_MOD = 9
D_FF = 5632
RMS_EPS = 1e-6
ROPE_BASE = 10000.0

GLA_HEADS = 4
GLA_DK = 128
GLA_DV = 256
GLA_LOWRANK = 16
GLA_TAU = 16.0
GLA_CHUNK = 64
FNET_GROUPS = 4
FNET_GROUP_DIM = 256
GLA_QK_W = GLA_HEADS * GLA_DK
GLA_V_W = GLA_HEADS * GLA_DV
FNET_W = FNET_GROUPS * FNET_GROUP_DIM

NA_HEADS = 32
NA_HEAD_DIM = 64
NA_KH = 8
NA_KW = 16
NA_QROWS = 8
NEG_BIG = -1e30
LOG2_E = 1.4426950408889634
NA_Q_SCALE = NA_HEAD_DIM ** -0.5 * LOG2_E

N_LAT = BATCH * SEQ
N_CTX = BATCH * CTX_LEN
N_TOK = N_LAT + N_CTX

F32 = jnp.float32
BF16 = jnp.bfloat16
MIB = 1024 * 1024


def _cparams(sem, vmem_mib):
    return pltpu.CompilerParams(dimension_semantics=sem, vmem_limit_bytes=vmem_mib * MIB)


def _dot(a, b):
    return jnp.dot(a, b, preferred_element_type=F32)


def _dot_nt(a, b):
    return lax.dot_general(a, b, (((1,), (1,)), ((), ())), preferred_element_type=F32)


def _dot_tn(a, b):
    return lax.dot_general(a, b, (((0,), (0,)), ((), ())), preferred_element_type=F32)


def _split2(x):
    hi = x.astype(BF16)
    lo = (x - hi.astype(F32)).astype(BF16)
    return hi, lo


def _dot3(a, b):
    ah, al = _split2(a)
    bh, bl = _split2(b)
    return _dot(ah, bh) + (_dot(al, bh) + _dot(ah, bl))


def _dot_exact_lhs(l_bf16, x):
    x1 = x.astype(BF16)
    r1 = x - x1.astype(F32)
    x2 = r1.astype(BF16)
    x3 = (r1 - x2.astype(F32)).astype(BF16)
    return _dot(l_bf16, x1) + (_dot(l_bf16, x2) + _dot(l_bf16, x3))


def _sigmoid(x):
    return 1.0 / (1.0 + jnp.exp(-x))


def _rms(x, g):
    ms = jnp.mean(x * x, axis=-1, keepdims=True)
    return x * lax.rsqrt(ms + RMS_EPS) * g


ROW_CHUNK = 16
ROW_GROUP = 8


def _row_groups(n_rows, body, group=ROW_GROUP):
    span = ROW_CHUNK * group

    def step(t, carry):
        base = t * span
        body([pl.ds(pl.multiple_of(base + c * ROW_CHUNK, ROW_CHUNK), ROW_CHUNK) for c in range(group)])
        return carry
    lax.fori_loop(0, n_rows // span, step, 0)


def _inv_rms(x):
    return lax.rsqrt(jnp.mean(x * x, axis=-1, keepdims=True) + RMS_EPS)


def _norm_mod_rows(x_ref, g_ref, sh_ref, sc_ref, h_ref, gs_ref, zero_ref=None):
    d = x_ref.shape[1]
    gs_ref[0] = jnp.broadcast_to(g_ref[...] * (1.0 + sc_ref[0]), (ROW_CHUNK, d))
    gs_ref[1] = jnp.broadcast_to(sh_ref[0], (ROW_CHUNK, d))

    def body(chunks):
        inv = [_inv_rms(x_ref[rows, :]) for rows in chunks]
        for rows, r in zip(chunks, inv):
            h_ref[rows, :] = (x_ref[rows, :] * r * gs_ref[0] + gs_ref[1]).astype(BF16)
            if zero_ref is not None:
                zero_ref[rows, :] = jnp.zeros((ROW_CHUNK, zero_ref.shape[1]), zero_ref.dtype)
    _row_groups(x_ref.shape[0], body)


def _mod_row(i, tm):
    return jnp.minimum(i // (SEQ // tm), BATCH)


def _mod_spec(j, tm, width=D_MODEL, col=None, tile0=0):
    if col is None:
        return pl.BlockSpec((1, 1, width), lambda i, *_: (_mod_row(i + tile0, tm) * N_MOD + j, 0, 0))
    return pl.BlockSpec((1, 1, width), lambda i, n, *_: (_mod_row(i + tile0, tm) * N_MOD + j, 0, n))


CAST_ROW_ALIGN = 16


def _cast_plan(rows, steps):
    per = pl.cdiv(pl.cdiv(rows, steps), CAST_ROW_ALIGN) * CAST_ROW_ALIGN
    while rows % per:
        per += CAST_ROW_ALIGN
    return per, rows // per


def _cast_specs(w, steps, flat_index):
    per, used = _cast_plan(w.shape[0], steps)
    spec = pl.BlockSpec((per, w.shape[1]), lambda *g: (jnp.minimum(flat_index(*g), used - 1), 0))
    return spec, jax.ShapeDtypeStruct(w.shape, BF16), used


def _hosting_casts(kernel_fn, n_in, n_out, used, flat_index, grid_rank):
    nc = len(used)

    def wrapped(*refs):
        a, b, c = n_in + nc, n_in + nc + n_out, n_in + nc + n_out + nc
        kernel_fn(*refs[:n_in], *refs[a:b], *refs[c:])
        t = flat_index(*[pl.program_id(d) for d in range(grid_rank)])
        for n_used, src_ref, dst_ref in zip(used, refs[n_in:a], refs[b:c]):
            @pl.when(t < n_used)
            def _(src_ref=src_ref, dst_ref=dst_ref):
                dst_ref[...] = src_ref[...].astype(BF16)
    return wrapped


ADA_TN = 1024


def _ada_kernel(c_ref, w_ref, b_ref, o_ref):
    c = c_ref[...]
    s = (c * _sigmoid(c)).astype(BF16)
    o_ref[...] = _dot(s, w_ref[...].astype(BF16)) + b_ref[...]


def _ada(cc, w, b):
    n = w.shape[1]
    out = pl.pallas_call(
        _ada_kernel,
        grid=(n // ADA_TN,),
        in_specs=[pl.BlockSpec((8, D_MODEL), lambda j: (0, 0)),
                  pl.BlockSpec((D_MODEL, ADA_TN), lambda j: (0, j)),
                  pl.BlockSpec((1, ADA_TN), lambda j: (0, j))],
        out_specs=pl.BlockSpec((8, ADA_TN), lambda j: (0, j)),
        out_shape=jax.ShapeDtypeStruct((8, n), F32),
        compiler_params=_cparams(("arbitrary",), 40),
        name="ada_mod",
    )(cc, w, b.reshape(1, n))
    return out.reshape(8 * N_MOD, 1, D_MODEL)


FFN_TM = 1024
FFN_TF = 512
FFN_SLAB = 128


def _ffn_kernel(x_hbm, xn_ref, g_ref, sh_ref, sc_ref, shn_ref, scn_ref, gt_ref, wg_ref, wu_ref, wd_ref, *rest,
                final_norm):
    go_ref = rest[0] if final_norm else None
    o_ref, h0_ref, h1_ref, gs_ref, x_ref, x_sem = rest[-6:]
    i = pl.program_id(0)
    f = pl.program_id(1)
    last = pl.num_programs(1) - 1
    tm, d = x_ref.shape

    def x_copy():
        return pltpu.make_async_copy(x_hbm.at[pl.ds(pl.multiple_of(i * tm, tm), tm), :], x_ref, x_sem)

    @pl.when(f == 0)
    def _():
        x_copy().start()

    @pl.when((i == 0) & (f == 0))
    def _():
        x_copy().wait()
        _norm_mod_rows(x_ref, g_ref, sh_ref, sc_ref, h0_ref, gs_ref)

    def step(h_ref, hn_ref, first):
        gs_ref[0] = jnp.broadcast_to(g_ref[...] * (1.0 + scn_ref[0]), (ROW_CHUNK, d))
        gs_ref[1] = jnp.broadcast_to(shn_ref[0], (ROW_CHUNK, d))
        slab0 = jnp.minimum(f, tm // FFN_SLAB - 1) * FFN_SLAB
        chunks = [pl.ds(k * ROW_CHUNK, ROW_CHUNK) for k in range(FFN_SLAB // ROW_CHUNK)]
        inv = [_inv_rms(xn_ref[rows, :]) for rows in chunks]
        for k, (rows, r) in enumerate(zip(chunks, inv)):
            dst = pl.ds(pl.multiple_of(slab0 + k * ROW_CHUNK, ROW_CHUNK), ROW_CHUNK)
            hn_ref[dst, :] = (xn_ref[rows, :] * r * gs_ref[0] + gs_ref[1]).astype(BF16)

        h = h_ref[...]
        a = _dot(h, wg_ref[...])
        u = _dot(h, wu_ref[...])
        act = (a * _sigmoid(a) * u).astype(BF16)
        if first:
            o_ref[...] = _dot(act, wd_ref[...])
        else:
            o_ref[...] += _dot(act, wd_ref[...])

    for parity, (h_ref, hn_ref) in enumerate(((h0_ref, h1_ref), (h1_ref, h0_ref))):
        for first in (True, False):
            @pl.when((i % 2 == parity) & ((f == 0) == first))
            def _(h_ref=h_ref, hn_ref=hn_ref, first=first):
                step(h_ref, hn_ref, first)

    @pl.when(f == last)
    def _():
        @pl.when(i > 0)
        def _():
            x_copy().wait()

        if not final_norm:
            o_ref[...] = x_ref[...] + (0.5 * gt_ref[0]) * o_ref[...]
        else:
            gs_ref[2] = jnp.broadcast_to(0.5 * gt_ref[0], (ROW_CHUNK, d))
            gs_ref[3] = jnp.broadcast_to(go_ref[...], (ROW_CHUNK, d))

            def body(chunks):
                resid = lambda rows: x_ref[rows, :] + gs_ref[2] * o_ref[rows, :]
                inv = [_inv_rms(resid(rows)) for rows in chunks]
                for rows, r in zip(chunks, inv):
                    o_ref[rows, :] = resid(rows) * r * gs_ref[3]
            _row_groups(tm, body)


def _ffn(x_in, mod, sub, norm_g, w_gu, w_down, *, out_rows, tile0=0, prev=None, final_g=None):
    tm, tf = FFN_TM, FFN_TF
    nf = D_FF // tf
    n_tiles = x_in.shape[0] // tm
    slabs = tm // FFN_SLAB
    assert slabs <= nf
    nxt = lambda i: jnp.minimum(i + 1, n_tiles - 1)
    in_specs = [pl.BlockSpec(memory_space=pl.ANY),
                pl.BlockSpec((FFN_SLAB, D_MODEL), lambda i, f: (nxt(i) * slabs + jnp.minimum(f, slabs - 1), 0)),
                pl.BlockSpec((1, D_MODEL), lambda i, f: (0, 0)),
                _mod_spec(3 * sub, tm, tile0=tile0), _mod_spec(3 * sub + 1, tm, tile0=tile0),
                pl.BlockSpec((1, 1, D_MODEL), lambda i, f: (_mod_row(nxt(i) + tile0, tm) * N_MOD + 3 * sub, 0, 0)),
                pl.BlockSpec((1, 1, D_MODEL), lambda i, f: (_mod_row(nxt(i) + tile0, tm) * N_MOD + 3 * sub + 1, 0, 0)),
                _mod_spec(3 * sub + 2, tm, tile0=tile0),
                pl.BlockSpec((D_MODEL, tf), lambda i, f: (0, f)),
                pl.BlockSpec((D_MODEL, tf), lambda i, f: (0, f + nf)),
                pl.BlockSpec((tf, D_MODEL), lambda i, f: (f, 0))]
    args = [x_in, x_in, norm_g.reshape(1, D_MODEL), mod, mod, mod, mod, mod, w_gu, w_gu, w_down]
    if final_g is not None:
        in_specs.append(pl.BlockSpec((1, D_MODEL), lambda i, f: (0, 0)))
        args.append(final_g.reshape(1, D_MODEL))
    aliases = {}
    if prev is not None:
        aliases = {len(args): 0}
        in_specs.append(pl.BlockSpec(memory_space=pl.ANY))
        args.append(prev)
    return pl.pallas_call(
        functools.partial(_ffn_kernel, final_norm=final_g is not None),
        grid=(n_tiles, nf),
        in_specs=in_specs,
        out_specs=pl.BlockSpec((tm, D_MODEL), lambda i, f: (i + tile0, 0)),
        out_shape=jax.ShapeDtypeStruct((out_rows, D_MODEL), F32),
        scratch_shapes=[pltpu.VMEM((tm, D_MODEL), BF16), pltpu.VMEM((tm, D_MODEL), BF16),
                        pltpu.VMEM((4, ROW_CHUNK, D_MODEL), F32), pltpu.VMEM((tm, D_MODEL), F32),
                        pltpu.SemaphoreType.DMA(())],
        input_output_aliases=aliases,
        compiler_params=_cparams(("arbitrary", "arbitrary"), 56),
        name="ffn_half",
    )(*args)


NMM_TM = 1024
NMM_TN = 1024


def _nmm_kernel(x_ref, g_ref, sh_ref, sc_ref, w_ref, *rest, gla):
    if gla:
        wn_ref, o_ref, ob_ref, on_ref, h_ref, gs_ref = rest
    else:
        o_ref, h_ref, gs_ref = rest
    j = pl.program_id(1)

    @pl.when(j == 0)
    def _():
        _norm_mod_rows(x_ref, g_ref, sh_ref, sc_ref, h_ref, gs_ref)
        if gla:
            on_ref[...] = _dot(h_ref[...], wn_ref[...])

    if not gla:
        o_ref[...] = _dot(h_ref[...], w_ref[...]).astype(o_ref.dtype)
    else:
        @pl.when(j % 2 == 0)
        def _():
            o_ref[...] = _dot(h_ref[...], w_ref[...])

        @pl.when(j % 2 == 1)
        def _():
            ob_ref[...] = _dot(h_ref[...], w_ref[...]).astype(BF16)


def _nmm(xs, mod, sub, norm_g, w, w_narrow=None, cast_ws=()):
    tm, tn = NMM_TM, NMM_TN
    n = w.shape[1]
    gla = w_narrow is not None
    in_specs = [pl.BlockSpec((tm, D_MODEL), lambda i, j: (i, 0)),
                pl.BlockSpec((1, D_MODEL), lambda i, j: (0, 0)),
                _mod_spec(3 * sub, tm), _mod_spec(3 * sub + 1, tm),
                pl.BlockSpec((D_MODEL, tn), lambda i, j: (0, j))]
    args = [xs, norm_g.reshape(1, D_MODEL), mod, mod, w]
    if gla:
        nn = w_narrow.shape[1]
        in_specs.append(pl.BlockSpec((D_MODEL, nn), lambda i, j: (0, 0)))
        args.append(w_narrow)
        out_specs = [pl.BlockSpec((tm, tn), lambda i, j: (i, j // 2)),
                     pl.BlockSpec((tm, tn), lambda i, j: (i, j // 2)),
                     pl.BlockSpec((tm, nn), lambda i, j: (i, 0))]
        out_shape = [jax.ShapeDtypeStruct((N_TOK, n // 2), F32), jax.ShapeDtypeStruct((N_TOK, n // 2), BF16),
                     jax.ShapeDtypeStruct((N_TOK, nn), F32)]
    else:
        out_specs = [pl.BlockSpec((tm, tn), lambda i, j: (i, j))]
        out_shape = [jax.ShapeDtypeStruct((N_TOK, n), BF16)]
    nj = n // tn
    flat = lambda i, j: i * nj + j
    casts = [_cast_specs(cw, (N_TOK // tm) * nj, flat) for cw in cast_ws]
    return pl.pallas_call(
        _hosting_casts(functools.partial(_nmm_kernel, gla=gla), len(args), len(out_specs),
                       [u for _, _, u in casts], flat, 2),
        grid=(N_TOK // tm, nj),
        in_specs=in_specs + [s for s, _, _ in casts],
        out_specs=out_specs + [s for s, _, _ in casts],
        out_shape=out_shape + [o for _, o, _ in casts],
        scratch_shapes=[pltpu.VMEM((tm, D_MODEL), BF16), pltpu.VMEM((2, ROW_CHUNK, D_MODEL), F32)],
        compiler_params=_cparams(("arbitrary", "arbitrary"), 56),
        name="norm_mod_matmul",
    )(*args, *cast_ws)


GLA_BLOCK = CTX_LEN
GLA_LAT_BLOCKS = SEQ // GLA_BLOCK
GLA_STEPS = 1 + GLA_LAT_BLOCKS
GLA_LR_PAD = 128


def _rope(x, cos, sin_signed):
    return x * cos + pltpu.roll(x, GLA_DK // 2, 1) * sin_signed


def _gla_kernel(qf, kf, vf, lf, cf, sf, qb, kb, vb, lb, cb, sb, gwf, gbf, gwb, gbb,
                of_ref, ob_ref, st_ref):
    @pl.when(pl.program_id(1) == 0)
    def _():
        st_ref[...] = jnp.zeros_like(st_ref)

    c = GLA_CHUNK
    n = GLA_BLOCK
    row = lax.broadcasted_iota(jnp.int32, (n, n), 0)
    col = lax.broadcasted_iota(jnp.int32, (n, n), 1)
    same_chunk = (row // c) == (col // c)
    row_c = lax.broadcasted_iota(jnp.int32, (c, c), 0)
    col_c = lax.broadcasted_iota(jnp.int32, (c, c), 1)
    scale = GLA_DK ** -0.5

    dirs = ((qf, kf, vf, lf, cf, sf, gwf, gbf, of_ref), (qb, kb, vb, lb, cb, sb, gwb, gbb, ob_ref))
    n_chunks = n // c
    heads = [(slice(h * GLA_DK, (h + 1) * GLA_DK), slice(h * GLA_DV, (h + 1) * GLA_DV)) for h in range(GLA_HEADS)]
    tris, decays = [], []
    for d, (q_ref, k_ref, v_ref, l_ref, c_ref, s_ref, gw_ref, gb_ref, o_ref) in enumerate(dirs):
        causal = (row >= col) if d == 0 else (row <= col)
        tri_b = jnp.where(same_chunk & causal, 1.0, 0.0).astype(BF16)
        tris.append((row_c >= col_c) if d == 0 else (row_c <= col_c))
        x = _dot3(l_ref[...], gw_ref[...]) + gb_ref[...]
        g = (jnp.minimum(x, 0.0) - jnp.log1p(jnp.exp(-jnp.abs(x)))) * (1.0 / GLA_TAU)
        bc_all = _dot_exact_lhs(tri_b, g)
        per_chunk = []
        for ci in (range(n_chunks) if d == 0 else range(n_chunks - 1, -1, -1)):
            rows = slice(ci * c, (ci + 1) * c)
            bc = bc_all[rows, :]
            mid = bc[c // 2:c // 2 + 1, :]
            last = bc[c - 1:c, :] if d == 0 else bc[0:1, :]
            per_chunk.append((rows, jnp.exp(bc), jnp.exp(bc - mid), jnp.exp(mid - bc), jnp.exp(last - bc),
                              jnp.exp(last)))
        decays.append(per_chunk)

    def local_part(k):
        out = {}
        for d, (q_ref, k_ref, v_ref, _, c_ref, s_ref, _, _, _) in enumerate(dirs):
            rows, e_q, e_qm, e_km, e_kl, e_l = decays[d][k]
            cos, sin = c_ref[rows, :], s_ref[rows, :]
            for h, (sk, sv) in enumerate(heads):
                qh = _rope(q_ref[rows, sk] * scale, cos, sin)
                kh = _rope(k_ref[rows, sk], cos, sin)
                vh = v_ref[rows, sv].astype(BF16)
                att = _dot_nt((qh * e_qm[:, sk]).astype(BF16), (kh * e_km[:, sk]).astype(BF16))
                kv = _dot_tn(vh, (kh * e_kl[:, sk]).astype(BF16))
                out[d, h] = ((qh * e_q[:, sk]).astype(BF16), jnp.where(tris[d], att, 0.0).astype(BF16), vh, kv,
                             e_l[:, sk])
        return out

    st = {(d, h): st_ref[d, h] for d in range(2) for h in range(GLA_HEADS)}
    ahead = 1
    pending = {k: local_part(k) for k in range(min(ahead, n_chunks))}
    for k in range(n_chunks):
        if k + ahead < n_chunks:
            pending[k + ahead] = local_part(k + ahead)
        local = pending.pop(k)
        for d in range(2):
            rows = decays[d][k][0]
            for h, (sk, sv) in enumerate(heads):
                qe, att, vh, kv, e_l = local[d, h]
                dirs[d][-1][rows, sv] = _dot_nt(qe, st[d, h].astype(BF16)) + _dot(att, vh)
                st[d, h] = st[d, h] * e_l + kv
    for (d, h), val in st.items():
        st_ref[d, h] = val


def _gla(pf, pb, lr, cos_t, sin_t, gwf, gbf, gwb, gbb, cast_ws):
    c = GLA_BLOCK
    ctx0 = N_LAT // c

    def fwd_blk(b, s):
        return jnp.where(s == 0, ctx0 + b, GLA_LAT_BLOCKS * b + s - 1)

    def bwd_blk(b, s):
        return jnp.where(s == 0, ctx0 + b, GLA_LAT_BLOCKS * b + (GLA_STEPS - 1 - s))

    def fwd_rope(b, s):
        return jnp.where(s == 0, GLA_LAT_BLOCKS, s - 1)

    def bwd_rope(b, s):
        return jnp.where(s == 0, GLA_LAT_BLOCKS, GLA_STEPS - 1 - s)

    def dir_specs(blk, rope):
        return [pl.BlockSpec((c, GLA_QK_W), lambda b, s: (blk(b, s), 0)),
                pl.BlockSpec((c, GLA_QK_W), lambda b, s: (blk(b, s), 1)),
                pl.BlockSpec((c, GLA_V_W), lambda b, s: (blk(b, s), 0)),
                pl.BlockSpec((c, GLA_LR_PAD), lambda b, s: (blk(b, s), 0)),
                pl.BlockSpec((c, GLA_DK), lambda b, s: (rope(b, s), 0)),
                pl.BlockSpec((c, GLA_DK), lambda b, s: (rope(b, s), 0))]

    const = lambda shape: pl.BlockSpec(shape, lambda b, s: (0, 0))
    in_specs = (dir_specs(fwd_blk, fwd_rope) + dir_specs(bwd_blk, bwd_rope)
                + [const((GLA_LR_PAD, GLA_QK_W)), const((1, GLA_QK_W)),
                   const((GLA_LR_PAD, GLA_QK_W)), const((1, GLA_QK_W))])
    flat = lambda b, s: b * GLA_STEPS + s
    casts = [_cast_specs(w, BATCH * GLA_STEPS, flat) for w in cast_ws]
    args = [pf, pf, pb, lr, cos_t, sin_t, pf, pf, pb, lr, cos_t, sin_t, gwf, gbf, gwb, gbb]
    return pl.pallas_call(
        _hosting_casts(_gla_kernel, len(args), 2, [u for _, _, u in casts], flat, 2),
        grid=(BATCH, GLA_STEPS),
        in_specs=in_specs + [s for s, _, _ in casts],
        out_specs=[pl.BlockSpec((c, GLA_V_W), lambda b, s: (fwd_blk(b, s), 0)),
                   pl.BlockSpec((c, GLA_V_W), lambda b, s: (bwd_blk(b, s), 0))] + [s for s, _, _ in casts],
        out_shape=[jax.ShapeDtypeStruct((N_TOK, GLA_V_W), F32)] * 2 + [o for _, o, _ in casts],
        scratch_shapes=[pltpu.VMEM((2, GLA_HEADS, GLA_DV, GLA_DK), F32)],
        compiler_params=_cparams(("arbitrary", "arbitrary"), 40),
        name="gla_scan",
    )(*args, *cast_ws)


CDFT_TM = 1024
PROJ_F_COL0 = GLA_V_W


def _cdft_kernel(x_ref, w_ref, zc_ref, zs_ref):
    z = _dot(x_ref[...].astype(BF16), w_ref[...])
    zc_ref[...] = z[:, :FNET_GROUP_DIM].astype(BF16)
    zs_ref[...] = z[:, FNET_GROUP_DIM:].astype(BF16)


def _chan_dft(proj, wc):
    tm, gd = CDFT_TM, FNET_GROUP_DIM
    col0 = PROJ_F_COL0 // gd
    return pl.pallas_call(
        _cdft_kernel,
        grid=(N_TOK // tm, FNET_GROUPS),
        in_specs=[pl.BlockSpec((tm, gd), lambda i, g: (i, col0 + g)),
                  pl.BlockSpec((gd, 2 * gd), lambda i, g: (0, 0))],
        out_specs=[pl.BlockSpec((tm, gd), lambda i, g: (i, g))] * 2,
        out_shape=[jax.ShapeDtypeStruct((N_TOK, FNET_W), BF16)] * 2,
        compiler_params=_cparams(("parallel", "parallel"), 32),
        name="fnet_channel_dft",
    )(proj, wc)


def _tdft_kernel(c_ref, s_ref, zc_ref, zs_ref, *rest):
    o_ref = rest[-1]
    o_ref[...] = _dot(c_ref[...], zc_ref[...]) - _dot(s_ref[...], zs_ref[...])


def _time_dft(cmat, smat, zc, zs, t_len, tm, tn, row0, prev=None, cast_ws=()):
    zb0 = row0 // t_len
    ob0 = row0 // tm
    mt = t_len // tm
    in_specs = [pl.BlockSpec((tm, t_len), lambda b, n, m: (m, 0)),
                pl.BlockSpec((tm, t_len), lambda b, n, m: (m, 0)),
                pl.BlockSpec((t_len, tn), lambda b, n, m: (zb0 + b, n)),
                pl.BlockSpec((t_len, tn), lambda b, n, m: (zb0 + b, n))]
    args = [cmat, smat, zc, zs]
    aliases = {}
    if prev is not None:
        in_specs.append(pl.BlockSpec(memory_space=pl.ANY))
        args.append(prev)
        aliases = {4: 0}
    nt = FNET_W // tn
    flat = lambda b, n, m: (b * nt + n) * mt + m
    casts = [_cast_specs(w, BATCH * nt * mt, flat) for w in cast_ws]
    return pl.pallas_call(
        _hosting_casts(_tdft_kernel, len(args), 1, [u for _, _, u in casts], flat, 3),
        grid=(BATCH, nt, mt),
        in_specs=in_specs + [s for s, _, _ in casts],
        out_specs=[pl.BlockSpec((tm, tn), lambda b, n, m: (ob0 + b * mt + m, n))] + [s for s, _, _ in casts],
        out_shape=[jax.ShapeDtypeStruct((N_TOK, FNET_W), F32)] + [o for _, o, _ in casts],
        input_output_aliases=aliases,
        compiler_params=_cparams(("arbitrary", "arbitrary", "arbitrary"), 48),
        name="fnet_time_dft",
    )(*args, *cast_ws)


def _dft_tables(n):
    k = np.arange(n)
    ang = 2.0 * np.pi * ((k[:, None] * k[None, :]) % n) / n
    return np.cos(ang) / np.sqrt(n), np.sin(ang) / np.sqrt(n)


def _dft_table_kernel(ac_ref, as_ref, bc_ref, bs_ref, co_ref, so_ref):
    ac, as_ = ac_ref[0], as_ref[0]
    bc, bs = bc_ref[...], bs_ref[...]
    co_ref[...] = (ac * bc - as_ * bs).astype(BF16)
    so_ref[...] = (ac * bs + as_ * bc).astype(BF16)


def _big_dft_tables():
    r = GRID_W
    k = np.arange(SEQ)
    j = np.arange(r)
    pa = 2.0 * np.pi * ((j[:, None] * r * k[None, :]) % SEQ) / SEQ
    pb = 2.0 * np.pi * ((j[:, None] * k[None, :]) % SEQ) / SEQ
    ac = jnp.asarray(np.cos(pa).reshape(r, 1, SEQ), F32)
    as_ = jnp.asarray(np.sin(pa).reshape(r, 1, SEQ), F32)
    bc = jnp.asarray(np.cos(pb) * SEQ ** -0.5, F32)
    bs = jnp.asarray(np.sin(pb) * SEQ ** -0.5, F32)
    row = pl.BlockSpec((1, 1, SEQ), lambda i: (i, 0, 0))
    full = pl.BlockSpec((r, SEQ), lambda i: (0, 0))
    return pl.pallas_call(
        _dft_table_kernel,
        grid=(r,),
        in_specs=[row, row, full, full],
        out_specs=[pl.BlockSpec((r, SEQ), lambda i: (i, 0))] * 2,
        out_shape=[jax.ShapeDtypeStruct((SEQ, SEQ), BF16)] * 2,
        compiler_params=_cparams(("parallel",), 16),
        name="dft_tables",
    )(ac, as_, bc, bs)


GO_TM = 512
GO_TN = 2048


def _glaout_kernel(of_ref, ob_ref, r_ref, fm_ref, gn_ref, x_ref, gt_ref, w_ref, o_ref, h_ref):
    @pl.when(pl.program_id(1) == 0)
    def _():
        heads = [slice(h * GLA_DV, (h + 1) * GLA_DV) for h in range(GLA_HEADS)]

        def body(chunks):
            inv = [[_inv_rms(of_ref[rows, sv] + ob_ref[rows, sv]) for sv in heads] for rows in chunks]
            for rows, inv_c in zip(chunks, inv):
                for sv, r in zip(heads, inv_c):
                    o = (of_ref[rows, sv] + ob_ref[rows, sv]) * r
                    gate = r_ref[rows, sv]
                    h_ref[rows, sv] = (o * gn_ref[:, sv] * (gate * _sigmoid(gate))).astype(BF16)
                h_ref[rows, GLA_V_W:] = fm_ref[rows, :].astype(BF16)
        _row_groups(of_ref.shape[0], body, group=4)

    o_ref[...] = x_ref[...] + gt_ref[0] * _dot(h_ref[...], w_ref[...])


def _glaout(of, ob, pf, fm, g_norm, xs, mod, w_out):
    tm, tn = GO_TM, GO_TN
    half = lambda col: pl.BlockSpec((tm, GLA_V_W), lambda i, j: (i, col))
    return pl.pallas_call(
        _glaout_kernel,
        grid=(N_TOK // tm, D_MODEL // tn),
        in_specs=[half(0), half(0), half(1), half(0),
                  pl.BlockSpec((1, GLA_V_W), lambda i, j: (0, 0)),
                  pl.BlockSpec((tm, tn), lambda i, j: (i, j)),
                  _mod_spec(5, tm, tn, col=True),
                  pl.BlockSpec((D_MODEL, tn), lambda i, j: (0, j))],
        out_specs=pl.BlockSpec((tm, tn), lambda i, j: (i, j)),
        out_shape=jax.ShapeDtypeStruct((N_TOK, D_MODEL), F32),
        scratch_shapes=[pltpu.VMEM((tm, D_MODEL), BF16)],
        compiler_params=_cparams(("parallel", "arbitrary"), 56),
        name="gla_fnet_out",
    )(of, ob, pf, fm, g_norm.reshape(1, GLA_V_W), xs, mod, w_out)


NA_HROWS = NA_QROWS // 2
NA_WROWS = NA_HROWS + NA_KH
NA_TQ = NA_QROWS * GRID_W
NA_HQ = NA_HROWS * GRID_W
NA_WK = NA_WROWS * GRID_W
NA_KBLK = NA_HROWS * GRID_W
NA_WBLKS = NA_WK // NA_KBLK
NA_MASKED = 2 * NA_KH - 1
NA_ROW_BLOCKS = (SEQ // GRID_W) // NA_QROWS
NA_CASES = (0, 1, NA_ROW_BLOCKS - 1)


def _na_window_row0(j, half):
    q0 = NA_QROWS * j + NA_HROWS * half
    return int(np.clip(q0 - NA_KH // 2, 0, SEQ // GRID_W - NA_WROWS))


def _na_bias_slots():
    rows = SEQ // GRID_W
    slots = np.full((len(NA_CASES), 2, NA_HROWS, NA_WROWS), NA_MASKED, np.int32)
    for case, j in enumerate(NA_CASES):
        for half in range(2):
            w0 = _na_window_row0(j, half)
            for i in range(NA_HROWS):
                qr = NA_QROWS * j + NA_HROWS * half + i
                r0 = int(np.clip(qr - NA_KH // 2, 0, rows - NA_KH))
                for l in range(NA_WROWS):
                    kr = w0 + l
                    if r0 <= kr < r0 + NA_KH:
                        slots[case, half, i, l] = kr - qr + NA_KH - 1
    return slots


def _na_kernel(q_ref, *refs):
    nb = NA_WBLKS
    ka, kb, va, vb = refs[0:nb], refs[nb:2 * nb], refs[2 * nb:3 * nb], refs[3 * nb:4 * nb]
    kc_ref, vc_ref, td_ref, o_ref, bias_ref = refs[4 * nb:]
    j = pl.program_id(1)

    @pl.when((j == 0) & (pl.program_id(2) == 0))
    def _():
        slots = _na_bias_slots()
        left = lax.broadcasted_iota(jnp.int32, (GRID_W, 2 * GRID_W), 1) < GRID_W
        for case in range(len(NA_CASES)):
            for half in range(2):
                for hh in range(2):
                    dst = (case * 2 + half) * 2 + hh
                    for i in range(NA_HROWS):
                        for p in range(NA_WROWS // 2):
                            sl = int(slots[case, half, i, 2 * p])
                            sr = int(slots[case, half, i, 2 * p + 1])
                            tile = td_ref[hh, sl] if sl == sr else jnp.where(left, td_ref[hh, sl], td_ref[hh, sr])
                            bias_ref[dst, i * GRID_W:(i + 1) * GRID_W, p * 2 * GRID_W:(p + 1) * 2 * GRID_W] = tile

    case = jnp.where(j == 0, 0, jnp.where(j == NA_ROW_BLOCKS - 1, 2, 1))
    kc = kc_ref[...]
    vc = vc_ref[...]
    lane = lax.broadcasted_iota(jnp.int32, (NA_HQ, 2 * NA_HEAD_DIM), 1)
    kv_refs = ((ka, va), (kb, vb))
    items = [(half, hh) for half in range(2) for hh in range(2)]

    def scores(half, hh):
        q = q_ref[half * NA_HQ:(half + 1) * NA_HQ, :]
        qm = jnp.where((lane // NA_HEAD_DIM) == hh, q, jnp.zeros_like(q))
        k = jnp.concatenate([r[...] for r in kv_refs[half][0]], axis=0)
        return _dot_nt(qm, k) + bias_ref[(case * 2 + half) * 2 + hh], _dot_nt(qm, kc)

    def softmax(s_loc, s_ctx):
        m = jnp.maximum(jnp.max(s_loc, axis=-1, keepdims=True), jnp.max(s_ctx, axis=-1, keepdims=True))
        p_loc = jnp.exp2(s_loc - m)
        p_ctx = jnp.exp2(s_ctx - m)
        denom = jnp.sum(p_loc, axis=-1, keepdims=True) + jnp.sum(p_ctx, axis=-1, keepdims=True)
        return p_loc.astype(BF16), p_ctx.astype(BF16), denom

    def weighted(half, p_loc, p_ctx, denom):
        v = jnp.concatenate([r[...] for r in kv_refs[half][1]], axis=0)
        return (_dot(p_loc, v) + _dot(p_ctx, vc)) / denom

    s, p, o = {}, {}, {}
    for t in range(len(items) + 2):
        if t < len(items):
            s[t] = scores(*items[t])
        if 0 <= t - 1 < len(items):
            p[t - 1] = softmax(*s.pop(t - 1))
        if 0 <= t - 2 < len(items):
            o[t - 2] = weighted(items[t - 2][0], *p.pop(t - 2))
    for half in range(2):
        first = (lane // NA_HEAD_DIM) == 0
        out = jnp.where(first, o[2 * half], o[2 * half + 1])
        o_ref[half * NA_HQ:(half + 1) * NA_HQ, :] = out.astype(o_ref.dtype)


def _na(qkv, tiles, cast_ws=()):
    hp_n = NA_HEADS // 2
    jn = NA_ROW_BLOCKS
    kb_per_img = SEQ // NA_KBLK
    last_blk = kb_per_img - NA_WBLKS

    def kv_spec(col0, half, t):
        return pl.BlockSpec((NA_KBLK, 128), lambda hp, j, b: (
            b * kb_per_img + jnp.clip(2 * j + half - 1, 0, last_blk) + t, col0 + hp))

    def kv_specs(col0):
        return [kv_spec(col0, half, t) for half in range(2) for t in range(NA_WBLKS)]

    ctx0 = N_LAT // CTX_LEN
    in_specs = ([pl.BlockSpec((NA_TQ, 128), lambda hp, j, b: (b * jn + j, hp))]
                + kv_specs(hp_n) + kv_specs(2 * hp_n)
                + [pl.BlockSpec((CTX_LEN, 128), lambda hp, j, b: (ctx0 + b, hp_n + hp)),
                   pl.BlockSpec((CTX_LEN, 128), lambda hp, j, b: (ctx0 + b, 2 * hp_n + hp)),
                   pl.BlockSpec((2, 2 * NA_KH, GRID_W, 2 * GRID_W), lambda hp, j, b: (hp, 0, 0, 0))])
    flat = lambda hp, j, b: (hp * jn + j) * BATCH + b
    casts = [_cast_specs(w, hp_n * jn * BATCH, flat) for w in cast_ws]
    return pl.pallas_call(
        _hosting_casts(_na_kernel, len(in_specs), 1, [u for _, _, u in casts], flat, 3),
        grid=(hp_n, jn, BATCH),
        in_specs=in_specs + [s for s, _, _ in casts],
        out_specs=[pl.BlockSpec((NA_TQ, 128), lambda hp, j, b: (b * jn + j, hp))] + [s for s, _, _ in casts],
        out_shape=[jax.ShapeDtypeStruct((N_LAT, D_MODEL), BF16)] + [o for _, o, _ in casts],
        scratch_shapes=[pltpu.VMEM((len(NA_CASES) * 4, NA_HQ, NA_WK), F32)],
        compiler_params=_cparams(("arbitrary", "arbitrary", "arbitrary"), 48),
        name="neighbourhood_attention",
    )(qkv, *([qkv] * (4 * NA_WBLKS)), qkv, qkv, tiles, *cast_ws)


def _na_bias_tiles(rpb):
    w = GRID_W
    col = np.arange(w)
    cs = np.clip(col - NA_KW // 2, 0, w - NA_KW)
    col_mask = (col[None, :] >= cs[:, None]) & (col[None, :] < cs[:, None] + NA_KW)
    dc_idx = np.clip(col[None, :] - col[:, None] + NA_KW - 1, 0, 2 * NA_KW - 2)
    t1 = (rpb.astype(F32) * LOG2_E)[:, :, dc_idx]
    t1 = jnp.where(col_mask[None, None], t1, NEG_BIG)
    t1 = jnp.concatenate([t1, jnp.full((NA_HEADS, 1, w, w), NEG_BIG, F32)], axis=1)
    return jnp.concatenate([t1, t1], axis=-1)


MR_TM = 1024
MR_TN = 1024


def _mmres_kernel(a_ref, w_ref, x_ref, gt_ref, o_ref):
    o_ref[...] = x_ref[...] + gt_ref[0] * _dot(a_ref[...], w_ref[...])


def _mm_res(a, w, xs, mod, n_rows):
    tm, tn = MR_TM, MR_TN
    return pl.pallas_call(
        _mmres_kernel,
        grid=(n_rows // tm, D_MODEL // tn),
        in_specs=[pl.BlockSpec((tm, D_MODEL), lambda i, j: (i, 0)),
                  pl.BlockSpec((D_MODEL, tn), lambda i, j: (0, j)),
                  pl.BlockSpec((tm, tn), lambda i, j: (i, j)),
                  _mod_spec(5, tm, tn, col=True)],
        out_specs=pl.BlockSpec((tm, tn), lambda i, j: (i, j)),
        out_shape=jax.ShapeDtypeStruct((n_rows, D_MODEL), F32),
        compiler_params=_cparams(("parallel", "arbitrary"), 40),
        name="matmul_gated_residual",
    )(a, w, xs, mod)


def _rope_tables():
    half = GLA_DK // 4
    inv_freq = ROPE_BASE ** (-jnp.arange(half, dtype=F32) / half)
    t = jnp.arange(SEQ)
    ang_r = (t // GRID_W).astype(F32)[:, None] * inv_freq[None, :]
    ang_c = (t % GRID_W).astype(F32)[:, None] * inv_freq[None, :]
    cr, sr, cc, sc = jnp.cos(ang_r), jnp.sin(ang_r), jnp.cos(ang_c), jnp.sin(ang_c)
    cos_t = jnp.concatenate([cr, cc, cr, cc], axis=-1)
    sin_t = jnp.concatenate([-sr, -sc, sr, sc], axis=-1)
    cos_t = jnp.concatenate([cos_t, jnp.ones((GLA_BLOCK, GLA_DK), F32)], axis=0)
    sin_t = jnp.concatenate([sin_t, jnp.zeros((GLA_BLOCK, GLA_DK), F32)], axis=0)
    return cos_t, sin_t


def _pair_layout(w):
    lead = w.shape[:-1]
    quarter = GLA_DK // 4
    w = w.reshape(*lead, GLA_HEADS, 2, 2, quarter)
    return jnp.swapaxes(w, -2, -3).reshape(*lead, GLA_QK_W)


def kernel(x, c, ctx, c_ctx, l0_ada_w, l0_ada_b, l0_norm_ffn1, l0_norm_mix, l0_norm_ffn2, l0_ffn1_w_gu, l0_ffn1_w_down, l0_ffn2_w_gu, l0_ffn2_w_down, l0_w_in, l0_gla_gate_w_fwd, l0_gla_gate_b_fwd, l0_gla_gate_w_bwd, l0_gla_gate_b_bwd, l0_gla_norm, l0_w_out, l1_ada_w, l1_ada_b, l1_norm_ffn1, l1_norm_mix, l1_norm_ffn2, l1_ffn1_w_gu, l1_ffn1_w_down, l1_ffn2_w_gu, l1_ffn2_w_down, l1_w_qkv, l1_rpb, l1_w_out, norm_out):
    bf = lambda w: w.astype(BF16)
    cc = jnp.concatenate([c, c_ctx[None, :], jnp.zeros((8 - BATCH - 1, D_MODEL), F32)], axis=0)

    mod0 = _ada(cc, l0_ada_w, l0_ada_b)
    w_gu, w_down = bf(l0_ffn1_w_gu), bf(l0_ffn1_w_down)
    xs = _ffn(x.reshape(N_LAT, D_MODEL), mod0, 0, l0_norm_ffn1, w_gu, w_down, out_rows=N_TOK)
    xs = _ffn(ctx.reshape(N_CTX, D_MODEL), mod0, 0, l0_norm_ffn1, w_gu, w_down, out_rows=N_TOK,
              tile0=N_LAT // FFN_TM, prev=xs)

    a0 = 2 * GLA_QK_W + GLA_V_W
    a1 = a0 + 2 * GLA_LOWRANK
    w_main = bf(jnp.concatenate([_pair_layout(l0_w_in[:, :GLA_QK_W]), _pair_layout(l0_w_in[:, GLA_QK_W:2 * GLA_QK_W]),
                                 l0_w_in[:, 2 * GLA_QK_W:a0], l0_w_in[:, a1:]], axis=1))
    w_lr = bf(jnp.pad(l0_w_in[:, a0:a1], ((0, 0), (0, GLA_LR_PAD - 2 * GLA_LOWRANK))))
    pf, pb, lr = _nmm(xs, mod0, 1, l0_norm_mix, w_main, w_narrow=w_lr)

    cos_t, sin_t = _rope_tables()
    gwf = jnp.pad(_pair_layout(l0_gla_gate_w_fwd), ((0, GLA_LR_PAD - GLA_LOWRANK), (0, 0)))
    gwb = jnp.pad(_pair_layout(l0_gla_gate_w_bwd), ((GLA_LOWRANK, GLA_LR_PAD - 2 * GLA_LOWRANK), (0, 0)))
    of, ob, w_gu2, w_down2 = _gla(pf, pb, lr, cos_t, sin_t, gwf, _pair_layout(l0_gla_gate_b_fwd).reshape(1, -1),
                                  gwb, _pair_layout(l0_gla_gate_b_bwd).reshape(1, -1),
                                  cast_ws=(l0_ffn2_w_gu, l0_ffn2_w_down))

    cc_np, sc_np = _dft_tables(FNET_GROUP_DIM)
    wc = jnp.asarray(np.concatenate([cc_np, sc_np], axis=1), F32).astype(BF16)
    zc, zs = _chan_dft(pb, wc)
    cos_big, sin_big = _big_dft_tables()
    fm, w_gu3, w_down3 = _time_dft(cos_big, sin_big, zc, zs, SEQ, 512, 512, 0, cast_ws=(l1_ffn1_w_gu, l1_ffn1_w_down))
    ct_np, st_np = _dft_tables(CTX_LEN)
    fm = _time_dft(jnp.asarray(ct_np, F32).astype(BF16), jnp.asarray(st_np, F32).astype(BF16),
                   zc, zs, CTX_LEN, CTX_LEN, 512, N_LAT, prev=fm)[0]

    xs = _glaout(of, ob, pf, fm, l0_gla_norm, xs, mod0, bf(l0_w_out))
    xs = _ffn(xs, mod0, 2, l0_norm_ffn2, w_gu2, w_down2, out_rows=N_TOK)

    mod1 = _ada(cc, l1_ada_w, l1_ada_b)
    xs = _ffn(xs, mod1, 0, l1_norm_ffn1, w_gu3, w_down3, out_rows=N_TOK)
    w_qkv = bf(jnp.concatenate([l1_w_qkv[:, :D_MODEL] * NA_Q_SCALE, l1_w_qkv[:, D_MODEL:]], axis=1))
    qkv, w_gu4, w_down4 = _nmm(xs, mod1, 1, l1_norm_mix, w_qkv, cast_ws=(l1_ffn2_w_gu, l1_ffn2_w_down))
    ao = _na(qkv, _na_bias_tiles(l1_rpb))[0]
    xl = _mm_res(ao, bf(l1_w_out), xs, mod1, N_LAT)
    out = _ffn(xl, mod1, 2, l1_norm_ffn2, w_gu4, w_down4, out_rows=N_LAT, final_g=norm_out)
    return out.reshape(BATCH, SEQ, D_MODEL)
```

```python
import functools

import numpy as np
import jax
import jax.numpy as jnp
from jax import lax
from jax.experimental import pallas as pl
from jax.experimental.pallas import tpu as pltpu

D_MODEL = 2048
BATCH = 4
SEQ = 4096
GRID_W = 64
CTX_LEN = 256
N_MOD = 9
D_FF = 5632
RMS_EPS = 1e-6
ROPE_BASE = 10000.0

GLA_HEADS = 4
GLA_DK = 128
GLA_DV = 256
GLA_LOWRANK = 16
GLA_TAU = 16.0
GLA_CHUNK = 64
FNET_GROUPS = 4
FNET_GROUP_DIM = 256
GLA_QK_W = GLA_HEADS * GLA_DK
GLA_V_W = GLA_HEADS * GLA_DV
FNET_W = FNET_GROUPS * FNET_GROUP_DIM

NA_HEADS = 32
NA_HEAD_DIM = 64
NA_KH = 8
NA_KW = 16
NA_QROWS = 8
NEG_BIG = -1e30
LOG2_E = 1.4426950408889634
NA_Q_SCALE = NA_HEAD_DIM ** -0.5 * LOG2_E

N_LAT = BATCH * SEQ
N_CTX = BATCH * CTX_LEN
N_TOK = N_LAT + N_CTX

F32 = jnp.float32
BF16 = jnp.bfloat16
MIB = 1024 * 1024


def _cparams(sem, vmem_mib):
    return pltpu.CompilerParams(dimension_semantics=sem, vmem_limit_bytes=vmem_mib * MIB)


def _dot(a, b):
    return jnp.dot(a, b, preferred_element_type=F32)


def _dot_nt(a, b):
    return lax.dot_general(a, b, (((1,), (1,)), ((), ())), preferred_element_type=F32)


def _dot_tn(a, b):
    return lax.dot_general(a, b, (((0,), (0,)), ((), ())), preferred_element_type=F32)


def _split2(x):
    hi = x.astype(BF16)
    lo = (x - hi.astype(F32)).astype(BF16)
    return hi, lo


def _dot3(a, b):
    ah, al = _split2(a)
    bh, bl = _split2(b)
    return _dot(ah, bh) + (_dot(al, bh) + _dot(ah, bl))


def _dot_exact_lhs(l_bf16, x):
    x1 = x.astype(BF16)
    r1 = x - x1.astype(F32)
    x2 = r1.astype(BF16)
    x3 = (r1 - x2.astype(F32)).astype(BF16)
    return _dot(l_bf16, x1) + (_dot(l_bf16, x2) + _dot(l_bf16, x3))


def _sigmoid(x):
    return 1.0 / (1.0 + jnp.exp(-x))


def _rms(x, g):
    ms = jnp.mean(x * x, axis=-1, keepdims=True)
    return x * lax.rsqrt(ms + RMS_EPS) * g


ROW_CHUNK = 16
ROW_GROUP = 8


def _row_groups(n_rows, body, group=ROW_GROUP):
    span = ROW_CHUNK * group

    def step(t, carry):
        base = t * span
        body([pl.ds(pl.multiple_of(base + c * ROW_CHUNK, ROW_CHUNK), ROW_CHUNK) for c in range(group)])
        return carry
    lax.fori_loop(0, n_rows // span, step, 0)


def _inv_rms(x):
    return lax.rsqrt(jnp.mean(x * x, axis=-1, keepdims=True) + RMS_EPS)


def _norm_mod_rows(x_ref, g_ref, sh_ref, sc_ref, h_ref, gs_ref, zero_ref=None):
    d = x_ref.shape[1]
    gs_ref[0] = jnp.broadcast_to(g_ref[...] * (1.0 + sc_ref[0]), (ROW_CHUNK, d))
    gs_ref[1] = jnp.broadcast_to(sh_ref[0], (ROW_CHUNK, d))

    def body(chunks):
        inv = [_inv_rms(x_ref[rows, :]) for rows in chunks]
        for rows, r in zip(chunks, inv):
            h_ref[rows, :] = (x_ref[rows, :] * r * gs_ref[0] + gs_ref[1]).astype(BF16)
            if zero_ref is not None:
                zero_ref[rows, :] = jnp.zeros((ROW_CHUNK, zero_ref.shape[1]), zero_ref.dtype)
    _row_groups(x_ref.shape[0], body)


def _mod_row(i, tm):
    return jnp.minimum(i // (SEQ // tm), BATCH)


def _mod_spec(j, tm, width=D_MODEL, col=None, tile0=0):
    if col is None:
        return pl.BlockSpec((1, 1, width), lambda i, *_: (_mod_row(i + tile0, tm) * N_MOD + j, 0, 0))
    return pl.BlockSpec((1, 1, width), lambda i, n, *_: (_mod_row(i + tile0, tm) * N_MOD + j, 0, n))


CAST_ROW_ALIGN = 16


def _cast_plan(rows, steps):
    per = pl.cdiv(pl.cdiv(rows, steps), CAST_ROW_ALIGN) * CAST_ROW_ALIGN
    while rows % per:
        per += CAST_ROW_ALIGN
    return per, rows // per


def _cast_specs(w, steps, flat_index):
    per, used = _cast_plan(w.shape[0], steps)
    spec = pl.BlockSpec((per, w.shape[1]), lambda *g: (jnp.minimum(flat_index(*g), used - 1), 0))
    return spec, jax.ShapeDtypeStruct(w.shape, BF16), used


def _hosting_casts(kernel_fn, n_in, n_out, used, flat_index, grid_rank):
    nc = len(used)

    def wrapped(*refs):
        a, b, c = n_in + nc, n_in + nc + n_out, n_in + nc + n_out + nc
        kernel_fn(*refs[:n_in], *refs[a:b], *refs[c:])
        t = flat_index(*[pl.program_id(d) for d in range(grid_rank)])
        for n_used, src_ref, dst_ref in zip(used, refs[n_in:a], refs[b:c]):
            @pl.when(t < n_used)
            def _(src_ref=src_ref, dst_ref=dst_ref):
                dst_ref[...] = src_ref[...].astype(BF16)
    return wrapped


ADA_TN = 1024


def _ada_kernel(c_ref, w_ref, b_ref, o_ref):
    c = c_ref[...]
    s = (c * _sigmoid(c)).astype(BF16)
    o_ref[...] = _dot(s, w_ref[...].astype(BF16)) + b_ref[...]


def _ada(cc, w, b):
    n = w.shape[1]
    out = pl.pallas_call(
        _ada_kernel,
        grid=(n // ADA_TN,),
        in_specs=[pl.BlockSpec((8, D_MODEL), lambda j: (0, 0)),
                  pl.BlockSpec((D_MODEL, ADA_TN), lambda j: (0, j)),
                  pl.BlockSpec((1, ADA_TN), lambda j: (0, j))],
        out_specs=pl.BlockSpec((8, ADA_TN), lambda j: (0, j)),
        out_shape=jax.ShapeDtypeStruct((8, n), F32),
        compiler_params=_cparams(("arbitrary",), 40),
        name="ada_mod",
    )(cc, w, b.reshape(1, n))
    return out.reshape(8 * N_MOD, 1, D_MODEL)


FFN_TM = 1024
FFN_TF = 512
FFN_SLAB = 128


def _ffn_kernel(x_hbm, xn_ref, g_ref, sh_ref, sc_ref, shn_ref, scn_ref, gt_ref, wg_ref, wu_ref, wd_ref, *rest,
                final_norm):
    go_ref = rest[0] if final_norm else None
    o_ref, h0_ref, h1_ref, gs_ref, x_ref, x_sem = rest[-6:]
    i = pl.program_id(0)
    f = pl.program_id(1)
    last = pl.num_programs(1) - 1
    tm, d = x_ref.shape

    def x_copy():
        return pltpu.make_async_copy(x_hbm.at[pl.ds(pl.multiple_of(i * tm, tm), tm), :], x_ref, x_sem)

    @pl.when(f == 0)
    def _():
        x_copy().start()

    @pl.when((i == 0) & (f == 0))
    def _():
        x_copy().wait()
        _norm_mod_rows(x_ref, g_ref, sh_ref, sc_ref, h0_ref, gs_ref)

    def step(h_ref, hn_ref, first):
        gs_ref[0] = jnp.broadcast_to(g_ref[...] * (1.0 + scn_ref[0]), (ROW_CHUNK, d))
        gs_ref[1] = jnp.broadcast_to(shn_ref[0], (ROW_CHUNK, d))
        slab0 = jnp.minimum(f, tm // FFN_SLAB - 1) * FFN_SLAB
        chunks = [pl.ds(k * ROW_CHUNK, ROW_CHUNK) for k in range(FFN_SLAB // ROW_CHUNK)]
        inv = [_inv_rms(xn_ref[rows, :]) for rows in chunks]
        for k, (rows, r) in enumerate(zip(chunks, inv)):
            dst = pl.ds(pl.multiple_of(slab0 + k * ROW_CHUNK, ROW_CHUNK), ROW_CHUNK)
            hn_ref[dst, :] = (xn_ref[rows, :] * r * gs_ref[0] + gs_ref[1]).astype(BF16)

        h = h_ref[...]
        a = _dot(h, wg_ref[...])
        u = _dot(h, wu_ref[...])
        act = (a * _sigmoid(a) * u).astype(BF16)
        if first:
            o_ref[...] = _dot(act, wd_ref[...])
        else:
            o_ref[...] += _dot(act, wd_ref[...])

    for parity, (h_ref, hn_ref) in enumerate(((h0_ref, h1_ref), (h1_ref, h0_ref))):
        for first in (True, False):
            @pl.when((i % 2 == parity) & ((f == 0) == first))
            def _(h_ref=h_ref, hn_ref=hn_ref, first=first):
                step(h_ref, hn_ref, first)

    @pl.when(f == last)
    def _():
        @pl.when(i > 0)
        def _():
            x_copy().wait()

        if not final_norm:
            o_ref[...] = x_ref[...] + (0.5 * gt_ref[0]) * o_ref[...]
        else:
            gs_ref[2] = jnp.broadcast_to(0.5 * gt_ref[0], (ROW_CHUNK, d))
            gs_ref[3] = jnp.broadcast_to(go_ref[...], (ROW_CHUNK, d))

            def body(chunks):
                resid = lambda rows: x_ref[rows, :] + gs_ref[2] * o_ref[rows, :]
                inv = [_inv_rms(resid(rows)) for rows in chunks]
                for rows, r in zip(chunks, inv):
                    o_ref[rows, :] = resid(rows) * r * gs_ref[3]
            _row_groups(tm, body)


def _ffn(x_in, mod, sub, norm_g, w_gu, w_down, *, out_rows, tile0=0, prev=None, final_g=None):
    tm, tf = FFN_TM, FFN_TF
    nf = D_FF // tf
    n_tiles = x_in.shape[0] // tm
    slabs = tm // FFN_SLAB
    assert slabs <= nf
    nxt = lambda i: jnp.minimum(i + 1, n_tiles - 1)
    in_specs = [pl.BlockSpec(memory_space=pl.ANY),
                pl.BlockSpec((FFN_SLAB, D_MODEL), lambda i, f: (nxt(i) * slabs + jnp.minimum(f, slabs - 1), 0)),
                pl.BlockSpec((1, D_MODEL), lambda i, f: (0, 0)),
                _mod_spec(3 * sub, tm, tile0=tile0), _mod_spec(3 * sub + 1, tm, tile0=tile0),
                pl.BlockSpec((1, 1, D_MODEL), lambda i, f: (_mod_row(nxt(i) + tile0, tm) * N_MOD + 3 * sub, 0, 0)),
                pl.BlockSpec((1, 1, D_MODEL), lambda i, f: (_mod_row(nxt(i) + tile0, tm) * N_MOD + 3 * sub + 1, 0, 0)),
                _mod_spec(3 * sub + 2, tm, tile0=tile0),
                pl.BlockSpec((D_MODEL, tf), lambda i, f: (0, f)),
                pl.BlockSpec((D_MODEL, tf), lambda i, f: (0, f + nf)),
                pl.BlockSpec((tf, D_MODEL), lambda i, f: (f, 0))]
    args = [x_in, x_in, norm_g.reshape(1, D_MODEL), mod, mod, mod, mod, mod, w_gu, w_gu, w_down]
    if final_g is not None:
        in_specs.append(pl.BlockSpec((1, D_MODEL), lambda i, f: (0, 0)))
        args.append(final_g.reshape(1, D_MODEL))
    aliases = {}
    if prev is not None:
        aliases = {len(args): 0}
        in_specs.append(pl.BlockSpec(memory_space=pl.ANY))
        args.append(prev)
    return pl.pallas_call(
        functools.partial(_ffn_kernel, final_norm=final_g is not None),
        grid=(n_tiles, nf),
        in_specs=in_specs,
        out_specs=pl.BlockSpec((tm, D_MODEL), lambda i, f: (i + tile0, 0)),
        out_shape=jax.ShapeDtypeStruct((out_rows, D_MODEL), F32),
        scratch_shapes=[pltpu.VMEM((tm, D_MODEL), BF16), pltpu.VMEM((tm, D_MODEL), BF16),
                        pltpu.VMEM((4, ROW_CHUNK, D_MODEL), F32), pltpu.VMEM((tm, D_MODEL), F32),
                        pltpu.SemaphoreType.DMA(())],
        input_output_aliases=aliases,
        compiler_params=_cparams(("arbitrary", "arbitrary"), 56),
        name="ffn_half",
    )(*args)


NMM_TM = 1024
NMM_TN = 1024


def _nmm_kernel(x_ref, g_ref, sh_ref, sc_ref, w_ref, *rest, gla):
    if gla:
        wn_ref, wc_ref, o_ref, ob_ref, zc_ref, zs_ref, on_ref, h_ref, gs_ref = rest
    else:
        o_ref, h_ref, gs_ref = rest
    j = pl.program_id(1)

    @pl.when(j == 0)
    def _():
        _norm_mod_rows(x_ref, g_ref, sh_ref, sc_ref, h_ref, gs_ref)
        if gla:
            on_ref[...] = _dot(h_ref[...], wn_ref[...])

    if not gla:
        o_ref[...] = _dot(h_ref[...], w_ref[...]).astype(o_ref.dtype)
    else:
        @pl.when(j % 2 == 0)
        def _():
            o_ref[...] = _dot(h_ref[...], w_ref[...])

        @pl.when(j == 1)
        def _():
            ob_ref[...] = _dot(h_ref[...], w_ref[...]).astype(BF16)

        @pl.when(j == 3)
        def _():
            f = _dot(h_ref[...], w_ref[...]).astype(BF16)
            for grp in range(FNET_GROUPS):
                cols = slice(grp * FNET_GROUP_DIM, (grp + 1) * FNET_GROUP_DIM)
                z = _dot(f[:, cols], wc_ref[...])
                zc_ref[:, cols] = z[:, :FNET_GROUP_DIM].astype(BF16)
                zs_ref[:, cols] = z[:, FNET_GROUP_DIM:].astype(BF16)


def _nmm(xs, mod, sub, norm_g, w, w_narrow=None, w_chan=None, cast_ws=()):
    tm, tn = NMM_TM, NMM_TN
    n = w.shape[1]
    gla = w_narrow is not None
    in_specs = [pl.BlockSpec((tm, D_MODEL), lambda i, j: (i, 0)),
                pl.BlockSpec((1, D_MODEL), lambda i, j: (0, 0)),
                _mod_spec(3 * sub, tm), _mod_spec(3 * sub + 1, tm),
                pl.BlockSpec((D_MODEL, tn), lambda i, j: (0, j))]
    args = [xs, norm_g.reshape(1, D_MODEL), mod, mod, w]
    if gla:
        nn = w_narrow.shape[1]
        assert n == 4 * tn and tn == GLA_V_W == FNET_W
        in_specs += [pl.BlockSpec((D_MODEL, nn), lambda i, j: (0, 0)),
                     pl.BlockSpec(w_chan.shape, lambda i, j: (0, 0))]
        args += [w_narrow, w_chan]
        row_tile = pl.BlockSpec((tm, tn), lambda i, j: (i, 0))
        out_specs = [pl.BlockSpec((tm, tn), lambda i, j: (i, j // 2)), row_tile, row_tile, row_tile,
                     pl.BlockSpec((tm, nn), lambda i, j: (i, 0))]
        out_shape = [jax.ShapeDtypeStruct((N_TOK, n // 2), F32)] + [jax.ShapeDtypeStruct((N_TOK, tn), BF16)] * 3 + [
            jax.ShapeDtypeStruct((N_TOK, nn), F32)]
    else:
        out_specs = [pl.BlockSpec((tm, tn), lambda i, j: (i, j))]
        out_shape = [jax.ShapeDtypeStruct((N_TOK, n), BF16)]
    nj = n // tn
    flat = lambda i, j: i * nj + j
    casts = [_cast_specs(cw, (N_TOK // tm) * nj, flat) for cw in cast_ws]
    return pl.pallas_call(
        _hosting_casts(functools.partial(_nmm_kernel, gla=gla), len(args), len(out_specs),
                       [u for _, _, u in casts], flat, 2),
        grid=(N_TOK // tm, nj),
        in_specs=in_specs + [s for s, _, _ in casts],
        out_specs=out_specs + [s for s, _, _ in casts],
        out_shape=out_shape + [o for _, o, _ in casts],
        scratch_shapes=[pltpu.VMEM((tm, D_MODEL), BF16), pltpu.VMEM((2, ROW_CHUNK, D_MODEL), F32)],
        compiler_params=_cparams(("arbitrary", "arbitrary"), 56),
        name="norm_mod_matmul",
    )(*args, *cast_ws)


GLA_BLOCK = CTX_LEN
GLA_LAT_BLOCKS = SEQ // GLA_BLOCK
GLA_STEPS = 1 + GLA_LAT_BLOCKS
GLA_LR_PAD = 128


def _rope(x, cos, sin_signed):
    return x * cos + pltpu.roll(x, GLA_DK // 2, 1) * sin_signed


def _gla_kernel(qf, kf, vf, lf, cf, sf, qb, kb, vb, lb, cb, sb, gwf, gbf, gwb, gbb,
                of_ref, ob_ref, st_ref):
    @pl.when(pl.program_id(1) == 0)
    def _():
        st_ref[...] = jnp.zeros_like(st_ref)

    c = GLA_CHUNK
    n = GLA_BLOCK
    row = lax.broadcasted_iota(jnp.int32, (n, n), 0)
    col = lax.broadcasted_iota(jnp.int32, (n, n), 1)
    same_chunk = (row // c) == (col // c)
    row_c = lax.broadcasted_iota(jnp.int32, (c, c), 0)
    col_c = lax.broadcasted_iota(jnp.int32, (c, c), 1)
    scale = GLA_DK ** -0.5

    dirs = ((qf, kf, vf, lf, cf, sf, gwf, gbf, of_ref), (qb, kb, vb, lb, cb, sb, gwb, gbb, ob_ref))
    n_chunks = n // c
    heads = [(slice(h * GLA_DK, (h + 1) * GLA_DK), slice(h * GLA_DV, (h + 1) * GLA_DV)) for h in range(GLA_HEADS)]
    tris, decays = [], []
    for d, (q_ref, k_ref, v_ref, l_ref, c_ref, s_ref, gw_ref, gb_ref, o_ref) in enumerate(dirs):
        causal = (row >= col) if d == 0 else (row <= col)
        tri_b = jnp.where(same_chunk & causal, 1.0, 0.0).astype(BF16)
        tris.append((row_c >= col_c) if d == 0 else (row_c <= col_c))
        x = _dot3(l_ref[...], gw_ref[...]) + gb_ref[...]
        g = (jnp.minimum(x, 0.0) - jnp.log1p(jnp.exp(-jnp.abs(x)))) * (1.0 / GLA_TAU)
        bc_all = _dot_exact_lhs(tri_b, g)
        per_chunk = []
        for ci in (range(n_chunks) if d == 0 else range(n_chunks - 1, -1, -1)):
            rows = slice(ci * c, (ci + 1) * c)
            bc = bc_all[rows, :]
            mid = bc[c // 2:c // 2 + 1, :]
            last = bc[c - 1:c, :] if d == 0 else bc[0:1, :]
            per_chunk.append((rows, jnp.exp(bc), jnp.exp(bc - mid), jnp.exp(mid - bc), jnp.exp(last - bc),
                              jnp.exp(last)))
        decays.append(per_chunk)

    def local_part(k):
        out = {}
        for d, (q_ref, k_ref, v_ref, _, c_ref, s_ref, _, _, _) in enumerate(dirs):
            rows, e_q, e_qm, e_km, e_kl, e_l = decays[d][k]
            cos, sin = c_ref[rows, :], s_ref[rows, :]
            for h, (sk, sv) in enumerate(heads):
                qh = _rope(q_ref[rows, sk] * scale, cos, sin)
                kh = _rope(k_ref[rows, sk], cos, sin)
                vh = v_ref[rows, sv].astype(BF16)
                att = _dot_nt((qh * e_qm[:, sk]).astype(BF16), (kh * e_km[:, sk]).astype(BF16))
                kv = _dot_tn(vh, (kh * e_kl[:, sk]).astype(BF16))
                out[d, h] = ((qh * e_q[:, sk]).astype(BF16), jnp.where(tris[d], att, 0.0).astype(BF16), vh, kv,
                             e_l[:, sk])
        return out

    st = {(d, h): st_ref[d, h] for d in range(2) for h in range(GLA_HEADS)}
    ahead = 1
    pending = {k: local_part(k) for k in range(min(ahead, n_chunks))}
    for k in range(n_chunks):
        if k + ahead < n_chunks:
            pending[k + ahead] = local_part(k + ahead)
        local = pending.pop(k)
        for d in range(2):
            rows = decays[d][k][0]
            for h, (sk, sv) in enumerate(heads):
                qe, att, vh, kv, e_l = local[d, h]
                dirs[d][-1][rows, sv] = _dot_nt(qe, st[d, h].astype(BF16)) + _dot(att, vh)
                st[d, h] = st[d, h] * e_l + kv
    for (d, h), val in st.items():
        st_ref[d, h] = val


def _gla(pf, pb, lr, cos_t, sin_t, gwf, gbf, gwb, gbb, cast_ws):
    c = GLA_BLOCK
    ctx0 = N_LAT // c

    def fwd_blk(b, s):
        return jnp.where(s == 0, ctx0 + b, GLA_LAT_BLOCKS * b + s - 1)

    def bwd_blk(b, s):
        return jnp.where(s == 0, ctx0 + b, GLA_LAT_BLOCKS * b + (GLA_STEPS - 1 - s))

    def fwd_rope(b, s):
        return jnp.where(s == 0, GLA_LAT_BLOCKS, s - 1)

    def bwd_rope(b, s):
        return jnp.where(s == 0, GLA_LAT_BLOCKS, GLA_STEPS - 1 - s)

    def dir_specs(blk, rope):
        return [pl.BlockSpec((c, GLA_QK_W), lambda b, s: (blk(b, s), 0)),
                pl.BlockSpec((c, GLA_QK_W), lambda b, s: (blk(b, s), 1)),
                pl.BlockSpec((c, GLA_V_W), lambda b, s: (blk(b, s), 0)),
                pl.BlockSpec((c, GLA_LR_PAD), lambda b, s: (blk(b, s), 0)),
                pl.BlockSpec((c, GLA_DK), lambda b, s: (rope(b, s), 0)),
                pl.BlockSpec((c, GLA_DK), lambda b, s: (rope(b, s), 0))]

    const = lambda shape: pl.BlockSpec(shape, lambda b, s: (0, 0))
    in_specs = (dir_specs(fwd_blk, fwd_rope) + dir_specs(bwd_blk, bwd_rope)
                + [const((GLA_LR_PAD, GLA_QK_W)), const((1, GLA_QK_W)),
                   const((GLA_LR_PAD, GLA_QK_W)), const((1, GLA_QK_W))])
    flat = lambda b, s: b * GLA_STEPS + s
    casts = [_cast_specs(w, BATCH * GLA_STEPS, flat) for w in cast_ws]
    args = [pf, pf, pb, lr, cos_t, sin_t, pf, pf, pb, lr, cos_t, sin_t, gwf, gbf, gwb, gbb]
    return pl.pallas_call(
        _hosting_casts(_gla_kernel, len(args), 2, [u for _, _, u in casts], flat, 2),
        grid=(BATCH, GLA_STEPS),
        in_specs=in_specs + [s for s, _, _ in casts],
        out_specs=[pl.BlockSpec((c, GLA_V_W), lambda b, s: (fwd_blk(b, s), 0)),
                   pl.BlockSpec((c, GLA_V_W), lambda b, s: (bwd_blk(b, s), 0))] + [s for s, _, _ in casts],
        out_shape=[jax.ShapeDtypeStruct((N_TOK, GLA_V_W), F32)] * 2 + [o for _, o, _ in casts],
        scratch_shapes=[pltpu.VMEM((2, GLA_HEADS, GLA_DV, GLA_DK), F32)],
        compiler_params=_cparams(("arbitrary", "arbitrary"), 40),
        name="gla_scan",
    )(*args, *cast_ws)


def _tdft_kernel(c_ref, s_ref, zc_ref, zs_ref, *rest):
    o_ref = rest[-1]
    o_ref[...] = _dot(c_ref[...], zc_ref[...]) - _dot(s_ref[...], zs_ref[...])


def _time_dft(cmat, smat, zc, zs, t_len, tm, tn, row0, prev=None, cast_ws=()):
    zb0 = row0 // t_len
    ob0 = row0 // tm
    mt = t_len // tm
    in_specs = [pl.BlockSpec((tm, t_len), lambda b, n, m: (m, 0)),
                pl.BlockSpec((tm, t_len), lambda b, n, m: (m, 0)),
                pl.BlockSpec((t_len, tn), lambda b, n, m: (zb0 + b, n)),
                pl.BlockSpec((t_len, tn), lambda b, n, m: (zb0 + b, n))]
    args = [cmat, smat, zc, zs]
    aliases = {}
    if prev is not None:
        in_specs.append(pl.BlockSpec(memory_space=pl.ANY))
        args.append(prev)
        aliases = {4: 0}
    nt = FNET_W // tn
    flat = lambda b, n, m: (b * nt + n) * mt + m
    casts = [_cast_specs(w, BATCH * nt * mt, flat) for w in cast_ws]
    return pl.pallas_call(
        _hosting_casts(_tdft_kernel, len(args), 1, [u for _, _, u in casts], flat, 3),
        grid=(BATCH, nt, mt),
        in_specs=in_specs + [s for s, _, _ in casts],
        out_specs=[pl.BlockSpec((tm, tn), lambda b, n, m: (ob0 + b * mt + m, n))] + [s for s, _, _ in casts],
        out_shape=[jax.ShapeDtypeStruct((N_TOK, FNET_W), F32)] + [o for _, o, _ in casts],
        input_output_aliases=aliases,
        compiler_params=_cparams(("arbitrary", "arbitrary", "arbitrary"), 48),
        name="fnet_time_dft",
    )(*args, *cast_ws)


def _tdft_fold_kernel(c_ref, s_ref, zc_ref, zs_ref, mid_ref, o_ref):
    tm = o_ref.shape[0]
    j = pl.program_id(2) * tm + lax.broadcasted_iota(jnp.int32, (tm, 1), 0)
    sign = jnp.where(j % 2 == 0, SEQ ** -0.5, -(SEQ ** -0.5))
    o_ref[...] = _dot(c_ref[...], zc_ref[...]) - _dot(s_ref[...], zs_ref[...]) + sign * mid_ref[0]


def _fold_mirror(z, sign):
    half = SEQ // 2
    z4 = z[:N_LAT].reshape(BATCH, SEQ, z.shape[1]).astype(F32)
    body = z4[:, 1:half] + sign * jnp.flip(z4[:, half + 1:], axis=1)
    first = z4[:, :1] if sign > 0 else jnp.zeros_like(z4[:, :1])
    return jnp.concatenate([first, body], axis=1).astype(BF16).reshape(BATCH * half, z.shape[1])


def _time_dft_folded(cmat, smat, zc, zs, tm, tn, cast_ws=()):
    half = SEQ // 2
    mt = SEQ // tm
    nt = FNET_W // tn
    mid = zc[:N_LAT].reshape(BATCH, SEQ, FNET_W)[:, half].astype(F32).reshape(BATCH, 1, FNET_W)
    args = [cmat, smat, _fold_mirror(zc, 1.0), _fold_mirror(zs, -1.0), mid]
    in_specs = [pl.BlockSpec((tm, half), lambda b, n, m: (m, 0)),
                pl.BlockSpec((tm, half), lambda b, n, m: (m, 0)),
                pl.BlockSpec((half, tn), lambda b, n, m: (b, n)),
                pl.BlockSpec((half, tn), lambda b, n, m: (b, n)),
                pl.BlockSpec((1, 1, tn), lambda b, n, m: (b, 0, n))]
    flat = lambda b, n, m: (b * nt + n) * mt + m
    casts = [_cast_specs(w, BATCH * nt * mt, flat) for w in cast_ws]
    return pl.pallas_call(
        _hosting_casts(_tdft_fold_kernel, len(args), 1, [u for _, _, u in casts], flat, 3),
        grid=(BATCH, nt, mt),
        in_specs=in_specs + [s for s, _, _ in casts],
        out_specs=[pl.BlockSpec((tm, tn), lambda b, n, m: (b * mt + m, n))] + [s for s, _, _ in casts],
        out_shape=[jax.ShapeDtypeStruct((N_TOK, FNET_W), F32)] + [o for _, o, _ in casts],
        compiler_params=_cparams(("arbitrary", "arbitrary", "arbitrary"), 48),
        name="fnet_time_dft_folded",
    )(*args, *cast_ws)


def _dft_tables(n):
    k = np.arange(n)
    ang = 2.0 * np.pi * ((k[:, None] * k[None, :]) % n) / n
    return np.cos(ang) / np.sqrt(n), np.sin(ang) / np.sqrt(n)


def _dft_table_kernel(ac_ref, as_ref, bc_ref, bs_ref, co_ref, so_ref):
    ac, as_ = ac_ref[0], as_ref[0]
    bc, bs = bc_ref[...], bs_ref[...]
    co_ref[...] = (ac * bc - as_ * bs).astype(BF16)
    so_ref[...] = (ac * bs + as_ * bc).astype(BF16)


def _big_dft_tables():
    r = GRID_W
    k = np.arange(SEQ)
    j = np.arange(r)
    pa = 2.0 * np.pi * ((j[:, None] * r * k[None, :]) % SEQ) / SEQ
    pb = 2.0 * np.pi * ((j[:, None] * k[None, :]) % SEQ) / SEQ
    ac = jnp.asarray(np.cos(pa).reshape(r, 1, SEQ), F32)
    as_ = jnp.asarray(np.sin(pa).reshape(r, 1, SEQ), F32)
    bc = jnp.asarray(np.cos(pb) * SEQ ** -0.5, F32)
    bs = jnp.asarray(np.sin(pb) * SEQ ** -0.5, F32)
    row = pl.BlockSpec((1, 1, SEQ), lambda i: (i, 0, 0))
    full = pl.BlockSpec((r, SEQ), lambda i: (0, 0))
    return pl.pallas_call(
        _dft_table_kernel,
        grid=(r,),
        in_specs=[row, row, full, full],
        out_specs=[pl.BlockSpec((r, SEQ), lambda i: (i, 0))] * 2,
        out_shape=[jax.ShapeDtypeStruct((SEQ, SEQ), BF16)] * 2,
        compiler_params=_cparams(("parallel",), 16),
        name="dft_tables",
    )(ac, as_, bc, bs)


GO_TM = 512
GO_TN = 2048


def _glaout_kernel(of_ref, ob_ref, r_ref, fm_ref, gn_ref, x_ref, gt_ref, w_ref, o_ref, h_ref):
    @pl.when(pl.program_id(1) == 0)
    def _():
        heads = [slice(h * GLA_DV, (h + 1) * GLA_DV) for h in range(GLA_HEADS)]

        def body(chunks):
            inv = [[_inv_rms(of_ref[rows, sv] + ob_ref[rows, sv]) for sv in heads] for rows in chunks]
            for rows, inv_c in zip(chunks, inv):
                for sv, r in zip(heads, inv_c):
                    o = (of_ref[rows, sv] + ob_ref[rows, sv]) * r
                    gate = r_ref[rows, sv]
                    h_ref[rows, sv] = (o * gn_ref[:, sv] * (gate * _sigmoid(gate))).astype(BF16)
                h_ref[rows, GLA_V_W:] = fm_ref[rows, :].astype(BF16)
        _row_groups(of_ref.shape[0], body, group=4)

    o_ref[...] = x_ref[...] + gt_ref[0] * _dot(h_ref[...], w_ref[...])


def _glaout(of, ob, pf, fm, g_norm, xs, mod, w_out):
    tm, tn = GO_TM, GO_TN
    half = lambda col: pl.BlockSpec((tm, GLA_V_W), lambda i, j: (i, col))
    return pl.pallas_call(
        _glaout_kernel,
        grid=(N_TOK // tm, D_MODEL // tn),
        in_specs=[half(0), half(0), half(1), half(0),
                  pl.BlockSpec((1, GLA_V_W), lambda i, j: (0, 0)),
                  pl.BlockSpec((tm, tn), lambda i, j: (i, j)),
                  _mod_spec(5, tm, tn, col=True),
                  pl.BlockSpec((D_MODEL, tn), lambda i, j: (0, j))],
        out_specs=pl.BlockSpec((tm, tn), lambda i, j: (i, j)),
        out_shape=jax.ShapeDtypeStruct((N_TOK, D_MODEL), F32),
        scratch_shapes=[pltpu.VMEM((tm, D_MODEL), BF16)],
        compiler_params=_cparams(("parallel", "arbitrary"), 56),
        name="gla_fnet_out",
    )(of, ob, pf, fm, g_norm.reshape(1, GLA_V_W), xs, mod, w_out)


NA_HROWS = NA_QROWS // 2
NA_WROWS = NA_HROWS + NA_KH
NA_TQ = NA_QROWS * GRID_W
NA_HQ = NA_HROWS * GRID_W
NA_WK = NA_WROWS * GRID_W
NA_KBLK = NA_HROWS * GRID_W
NA_WBLKS = NA_WK // NA_KBLK
NA_MASKED = 2 * NA_KH - 1
NA_ROW_BLOCKS = (SEQ // GRID_W) // NA_QROWS
NA_CASES = (0, 1, NA_ROW_BLOCKS - 1)


def _na_window_row0(j, half):
    q0 = NA_QROWS * j + NA_HROWS * half
    return int(np.clip(q0 - NA_KH // 2, 0, SEQ // GRID_W - NA_WROWS))


def _na_bias_slots():
    rows = SEQ // GRID_W
    slots = np.full((len(NA_CASES), 2, NA_HROWS, NA_WROWS), NA_MASKED, np.int32)
    for case, j in enumerate(NA_CASES):
        for half in range(2):
            w0 = _na_window_row0(j, half)
            for i in range(NA_HROWS):
                qr = NA_QROWS * j + NA_HROWS * half + i
                r0 = int(np.clip(qr - NA_KH // 2, 0, rows - NA_KH))
                for l in range(NA_WROWS):
                    kr = w0 + l
                    if r0 <= kr < r0 + NA_KH:
                        slots[case, half, i, l] = kr - qr + NA_KH - 1
    return slots


def _na_kernel(q_ref, *refs):
    nb = NA_WBLKS
    ka, kb, va, vb = refs[0:nb], refs[nb:2 * nb], refs[2 * nb:3 * nb], refs[3 * nb:4 * nb]
    kc_ref, vc_ref, td_ref, o_ref, bias_ref = refs[4 * nb:]
    j = pl.program_id(1)

    @pl.when((j == 0) & (pl.program_id(2) == 0))
    def _():
        slots = _na_bias_slots()
        left = lax.broadcasted_iota(jnp.int32, (GRID_W, 2 * GRID_W), 1) < GRID_W
        for case in range(len(NA_CASES)):
            for half in range(2):
                for hh in range(2):
                    dst = (case * 2 + half) * 2 + hh
                    for i in range(NA_HROWS):
                        for p in range(NA_WROWS // 2):
                            sl = int(slots[case, half, i, 2 * p])
                            sr = int(slots[case, half, i, 2 * p + 1])
                            tile = td_ref[hh, sl] if sl == sr else jnp.where(left, td_ref[hh, sl], td_ref[hh, sr])
                            bias_ref[dst, i * GRID_W:(i + 1) * GRID_W, p * 2 * GRID_W:(p + 1) * 2 * GRID_W] = tile

    case = jnp.where(j == 0, 0, jnp.where(j == NA_ROW_BLOCKS - 1, 2, 1))
    kc = kc_ref[...]
    vc = vc_ref[...]
    lane = lax.broadcasted_iota(jnp.int32, (NA_HQ, 2 * NA_HEAD_DIM), 1)
    kv_refs = ((ka, va), (kb, vb))
    items = [(half, hh) for half in range(2) for hh in range(2)]

    def scores(half, hh):
        q = q_ref[half * NA_HQ:(half + 1) * NA_HQ, :]
        qm = jnp.where((lane // NA_HEAD_DIM) == hh, q, jnp.zeros_like(q))
        k = jnp.concatenate([r[...] for r in kv_refs[half][0]], axis=0)
        return _dot_nt(qm, k) + bias_ref[(case * 2 + half) * 2 + hh], _dot_nt(qm, kc)

    def softmax(s_loc, s_ctx):
        m = jnp.maximum(jnp.max(s_loc, axis=-1, keepdims=True), jnp.max(s_ctx, axis=-1, keepdims=True))
        p_loc = jnp.exp2(s_loc - m)
        p_ctx = jnp.exp2(s_ctx - m)
        denom = jnp.sum(p_loc, axis=-1, keepdims=True) + jnp.sum(p_ctx, axis=-1, keepdims=True)
        return p_loc.astype(BF16), p_ctx.astype(BF16), denom

    def weighted(half, p_loc, p_ctx, denom):
        v = jnp.concatenate([r[...] for r in kv_refs[half][1]], axis=0)
        return (_dot(p_loc, v) + _dot(p_ctx, vc)) / denom

    s, p, o = {}, {}, {}
    for t in range(len(items) + 2):
        if t < len(items):
            s[t] = scores(*items[t])
        if 0 <= t - 1 < len(items):
            p[t - 1] = softmax(*s.pop(t - 1))
        if 0 <= t - 2 < len(items):
            o[t - 2] = weighted(items[t - 2][0], *p.pop(t - 2))
    for half in range(2):
        first = (lane // NA_HEAD_DIM) == 0
        out = jnp.where(first, o[2 * half], o[2 * half + 1])
        o_ref[half * NA_HQ:(half + 1) * NA_HQ, :] = out.astype(o_ref.dtype)


def _na(qkv, tiles, cast_ws=()):
    hp_n = NA_HEADS // 2
    jn = NA_ROW_BLOCKS
    kb_per_img = SEQ // NA_KBLK
    last_blk = kb_per_img - NA_WBLKS

    def kv_spec(col0, half, t):
        return pl.BlockSpec((NA_KBLK, 128), lambda hp, j, b: (
            b * kb_per_img + jnp.clip(2 * j + half - 1, 0, last_blk) + t, col0 + hp))

    def kv_specs(col0):
        return [kv_spec(col0, half, t) for half in range(2) for t in range(NA_WBLKS)]

    ctx0 = N_LAT // CTX_LEN
    in_specs = ([pl.BlockSpec((NA_TQ, 128), lambda hp, j, b: (b * jn + j, hp))]
                + kv_specs(hp_n) + kv_specs(2 * hp_n)
                + [pl.BlockSpec((CTX_LEN, 128), lambda hp, j, b: (ctx0 + b, hp_n + hp)),
                   pl.BlockSpec((CTX_LEN, 128), lambda hp, j, b: (ctx0 + b, 2 * hp_n + hp)),
                   pl.BlockSpec((2, 2 * NA_KH, GRID_W, 2 * GRID_W), lambda hp, j, b: (hp, 0, 0, 0))])
    flat = lambda hp, j, b: (hp * jn + j) * BATCH + b
    casts = [_cast_specs(w, hp_n * jn * BATCH, flat) for w in cast_ws]
    return pl.pallas_call(
        _hosting_casts(_na_kernel, len(in_specs), 1, [u for _, _, u in casts], flat, 3),
        grid=(hp_n, jn, BATCH),
        in_specs=in_specs + [s for s, _, _ in casts],
        out_specs=[pl.BlockSpec((NA_TQ, 128), lambda hp, j, b: (b * jn + j, hp))] + [s for s, _, _ in casts],
        out_shape=[jax.ShapeDtypeStruct((N_LAT, D_MODEL), BF16)] + [o for _, o, _ in casts],
        scratch_shapes=[pltpu.VMEM((len(NA_CASES) * 4, NA_HQ, NA_WK), F32)],
        compiler_params=_cparams(("arbitrary", "arbitrary", "arbitrary"), 48),
        name="neighbourhood_attention",
    )(qkv, *([qkv] * (4 * NA_WBLKS)), qkv, qkv, tiles, *cast_ws)


def _na_bias_tiles(rpb):
    w = GRID_W
    col = np.arange(w)
    cs = np.clip(col - NA_KW // 2, 0, w - NA_KW)
    col_mask = (col[None, :] >= cs[:, None]) & (col[None, :] < cs[:, None] + NA_KW)
    dc_idx = np.clip(col[None, :] - col[:, None] + NA_KW - 1, 0, 2 * NA_KW - 2)
    scaled = rpb.astype(F32) * LOG2_E
    t1 = sum(jnp.where(dc_idx[None, None] == j, scaled[:, :, j, None, None], 0.0)
             for j in range(2 * NA_KW - 1))
    t1 = jnp.where(col_mask[None, None], t1, NEG_BIG)
    t1 = jnp.concatenate([t1, jnp.full((NA_HEADS, 1, w, w), NEG_BIG, F32)], axis=1)
    return jnp.concatenate([t1, t1], axis=-1)


MR_TM = 1024
MR_TN = 1024


def _mmres_kernel(a_ref, w_ref, x_ref, gt_ref, o_ref):
    o_ref[...] = x_ref[...] + gt_ref[0] * _dot(a_ref[...], w_ref[...])


def _mm_res(a, w, xs, mod, n_rows):
    tm, tn = MR_TM, MR_TN
    return pl.pallas_call(
        _mmres_kernel,
        grid=(n_rows // tm, D_MODEL // tn),
        in_specs=[pl.BlockSpec((tm, D_MODEL), lambda i, j: (i, 0)),
                  pl.BlockSpec((D_MODEL, tn), lambda i, j: (0, j)),
                  pl.BlockSpec((tm, tn), lambda i, j: (i, j)),
                  _mod_spec(5, tm, tn, col=True)],
        out_specs=pl.BlockSpec((tm, tn), lambda i, j: (i, j)),
        out_shape=jax.ShapeDtypeStruct((n_rows, D_MODEL), F32),
        compiler_params=_cparams(("parallel", "arbitrary"), 40),
        name="matmul_gated_residual",
    )(a, w, xs, mod)


def _rope_tables():
    half = GLA_DK // 4
    inv_freq = ROPE_BASE ** (-jnp.arange(half, dtype=F32) / half)
    t = jnp.arange(SEQ)
    ang_r = (t // GRID_W).astype(F32)[:, None] * inv_freq[None, :]
    ang_c = (t % GRID_W).astype(F32)[:, None] * inv_freq[None, :]
    cr, sr, cc, sc = jnp.cos(ang_r), jnp.sin(ang_r), jnp.cos(ang_c), jnp.sin(ang_c)
    cos_t = jnp.concatenate([cr, cc, cr, cc], axis=-1)
    sin_t = jnp.concatenate([-sr, -sc, sr, sc], axis=-1)
    cos_t = jnp.concatenate([cos_t, jnp.ones((GLA_BLOCK, GLA_DK), F32)], axis=0)
    sin_t = jnp.concatenate([sin_t, jnp.zeros((GLA_BLOCK, GLA_DK), F32)], axis=0)
    return cos_t, sin_t


def _pair_layout(w):
    lead = w.shape[:-1]
    quarter = GLA_DK // 4
    w = w.reshape(*lead, GLA_HEADS, 2, 2, quarter)
    return jnp.swapaxes(w, -2, -3).reshape(*lead, GLA_QK_W)


def kernel(x, c, ctx, c_ctx, l0_ada_w, l0_ada_b, l0_norm_ffn1, l0_norm_mix, l0_norm_ffn2, l0_ffn1_w_gu, l0_ffn1_w_down, l0_ffn2_w_gu, l0_ffn2_w_down, l0_w_in, l0_gla_gate_w_fwd, l0_gla_gate_b_fwd, l0_gla_gate_w_bwd, l0_gla_gate_b_bwd, l0_gla_norm, l0_w_out, l1_ada_w, l1_ada_b, l1_norm_ffn1, l1_norm_mix, l1_norm_ffn2, l1_ffn1_w_gu, l1_ffn1_w_down, l1_ffn2_w_gu, l1_ffn2_w_down, l1_w_qkv, l1_rpb, l1_w_out, norm_out):
    bf = lambda w: w.astype(BF16)
    cc = jnp.concatenate([c, c_ctx[None, :], jnp.zeros((8 - BATCH - 1, D_MODEL), F32)], axis=0)

    mod0 = _ada(cc, l0_ada_w, l0_ada_b)
    w_gu, w_down = bf(l0_ffn1_w_gu), bf(l0_ffn1_w_down)
    xs = _ffn(x.reshape(N_LAT, D_MODEL), mod0, 0, l0_norm_ffn1, w_gu, w_down, out_rows=N_TOK)
    xs = _ffn(ctx.reshape(N_CTX, D_MODEL), mod0, 0, l0_norm_ffn1, w_gu, w_down, out_rows=N_TOK,
              tile0=N_LAT // FFN_TM, prev=xs)

    a0 = 2 * GLA_QK_W + GLA_V_W
    a1 = a0 + 2 * GLA_LOWRANK
    w_main = bf(jnp.concatenate([_pair_layout(l0_w_in[:, :GLA_QK_W]), _pair_layout(l0_w_in[:, GLA_QK_W:2 * GLA_QK_W]),
                                 l0_w_in[:, 2 * GLA_QK_W:a0], l0_w_in[:, a1:]], axis=1))
    w_lr = bf(jnp.pad(l0_w_in[:, a0:a1], ((0, 0), (0, GLA_LR_PAD - 2 * GLA_LOWRANK))))
    cc_np, sc_np = _dft_tables(FNET_GROUP_DIM)
    wc = jnp.asarray(np.concatenate([cc_np, sc_np], axis=1), F32).astype(BF16)
    pf, pb, zc, zs, lr = _nmm(xs, mod0, 1, l0_norm_mix, w_main, w_narrow=w_lr, w_chan=wc)

    cos_t, sin_t = _rope_tables()
    gwf = jnp.pad(_pair_layout(l0_gla_gate_w_fwd), ((0, GLA_LR_PAD - GLA_LOWRANK), (0, 0)))
    gwb = jnp.pad(_pair_layout(l0_gla_gate_w_bwd), ((GLA_LOWRANK, GLA_LR_PAD - 2 * GLA_LOWRANK), (0, 0)))
    of, ob, w_gu2, w_down2 = _gla(pf, pb, lr, cos_t, sin_t, gwf, _pair_layout(l0_gla_gate_b_fwd).reshape(1, -1),
                                  gwb, _pair_layout(l0_gla_gate_b_bwd).reshape(1, -1),
                                  cast_ws=(l0_ffn2_w_gu, l0_ffn2_w_down))

    cos_big, sin_big = _big_dft_tables()
    fm, w_gu3, w_down3 = _time_dft_folded(cos_big, sin_big, zc, zs, 512, 512, cast_ws=(l1_ffn1_w_gu, l1_ffn1_w_down))
    ct_np, st_np = _dft_tables(CTX_LEN)
    fm = _time_dft(jnp.asarray(ct_np, F32).astype(BF16), jnp.asarray(st_np, F32).astype(BF16),
                   zc, zs, CTX_LEN, CTX_LEN, 512, N_LAT, prev=fm)[0]

    xs = _glaout(of, ob, pf, fm, l0_gla_norm, xs, mod0, bf(l0_w_out))
    xs = _ffn(xs, mod0, 2, l0_norm_ffn2, w_gu2, w_down2, out_rows=N_TOK)

    mod1 = _ada(cc, l1_ada_w, l1_ada_b)
    xs = _ffn(xs, mod1, 0, l1_norm_ffn1, w_gu3, w_down3, out_rows=N_TOK)
    w_qkv = bf(jnp.concatenate([l1_w_qkv[:, :D_MODEL] * NA_Q_SCALE, l1_w_qkv[:, D_MODEL:]], axis=1))
    qkv, w_gu4, w_down4 = _nmm(xs, mod1, 1, l1_norm_mix, w_qkv, cast_ws=(l1_ffn2_w_gu, l1_ffn2_w_down))
    ao = _na(qkv, _na_bias_tiles(l1_rpb))[0]
    xl = _mm_res(ao, bf(l1_w_out), xs, mod1, N_LAT)
    out = _ffn(xl, mod1, 2, l1_norm_ffn2, w_gu4, w_down4, out_rows=N_LAT, final_g=norm_out)
    return out.reshape(BATCH, SEQ, D_MODEL)
```

```python
import functools

import numpy as np
import jax
import jax.numpy as jnp
from jax import lax
from jax.experimental import pallas as pl
from jax.experimental.pallas import tpu as pltpu

D_MODEL = 2048
BATCH = 4
SEQ = 4096
GRID_W = 64
CTX_LEN = 256
N_MOD = 9
D_FF = 5632
RMS_EPS = 1e-6
ROPE_BASE = 10000.0

GLA_HEADS = 4
GLA_DK = 128
GLA_DV = 256
GLA_LOWRANK = 16
GLA_TAU = 16.0
GLA_CHUNK = 64
FNET_GROUPS = 4
FNET_GROUP_DIM = 256
GLA_QK_W = GLA_HEADS * GLA_DK
GLA_V_W = GLA_HEADS * GLA_DV
FNET_W = FNET_GROUPS * FNET_GROUP_DIM

NA_HEADS = 32
NA_HEAD_DIM = 64
NA_KH = 8
NA_KW = 16
NA_QROWS = 8
NEG_BIG = -1e30
LOG2_E = 1.4426950408889634
NA_Q_SCALE = NA_HEAD_DIM ** -0.5 * LOG2_E

N_LAT = BATCH * SEQ
N_CTX = BATCH * CTX_LEN
N_TOK = N_LAT + N_CTX

F32 = jnp.float32
BF16 = jnp.bfloat16
MIB = 1024 * 1024


def _cparams(sem, vmem_mib):
    return pltpu.CompilerParams(dimension_semantics=sem, vmem_limit_bytes=vmem_mib * MIB)


def _dot(a, b):
    return jnp.dot(a, b, preferred_element_type=F32)


def _dot_nt(a, b):
    return lax.dot_general(a, b, (((1,), (1,)), ((), ())), preferred_element_type=F32)


def _dot_tn(a, b):
    return lax.dot_general(a, b, (((0,), (0,)), ((), ())), preferred_element_type=F32)


def _split2(x):
    hi = x.astype(BF16)
    lo = (x - hi.astype(F32)).astype(BF16)
    return hi, lo


def _dot3(a, b):
    ah, al = _split2(a)
    bh, bl = _split2(b)
    return _dot(ah, bh) + (_dot(al, bh) + _dot(ah, bl))


def _dot_exact_lhs(l_bf16, x):
    x1 = x.astype(BF16)
    r1 = x - x1.astype(F32)
    x2 = r1.astype(BF16)
    x3 = (r1 - x2.astype(F32)).astype(BF16)
    return _dot(l_bf16, x1) + (_dot(l_bf16, x2) + _dot(l_bf16, x3))


def _sigmoid(x):
    return 1.0 / (1.0 + jnp.exp(-x))


def _rms(x, g):
    ms = jnp.mean(x * x, axis=-1, keepdims=True)
    return x * lax.rsqrt(ms + RMS_EPS) * g


ROW_CHUNK = 16
ROW_GROUP = 8


def _row_groups(n_rows, body, group=ROW_GROUP):
    span = ROW_CHUNK * group

    def step(t, carry):
        base = t * span
        body([pl.ds(pl.multiple_of(base + c * ROW_CHUNK, ROW_CHUNK), ROW_CHUNK) for c in range(group)])
        return carry
    lax.fori_loop(0, n_rows // span, step, 0)


def _inv_rms(x):
    return lax.rsqrt(jnp.mean(x * x, axis=-1, keepdims=True) + RMS_EPS)


def _norm_mod_rows(x_ref, g_ref, sh_ref, sc_ref, h_ref, gs_ref, zero_ref=None):
    d = x_ref.shape[1]
    gs_ref[0] = jnp.broadcast_to(g_ref[...] * (1.0 + sc_ref[0]), (ROW_CHUNK, d))
    gs_ref[1] = jnp.broadcast_to(sh_ref[0], (ROW_CHUNK, d))

    def body(chunks):
        inv = [_inv_rms(x_ref[rows, :]) for rows in chunks]
        for rows, r in zip(chunks, inv):
            h_ref[rows, :] = (x_ref[rows, :] * r * gs_ref[0] + gs_ref[1]).astype(BF16)
            if zero_ref is not None:
                zero_ref[rows, :] = jnp.zeros((ROW_CHUNK, zero_ref.shape[1]), zero_ref.dtype)
    _row_groups(x_ref.shape[0], body)


def _mod_row(i, tm):
    return jnp.minimum(i // (SEQ // tm), BATCH)


def _mod_spec(j, tm, width=D_MODEL, col=None, tile0=0):
    if col is None:
        return pl.BlockSpec((1, 1, width), lambda i, *_: (_mod_row(i + tile0, tm) * N_MOD + j, 0, 0))
    return pl.BlockSpec((1, 1, width), lambda i, n, *_: (_mod_row(i + tile0, tm) * N_MOD + j, 0, n))


CAST_ROW_ALIGN = 16


def _cast_plan(rows, steps):
    per = pl.cdiv(pl.cdiv(rows, steps), CAST_ROW_ALIGN) * CAST_ROW_ALIGN
    while rows % per:
        per += CAST_ROW_ALIGN
    return per, rows // per


def _cast_specs(w, steps, flat_index):
    per, used = _cast_plan(w.shape[0], steps)
    spec = pl.BlockSpec((per, w.shape[1]), lambda *g: (jnp.minimum(flat_index(*g), used - 1), 0))
    return spec, jax.ShapeDtypeStruct(w.shape, BF16), used


def _hosting_casts(kernel_fn, n_in, n_out, used, flat_index, grid_rank):
    nc = len(used)

    def wrapped(*refs):
        a, b, c = n_in + nc, n_in + nc + n_out, n_in + nc + n_out + nc
        kernel_fn(*refs[:n_in], *refs[a:b], *refs[c:])
        t = flat_index(*[pl.program_id(d) for d in range(grid_rank)])
        for n_used, src_ref, dst_ref in zip(used, refs[n_in:a], refs[b:c]):
            @pl.when(t < n_used)
            def _(src_ref=src_ref, dst_ref=dst_ref):
                dst_ref[...] = src_ref[...].astype(BF16)
    return wrapped


ADA_TN = 1024


def _ada_kernel(c_ref, w_ref, b_ref, o_ref):
    c = c_ref[...]
    s = (c * _sigmoid(c)).astype(BF16)
    o_ref[...] = _dot(s, w_ref[...].astype(BF16)) + b_ref[...]


def _ada(cc, w, b):
    n = w.shape[1]
    out = pl.pallas_call(
        _ada_kernel,
        grid=(n // ADA_TN,),
        in_specs=[pl.BlockSpec((8, D_MODEL), lambda j: (0, 0)),
                  pl.BlockSpec((D_MODEL, ADA_TN), lambda j: (0, j)),
                  pl.BlockSpec((1, ADA_TN), lambda j: (0, j))],
        out_specs=pl.BlockSpec((8, ADA_TN), lambda j: (0, j)),
        out_shape=jax.ShapeDtypeStruct((8, n), F32),
        compiler_params=_cparams(("arbitrary",), 40),
        name="ada_mod",
    )(cc, w, b.reshape(1, n))
    return out.reshape(8 * N_MOD, 1, D_MODEL)


FFN_TM = 1024
FFN_TF = 512
FFN_SLAB = 128


def _ffn_kernel(x_hbm, xn_ref, g_ref, sh_ref, sc_ref, shn_ref, scn_ref, gt_ref, wg_ref, wu_ref, wd_ref, *rest,
                final_norm):
    go_ref = rest[0] if final_norm else None
    o_ref, h0_ref, h1_ref, gs_ref, x_ref, x_sem = rest[-6:]
    i = pl.program_id(0)
    f = pl.program_id(1)
    last = pl.num_programs(1) - 1
    tm, d = x_ref.shape

    def x_copy():
        return pltpu.make_async_copy(x_hbm.at[pl.ds(pl.multiple_of(i * tm, tm), tm), :], x_ref, x_sem)

    @pl.when(f == 0)
    def _():
        x_copy().start()

    @pl.when((i == 0) & (f == 0))
    def _():
        x_copy().wait()
        _norm_mod_rows(x_ref, g_ref, sh_ref, sc_ref, h0_ref, gs_ref)

    def step(h_ref, hn_ref, first):
        gs_ref[0] = jnp.broadcast_to(g_ref[...] * (1.0 + scn_ref[0]), (ROW_CHUNK, d))
        gs_ref[1] = jnp.broadcast_to(shn_ref[0], (ROW_CHUNK, d))
        slab0 = jnp.minimum(f, tm // FFN_SLAB - 1) * FFN_SLAB
        chunks = [pl.ds(k * ROW_CHUNK, ROW_CHUNK) for k in range(FFN_SLAB // ROW_CHUNK)]
        inv = [_inv_rms(xn_ref[rows, :]) for rows in chunks]
        for k, (rows, r) in enumerate(zip(chunks, inv)):
            dst = pl.ds(pl.multiple_of(slab0 + k * ROW_CHUNK, ROW_CHUNK), ROW_CHUNK)
            hn_ref[dst, :] = (xn_ref[rows, :] * r * gs_ref[0] + gs_ref[1]).astype(BF16)

        h = h_ref[...]
        a = _dot(h, wg_ref[...])
        u = _dot(h, wu_ref[...])
        act = (a * _sigmoid(a) * u).astype(BF16)
        if first:
            o_ref[...] = _dot(act, wd_ref[...])
        else:
            o_ref[...] += _dot(act, wd_ref[...])

    for parity, (h_ref, hn_ref) in enumerate(((h0_ref, h1_ref), (h1_ref, h0_ref))):
        for first in (True, False):
            @pl.when((i % 2 == parity) & ((f == 0) == first))
            def _(h_ref=h_ref, hn_ref=hn_ref, first=first):
                step(h_ref, hn_ref, first)

    @pl.when(f == last)
    def _():
        @pl.when(i > 0)
        def _():
            x_copy().wait()

        if not final_norm:
            o_ref[...] = x_ref[...] + (0.5 * gt_ref[0]) * o_ref[...]
        else:
            gs_ref[2] = jnp.broadcast_to(0.5 * gt_ref[0], (ROW_CHUNK, d))
            gs_ref[3] = jnp.broadcast_to(go_ref[...], (ROW_CHUNK, d))

            def body(chunks):
                resid = lambda rows: x_ref[rows, :] + gs_ref[2] * o_ref[rows, :]
                inv = [_inv_rms(resid(rows)) for rows in chunks]
                for rows, r in zip(chunks, inv):
                    o_ref[rows, :] = resid(rows) * r * gs_ref[3]
            _row_groups(tm, body)


def _ffn(x_in, mod, sub, norm_g, w_gu, w_down, *, out_rows, tile0=0, prev=None, final_g=None):
    tm, tf = FFN_TM, FFN_TF
    nf = D_FF // tf
    n_tiles = x_in.shape[0] // tm
    slabs = tm // FFN_SLAB
    assert slabs <= nf
    nxt = lambda i: jnp.minimum(i + 1, n_tiles - 1)
    in_specs = [pl.BlockSpec(memory_space=pl.ANY),
                pl.BlockSpec((FFN_SLAB, D_MODEL), lambda i, f: (nxt(i) * slabs + jnp.minimum(f, slabs - 1), 0)),
                pl.BlockSpec((1, D_MODEL), lambda i, f: (0, 0)),
                _mod_spec(3 * sub, tm, tile0=tile0), _mod_spec(3 * sub + 1, tm, tile0=tile0),
                pl.BlockSpec((1, 1, D_MODEL), lambda i, f: (_mod_row(nxt(i) + tile0, tm) * N_MOD + 3 * sub, 0, 0)),
                pl.BlockSpec((1, 1, D_MODEL), lambda i, f: (_mod_row(nxt(i) + tile0, tm) * N_MOD + 3 * sub + 1, 0, 0)),
                _mod_spec(3 * sub + 2, tm, tile0=tile0),
                pl.BlockSpec((D_MODEL, tf), lambda i, f: (0, f)),
                pl.BlockSpec((D_MODEL, tf), lambda i, f: (0, f + nf)),
                pl.BlockSpec((tf, D_MODEL), lambda i, f: (f, 0))]
    args = [x_in, x_in, norm_g.reshape(1, D_MODEL), mod, mod, mod, mod, mod, w_gu, w_gu, w_down]
    if final_g is not None:
        in_specs.append(pl.BlockSpec((1, D_MODEL), lambda i, f: (0, 0)))
        args.append(final_g.reshape(1, D_MODEL))
    aliases = {}
    if prev is not None:
        aliases = {len(args): 0}
        in_specs.append(pl.BlockSpec(memory_space=pl.ANY))
        args.append(prev)
    return pl.pallas_call(
        functools.partial(_ffn_kernel, final_norm=final_g is not None),
        grid=(n_tiles, nf),
        in_specs=in_specs,
        out_specs=pl.BlockSpec((tm, D_MODEL), lambda i, f: (i + tile0, 0)),
        out_shape=jax.ShapeDtypeStruct((out_rows, D_MODEL), F32),
        scratch_shapes=[pltpu.VMEM((tm, D_MODEL), BF16), pltpu.VMEM((tm, D_MODEL), BF16),
                        pltpu.VMEM((4, ROW_CHUNK, D_MODEL), F32), pltpu.VMEM((tm, D_MODEL), F32),
                        pltpu.SemaphoreType.DMA(())],
        input_output_aliases=aliases,
        compiler_params=_cparams(("arbitrary", "arbitrary"), 56),
        name="ffn_half",
    )(*args)


NMM_TM = 1024
NMM_TN = 1024


def _nmm_kernel(x_ref, g_ref, sh_ref, sc_ref, w_ref, *rest, gla):
    if gla:
        wn_ref, wc_ref, o_ref, ob_ref, zc_ref, zs_ref, on_ref, h_ref, gs_ref = rest
    else:
        o_ref, h_ref, gs_ref = rest
    j = pl.program_id(1)

    @pl.when(j == 0)
    def _():
        _norm_mod_rows(x_ref, g_ref, sh_ref, sc_ref, h_ref, gs_ref)
        if gla:
            on_ref[...] = _dot(h_ref[...], wn_ref[...])

    if not gla:
        o_ref[...] = _dot(h_ref[...], w_ref[...]).astype(o_ref.dtype)
    else:
        @pl.when(j % 2 == 0)
        def _():
            o_ref[...] = _dot(h_ref[...], w_ref[...])

        @pl.when(j == 1)
        def _():
            ob_ref[...] = _dot(h_ref[...], w_ref[...]).astype(BF16)

        @pl.when(j == 3)
        def _():
            f = _dot(h_ref[...], w_ref[...]).astype(BF16)
            for grp in range(FNET_GROUPS):
                cols = slice(grp * FNET_GROUP_DIM, (grp + 1) * FNET_GROUP_DIM)
                z = _dot(f[:, cols], wc_ref[...])
                zc_ref[:, cols] = z[:, :FNET_GROUP_DIM].astype(BF16)
                zs_ref[:, cols] = z[:, FNET_GROUP_DIM:].astype(BF16)


def _nmm(xs, mod, sub, norm_g, w, w_narrow=None, w_chan=None, cast_ws=()):
    tm, tn = NMM_TM, NMM_TN
    n = w.shape[1]
    gla = w_narrow is not None
    in_specs = [pl.BlockSpec((tm, D_MODEL), lambda i, j: (i, 0)),
                pl.BlockSpec((1, D_MODEL), lambda i, j: (0, 0)),
                _mod_spec(3 * sub, tm), _mod_spec(3 * sub + 1, tm),
                pl.BlockSpec((D_MODEL, tn), lambda i, j: (0, j))]
    args = [xs, norm_g.reshape(1, D_MODEL), mod, mod, w]
    if gla:
        nn = w_narrow.shape[1]
        assert n == 4 * tn and tn == GLA_V_W == FNET_W
        in_specs += [pl.BlockSpec((D_MODEL, nn), lambda i, j: (0, 0)),
                     pl.BlockSpec(w_chan.shape, lambda i, j: (0, 0))]
        args += [w_narrow, w_chan]
        row_tile = pl.BlockSpec((tm, tn), lambda i, j: (i, 0))
        out_specs = [pl.BlockSpec((tm, tn), lambda i, j: (i, j // 2)), row_tile, row_tile, row_tile,
                     pl.BlockSpec((tm, nn), lambda i, j: (i, 0))]
        out_shape = [jax.ShapeDtypeStruct((N_TOK, n // 2), F32)] + [jax.ShapeDtypeStruct((N_TOK, tn), BF16)] * 3 + [
            jax.ShapeDtypeStruct((N_TOK, nn), F32)]
    else:
        out_specs = [pl.BlockSpec((tm, tn), lambda i, j: (i, j))]
        out_shape = [jax.ShapeDtypeStruct((N_TOK, n), BF16)]
    nj = n // tn
    flat = lambda i, j: i * nj + j
    casts = [_cast_specs(cw, (N_TOK // tm) * nj, flat) for cw in cast_ws]
    return pl.pallas_call(
        _hosting_casts(functools.partial(_nmm_kernel, gla=gla), len(args), len(out_specs),
                       [u for _, _, u in casts], flat, 2),
        grid=(N_TOK // tm, nj),
        in_specs=in_specs + [s for s, _, _ in casts],
        out_specs=out_specs + [s for s, _, _ in casts],
        out_shape=out_shape + [o for _, o, _ in casts],
        scratch_shapes=[pltpu.VMEM((tm, D_MODEL), BF16), pltpu.VMEM((2, ROW_CHUNK, D_MODEL), F32)],
        compiler_params=_cparams(("arbitrary", "arbitrary"), 56),
        name="norm_mod_matmul",
    )(*args, *cast_ws)


GLA_BLOCK = CTX_LEN
GLA_LAT_BLOCKS = SEQ // GLA_BLOCK
GLA_STEPS = 1 + GLA_LAT_BLOCKS
GLA_LR_PAD = 128


def _rope(x, cos, sin_signed):
    return x * cos + pltpu.roll(x, GLA_DK // 2, 1) * sin_signed


def _gla_kernel(qf, kf, vf, lf, cf, sf, qb, kb, vb, lb, cb, sb, gwf, gbf, gwb, gbb,
                of_ref, ob_ref, st_ref):
    @pl.when(pl.program_id(1) == 0)
    def _():
        st_ref[...] = jnp.zeros_like(st_ref)

    c = GLA_CHUNK
    n = GLA_BLOCK
    row = lax.broadcasted_iota(jnp.int32, (n, n), 0)
    col = lax.broadcasted_iota(jnp.int32, (n, n), 1)
    same_chunk = (row // c) == (col // c)
    row_c = lax.broadcasted_iota(jnp.int32, (c, c), 0)
    col_c = lax.broadcasted_iota(jnp.int32, (c, c), 1)
    scale = GLA_DK ** -0.5

    dirs = ((qf, kf, vf, lf, cf, sf, gwf, gbf, of_ref), (qb, kb, vb, lb, cb, sb, gwb, gbb, ob_ref))
    n_chunks = n // c
    heads = [(slice(h * GLA_DK, (h + 1) * GLA_DK), slice(h * GLA_DV, (h + 1) * GLA_DV)) for h in range(GLA_HEADS)]
    tris, decays = [], []
    for d, (q_ref, k_ref, v_ref, l_ref, c_ref, s_ref, gw_ref, gb_ref, o_ref) in enumerate(dirs):
        causal = (row >= col) if d == 0 else (row <= col)
        tri_b = jnp.where(same_chunk & causal, 1.0, 0.0).astype(BF16)
        tris.append((row_c >= col_c) if d == 0 else (row_c <= col_c))
        x = _dot3(l_ref[...], gw_ref[...]) + gb_ref[...]
        g = (jnp.minimum(x, 0.0) - jnp.log1p(jnp.exp(-jnp.abs(x)))) * (1.0 / GLA_TAU)
        bc_all = _dot_exact_lhs(tri_b, g)
        per_chunk = []
        for ci in (range(n_chunks) if d == 0 else range(n_chunks - 1, -1, -1)):
            rows = slice(ci * c, (ci + 1) * c)
            bc = bc_all[rows, :]
            mid = bc[c // 2:c // 2 + 1, :]
            last = bc[c - 1:c, :] if d == 0 else bc[0:1, :]
            per_chunk.append((rows, jnp.exp(bc), jnp.exp(bc - mid), jnp.exp(mid - bc), jnp.exp(last - bc),
                              jnp.exp(last)))
        decays.append(per_chunk)

    def local_part(k):
        out = {}
        for d, (q_ref, k_ref, v_ref, _, c_ref, s_ref, _, _, _) in enumerate(dirs):
            rows, e_q, e_qm, e_km, e_kl, e_l = decays[d][k]
            cos, sin = c_ref[rows, :], s_ref[rows, :]
            for h, (sk, sv) in enumerate(heads):
                qh = _rope(q_ref[rows, sk] * scale, cos, sin)
                kh = _rope(k_ref[rows, sk], cos, sin)
                vh = v_ref[rows, sv].astype(BF16)
                att = _dot_nt((qh * e_qm[:, sk]).astype(BF16), (kh * e_km[:, sk]).astype(BF16))
                kv = _dot_tn(vh, (kh * e_kl[:, sk]).astype(BF16))
                out[d, h] = ((qh * e_q[:, sk]).astype(BF16), jnp.where(tris[d], att, 0.0).astype(BF16), vh, kv,
                             e_l[:, sk])
        return out

    st = {(d, h): st_ref[d, h] for d in range(2) for h in range(GLA_HEADS)}
    ahead = 1
    pending = {k: local_part(k) for k in range(min(ahead, n_chunks))}
    for k in range(n_chunks):
        if k + ahead < n_chunks:
            pending[k + ahead] = local_part(k + ahead)
        local = pending.pop(k)
        for d in range(2):
            rows = decays[d][k][0]
            for h, (sk, sv) in enumerate(heads):
                qe, att, vh, kv, e_l = local[d, h]
                dirs[d][-1][rows, sv] = _dot_nt(qe, st[d, h].astype(BF16)) + _dot(att, vh)
                st[d, h] = st[d, h] * e_l + kv
    for (d, h), val in st.items():
        st_ref[d, h] = val


def _gla(pf, pb, lr, cos_t, sin_t, gwf, gbf, gwb, gbb, cast_ws):
    c = GLA_BLOCK
    ctx0 = N_LAT // c

    def fwd_blk(b, s):
        return jnp.where(s == 0, ctx0 + b, GLA_LAT_BLOCKS * b + s - 1)

    def bwd_blk(b, s):
        return jnp.where(s == 0, ctx0 + b, GLA_LAT_BLOCKS * b + (GLA_STEPS - 1 - s))

    def fwd_rope(b, s):
        return jnp.where(s == 0, GLA_LAT_BLOCKS, s - 1)

    def bwd_rope(b, s):
        return jnp.where(s == 0, GLA_LAT_BLOCKS, GLA_STEPS - 1 - s)

    def dir_specs(blk, rope):
        return [pl.BlockSpec((c, GLA_QK_W), lambda b, s: (blk(b, s), 0)),
                pl.BlockSpec((c, GLA_QK_W), lambda b, s: (blk(b, s), 1)),
                pl.BlockSpec((c, GLA_V_W), lambda b, s: (blk(b, s), 0)),
                pl.BlockSpec((c, GLA_LR_PAD), lambda b, s: (blk(b, s), 0)),
                pl.BlockSpec((c, GLA_DK), lambda b, s: (rope(b, s), 0)),
                pl.BlockSpec((c, GLA_DK), lambda b, s: (rope(b, s), 0))]

    const = lambda shape: pl.BlockSpec(shape, lambda b, s: (0, 0))
    in_specs = (dir_specs(fwd_blk, fwd_rope) + dir_specs(bwd_blk, bwd_rope)
                + [const((GLA_LR_PAD, GLA_QK_W)), const((1, GLA_QK_W)),
                   const((GLA_LR_PAD, GLA_QK_W)), const((1, GLA_QK_W))])
    flat = lambda b, s: b * GLA_STEPS + s
    casts = [_cast_specs(w, BATCH * GLA_STEPS, flat) for w in cast_ws]
    args = [pf, pf, pb, lr, cos_t, sin_t, pf, pf, pb, lr, cos_t, sin_t, gwf, gbf, gwb, gbb]
    return pl.pallas_call(
        _hosting_casts(_gla_kernel, len(args), 2, [u for _, _, u in casts], flat, 2),
        grid=(BATCH, GLA_STEPS),
        in_specs=in_specs + [s for s, _, _ in casts],
        out_specs=[pl.BlockSpec((c, GLA_V_W), lambda b, s: (fwd_blk(b, s), 0)),
                   pl.BlockSpec((c, GLA_V_W), lambda b, s: (bwd_blk(b, s), 0))] + [s for s, _, _ in casts],
        out_shape=[jax.ShapeDtypeStruct((N_TOK, GLA_V_W), F32)] * 2 + [o for _, o, _ in casts],
        scratch_shapes=[pltpu.VMEM((2, GLA_HEADS, GLA_DV, GLA_DK), F32)],
        compiler_params=_cparams(("arbitrary", "arbitrary"), 40),
        name="gla_scan",
    )(*args, *cast_ws)


def _tdft_kernel(c_ref, s_ref, zc_ref, zs_ref, *rest):
    o_ref = rest[-1]
    o_ref[...] = _dot(c_ref[...], zc_ref[...]) - _dot(s_ref[...], zs_ref[...])


def _time_dft(cmat, smat, zc, zs, t_len, tm, tn, row0, prev=None, cast_ws=()):
    zb0 = row0 // t_len
    ob0 = row0 // tm
    mt = t_len // tm
    in_specs = [pl.BlockSpec((tm, t_len), lambda b, n, m: (m, 0)),
                pl.BlockSpec((tm, t_len), lambda b, n, m: (m, 0)),
                pl.BlockSpec((t_len, tn), lambda b, n, m: (zb0 + b, n)),
                pl.BlockSpec((t_len, tn), lambda b, n, m: (zb0 + b, n))]
    args = [cmat, smat, zc, zs]
    aliases = {}
    if prev is not None:
        in_specs.append(pl.BlockSpec(memory_space=pl.ANY))
        args.append(prev)
        aliases = {4: 0}
    nt = FNET_W // tn
    flat = lambda b, n, m: (b * nt + n) * mt + m
    casts = [_cast_specs(w, BATCH * nt * mt, flat) for w in cast_ws]
    return pl.pallas_call(
        _hosting_casts(_tdft_kernel, len(args), 1, [u for _, _, u in casts], flat, 3),
        grid=(BATCH, nt, mt),
        in_specs=in_specs + [s for s, _, _ in casts],
        out_specs=[pl.BlockSpec((tm, tn), lambda b, n, m: (ob0 + b * mt + m, n))] + [s for s, _, _ in casts],
        out_shape=[jax.ShapeDtypeStruct((N_TOK, FNET_W), F32)] + [o for _, o, _ in casts],
        input_output_aliases=aliases,
        compiler_params=_cparams(("arbitrary", "arbitrary", "arbitrary"), 48),
        name="fnet_time_dft",
    )(*args, *cast_ws)


FOLD_BLK = 256


def _tdft_fold_kernel(c_ref, s_ref, zc_ref, zs_ref, j1_ref, j2_ref, o_ref, zcf_ref, zsf_ref):
    half, blk = zcf_ref.shape[0], FOLD_BLK
    nblk = 2 * half // blk

    @pl.when(pl.program_id(2) == 0)
    def _():
        for kb in range(half // blk):
            rows = slice(kb * blk, (kb + 1) * blk)
            mirror = slice((nblk - 1 - kb) * blk, (nblk - kb) * blk)
            after = slice((nblk - kb) * blk, (nblk - kb + 1) * blk)
            for z_ref, zf_ref, sign in ((zc_ref, zcf_ref, 1.0), (zs_ref, zsf_ref, -1.0)):
                partner = _dot(j1_ref[...], z_ref[mirror, :])
                if kb > 0:
                    partner = partner + _dot(j2_ref[...], z_ref[after, :])
                zf_ref[rows, :] = (z_ref[rows, :].astype(F32) + sign * partner).astype(BF16)

    tm = o_ref.shape[0]
    j = pl.program_id(2) * tm + lax.broadcasted_iota(jnp.int32, (tm, 1), 0)
    sign = jnp.where(j % 2 == 0, SEQ ** -0.5, -(SEQ ** -0.5))
    mid = zc_ref[half:half + 1, :].astype(F32)
    o_ref[...] = _dot(c_ref[...], zcf_ref[...]) - _dot(s_ref[...], zsf_ref[...]) + sign * mid


def _time_dft_folded(cmat, smat, zc, zs, tm, tn, cast_ws=()):
    half = SEQ // 2
    mt = SEQ // tm
    nt = FNET_W // tn
    i = np.arange(FOLD_BLK)
    j1 = jnp.asarray((i[:, None] + i[None, :] == FOLD_BLK) & (i[:, None] > 0), BF16)
    j2 = jnp.asarray((i[:, None] == 0) & (i[None, :] == 0), BF16)
    args = [cmat, smat, zc, zs, j1, j2]
    in_specs = [pl.BlockSpec((tm, half), lambda b, n, m: (m, 0)),
                pl.BlockSpec((tm, half), lambda b, n, m: (m, 0)),
                pl.BlockSpec((SEQ, tn), lambda b, n, m: (b, n)),
                pl.BlockSpec((SEQ, tn), lambda b, n, m: (b, n)),
                pl.BlockSpec((FOLD_BLK, FOLD_BLK), lambda b, n, m: (0, 0)),
                pl.BlockSpec((FOLD_BLK, FOLD_BLK), lambda b, n, m: (0, 0))]
    flat = lambda b, n, m: (b * nt + n) * mt + m
    casts = [_cast_specs(w, BATCH * nt * mt, flat) for w in cast_ws]
    return pl.pallas_call(
        _hosting_casts(_tdft_fold_kernel, len(args), 1, [u for _, _, u in casts], flat, 3),
        grid=(BATCH, nt, mt),
        in_specs=in_specs + [s for s, _, _ in casts],
        out_specs=[pl.BlockSpec((tm, tn), lambda b, n, m: (b * mt + m, n))] + [s for s, _, _ in casts],
        out_shape=[jax.ShapeDtypeStruct((N_TOK, FNET_W), F32)] + [o for _, o, _ in casts],
        scratch_shapes=[pltpu.VMEM((half, tn), BF16), pltpu.VMEM((half, tn), BF16)],
        compiler_params=_cparams(("arbitrary", "arbitrary", "arbitrary"), 48),
        name="fnet_time_dft_folded",
    )(*args, *cast_ws)


def _dft_tables(n):
    k = np.arange(n)
    ang = 2.0 * np.pi * ((k[:, None] * k[None, :]) % n) / n
    return np.cos(ang) / np.sqrt(n), np.sin(ang) / np.sqrt(n)


def _dft_table_kernel(ac_ref, as_ref, bc_ref, bs_ref, co_ref, so_ref):
    ac, as_ = ac_ref[0], as_ref[0]
    bc, bs = bc_ref[...], bs_ref[...]
    co_ref[...] = (ac * bc - as_ * bs).astype(BF16)
    so_ref[...] = (ac * bs + as_ * bc).astype(BF16)


def _big_dft_tables():
    r = GRID_W
    k = np.arange(SEQ)
    j = np.arange(r)
    pa = 2.0 * np.pi * ((j[:, None] * r * k[None, :]) % SEQ) / SEQ
    pb = 2.0 * np.pi * ((j[:, None] * k[None, :]) % SEQ) / SEQ
    ac = jnp.asarray(np.cos(pa).reshape(r, 1, SEQ), F32)
    as_ = jnp.asarray(np.sin(pa).reshape(r, 1, SEQ), F32)
    bc = jnp.asarray(np.cos(pb) * SEQ ** -0.5, F32)
    bs = jnp.asarray(np.sin(pb) * SEQ ** -0.5, F32)
    row = pl.BlockSpec((1, 1, SEQ), lambda i: (i, 0, 0))
    full = pl.BlockSpec((r, SEQ), lambda i: (0, 0))
    return pl.pallas_call(
        _dft_table_kernel,
        grid=(r,),
        in_specs=[row, row, full, full],
        out_specs=[pl.BlockSpec((r, SEQ), lambda i: (i, 0))] * 2,
        out_shape=[jax.ShapeDtypeStruct((SEQ, SEQ), BF16)] * 2,
        compiler_params=_cparams(("parallel",), 16),
        name="dft_tables",
    )(ac, as_, bc, bs)


GO_TM = 512
GO_TN = 2048


def _glaout_kernel(of_ref, ob_ref, r_ref, fm_ref, gn_ref, x_ref, gt_ref, w_ref, o_ref, h_ref):
    @pl.when(pl.program_id(1) == 0)
    def _():
        heads = [slice(h * GLA_DV, (h + 1) * GLA_DV) for h in range(GLA_HEADS)]

        def body(chunks):
            inv = [[_inv_rms(of_ref[rows, sv] + ob_ref[rows, sv]) for sv in heads] for rows in chunks]
            for rows, inv_c in zip(chunks, inv):
                for sv, r in zip(heads, inv_c):
                    o = (of_ref[rows, sv] + ob_ref[rows, sv]) * r
                    gate = r_ref[rows, sv]
                    h_ref[rows, sv] = (o * gn_ref[:, sv] * (gate * _sigmoid(gate))).astype(BF16)
                h_ref[rows, GLA_V_W:] = fm_ref[rows, :].astype(BF16)
        _row_groups(of_ref.shape[0], body, group=4)

    o_ref[...] = x_ref[...] + gt_ref[0] * _dot(h_ref[...], w_ref[...])


def _glaout(of, ob, pf, fm, g_norm, xs, mod, w_out):
    tm, tn = GO_TM, GO_TN
    half = lambda col: pl.BlockSpec((tm, GLA_V_W), lambda i, j: (i, col))
    return pl.pallas_call(
        _glaout_kernel,
        grid=(N_TOK // tm, D_MODEL // tn),
        in_specs=[half(0), half(0), half(1), half(0),
                  pl.BlockSpec((1, GLA_V_W), lambda i, j: (0, 0)),
                  pl.BlockSpec((tm, tn), lambda i, j: (i, j)),
                  _mod_spec(5, tm, tn, col=True),
                  pl.BlockSpec((D_MODEL, tn), lambda i, j: (0, j))],
        out_specs=pl.BlockSpec((tm, tn), lambda i, j: (i, j)),
        out_shape=jax.ShapeDtypeStruct((N_TOK, D_MODEL), F32),
        scratch_shapes=[pltpu.VMEM((tm, D_MODEL), BF16)],
        compiler_params=_cparams(("parallel", "arbitrary"), 56),
        name="gla_fnet_out",
    )(of, ob, pf, fm, g_norm.reshape(1, GLA_V_W), xs, mod, w_out)


NA_HROWS = NA_QROWS // 2
NA_WROWS = NA_HROWS + NA_KH
NA_TQ = NA_QROWS * GRID_W
NA_HQ = NA_HROWS * GRID_W
NA_WK = NA_WROWS * GRID_W
NA_KBLK = NA_HROWS * GRID_W
NA_WBLKS = NA_WK // NA_KBLK
NA_MASKED = 2 * NA_KH - 1
NA_ROW_BLOCKS = (SEQ // GRID_W) // NA_QROWS
NA_CASES = (0, 1, NA_ROW_BLOCKS - 1)


def _na_window_row0(j, half):
    q0 = NA_QROWS * j + NA_HROWS * half
    return int(np.clip(q0 - NA_KH // 2, 0, SEQ // GRID_W - NA_WROWS))


def _na_bias_slots():
    rows = SEQ // GRID_W
    slots = np.full((len(NA_CASES), 2, NA_HROWS, NA_WROWS), NA_MASKED, np.int32)
    for case, j in enumerate(NA_CASES):
        for half in range(2):
            w0 = _na_window_row0(j, half)
            for i in range(NA_HROWS):
                qr = NA_QROWS * j + NA_HROWS * half + i
                r0 = int(np.clip(qr - NA_KH // 2, 0, rows - NA_KH))
                for l in range(NA_WROWS):
                    kr = w0 + l
                    if r0 <= kr < r0 + NA_KH:
                        slots[case, half, i, l] = kr - qr + NA_KH - 1
    return slots


def _na_kernel(q_ref, *refs):
    nb = NA_WBLKS
    ka, kb, va, vb = refs[0:nb], refs[nb:2 * nb], refs[2 * nb:3 * nb], refs[3 * nb:4 * nb]
    kc_ref, vc_ref, td_ref, o_ref, bias_ref = refs[4 * nb:]
    j = pl.program_id(1)

    @pl.when((j == 0) & (pl.program_id(2) == 0))
    def _():
        slots = _na_bias_slots()
        left = lax.broadcasted_iota(jnp.int32, (GRID_W, 2 * GRID_W), 1) < GRID_W
        for case in range(len(NA_CASES)):
            for half in range(2):
                for hh in range(2):
                    dst = (case * 2 + half) * 2 + hh
                    for i in range(NA_HROWS):
                        for p in range(NA_WROWS // 2):
                            sl = int(slots[case, half, i, 2 * p])
                            sr = int(slots[case, half, i, 2 * p + 1])
                            tile = td_ref[hh, sl] if sl == sr else jnp.where(left, td_ref[hh, sl], td_ref[hh, sr])
                            bias_ref[dst, i * GRID_W:(i + 1) * GRID_W, p * 2 * GRID_W:(p + 1) * 2 * GRID_W] = tile

    case = jnp.where(j == 0, 0, jnp.where(j == NA_ROW_BLOCKS - 1, 2, 1))
    kc = kc_ref[...]
    vc = vc_ref[...]
    lane = lax.broadcasted_iota(jnp.int32, (NA_HQ, 2 * NA_HEAD_DIM), 1)
    kv_refs = ((ka, va), (kb, vb))
    items = [(half, hh) for half in range(2) for hh in range(2)]

    def scores(half, hh):
        q = q_ref[half * NA_HQ:(half + 1) * NA_HQ, :]
        qm = jnp.where((lane // NA_HEAD_DIM) == hh, q, jnp.zeros_like(q))
        k = jnp.concatenate([r[...] for r in kv_refs[half][0]], axis=0)
        return _dot_nt(qm, k) + bias_ref[(case * 2 + half) * 2 + hh], _dot_nt(qm, kc)

    def softmax(s_loc, s_ctx):
        m = jnp.maximum(jnp.max(s_loc, axis=-1, keepdims=True), jnp.max(s_ctx, axis=-1, keepdims=True))
        p_loc = jnp.exp2(s_loc - m)
        p_ctx = jnp.exp2(s_ctx - m)
        denom = jnp.sum(p_loc, axis=-1, keepdims=True) + jnp.sum(p_ctx, axis=-1, keepdims=True)
        return p_loc.astype(BF16), p_ctx.astype(BF16), denom

    def weighted(half, p_loc, p_ctx, denom):
        v = jnp.concatenate([r[...] for r in kv_refs[half][1]], axis=0)
        return (_dot(p_loc, v) + _dot(p_ctx, vc)) / denom

    s, p, o = {}, {}, {}
    for t in range(len(items) + 2):
        if t < len(items):
            s[t] = scores(*items[t])
        if 0 <= t - 1 < len(items):
            p[t - 1] = softmax(*s.pop(t - 1))
        if 0 <= t - 2 < len(items):
            o[t - 2] = weighted(items[t - 2][0], *p.pop(t - 2))
    for half in range(2):
        first = (lane // NA_HEAD_DIM) == 0
        out = jnp.where(first, o[2 * half], o[2 * half + 1])
        o_ref[half * NA_HQ:(half + 1) * NA_HQ, :] = out.astype(o_ref.dtype)


def _na(qkv, tiles, cast_ws=()):
    hp_n = NA_HEADS // 2
    jn = NA_ROW_BLOCKS
    kb_per_img = SEQ // NA_KBLK
    last_blk = kb_per_img - NA_WBLKS

    def kv_spec(col0, half, t):
        return pl.BlockSpec((NA_KBLK, 128), lambda hp, j, b: (
            b * kb_per_img + jnp.clip(2 * j + half - 1, 0, last_blk) + t, col0 + hp))

    def kv_specs(col0):
        return [kv_spec(col0, half, t) for half in range(2) for t in range(NA_WBLKS)]

    ctx0 = N_LAT // CTX_LEN
    in_specs = ([pl.BlockSpec((NA_TQ, 128), lambda hp, j, b: (b * jn + j, hp))]
                + kv_specs(hp_n) + kv_specs(2 * hp_n)
                + [pl.BlockSpec((CTX_LEN, 128), lambda hp, j, b: (ctx0 + b, hp_n + hp)),
                   pl.BlockSpec((CTX_LEN, 128), lambda hp, j, b: (ctx0 + b, 2 * hp_n + hp)),
                   pl.BlockSpec((2, 2 * NA_KH, GRID_W, 2 * GRID_W), lambda hp, j, b: (hp, 0, 0, 0))])
    flat = lambda hp, j, b: (hp * jn + j) * BATCH + b
    casts = [_cast_specs(w, hp_n * jn * BATCH, flat) for w in cast_ws]
    return pl.pallas_call(
        _hosting_casts(_na_kernel, len(in_specs), 1, [u for _, _, u in casts], flat, 3),
        grid=(hp_n, jn, BATCH),
        in_specs=in_specs + [s for s, _, _ in casts],
        out_specs=[pl.BlockSpec((NA_TQ, 128), lambda hp, j, b: (b * jn + j, hp))] + [s for s, _, _ in casts],
        out_shape=[jax.ShapeDtypeStruct((N_LAT, D_MODEL), BF16)] + [o for _, o, _ in casts],
        scratch_shapes=[pltpu.VMEM((len(NA_CASES) * 4, NA_HQ, NA_WK), F32)],
        compiler_params=_cparams(("arbitrary", "arbitrary", "arbitrary"), 48),
        name="neighbourhood_attention",
    )(qkv, *([qkv] * (4 * NA_WBLKS)), qkv, qkv, tiles, *cast_ws)


def _na_bias_tiles(rpb):
    w = GRID_W
    col = np.arange(w)
    cs = np.clip(col - NA_KW // 2, 0, w - NA_KW)
    col_mask = (col[None, :] >= cs[:, None]) & (col[None, :] < cs[:, None] + NA_KW)
    dc_idx = np.clip(col[None, :] - col[:, None] + NA_KW - 1, 0, 2 * NA_KW - 2)
    t1 = (rpb.astype(F32) * LOG2_E)[:, :, dc_idx]
    t1 = jnp.where(col_mask[None, None], t1, NEG_BIG)
    t1 = jnp.concatenate([t1, jnp.full((NA_HEADS, 1, w, w), NEG_BIG, F32)], axis=1)
    return jnp.concatenate([t1, t1], axis=-1)


MR_TM = 1024
MR_TN = 1024


def _mmres_kernel(a_ref, w_ref, x_ref, gt_ref, o_ref):
    o_ref[...] = x_ref[...] + gt_ref[0] * _dot(a_ref[...], w_ref[...])


def _mm_res(a, w, xs, mod, n_rows):
    tm, tn = MR_TM, MR_TN
    return pl.pallas_call(
        _mmres_kernel,
        grid=(n_rows // tm, D_MODEL // tn),
        in_specs=[pl.BlockSpec((tm, D_MODEL), lambda i, j: (i, 0)),
                  pl.BlockSpec((D_MODEL, tn), lambda i, j: (0, j)),
                  pl.BlockSpec((tm, tn), lambda i, j: (i, j)),
                  _mod_spec(5, tm, tn, col=True)],
        out_specs=pl.BlockSpec((tm, tn), lambda i, j: (i, j)),
        out_shape=jax.ShapeDtypeStruct((n_rows, D_MODEL), F32),
        compiler_params=_cparams(("parallel", "arbitrary"), 40),
        name="matmul_gated_residual",
    )(a, w, xs, mod)


def _rope_tables():
    half = GLA_DK // 4
    inv_freq = ROPE_BASE ** (-jnp.arange(half, dtype=F32) / half)
    t = jnp.arange(SEQ)
    ang_r = (t // GRID_W).astype(F32)[:, None] * inv_freq[None, :]
    ang_c = (t % GRID_W).astype(F32)[:, None] * inv_freq[None, :]
    cr, sr, cc, sc = jnp.cos(ang_r), jnp.sin(ang_r), jnp.cos(ang_c), jnp.sin(ang_c)
    cos_t = jnp.concatenate([cr, cc, cr, cc], axis=-1)
    sin_t = jnp.concatenate([-sr, -sc, sr, sc], axis=-1)
    cos_t = jnp.concatenate([cos_t, jnp.ones((GLA_BLOCK, GLA_DK), F32)], axis=0)
    sin_t = jnp.concatenate([sin_t, jnp.zeros((GLA_BLOCK, GLA_DK), F32)], axis=0)
    return cos_t, sin_t


def _pair_layout(w):
    lead = w.shape[:-1]
    quarter = GLA_DK // 4
    w = w.reshape(*lead, GLA_HEADS, 2, 2, quarter)
    return jnp.swapaxes(w, -2, -3).reshape(*lead, GLA_QK_W)


def kernel(x, c, ctx, c_ctx, l0_ada_w, l0_ada_b, l0_norm_ffn1, l0_norm_mix, l0_norm_ffn2, l0_ffn1_w_gu, l0_ffn1_w_down, l0_ffn2_w_gu, l0_ffn2_w_down, l0_w_in, l0_gla_gate_w_fwd, l0_gla_gate_b_fwd, l0_gla_gate_w_bwd, l0_gla_gate_b_bwd, l0_gla_norm, l0_w_out, l1_ada_w, l1_ada_b, l1_norm_ffn1, l1_norm_mix, l1_norm_ffn2, l1_ffn1_w_gu, l1_ffn1_w_down, l1_ffn2_w_gu, l1_ffn2_w_down, l1_w_qkv, l1_rpb, l1_w_out, norm_out):
    bf = lambda w: w.astype(BF16)
    cc = jnp.concatenate([c, c_ctx[None, :], jnp.zeros((8 - BATCH - 1, D_MODEL), F32)], axis=0)

    mod0 = _ada(cc, l0_ada_w, l0_ada_b)
    w_gu, w_down = bf(l0_ffn1_w_gu), bf(l0_ffn1_w_down)
    xs = _ffn(x.reshape(N_LAT, D_MODEL), mod0, 0, l0_norm_ffn1, w_gu, w_down, out_rows=N_TOK)
    xs = _ffn(ctx.reshape(N_CTX, D_MODEL), mod0, 0, l0_norm_ffn1, w_gu, w_down, out_rows=N_TOK,
              tile0=N_LAT // FFN_TM, prev=xs)

    a0 = 2 * GLA_QK_W + GLA_V_W
    a1 = a0 + 2 * GLA_LOWRANK
    w_main = bf(jnp.concatenate([_pair_layout(l0_w_in[:, :GLA_QK_W]), _pair_layout(l0_w_in[:, GLA_QK_W:2 * GLA_QK_W]),
                                 l0_w_in[:, 2 * GLA_QK_W:a0], l0_w_in[:, a1:]], axis=1))
    w_lr = bf(jnp.pad(l0_w_in[:, a0:a1], ((0, 0), (0, GLA_LR_PAD - 2 * GLA_LOWRANK))))
    cc_np, sc_np = _dft_tables(FNET_GROUP_DIM)
    wc = jnp.asarray(np.concatenate([cc_np, sc_np], axis=1), F32).astype(BF16)
    pf, pb, zc, zs, lr = _nmm(xs, mod0, 1, l0_norm_mix, w_main, w_narrow=w_lr, w_chan=wc)

    cos_t, sin_t = _rope_tables()
    gwf = jnp.pad(_pair_layout(l0_gla_gate_w_fwd), ((0, GLA_LR_PAD - GLA_LOWRANK), (0, 0)))
    gwb = jnp.pad(_pair_layout(l0_gla_gate_w_bwd), ((GLA_LOWRANK, GLA_LR_PAD - 2 * GLA_LOWRANK), (0, 0)))
    of, ob, w_gu2, w_down2 = _gla(pf, pb, lr, cos_t, sin_t, gwf, _pair_layout(l0_gla_gate_b_fwd).reshape(1, -1),
                                  gwb, _pair_layout(l0_gla_gate_b_bwd).reshape(1, -1),
                                  cast_ws=(l0_ffn2_w_gu, l0_ffn2_w_down))

    cos_big, sin_big = _big_dft_tables()
    fm, w_gu3, w_down3 = _time_dft_folded(cos_big, sin_big, zc, zs, 512, 512, cast_ws=(l1_ffn1_w_gu, l1_ffn1_w_down))
    ct_np, st_np = _dft_tables(CTX_LEN)
    fm = _time_dft(jnp.asarray(ct_np, F32).astype(BF16), jnp.asarray(st_np, F32).astype(BF16),
                   zc, zs, CTX_LEN, CTX_LEN, 512, N_LAT, prev=fm)[0]

    xs = _glaout(of, ob, pf, fm, l0_gla_norm, xs, mod0, bf(l0_w_out))
    xs = _ffn(xs, mod0, 2, l0_norm_ffn2, w_gu2, w_down2, out_rows=N_TOK)

    mod1 = _ada(cc, l1_ada_w, l1_ada_b)
    xs = _ffn(xs, mod1, 0, l1_norm_ffn1, w_gu3, w_down3, out_rows=N_TOK)
    w_qkv = bf(jnp.concatenate([l1_w_qkv[:, :D_MODEL] * NA_Q_SCALE, l1_w_qkv[:, D_MODEL:]], axis=1))
    qkv, w_gu4, w_down4 = _nmm(xs, mod1, 1, l1_norm_mix, w_qkv, cast_ws=(l1_ffn2_w_gu, l1_ffn2_w_down))
    ao = _na(qkv, _na_bias_tiles(l1_rpb))[0]
    xl = _mm_res(ao, bf(l1_w_out), xs, mod1, N_LAT)
    out = _ffn(xl, mod1, 2, l1_norm_ffn2, w_gu4, w_down4, out_rows=N_LAT, final_g=norm_out)
    return out.reshape(BATCH, SEQ, D_MODEL)
```

```python
import functools

import numpy as np
import jax
import jax.numpy as jnp
from jax import lax
from jax.experimental import pallas as pl
from jax.experimental.pallas import tpu as pltpu

D_MODEL = 2048
BATCH = 4
SEQ = 4096
GRID_W = 64
CTX_LEN = 256
N_MOD = 9
D_FF = 5632
RMS_EPS = 1e-6
ROPE_BASE = 10000.0

GLA_HEADS = 4
GLA_DK = 128
GLA_DV = 256
GLA_LOWRANK = 16
GLA_TAU = 16.0
GLA_CHUNK = 64
FNET_GROUPS = 4
FNET_GROUP_DIM = 256
GLA_QK_W = GLA_HEADS * GLA_DK
GLA_V_W = GLA_HEADS * GLA_DV
FNET_W = FNET_GROUPS * FNET_GROUP_DIM

NA_HEADS = 32
NA_HEAD_DIM = 64
NA_KH = 8
NA_KW = 16
NA_QROWS = 8
NEG_BIG = -1e30
LOG2_E = 1.4426950408889634
NA_Q_SCALE = NA_HEAD_DIM ** -0.5 * LOG2_E

N_LAT = BATCH * SEQ
N_CTX = BATCH * CTX_LEN
N_TOK = N_LAT + N_CTX

F32 = jnp.float32
BF16 = jnp.bfloat16
MIB = 1024 * 1024


def _cparams(sem, vmem_mib):
    return pltpu.CompilerParams(dimension_semantics=sem, vmem_limit_bytes=vmem_mib * MIB)


def _dot(a, b):
    return jnp.dot(a, b, preferred_element_type=F32)


def _dot_nt(a, b):
    return lax.dot_general(a, b, (((1,), (1,)), ((), ())), preferred_element_type=F32)


def _dot_tn(a, b):
    return lax.dot_general(a, b, (((0,), (0,)), ((), ())), preferred_element_type=F32)


def _split2(x):
    hi = x.astype(BF16)
    lo = (x - hi.astype(F32)).astype(BF16)
    return hi, lo


def _dot3(a, b):
    ah, al = _split2(a)
    bh, bl = _split2(b)
    return _dot(ah, bh) + (_dot(al, bh) + _dot(ah, bl))


def _dot_exact_lhs(l_bf16, x):
    x1 = x.astype(BF16)
    r1 = x - x1.astype(F32)
    x2 = r1.astype(BF16)
    x3 = (r1 - x2.astype(F32)).astype(BF16)
    return _dot(l_bf16, x1) + (_dot(l_bf16, x2) + _dot(l_bf16, x3))


def _sigmoid(x):
    return 1.0 / (1.0 + jnp.exp(-x))


def _rms(x, g):
    ms = jnp.mean(x * x, axis=-1, keepdims=True)
    return x * lax.rsqrt(ms + RMS_EPS) * g


ROW_CHUNK = 16
ROW_GROUP = 8


def _row_groups(n_rows, body, group=ROW_GROUP):
    span = ROW_CHUNK * group

    def step(t, carry):
        base = t * span
        body([pl.ds(pl.multiple_of(base + c * ROW_CHUNK, ROW_CHUNK), ROW_CHUNK) for c in range(group)])
        return carry
    lax.fori_loop(0, n_rows // span, step, 0)


def _inv_rms(x):
    return lax.rsqrt(jnp.mean(x * x, axis=-1, keepdims=True) + RMS_EPS)


def _norm_mod_rows(x_ref, g_ref, sh_ref, sc_ref, h_ref, gs_ref, zero_ref=None):
    d = x_ref.shape[1]
    gs_ref[0] = jnp.broadcast_to(g_ref[...] * (1.0 + sc_ref[0]), (ROW_CHUNK, d))
    gs_ref[1] = jnp.broadcast_to(sh_ref[0], (ROW_CHUNK, d))

    def body(chunks):
        inv = [_inv_rms(x_ref[rows, :]) for rows in chunks]
        for rows, r in zip(chunks, inv):
            h_ref[rows, :] = (x_ref[rows, :] * r * gs_ref[0] + gs_ref[1]).astype(BF16)
            if zero_ref is not None:
                zero_ref[rows, :] = jnp.zeros((ROW_CHUNK, zero_ref.shape[1]), zero_ref.dtype)
    _row_groups(x_ref.shape[0], body)


def _mod_row(i, tm):
    return jnp.minimum(i // (SEQ // tm), BATCH)


def _mod_spec(j, tm, width=D_MODEL, col=None, tile0=0):
    if col is None:
        return pl.BlockSpec((1, 1, width), lambda i, *_: (_mod_row(i + tile0, tm) * N_MOD + j, 0, 0))
    return pl.BlockSpec((1, 1, width), lambda i, n, *_: (_mod_row(i + tile0, tm) * N_MOD + j, 0, n))


CAST_ROW_ALIGN = 16


def _cast_plan(rows, steps):
    per = pl.cdiv(pl.cdiv(rows, steps), CAST_ROW_ALIGN) * CAST_ROW_ALIGN
    while rows % per:
        per += CAST_ROW_ALIGN
    return per, rows // per


def _cast_specs(w, steps, flat_index):
    per, used = _cast_plan(w.shape[0], steps)
    spec = pl.BlockSpec((per, w.shape[1]), lambda *g: (jnp.minimum(flat_index(*g), used - 1), 0))
    return spec, jax.ShapeDtypeStruct(w.shape, BF16), used


def _hosting_casts(kernel_fn, n_in, n_out, used, flat_index, grid_rank):
    nc = len(used)

    def wrapped(*refs):
        a, b, c = n_in + nc, n_in + nc + n_out, n_in + nc + n_out + nc
        kernel_fn(*refs[:n_in], *refs[a:b], *refs[c:])
        t = flat_index(*[pl.program_id(d) for d in range(grid_rank)])
        for n_used, src_ref, dst_ref in zip(used, refs[n_in:a], refs[b:c]):
            @pl.when(t < n_used)
            def _(src_ref=src_ref, dst_ref=dst_ref):
                dst_ref[...] = src_ref[...].astype(BF16)
    return wrapped


ADA_TN = 1024


def _ada_kernel(c_ref, w_ref, b_ref, o_ref):
    c = c_ref[...]
    s = (c * _sigmoid(c)).astype(BF16)
    o_ref[...] = _dot(s, w_ref[...].astype(BF16)) + b_ref[...]


def _ada(cc, w, b):
    n = w.shape[1]
    out = pl.pallas_call(
        _ada_kernel,
        grid=(n // ADA_TN,),
        in_specs=[pl.BlockSpec((8, D_MODEL), lambda j: (0, 0)),
                  pl.BlockSpec((D_MODEL, ADA_TN), lambda j: (0, j)),
                  pl.BlockSpec((1, ADA_TN), lambda j: (0, j))],
        out_specs=pl.BlockSpec((8, ADA_TN), lambda j: (0, j)),
        out_shape=jax.ShapeDtypeStruct((8, n), F32),
        compiler_params=_cparams(("arbitrary",), 40),
        name="ada_mod",
    )(cc, w, b.reshape(1, n))
    return out.reshape(8 * N_MOD, 1, D_MODEL)


FFN_TM = 1024
FFN_TF = 512
FFN_SLAB = 128


def _ffn_kernel(x_hbm, xn_ref, g_ref, sh_ref, sc_ref, shn_ref, scn_ref, gt_ref, wg_ref, wu_ref, wd_ref, *rest,
                final_norm):
    go_ref = rest[0] if final_norm else None
    o_ref, h0_ref, h1_ref, gs_ref, x_ref, x_sem = rest[-6:]
    i = pl.program_id(0)
    f = pl.program_id(1)
    last = pl.num_programs(1) - 1
    tm, d = x_ref.shape

    def x_copy():
        return pltpu.make_async_copy(x_hbm.at[pl.ds(pl.multiple_of(i * tm, tm), tm), :], x_ref, x_sem)

    @pl.when(f == 0)
    def _():
        x_copy().start()

    @pl.when((i == 0) & (f == 0))
    def _():
        x_copy().wait()
        _norm_mod_rows(x_ref, g_ref, sh_ref, sc_ref, h0_ref, gs_ref)

    def step(h_ref, hn_ref, first):
        gs_ref[0] = jnp.broadcast_to(g_ref[...] * (1.0 + scn_ref[0]), (ROW_CHUNK, d))
        gs_ref[1] = jnp.broadcast_to(shn_ref[0], (ROW_CHUNK, d))
        slab0 = jnp.minimum(f, tm // FFN_SLAB - 1) * FFN_SLAB
        chunks = [pl.ds(k * ROW_CHUNK, ROW_CHUNK) for k in range(FFN_SLAB // ROW_CHUNK)]
        inv = [_inv_rms(xn_ref[rows, :]) for rows in chunks]
        for k, (rows, r) in enumerate(zip(chunks, inv)):
            dst = pl.ds(pl.multiple_of(slab0 + k * ROW_CHUNK, ROW_CHUNK), ROW_CHUNK)
            hn_ref[dst, :] = (xn_ref[rows, :] * r * gs_ref[0] + gs_ref[1]).astype(BF16)

        h = h_ref[...]
        a = _dot(h, wg_ref[...])
        u = _dot(h, wu_ref[...])
        act = (a * _sigmoid(a) * u).astype(BF16)
        if first:
            o_ref[...] = _dot(act, wd_ref[...])
        else:
            o_ref[...] += _dot(act, wd_ref[...])

    for parity, (h_ref, hn_ref) in enumerate(((h0_ref, h1_ref), (h1_ref, h0_ref))):
        for first in (True, False):
            @pl.when((i % 2 == parity) & ((f == 0) == first))
            def _(h_ref=h_ref, hn_ref=hn_ref, first=first):
                step(h_ref, hn_ref, first)

    @pl.when(f == last)
    def _():
        @pl.when(i > 0)
        def _():
            x_copy().wait()

        if not final_norm:
            o_ref[...] = x_ref[...] + (0.5 * gt_ref[0]) * o_ref[...]
        else:
            gs_ref[2] = jnp.broadcast_to(0.5 * gt_ref[0], (ROW_CHUNK, d))
            gs_ref[3] = jnp.broadcast_to(go_ref[...], (ROW_CHUNK, d))

            def body(chunks):
                resid = lambda rows: x_ref[rows, :] + gs_ref[2] * o_ref[rows, :]
                inv = [_inv_rms(resid(rows)) for rows in chunks]
                for rows, r in zip(chunks, inv):
                    o_ref[rows, :] = resid(rows) * r * gs_ref[3]
            _row_groups(tm, body)


def _ffn(x_in, mod, sub, norm_g, w_gu, w_down, *, out_rows, tile0=0, prev=None, final_g=None):
    tm, tf = FFN_TM, FFN_TF
    nf = D_FF // tf
    n_tiles = x_in.shape[0] // tm
    slabs = tm // FFN_SLAB
    assert slabs <= nf
    nxt = lambda i: jnp.minimum(i + 1, n_tiles - 1)
    in_specs = [pl.BlockSpec(memory_space=pl.ANY),
                pl.BlockSpec((FFN_SLAB, D_MODEL), lambda i, f: (nxt(i) * slabs + jnp.minimum(f, slabs - 1), 0)),
                pl.BlockSpec((1, D_MODEL), lambda i, f: (0, 0)),
                _mod_spec(3 * sub, tm, tile0=tile0), _mod_spec(3 * sub + 1, tm, tile0=tile0),
                pl.BlockSpec((1, 1, D_MODEL), lambda i, f: (_mod_row(nxt(i) + tile0, tm) * N_MOD + 3 * sub, 0, 0)),
                pl.BlockSpec((1, 1, D_MODEL), lambda i, f: (_mod_row(nxt(i) + tile0, tm) * N_MOD + 3 * sub + 1, 0, 0)),
                _mod_spec(3 * sub + 2, tm, tile0=tile0),
                pl.BlockSpec((D_MODEL, tf), lambda i, f: (0, f)),
                pl.BlockSpec((D_MODEL, tf), lambda i, f: (0, f + nf)),
                pl.BlockSpec((tf, D_MODEL), lambda i, f: (f, 0))]
    args = [x_in, x_in, norm_g.reshape(1, D_MODEL), mod, mod, mod, mod, mod, w_gu, w_gu, w_down]
    if final_g is not None:
        in_specs.append(pl.BlockSpec((1, D_MODEL), lambda i, f: (0, 0)))
        args.append(final_g.reshape(1, D_MODEL))
    aliases = {}
    if prev is not None:
        aliases = {len(args): 0}
        in_specs.append(pl.BlockSpec(memory_space=pl.ANY))
        args.append(prev)
    return pl.pallas_call(
        functools.partial(_ffn_kernel, final_norm=final_g is not None),
        grid=(n_tiles, nf),
        in_specs=in_specs,
        out_specs=pl.BlockSpec((tm, D_MODEL), lambda i, f: (i + tile0, 0)),
        out_shape=jax.ShapeDtypeStruct((out_rows, D_MODEL), F32),
        scratch_shapes=[pltpu.VMEM((tm, D_MODEL), BF16), pltpu.VMEM((tm, D_MODEL), BF16),
                        pltpu.VMEM((4, ROW_CHUNK, D_MODEL), F32), pltpu.VMEM((tm, D_MODEL), F32),
                        pltpu.SemaphoreType.DMA(())],
        input_output_aliases=aliases,
        compiler_params=_cparams(("arbitrary", "arbitrary"), 56),
        name="ffn_half",
    )(*args)


NMM_TM = 1024
NMM_TN = 1024


def _nmm_kernel(x_ref, g_ref, sh_ref, sc_ref, w_ref, *rest, gla):
    if gla:
        wn_ref, wc_ref, o_ref, ob_ref, zc_ref, zs_ref, on_ref, h_ref, gs_ref = rest
    else:
        o_ref, h_ref, gs_ref = rest
    j = pl.program_id(1)

    @pl.when(j == 0)
    def _():
        _norm_mod_rows(x_ref, g_ref, sh_ref, sc_ref, h_ref, gs_ref)
        if gla:
            on_ref[...] = _dot(h_ref[...], wn_ref[...])

    if not gla:
        o_ref[...] = _dot(h_ref[...], w_ref[...]).astype(o_ref.dtype)
    else:
        @pl.when(j % 2 == 0)
        def _():
            o_ref[...] = _dot(h_ref[...], w_ref[...])

        @pl.when(j == 1)
        def _():
            ob_ref[...] = _dot(h_ref[...], w_ref[...]).astype(BF16)

        @pl.when(j == 3)
        def _():
            f = _dot(h_ref[...], w_ref[...]).astype(BF16)
            for grp in range(FNET_GROUPS):
                cols = slice(grp * FNET_GROUP_DIM, (grp + 1) * FNET_GROUP_DIM)
                z = _dot(f[:, cols], wc_ref[...])
                zc_ref[:, cols] = z[:, :FNET_GROUP_DIM].astype(BF16)
                zs_ref[:, cols] = z[:, FNET_GROUP_DIM:].astype(BF16)


def _nmm(xs, mod, sub, norm_g, w, w_narrow=None, w_chan=None, cast_ws=()):
    tm, tn = NMM_TM, NMM_TN
    n = w.shape[1]
    gla = w_narrow is not None
    in_specs = [pl.BlockSpec((tm, D_MODEL), lambda i, j: (i, 0)),
                pl.BlockSpec((1, D_MODEL), lambda i, j: (0, 0)),
                _mod_spec(3 * sub, tm), _mod_spec(3 * sub + 1, tm),
                pl.BlockSpec((D_MODEL, tn), lambda i, j: (0, j))]
    args = [xs, norm_g.reshape(1, D_MODEL), mod, mod, w]
    if gla:
        nn = w_narrow.shape[1]
        assert n == 4 * tn and tn == GLA_V_W == FNET_W
        in_specs += [pl.BlockSpec((D_MODEL, nn), lambda i, j: (0, 0)),
                     pl.BlockSpec(w_chan.shape, lambda i, j: (0, 0))]
        args += [w_narrow, w_chan]
        row_tile = pl.BlockSpec((tm, tn), lambda i, j: (i, 0))
        out_specs = [pl.BlockSpec((tm, tn), lambda i, j: (i, j // 2)), row_tile, row_tile, row_tile,
                     pl.BlockSpec((tm, nn), lambda i, j: (i, 0))]
        out_shape = [jax.ShapeDtypeStruct((N_TOK, n // 2), F32)] + [jax.ShapeDtypeStruct((N_TOK, tn), BF16)] * 3 + [
            jax.ShapeDtypeStruct((N_TOK, nn), F32)]
    else:
        out_specs = [pl.BlockSpec((tm, tn), lambda i, j: (i, j))]
        out_shape = [jax.ShapeDtypeStruct((N_TOK, n), BF16)]
    nj = n // tn
    flat = lambda i, j: i * nj + j
    casts = [_cast_specs(cw, (N_TOK // tm) * nj, flat) for cw in cast_ws]
    return pl.pallas_call(
        _hosting_casts(functools.partial(_nmm_kernel, gla=gla), len(args), len(out_specs),
                       [u for _, _, u in casts], flat, 2),
        grid=(N_TOK // tm, nj),
        in_specs=in_specs + [s for s, _, _ in casts],
        out_specs=out_specs + [s for s, _, _ in casts],
        out_shape=out_shape + [o for _, o, _ in casts],
        scratch_shapes=[pltpu.VMEM((tm, D_MODEL), BF16), pltpu.VMEM((2, ROW_CHUNK, D_MODEL), F32)],
        compiler_params=_cparams(("arbitrary", "arbitrary"), 56),
        name="norm_mod_matmul",
    )(*args, *cast_ws)


GLA_BLOCK = CTX_LEN
GLA_LAT_BLOCKS = SEQ // GLA_BLOCK
GLA_STEPS = 1 + GLA_LAT_BLOCKS
GLA_LR_PAD = 128


def _rope(x, cos, sin_signed):
    return x * cos + pltpu.roll(x, GLA_DK // 2, 1) * sin_signed


def _gla_kernel(qf, kf, vf, lf, cf, sf, qb, kb, vb, lb, cb, sb, gwf, gbf, gwb, gbb,
                of_ref, ob_ref, st_ref):
    @pl.when(pl.program_id(1) == 0)
    def _():
        st_ref[...] = jnp.zeros_like(st_ref)

    c = GLA_CHUNK
    n = GLA_BLOCK
    row = lax.broadcasted_iota(jnp.int32, (n, n), 0)
    col = lax.broadcasted_iota(jnp.int32, (n, n), 1)
    same_chunk = (row // c) == (col // c)
    row_c = lax.broadcasted_iota(jnp.int32, (c, c), 0)
    col_c = lax.broadcasted_iota(jnp.int32, (c, c), 1)
    scale = GLA_DK ** -0.5

    dirs = ((qf, kf, vf, lf, cf, sf, gwf, gbf, of_ref), (qb, kb, vb, lb, cb, sb, gwb, gbb, ob_ref))
    n_chunks = n // c
    heads = [(slice(h * GLA_DK, (h + 1) * GLA_DK), slice(h * GLA_DV, (h + 1) * GLA_DV)) for h in range(GLA_HEADS)]
    tris, decays = [], []
    for d, (q_ref, k_ref, v_ref, l_ref, c_ref, s_ref, gw_ref, gb_ref, o_ref) in enumerate(dirs):
        causal = (row >= col) if d == 0 else (row <= col)
        tri_b = jnp.where(same_chunk & causal, 1.0, 0.0).astype(BF16)
        tris.append((row_c >= col_c) if d == 0 else (row_c <= col_c))
        x = _dot3(l_ref[...], gw_ref[...]) + gb_ref[...]
        g = (jnp.minimum(x, 0.0) - jnp.log1p(jnp.exp(-jnp.abs(x)))) * (1.0 / GLA_TAU)
        bc_all = _dot_exact_lhs(tri_b, g)
        per_chunk = []
        for ci in (range(n_chunks) if d == 0 else range(n_chunks - 1, -1, -1)):
            rows = slice(ci * c, (ci + 1) * c)
            bc = bc_all[rows, :]
            mid = bc[c // 2:c // 2 + 1, :]
            last = bc[c - 1:c, :] if d == 0 else bc[0:1, :]
            per_chunk.append((rows, jnp.exp(bc), jnp.exp(bc - mid), jnp.exp(mid - bc), jnp.exp(last - bc),
                              jnp.exp(last)))
        decays.append(per_chunk)

    def local_part(k):
        out = {}
        for d, (q_ref, k_ref, v_ref, _, c_ref, s_ref, _, _, _) in enumerate(dirs):
            rows, e_q, e_qm, e_km, e_kl, e_l = decays[d][k]
            cos, sin = c_ref[rows, :], s_ref[rows, :]
            for h, (sk, sv) in enumerate(heads):
                qh = _rope(q_ref[rows, sk] * scale, cos, sin)
                kh = _rope(k_ref[rows, sk], cos, sin)
                vh = v_ref[rows, sv].astype(BF16)
                att = _dot_nt((qh * e_qm[:, sk]).astype(BF16), (kh * e_km[:, sk]).astype(BF16))
                kv = _dot_tn(vh, (kh * e_kl[:, sk]).astype(BF16))
                out[d, h] = ((qh * e_q[:, sk]).astype(BF16), jnp.where(tris[d], att, 0.0).astype(BF16), vh, kv,
                             e_l[:, sk])
        return out

    st = {(d, h): st_ref[d, h] for d in range(2) for h in range(GLA_HEADS)}
    ahead = 1
    pending = {k: local_part(k) for k in range(min(ahead, n_chunks))}
    for k in range(n_chunks):
        if k + ahead < n_chunks:
            pending[k + ahead] = local_part(k + ahead)
        local = pending.pop(k)
        for d in range(2):
            rows = decays[d][k][0]
            for h, (sk, sv) in enumerate(heads):
                qe, att, vh, kv, e_l = local[d, h]
                dirs[d][-1][rows, sv] = _dot_nt(qe, st[d, h].astype(BF16)) + _dot(att, vh)
                st[d, h] = st[d, h] * e_l + kv
    for (d, h), val in st.items():
        st_ref[d, h] = val


def _gla(pf, pb, lr, cos_t, sin_t, gwf, gbf, gwb, gbb, cast_ws):
    c = GLA_BLOCK
    ctx0 = N_LAT // c

    def fwd_blk(b, s):
        return jnp.where(s == 0, ctx0 + b, GLA_LAT_BLOCKS * b + s - 1)

    def bwd_blk(b, s):
        return jnp.where(s == 0, ctx0 + b, GLA_LAT_BLOCKS * b + (GLA_STEPS - 1 - s))

    def fwd_rope(b, s):
        return jnp.where(s == 0, GLA_LAT_BLOCKS, s - 1)

    def bwd_rope(b, s):
        return jnp.where(s == 0, GLA_LAT_BLOCKS, GLA_STEPS - 1 - s)

    def dir_specs(blk, rope):
        return [pl.BlockSpec((c, GLA_QK_W), lambda b, s: (blk(b, s), 0)),
                pl.BlockSpec((c, GLA_QK_W), lambda b, s: (blk(b, s), 1)),
                pl.BlockSpec((c, GLA_V_W), lambda b, s: (blk(b, s), 0)),
                pl.BlockSpec((c, GLA_LR_PAD), lambda b, s: (blk(b, s), 0)),
                pl.BlockSpec((c, GLA_DK), lambda b, s: (rope(b, s), 0)),
                pl.BlockSpec((c, GLA_DK), lambda b, s: (rope(b, s), 0))]

    const = lambda shape: pl.BlockSpec(shape, lambda b, s: (0, 0))
    in_specs = (dir_specs(fwd_blk, fwd_rope) + dir_specs(bwd_blk, bwd_rope)
                + [const((GLA_LR_PAD, GLA_QK_W)), const((1, GLA_QK_W)),
                   const((GLA_LR_PAD, GLA_QK_W)), const((1, GLA_QK_W))])
    flat = lambda b, s: b * GLA_STEPS + s
    casts = [_cast_specs(w, BATCH * GLA_STEPS, flat) for w in cast_ws]
    args = [pf, pf, pb, lr, cos_t, sin_t, pf, pf, pb, lr, cos_t, sin_t, gwf, gbf, gwb, gbb]
    return pl.pallas_call(
        _hosting_casts(_gla_kernel, len(args), 2, [u for _, _, u in casts], flat, 2),
        grid=(BATCH, GLA_STEPS),
        in_specs=in_specs + [s for s, _, _ in casts],
        out_specs=[pl.BlockSpec((c, GLA_V_W), lambda b, s: (fwd_blk(b, s), 0)),
                   pl.BlockSpec((c, GLA_V_W), lambda b, s: (bwd_blk(b, s), 0))] + [s for s, _, _ in casts],
        out_shape=[jax.ShapeDtypeStruct((N_TOK, GLA_V_W), F32)] * 2 + [o for _, o, _ in casts],
        scratch_shapes=[pltpu.VMEM((2, GLA_HEADS, GLA_DV, GLA_DK), F32)],
        compiler_params=_cparams(("arbitrary", "arbitrary"), 40),
        name="gla_scan",
    )(*args, *cast_ws)


def _tdft_kernel(c_ref, s_ref, zc_ref, zs_ref, *rest):
    o_ref = rest[-1]
    o_ref[...] = (_dot(c_ref[...], zc_ref[...]) - _dot(s_ref[...], zs_ref[...])).astype(o_ref.dtype)


def _time_dft(cmat, smat, zc, zs, t_len, tm, tn, row0, prev=None, cast_ws=()):
    zb0 = row0 // t_len
    ob0 = row0 // tm
    mt = t_len // tm
    in_specs = [pl.BlockSpec((tm, t_len), lambda b, n, m: (m, 0)),
                pl.BlockSpec((tm, t_len), lambda b, n, m: (m, 0)),
                pl.BlockSpec((t_len, tn), lambda b, n, m: (zb0 + b, n)),
                pl.BlockSpec((t_len, tn), lambda b, n, m: (zb0 + b, n))]
    args = [cmat, smat, zc, zs]
    aliases = {}
    if prev is not None:
        in_specs.append(pl.BlockSpec(memory_space=pl.ANY))
        args.append(prev)
        aliases = {4: 0}
    nt = FNET_W // tn
    flat = lambda b, n, m: (b * nt + n) * mt + m
    casts = [_cast_specs(w, BATCH * nt * mt, flat) for w in cast_ws]
    return pl.pallas_call(
        _hosting_casts(_tdft_kernel, len(args), 1, [u for _, _, u in casts], flat, 3),
        grid=(BATCH, nt, mt),
        in_specs=in_specs + [s for s, _, _ in casts],
        out_specs=[pl.BlockSpec((tm, tn), lambda b, n, m: (ob0 + b * mt + m, n))] + [s for s, _, _ in casts],
        out_shape=[jax.ShapeDtypeStruct((N_TOK, FNET_W), BF16)] + [o for _, o, _ in casts],
        input_output_aliases=aliases,
        compiler_params=_cparams(("arbitrary", "arbitrary", "arbitrary"), 48),
        name="fnet_time_dft",
    )(*args, *cast_ws)


FOLD_BLK = 256


def _tdft_fold_kernel(c_ref, s_ref, zc_ref, zs_ref, j1_ref, j2_ref, o_ref, zcf_ref, zsf_ref):
    half, blk = zcf_ref.shape[0], FOLD_BLK
    nblk = 2 * half // blk

    @pl.when(pl.program_id(2) == 0)
    def _():
        for kb in range(half // blk):
            rows = slice(kb * blk, (kb + 1) * blk)
            mirror = slice((nblk - 1 - kb) * blk, (nblk - kb) * blk)
            after = slice((nblk - kb) * blk, (nblk - kb + 1) * blk)
            for z_ref, zf_ref, sign in ((zc_ref, zcf_ref, 1.0), (zs_ref, zsf_ref, -1.0)):
                partner = _dot(j1_ref[...], z_ref[mirror, :])
                if kb > 0:
                    partner = partner + _dot(j2_ref[...], z_ref[after, :])
                zf_ref[rows, :] = (z_ref[rows, :].astype(F32) + sign * partner).astype(BF16)

    tm = o_ref.shape[0]
    j = pl.program_id(2) * tm + lax.broadcasted_iota(jnp.int32, (tm, 1), 0)
    sign = jnp.where(j % 2 == 0, SEQ ** -0.5, -(SEQ ** -0.5))
    mid = zc_ref[half:half + 1, :].astype(F32)
    o = _dot(c_ref[...], zcf_ref[...]) - _dot(s_ref[...], zsf_ref[...]) + sign * mid
    o_ref[...] = o.astype(o_ref.dtype)


def _time_dft_folded(cmat, smat, zc, zs, tm, tn, cast_ws=()):
    half = SEQ // 2
    mt = SEQ // tm
    nt = FNET_W // tn
    i = np.arange(FOLD_BLK)
    j1 = jnp.asarray((i[:, None] + i[None, :] == FOLD_BLK) & (i[:, None] > 0), BF16)
    j2 = jnp.asarray((i[:, None] == 0) & (i[None, :] == 0), BF16)
    args = [cmat, smat, zc, zs, j1, j2]
    in_specs = [pl.BlockSpec((tm, half), lambda b, n, m: (m, 0)),
                pl.BlockSpec((tm, half), lambda b, n, m: (m, 0)),
                pl.BlockSpec((SEQ, tn), lambda b, n, m: (b, n)),
                pl.BlockSpec((SEQ, tn), lambda b, n, m: (b, n)),
                pl.BlockSpec((FOLD_BLK, FOLD_BLK), lambda b, n, m: (0, 0)),
                pl.BlockSpec((FOLD_BLK, FOLD_BLK), lambda b, n, m: (0, 0))]
    flat = lambda b, n, m: (b * nt + n) * mt + m
    casts = [_cast_specs(w, BATCH * nt * mt, flat) for w in cast_ws]
    return pl.pallas_call(
        _hosting_casts(_tdft_fold_kernel, len(args), 1, [u for _, _, u in casts], flat, 3),
        grid=(BATCH, nt, mt),
        in_specs=in_specs + [s for s, _, _ in casts],
        out_specs=[pl.BlockSpec((tm, tn), lambda b, n, m: (b * mt + m, n))] + [s for s, _, _ in casts],
        out_shape=[jax.ShapeDtypeStruct((N_TOK, FNET_W), BF16)] + [o for _, o, _ in casts],
        scratch_shapes=[pltpu.VMEM((half, tn), BF16), pltpu.VMEM((half, tn), BF16)],
        compiler_params=_cparams(("arbitrary", "arbitrary", "arbitrary"), 48),
        name="fnet_time_dft_folded",
    )(*args, *cast_ws)


def _dft_tables(n):
    k = np.arange(n)
    ang = 2.0 * np.pi * ((k[:, None] * k[None, :]) % n) / n
    return np.cos(ang) / np.sqrt(n), np.sin(ang) / np.sqrt(n)


def _dft_table_kernel(ac_ref, as_ref, bc_ref, bs_ref, co_ref, so_ref):
    ac, as_ = ac_ref[0], as_ref[0]
    bc, bs = bc_ref[...], bs_ref[...]
    co_ref[...] = (ac * bc - as_ * bs).astype(BF16)
    so_ref[...] = (ac * bs + as_ * bc).astype(BF16)


def _big_dft_tables():
    r = GRID_W
    k = np.arange(SEQ)
    j = np.arange(r)
    pa = 2.0 * np.pi * ((j[:, None] * r * k[None, :]) % SEQ) / SEQ
    pb = 2.0 * np.pi * ((j[:, None] * k[None, :]) % SEQ) / SEQ
    ac = jnp.asarray(np.cos(pa).reshape(r, 1, SEQ), F32)
    as_ = jnp.asarray(np.sin(pa).reshape(r, 1, SEQ), F32)
    bc = jnp.asarray(np.cos(pb) * SEQ ** -0.5, F32)
    bs = jnp.asarray(np.sin(pb) * SEQ ** -0.5, F32)
    row = pl.BlockSpec((1, 1, SEQ), lambda i: (i, 0, 0))
    full = pl.BlockSpec((r, SEQ), lambda i: (0, 0))
    return pl.pallas_call(
        _dft_table_kernel,
        grid=(r,),
        in_specs=[row, row, full, full],
        out_specs=[pl.BlockSpec((r, SEQ), lambda i: (i, 0))] * 2,
        out_shape=[jax.ShapeDtypeStruct((SEQ, SEQ), BF16)] * 2,
        compiler_params=_cparams(("parallel",), 16),
        name="dft_tables",
    )(ac, as_, bc, bs)


GO_TM = 512
GO_TN = 2048


def _glaout_kernel(of_ref, ob_ref, r_ref, fm_ref, gn_ref, x_ref, gt_ref, w_ref, o_ref, h_ref):
    @pl.when(pl.program_id(1) == 0)
    def _():
        heads = [slice(h * GLA_DV, (h + 1) * GLA_DV) for h in range(GLA_HEADS)]

        def body(chunks):
            inv = [[_inv_rms(of_ref[rows, sv] + ob_ref[rows, sv]) for sv in heads] for rows in chunks]
            for rows, inv_c in zip(chunks, inv):
                for sv, r in zip(heads, inv_c):
                    o = (of_ref[rows, sv] + ob_ref[rows, sv]) * r
                    gate = r_ref[rows, sv]
                    h_ref[rows, sv] = (o * gn_ref[:, sv] * (gate * _sigmoid(gate))).astype(BF16)
                h_ref[rows, GLA_V_W:] = fm_ref[rows, :].astype(BF16)
        _row_groups(of_ref.shape[0], body, group=4)

    o_ref[...] = x_ref[...] + gt_ref[0] * _dot(h_ref[...], w_ref[...])


def _glaout(of, ob, pf, fm, g_norm, xs, mod, w_out):
    tm, tn = GO_TM, GO_TN
    half = lambda col: pl.BlockSpec((tm, GLA_V_W), lambda i, j: (i, col))
    return pl.pallas_call(
        _glaout_kernel,
        grid=(N_TOK // tm, D_MODEL // tn),
        in_specs=[half(0), half(0), half(1), half(0),
                  pl.BlockSpec((1, GLA_V_W), lambda i, j: (0, 0)),
                  pl.BlockSpec((tm, tn), lambda i, j: (i, j)),
                  _mod_spec(5, tm, tn, col=True),
                  pl.BlockSpec((D_MODEL, tn), lambda i, j: (0, j))],
        out_specs=pl.BlockSpec((tm, tn), lambda i, j: (i, j)),
        out_shape=jax.ShapeDtypeStruct((N_TOK, D_MODEL), F32),
        scratch_shapes=[pltpu.VMEM((tm, D_MODEL), BF16)],
        compiler_params=_cparams(("parallel", "arbitrary"), 56),
        name="gla_fnet_out",
    )(of, ob, pf, fm, g_norm.reshape(1, GLA_V_W), xs, mod, w_out)


NA_HROWS = NA_QROWS // 2
NA_WROWS = NA_HROWS + NA_KH
NA_TQ = NA_QROWS * GRID_W
NA_HQ = NA_HROWS * GRID_W
NA_WK = NA_WROWS * GRID_W
NA_KBLK = NA_HROWS * GRID_W
NA_WBLKS = NA_WK // NA_KBLK
NA_MASKED = 2 * NA_KH - 1
NA_ROW_BLOCKS = (SEQ // GRID_W) // NA_QROWS
NA_CASES = (0, 1, NA_ROW_BLOCKS - 1)
NA_PAIRS = 2
NA_LANES = NA_PAIRS * 2 * NA_HEAD_DIM


def _na_window_row0(j, half):
    q0 = NA_QROWS * j + NA_HROWS * half
    return int(np.clip(q0 - NA_KH // 2, 0, SEQ // GRID_W - NA_WROWS))


def _na_bias_slots():
    rows = SEQ // GRID_W
    slots = np.full((len(NA_CASES), 2, NA_HROWS, NA_WROWS), NA_MASKED, np.int32)
    for case, j in enumerate(NA_CASES):
        for half in range(2):
            w0 = _na_window_row0(j, half)
            for i in range(NA_HROWS):
                qr = NA_QROWS * j + NA_HROWS * half + i
                r0 = int(np.clip(qr - NA_KH // 2, 0, rows - NA_KH))
                for l in range(NA_WROWS):
                    kr = w0 + l
                    if r0 <= kr < r0 + NA_KH:
                        slots[case, half, i, l] = kr - qr + NA_KH - 1
    return slots


def _na_kernel(q_ref, *refs):
    nb = NA_WBLKS
    ka, kb, va, vb = refs[0:nb], refs[nb:2 * nb], refs[2 * nb:3 * nb], refs[3 * nb:4 * nb]
    kc_ref, vc_ref, td_ref, o_ref, bias_ref = refs[4 * nb:]
    j = pl.program_id(1)

    @pl.when((j == 0) & (pl.program_id(2) == 0))
    def _():
        slots = _na_bias_slots()
        left = lax.broadcasted_iota(jnp.int32, (GRID_W, 2 * GRID_W), 1) < GRID_W
        for case in range(len(NA_CASES)):
            for half in range(2):
                for hh in range(2 * NA_PAIRS):
                    dst = (case * 2 + half) * 2 * NA_PAIRS + hh
                    for i in range(NA_HROWS):
                        for p in range(NA_WROWS // 2):
                            sl = int(slots[case, half, i, 2 * p])
                            sr = int(slots[case, half, i, 2 * p + 1])
                            tile = td_ref[hh, sl] if sl == sr else jnp.where(left, td_ref[hh, sl], td_ref[hh, sr])
                            bias_ref[dst, i * GRID_W:(i + 1) * GRID_W, p * 2 * GRID_W:(p + 1) * 2 * GRID_W] = tile

    case = jnp.where(j == 0, 0, jnp.where(j == NA_ROW_BLOCKS - 1, 2, 1))
    lane = lax.broadcasted_iota(jnp.int32, (NA_HQ, 2 * NA_HEAD_DIM), 1)
    kv_refs = ((ka, va), (kb, vb))
    items = [(half, pair, hh) for half in range(2) for pair in range(NA_PAIRS) for hh in range(2)]
    pair_lanes = lambda pair: slice(pair * 2 * NA_HEAD_DIM, (pair + 1) * 2 * NA_HEAD_DIM)

    def scores(half, pair, hh):
        lanes = pair_lanes(pair)
        q = q_ref[half * NA_HQ:(half + 1) * NA_HQ, lanes]
        qm = jnp.where((lane // NA_HEAD_DIM) == hh, q, jnp.zeros_like(q))
        k = jnp.concatenate([r[:, lanes] for r in kv_refs[half][0]], axis=0)
        bias = bias_ref[(case * 2 + half) * 2 * NA_PAIRS + 2 * pair + hh]
        return _dot_nt(qm, k) + bias, _dot_nt(qm, kc_ref[:, lanes])

    def softmax(s_loc, s_ctx):
        m = jnp.maximum(jnp.max(s_loc, axis=-1, keepdims=True), jnp.max(s_ctx, axis=-1, keepdims=True))
        p_loc = jnp.exp2(s_loc - m)
        p_ctx = jnp.exp2(s_ctx - m)
        denom = jnp.sum(p_loc, axis=-1, keepdims=True) + jnp.sum(p_ctx, axis=-1, keepdims=True)
        return p_loc.astype(BF16), p_ctx.astype(BF16), denom

    def weighted(half, pair, p_loc, p_ctx, denom):
        lanes = pair_lanes(pair)
        v = jnp.concatenate([r[:, lanes] for r in kv_refs[half][1]], axis=0)
        return (_dot(p_loc, v) + _dot(p_ctx, vc_ref[:, lanes])) / denom

    s, p, o = {}, {}, {}
    for t in range(len(items) + 2):
        if t < len(items):
            s[t] = scores(*items[t])
        if 0 <= t - 1 < len(items):
            p[t - 1] = softmax(*s.pop(t - 1))
        if 0 <= t - 2 < len(items):
            o[t - 2] = weighted(*items[t - 2][:2], *p.pop(t - 2))
    first = (lane // NA_HEAD_DIM) == 0
    for n, (half, pair, hh) in enumerate(items):
        if hh == 0:
            out = jnp.where(first, o[n], o[n + 1])
            o_ref[half * NA_HQ:(half + 1) * NA_HQ, pair_lanes(pair)] = out.astype(o_ref.dtype)


def _na(qkv, tiles, cast_ws=()):
    hp_n = NA_HEADS // (2 * NA_PAIRS)
    jn = NA_ROW_BLOCKS
    kb_per_img = SEQ // NA_KBLK
    last_blk = kb_per_img - NA_WBLKS

    def kv_spec(col0, half, t):
        return pl.BlockSpec((NA_KBLK, NA_LANES), lambda hp, j, b: (
            b * kb_per_img + jnp.clip(2 * j + half - 1, 0, last_blk) + t, col0 + hp))

    def kv_specs(col0):
        return [kv_spec(col0, half, t) for half in range(2) for t in range(NA_WBLKS)]

    ctx0 = N_LAT // CTX_LEN
    in_specs = ([pl.BlockSpec((NA_TQ, NA_LANES), lambda hp, j, b: (b * jn + j, hp))]
                + kv_specs(hp_n) + kv_specs(2 * hp_n)
                + [pl.BlockSpec((CTX_LEN, NA_LANES), lambda hp, j, b: (ctx0 + b, hp_n + hp)),
                   pl.BlockSpec((CTX_LEN, NA_LANES), lambda hp, j, b: (ctx0 + b, 2 * hp_n + hp)),
                   pl.BlockSpec((2 * NA_PAIRS, 2 * NA_KH, GRID_W, 2 * GRID_W), lambda hp, j, b: (hp, 0, 0, 0))])
    flat = lambda hp, j, b: (hp * jn + j) * BATCH + b
    casts = [_cast_specs(w, hp_n * jn * BATCH, flat) for w in cast_ws]
    return pl.pallas_call(
        _hosting_casts(_na_kernel, len(in_specs), 1, [u for _, _, u in casts], flat, 3),
        grid=(hp_n, jn, BATCH),
        in_specs=in_specs + [s for s, _, _ in casts],
        out_specs=[pl.BlockSpec((NA_TQ, NA_LANES), lambda hp, j, b: (b * jn + j, hp))] + [s for s, _, _ in casts],
        out_shape=[jax.ShapeDtypeStruct((N_LAT, D_MODEL), BF16)] + [o for _, o, _ in casts],
        scratch_shapes=[pltpu.VMEM((len(NA_CASES) * 4 * NA_PAIRS, NA_HQ, NA_WK), F32)],
        compiler_params=_cparams(("arbitrary", "arbitrary", "arbitrary"), 48),
        name="neighbourhood_attention",
    )(qkv, *([qkv] * (4 * NA_WBLKS)), qkv, qkv, tiles, *cast_ws)


def _na_bias_tiles(rpb):
    w = GRID_W
    col = np.arange(w)
    cs = np.clip(col - NA_KW // 2, 0, w - NA_KW)
    col_mask = (col[None, :] >= cs[:, None]) & (col[None, :] < cs[:, None] + NA_KW)
    dc_idx = np.clip(col[None, :] - col[:, None] + NA_KW - 1, 0, 2 * NA_KW - 2)
    t1 = (rpb.astype(F32) * LOG2_E)[:, :, dc_idx]
    t1 = jnp.where(col_mask[None, None], t1, NEG_BIG)
    t1 = jnp.concatenate([t1, jnp.full((NA_HEADS, 1, w, w), NEG_BIG, F32)], axis=1)
    return jnp.concatenate([t1, t1], axis=-1)


MR_TM = 1024
MR_TN = 1024


def _mmres_kernel(a_ref, w_ref, x_ref, gt_ref, o_ref):
    o_ref[...] = x_ref[...] + gt_ref[0] * _dot(a_ref[...], w_ref[...])


def _mm_res(a, w, xs, mod, n_rows):
    tm, tn = MR_TM, MR_TN
    return pl.pallas_call(
        _mmres_kernel,
        grid=(n_rows // tm, D_MODEL // tn),
        in_specs=[pl.BlockSpec((tm, D_MODEL), lambda i, j: (i, 0)),
                  pl.BlockSpec((D_MODEL, tn), lambda i, j: (0, j)),
                  pl.BlockSpec((tm, tn), lambda i, j: (i, j)),
                  _mod_spec(5, tm, tn, col=True)],
        out_specs=pl.BlockSpec((tm, tn), lambda i, j: (i, j)),
        out_shape=jax.ShapeDtypeStruct((n_rows, D_MODEL), F32),
        compiler_params=_cparams(("parallel", "arbitrary"), 40),
        name="matmul_gated_residual",
    )(a, w, xs, mod)


def _rope_tables():
    half = GLA_DK // 4
    inv_freq = ROPE_BASE ** (-jnp.arange(half, dtype=F32) / half)
    t = jnp.arange(SEQ)
    ang_r = (t // GRID_W).astype(F32)[:, None] * inv_freq[None, :]
    ang_c = (t % GRID_W).astype(F32)[:, None] * inv_freq[None, :]
    cr, sr, cc, sc = jnp.cos(ang_r), jnp.sin(ang_r), jnp.cos(ang_c), jnp.sin(ang_c)
    cos_t = jnp.concatenate([cr, cc, cr, cc], axis=-1)
    sin_t = jnp.concatenate([-sr, -sc, sr, sc], axis=-1)
    cos_t = jnp.concatenate([cos_t, jnp.ones((GLA_BLOCK, GLA_DK), F32)], axis=0)
    sin_t = jnp.concatenate([sin_t, jnp.zeros((GLA_BLOCK, GLA_DK), F32)], axis=0)
    return cos_t, sin_t


def _pair_layout(w):
    lead = w.shape[:-1]
    quarter = GLA_DK // 4
    w = w.reshape(*lead, GLA_HEADS, 2, 2, quarter)
    return jnp.swapaxes(w, -2, -3).reshape(*lead, GLA_QK_W)


def kernel(x, c, ctx, c_ctx, l0_ada_w, l0_ada_b, l0_norm_ffn1, l0_norm_mix, l0_norm_ffn2, l0_ffn1_w_gu, l0_ffn1_w_down, l0_ffn2_w_gu, l0_ffn2_w_down, l0_w_in, l0_gla_gate_w_fwd, l0_gla_gate_b_fwd, l0_gla_gate_w_bwd, l0_gla_gate_b_bwd, l0_gla_norm, l0_w_out, l1_ada_w, l1_ada_b, l1_norm_ffn1, l1_norm_mix, l1_norm_ffn2, l1_ffn1_w_gu, l1_ffn1_w_down, l1_ffn2_w_gu, l1_ffn2_w_down, l1_w_qkv, l1_rpb, l1_w_out, norm_out):
    bf = lambda w: w.astype(BF16)
    cc = jnp.concatenate([c, c_ctx[None, :], jnp.zeros((8 - BATCH - 1, D_MODEL), F32)], axis=0)

    mod0 = _ada(cc, l0_ada_w, l0_ada_b)
    w_gu, w_down = bf(l0_ffn1_w_gu), bf(l0_ffn1_w_down)
    xs = _ffn(x.reshape(N_LAT, D_MODEL), mod0, 0, l0_norm_ffn1, w_gu, w_down, out_rows=N_TOK)
    xs = _ffn(ctx.reshape(N_CTX, D_MODEL), mod0, 0, l0_norm_ffn1, w_gu, w_down, out_rows=N_TOK,
              tile0=N_LAT // FFN_TM, prev=xs)

    a0 = 2 * GLA_QK_W + GLA_V_W
    a1 = a0 + 2 * GLA_LOWRANK
    w_in = bf(l0_w_in)
    w_main = jnp.concatenate([_pair_layout(w_in[:, :GLA_QK_W]), _pair_layout(w_in[:, GLA_QK_W:2 * GLA_QK_W]),
                              w_in[:, 2 * GLA_QK_W:a0], w_in[:, a1:]], axis=1)
    w_lr = jnp.pad(w_in[:, a0:a1], ((0, 0), (0, GLA_LR_PAD - 2 * GLA_LOWRANK)))
    cc_np, sc_np = _dft_tables(FNET_GROUP_DIM)
    wc = jnp.asarray(np.concatenate([cc_np, sc_np], axis=1), F32).astype(BF16)
    pf, pb, zc, zs, lr = _nmm(xs, mod0, 1, l0_norm_mix, w_main, w_narrow=w_lr, w_chan=wc)

    cos_t, sin_t = _rope_tables()
    gwf = jnp.pad(_pair_layout(l0_gla_gate_w_fwd), ((0, GLA_LR_PAD - GLA_LOWRANK), (0, 0)))
    gwb = jnp.pad(_pair_layout(l0_gla_gate_w_bwd), ((GLA_LOWRANK, GLA_LR_PAD - 2 * GLA_LOWRANK), (0, 0)))
    of, ob, w_gu2, w_down2 = _gla(pf, pb, lr, cos_t, sin_t, gwf, _pair_layout(l0_gla_gate_b_fwd).reshape(1, -1),
                                  gwb, _pair_layout(l0_gla_gate_b_bwd).reshape(1, -1),
                                  cast_ws=(l0_ffn2_w_gu, l0_ffn2_w_down))

    cos_big, sin_big = _big_dft_tables()
    fm, w_gu3, w_down3 = _time_dft_folded(cos_big, sin_big, zc, zs, 512, 512, cast_ws=(l1_ffn1_w_gu, l1_ffn1_w_down))
    ct_np, st_np = _dft_tables(CTX_LEN)
    fm = _time_dft(jnp.asarray(ct_np, F32).astype(BF16), jnp.asarray(st_np, F32).astype(BF16),
                   zc, zs, CTX_LEN, CTX_LEN, 512, N_LAT, prev=fm)[0]

    xs = _glaout(of, ob, pf, fm, l0_gla_norm, xs, mod0, bf(l0_w_out))
    xs = _ffn(xs, mod0, 2, l0_norm_ffn2, w_gu2, w_down2, out_rows=N_TOK)

    mod1 = _ada(cc, l1_ada_w, l1_ada_b)
    xs = _ffn(xs, mod1, 0, l1_norm_ffn1, w_gu3, w_down3, out_rows=N_TOK)
    w_qkv = bf(jnp.concatenate([l1_w_qkv[:, :D_MODEL] * NA_Q_SCALE, l1_w_qkv[:, D_MODEL:]], axis=1))
    qkv, w_gu4, w_down4 = _nmm(xs, mod1, 1, l1_norm_mix, w_qkv, cast_ws=(l1_ffn2_w_gu, l1_ffn2_w_down))
    ao = _na(qkv, _na_bias_tiles(l1_rpb))[0]
    xl = _mm_res(ao, bf(l1_w_out), xs, mod1, N_LAT)
    out = _ffn(xl, mod1, 2, l1_norm_ffn2, w_gu4, w_down4, out_rows=N_LAT, final_g=norm_out)
    return out.reshape(BATCH, SEQ, D_MODEL)
```

```python
import functools

import numpy as np
import jax
import jax.numpy as jnp
from jax import lax
from jax.experimental import pallas as pl
from jax.experimental.pallas import tpu as pltpu

D_MODEL = 2048
BATCH = 4
SEQ = 4096
GRID_W = 64
CTX_LEN = 256
N_MOD = 9
D_FF = 5632
RMS_EPS = 1e-6
ROPE_BASE = 10000.0

GLA_HEADS = 4
GLA_DK = 128
GLA_DV = 256
GLA_LOWRANK = 16
GLA_TAU = 16.0
GLA_CHUNK = 64
FNET_GROUPS = 4
FNET_GROUP_DIM = 256
GLA_QK_W = GLA_HEADS * GLA_DK
GLA_V_W = GLA_HEADS * GLA_DV
FNET_W = FNET_GROUPS * FNET_GROUP_DIM

NA_HEADS = 32
NA_HEAD_DIM = 64
NA_KH = 8
NA_KW = 16
NA_QROWS = 8
NEG_BIG = -1e30
LOG2_E = 1.4426950408889634
NA_Q_SCALE = NA_HEAD_DIM ** -0.5 * LOG2_E

N_LAT = BATCH * SEQ
N_CTX = BATCH * CTX_LEN
N_TOK = N_LAT + N_CTX

F32 = jnp.float32
BF16 = jnp.bfloat16
MIB = 1024 * 1024


def _cparams(sem, vmem_mib):
    return pltpu.CompilerParams(dimension_semantics=sem, vmem_limit_bytes=vmem_mib * MIB)


def _dot(a, b):
    return jnp.dot(a, b, preferred_element_type=F32)


def _dot_nt(a, b):
    return lax.dot_general(a, b, (((1,), (1,)), ((), ())), preferred_element_type=F32)


def _dot_tn(a, b):
    return lax.dot_general(a, b, (((0,), (0,)), ((), ())), preferred_element_type=F32)


def _split2(x):
    hi = x.astype(BF16)
    lo = (x - hi.astype(F32)).astype(BF16)
    return hi, lo


def _dot3(a, b):
    ah, al = _split2(a)
    bh, bl = _split2(b)
    return _dot(ah, bh) + (_dot(al, bh) + _dot(ah, bl))


def _dot_exact_lhs(l_bf16, x):
    x1 = x.astype(BF16)
    r1 = x - x1.astype(F32)
    x2 = r1.astype(BF16)
    x3 = (r1 - x2.astype(F32)).astype(BF16)
    return _dot(l_bf16, x1) + (_dot(l_bf16, x2) + _dot(l_bf16, x3))


def _sigmoid(x):
    return 1.0 / (1.0 + jnp.exp(-x))


def _rms(x, g):
    ms = jnp.mean(x * x, axis=-1, keepdims=True)
    return x * lax.rsqrt(ms + RMS_EPS) * g


ROW_CHUNK = 16
ROW_GROUP = 8


def _row_groups(n_rows, body, group=ROW_GROUP):
    span = ROW_CHUNK * group

    def step(t, carry):
        base = t * span
        body([pl.ds(pl.multiple_of(base + c * ROW_CHUNK, ROW_CHUNK), ROW_CHUNK) for c in range(group)])
        return carry
    lax.fori_loop(0, n_rows // span, step, 0)


def _inv_rms(x):
    return lax.rsqrt(jnp.mean(x * x, axis=-1, keepdims=True) + RMS_EPS)


def _norm_mod_rows(x_ref, g_ref, sh_ref, sc_ref, h_ref, gs_ref, zero_ref=None):
    d = x_ref.shape[1]
    gs_ref[0] = jnp.broadcast_to(g_ref[...] * (1.0 + sc_ref[0]), (ROW_CHUNK, d))
    gs_ref[1] = jnp.broadcast_to(sh_ref[0], (ROW_CHUNK, d))

    def body(chunks):
        inv = [_inv_rms(x_ref[rows, :]) for rows in chunks]
        for rows, r in zip(chunks, inv):
            h_ref[rows, :] = (x_ref[rows, :] * r * gs_ref[0] + gs_ref[1]).astype(BF16)
            if zero_ref is not None:
                zero_ref[rows, :] = jnp.zeros((ROW_CHUNK, zero_ref.shape[1]), zero_ref.dtype)
    _row_groups(x_ref.shape[0], body)


def _mod_row(i, tm):
    return jnp.minimum(i // (SEQ // tm), BATCH)


def _mod_spec(j, tm, width=D_MODEL, col=None, tile0=0):
    if col is None:
        return pl.BlockSpec((1, 1, width), lambda i, *_: (_mod_row(i + tile0, tm) * N_MOD + j, 0, 0))
    return pl.BlockSpec((1, 1, width), lambda i, n, *_: (_mod_row(i + tile0, tm) * N_MOD + j, 0, n))


CAST_ROW_ALIGN = 16


def _cast_plan(rows, steps):
    per = pl.cdiv(pl.cdiv(rows, steps), CAST_ROW_ALIGN) * CAST_ROW_ALIGN
    while rows % per:
        per += CAST_ROW_ALIGN
    return per, rows // per


def _cast_specs(w, steps, flat_index):
    per, used = _cast_plan(w.shape[0], steps)
    spec = pl.BlockSpec((per, w.shape[1]), lambda *g: (jnp.minimum(flat_index(*g), used - 1), 0))
    return spec, jax.ShapeDtypeStruct(w.shape, BF16), used


def _hosting_casts(kernel_fn, n_in, n_out, used, flat_index, grid_rank):
    nc = len(used)

    def wrapped(*refs):
        a, b, c = n_in + nc, n_in + nc + n_out, n_in + nc + n_out + nc
        kernel_fn(*refs[:n_in], *refs[a:b], *refs[c:])
        t = flat_index(*[pl.program_id(d) for d in range(grid_rank)])
        for n_used, src_ref, dst_ref in zip(used, refs[n_in:a], refs[b:c]):
            @pl.when(t < n_used)
            def _(src_ref=src_ref, dst_ref=dst_ref):
                dst_ref[...] = src_ref[...].astype(BF16)
    return wrapped


ADA_TN = 1024


def _ada_kernel(c_ref, w_ref, b_ref, o_ref):
    c = c_ref[...]
    s = (c * _sigmoid(c)).astype(BF16)
    o_ref[...] = _dot(s, w_ref[...].astype(BF16)) + b_ref[...]


def _ada(cc, w, b):
    n = w.shape[1]
    out = pl.pallas_call(
        _ada_kernel,
        grid=(n // ADA_TN,),
        in_specs=[pl.BlockSpec((8, D_MODEL), lambda j: (0, 0)),
                  pl.BlockSpec((D_MODEL, ADA_TN), lambda j: (0, j)),
                  pl.BlockSpec((1, ADA_TN), lambda j: (0, j))],
        out_specs=pl.BlockSpec((8, ADA_TN), lambda j: (0, j)),
        out_shape=jax.ShapeDtypeStruct((8, n), F32),
        compiler_params=_cparams(("arbitrary",), 40),
        name="ada_mod",
    )(cc, w, b.reshape(1, n))
    return out.reshape(8 * N_MOD, 1, D_MODEL)


FFN_TM = 1024
FFN_TF = 512
FFN_SLAB = 128


def _ffn_kernel(x_hbm, xn_ref, g_ref, sh_ref, sc_ref, shn_ref, scn_ref, gt_ref, wg_ref, wu_ref, wd_ref, *rest,
                final_norm):
    go_ref = rest[0] if final_norm else None
    o_ref, h0_ref, h1_ref, gs_ref, x_ref, x_sem = rest[-6:]
    i = pl.program_id(0)
    f = pl.program_id(1)
    last = pl.num_programs(1) - 1
    tm, d = x_ref.shape

    def x_copy():
        return pltpu.make_async_copy(x_hbm.at[pl.ds(pl.multiple_of(i * tm, tm), tm), :], x_ref, x_sem)

    @pl.when(f == 0)
    def _():
        x_copy().start()

    @pl.when((i == 0) & (f == 0))
    def _():
        x_copy().wait()
        _norm_mod_rows(x_ref, g_ref, sh_ref, sc_ref, h0_ref, gs_ref)

    def step(h_ref, hn_ref, first):
        gs_ref[0] = jnp.broadcast_to(g_ref[...] * (1.0 + scn_ref[0]), (ROW_CHUNK, d))
        gs_ref[1] = jnp.broadcast_to(shn_ref[0], (ROW_CHUNK, d))
        slab0 = jnp.minimum(f, tm // FFN_SLAB - 1) * FFN_SLAB
        chunks = [pl.ds(k * ROW_CHUNK, ROW_CHUNK) for k in range(FFN_SLAB // ROW_CHUNK)]
        inv = [_inv_rms(xn_ref[rows, :]) for rows in chunks]
        for k, (rows, r) in enumerate(zip(chunks, inv)):
            dst = pl.ds(pl.multiple_of(slab0 + k * ROW_CHUNK, ROW_CHUNK), ROW_CHUNK)
            hn_ref[dst, :] = (xn_ref[rows, :] * r * gs_ref[0] + gs_ref[1]).astype(BF16)

        h = h_ref[...]
        a = _dot(h, wg_ref[...])
        u = _dot(h, wu_ref[...])
        act = (a * _sigmoid(a) * u).astype(BF16)
        if first:
            o_ref[...] = _dot(act, wd_ref[...])
        else:
            o_ref[...] += _dot(act, wd_ref[...])

    for parity, (h_ref, hn_ref) in enumerate(((h0_ref, h1_ref), (h1_ref, h0_ref))):
        for first in (True, False):
            @pl.when((i % 2 == parity) & ((f == 0) == first))
            def _(h_ref=h_ref, hn_ref=hn_ref, first=first):
                step(h_ref, hn_ref, first)

    @pl.when(f == last)
    def _():
        @pl.when(i > 0)
        def _():
            x_copy().wait()

        if not final_norm:
            o_ref[...] = x_ref[...] + (0.5 * gt_ref[0]) * o_ref[...]
        else:
            gs_ref[2] = jnp.broadcast_to(0.5 * gt_ref[0], (ROW_CHUNK, d))
            gs_ref[3] = jnp.broadcast_to(go_ref[...], (ROW_CHUNK, d))

            def body(chunks):
                resid = lambda rows: x_ref[rows, :] + gs_ref[2] * o_ref[rows, :]
                inv = [_inv_rms(resid(rows)) for rows in chunks]
                for rows, r in zip(chunks, inv):
                    o_ref[rows, :] = resid(rows) * r * gs_ref[3]
            _row_groups(tm, body)


def _ffn(x_in, mod, sub, norm_g, w_gu, w_down, *, out_rows, tile0=0, prev=None, final_g=None):
    tm, tf = FFN_TM, FFN_TF
    nf = D_FF // tf
    n_tiles = x_in.shape[0] // tm
    slabs = tm // FFN_SLAB
    assert slabs <= nf
    nxt = lambda i: jnp.minimum(i + 1, n_tiles - 1)
    in_specs = [pl.BlockSpec(memory_space=pl.ANY),
                pl.BlockSpec((FFN_SLAB, D_MODEL), lambda i, f: (nxt(i) * slabs + jnp.minimum(f, slabs - 1), 0)),
                pl.BlockSpec((1, D_MODEL), lambda i, f: (0, 0)),
                _mod_spec(3 * sub, tm, tile0=tile0), _mod_spec(3 * sub + 1, tm, tile0=tile0),
                pl.BlockSpec((1, 1, D_MODEL), lambda i, f: (_mod_row(nxt(i) + tile0, tm) * N_MOD + 3 * sub, 0, 0)),
                pl.BlockSpec((1, 1, D_MODEL), lambda i, f: (_mod_row(nxt(i) + tile0, tm) * N_MOD + 3 * sub + 1, 0, 0)),
                _mod_spec(3 * sub + 2, tm, tile0=tile0),
                pl.BlockSpec((D_MODEL, tf), lambda i, f: (0, f)),
                pl.BlockSpec((D_MODEL, tf), lambda i, f: (0, f + nf)),
                pl.BlockSpec((tf, D_MODEL), lambda i, f: (f, 0))]
    args = [x_in, x_in, norm_g.reshape(1, D_MODEL), mod, mod, mod, mod, mod, w_gu, w_gu, w_down]
    if final_g is not None:
        in_specs.append(pl.BlockSpec((1, D_MODEL), lambda i, f: (0, 0)))
        args.append(final_g.reshape(1, D_MODEL))
    aliases = {}
    if prev is not None:
        aliases = {len(args): 0}
        in_specs.append(pl.BlockSpec(memory_space=pl.ANY))
        args.append(prev)
    return pl.pallas_call(
        functools.partial(_ffn_kernel, final_norm=final_g is not None),
        grid=(n_tiles, nf),
        in_specs=in_specs,
        out_specs=pl.BlockSpec((tm, D_MODEL), lambda i, f: (i + tile0, 0)),
        out_shape=jax.ShapeDtypeStruct((out_rows, D_MODEL), F32),
        scratch_shapes=[pltpu.VMEM((tm, D_MODEL), BF16), pltpu.VMEM((tm, D_MODEL), BF16),
                        pltpu.VMEM((4, ROW_CHUNK, D_MODEL), F32), pltpu.VMEM((tm, D_MODEL), F32),
                        pltpu.SemaphoreType.DMA(())],
        input_output_aliases=aliases,
        compiler_params=_cparams(("arbitrary", "arbitrary"), 56),
        name="ffn_half",
    )(*args)


NMM_TM = 1024
NMM_TN = 1024


def _nmm_kernel(x_ref, g_ref, sh_ref, sc_ref, w_ref, *rest, gla):
    if gla:
        wn_ref, wc_ref, o_ref, ob_ref, zc_ref, zs_ref, on_ref, h_ref, gs_ref = rest
    else:
        o_ref, h_ref, gs_ref = rest
    j = pl.program_id(1)

    @pl.when(j == 0)
    def _():
        _norm_mod_rows(x_ref, g_ref, sh_ref, sc_ref, h_ref, gs_ref)
        if gla:
            on_ref[...] = _dot(h_ref[...], wn_ref[...])

    if not gla:
        o_ref[...] = _dot(h_ref[...], w_ref[...]).astype(o_ref.dtype)
    else:
        @pl.when(j % 2 == 0)
        def _():
            o_ref[...] = _dot(h_ref[...], w_ref[...])

        @pl.when(j == 1)
        def _():
            ob_ref[...] = _dot(h_ref[...], w_ref[...]).astype(BF16)

        @pl.when(j == 3)
        def _():
            f = _dot(h_ref[...], w_ref[...]).astype(BF16)
            for grp in range(FNET_GROUPS):
                cols = slice(grp * FNET_GROUP_DIM, (grp + 1) * FNET_GROUP_DIM)
                z = _dot(f[:, cols], wc_ref[...])
                zc_ref[:, cols] = z[:, :FNET_GROUP_DIM].astype(BF16)
                zs_ref[:, cols] = z[:, FNET_GROUP_DIM:].astype(BF16)


def _nmm(xs, mod, sub, norm_g, w, w_narrow=None, w_chan=None, cast_ws=()):
    tm, tn = NMM_TM, NMM_TN
    n = w.shape[1]
    gla = w_narrow is not None
    in_specs = [pl.BlockSpec((tm, D_MODEL), lambda i, j: (i, 0)),
                pl.BlockSpec((1, D_MODEL), lambda i, j: (0, 0)),
                _mod_spec(3 * sub, tm), _mod_spec(3 * sub + 1, tm),
                pl.BlockSpec((D_MODEL, tn), lambda i, j: (0, j))]
    args = [xs, norm_g.reshape(1, D_MODEL), mod, mod, w]
    if gla:
        nn = w_narrow.shape[1]
        assert n == 4 * tn and tn == GLA_V_W == FNET_W
        in_specs += [pl.BlockSpec((D_MODEL, nn), lambda i, j: (0, 0)),
                     pl.BlockSpec(w_chan.shape, lambda i, j: (0, 0))]
        args += [w_narrow, w_chan]
        row_tile = pl.BlockSpec((tm, tn), lambda i, j: (i, 0))
        out_specs = [pl.BlockSpec((tm, tn), lambda i, j: (i, j // 2)), row_tile, row_tile, row_tile,
                     pl.BlockSpec((tm, nn), lambda i, j: (i, 0))]
        out_shape = [jax.ShapeDtypeStruct((N_TOK, n // 2), F32)] + [jax.ShapeDtypeStruct((N_TOK, tn), BF16)] * 3 + [
            jax.ShapeDtypeStruct((N_TOK, nn), F32)]
    else:
        out_specs = [pl.BlockSpec((tm, tn), lambda i, j: (i, j))]
        out_shape = [jax.ShapeDtypeStruct((N_TOK, n), BF16)]
    nj = n // tn
    flat = lambda i, j: i * nj + j
    casts = [_cast_specs(cw, (N_TOK // tm) * nj, flat) for cw in cast_ws]
    return pl.pallas_call(
        _hosting_casts(functools.partial(_nmm_kernel, gla=gla), len(args), len(out_specs),
                       [u for _, _, u in casts], flat, 2),
        grid=(N_TOK // tm, nj),
        in_specs=in_specs + [s for s, _, _ in casts],
        out_specs=out_specs + [s for s, _, _ in casts],
        out_shape=out_shape + [o for _, o, _ in casts],
        scratch_shapes=[pltpu.VMEM((tm, D_MODEL), BF16), pltpu.VMEM((2, ROW_CHUNK, D_MODEL), F32)],
        compiler_params=_cparams(("arbitrary", "arbitrary"), 56),
        name="norm_mod_matmul",
    )(*args, *cast_ws)


GLA_BLOCK = CTX_LEN
GLA_LAT_BLOCKS = SEQ // GLA_BLOCK
GLA_STEPS = 1 + GLA_LAT_BLOCKS
GLA_LR_PAD = 128


def _rope(x, cos, sin_signed):
    return x * cos + pltpu.roll(x, GLA_DK // 2, 1) * sin_signed


def _gla_kernel(qf, kf, vf, lf, cf, sf, qb, kb, vb, lb, cb, sb, gwf, gbf, gwb, gbb,
                of_ref, ob_ref, st_ref):
    @pl.when(pl.program_id(1) == 0)
    def _():
        st_ref[...] = jnp.zeros_like(st_ref)

    c = GLA_CHUNK
    n = GLA_BLOCK
    row = lax.broadcasted_iota(jnp.int32, (n, n), 0)
    col = lax.broadcasted_iota(jnp.int32, (n, n), 1)
    same_chunk = (row // c) == (col // c)
    row_c = lax.broadcasted_iota(jnp.int32, (c, c), 0)
    col_c = lax.broadcasted_iota(jnp.int32, (c, c), 1)
    scale = GLA_DK ** -0.5

    dirs = ((qf, kf, vf, lf, cf, sf, gwf, gbf, of_ref), (qb, kb, vb, lb, cb, sb, gwb, gbb, ob_ref))
    n_chunks = n // c
    heads = [(slice(h * GLA_DK, (h + 1) * GLA_DK), slice(h * GLA_DV, (h + 1) * GLA_DV)) for h in range(GLA_HEADS)]
    tris, decays = [], []
    for d, (q_ref, k_ref, v_ref, l_ref, c_ref, s_ref, gw_ref, gb_ref, o_ref) in enumerate(dirs):
        causal = (row >= col) if d == 0 else (row <= col)
        tri_b = jnp.where(same_chunk & causal, 1.0, 0.0).astype(BF16)
        tris.append((row_c >= col_c) if d == 0 else (row_c <= col_c))
        x = _dot3(l_ref[...], gw_ref[...]) + gb_ref[...]
        g = (jnp.minimum(x, 0.0) - jnp.log1p(jnp.exp(-jnp.abs(x)))) * (1.0 / GLA_TAU)
        bc_all = _dot_exact_lhs(tri_b, g)
        per_chunk = []
        for ci in (range(n_chunks) if d == 0 else range(n_chunks - 1, -1, -1)):
            rows = slice(ci * c, (ci + 1) * c)
            bc = bc_all[rows, :]
            mid = bc[c // 2:c // 2 + 1, :]
            last = bc[c - 1:c, :] if d == 0 else bc[0:1, :]
            per_chunk.append((rows, jnp.exp(bc), jnp.exp(bc - mid), jnp.exp(mid - bc), jnp.exp(last - bc),
                              jnp.exp(last)))
        decays.append(per_chunk)

    def local_part(k):
        out = {}
        for d, (q_ref, k_ref, v_ref, _, c_ref, s_ref, _, _, _) in enumerate(dirs):
            rows, e_q, e_qm, e_km, e_kl, e_l = decays[d][k]
            cos, sin = c_ref[rows, :], s_ref[rows, :]
            for h, (sk, sv) in enumerate(heads):
                qh = _rope(q_ref[rows, sk] * scale, cos, sin)
                kh = _rope(k_ref[rows, sk], cos, sin)
                vh = v_ref[rows, sv].astype(BF16)
                att = _dot_nt((qh * e_qm[:, sk]).astype(BF16), (kh * e_km[:, sk]).astype(BF16))
                kv = _dot_tn(vh, (kh * e_kl[:, sk]).astype(BF16))
                out[d, h] = ((qh * e_q[:, sk]).astype(BF16), jnp.where(tris[d], att, 0.0).astype(BF16), vh, kv,
                             e_l[:, sk])
        return out

    st = {(d, h): st_ref[d, h] for d in range(2) for h in range(GLA_HEADS)}
    ahead = 1
    pending = {k: local_part(k) for k in range(min(ahead, n_chunks))}
    for k in range(n_chunks):
        if k + ahead < n_chunks:
            pending[k + ahead] = local_part(k + ahead)
        local = pending.pop(k)
        for d in range(2):
            rows = decays[d][k][0]
            for h, (sk, sv) in enumerate(heads):
                qe, att, vh, kv, e_l = local[d, h]
                dirs[d][-1][rows, sv] = _dot_nt(qe, st[d, h].astype(BF16)) + _dot(att, vh)
                st[d, h] = st[d, h] * e_l + kv
    for (d, h), val in st.items():
        st_ref[d, h] = val


def _gla(pf, pb, lr, cos_t, sin_t, gwf, gbf, gwb, gbb, cast_ws):
    c = GLA_BLOCK
    ctx0 = N_LAT // c

    def fwd_blk(b, s):
        return jnp.where(s == 0, ctx0 + b, GLA_LAT_BLOCKS * b + s - 1)

    def bwd_blk(b, s):
        return jnp.where(s == 0, ctx0 + b, GLA_LAT_BLOCKS * b + (GLA_STEPS - 1 - s))

    def fwd_rope(b, s):
        return jnp.where(s == 0, GLA_LAT_BLOCKS, s - 1)

    def bwd_rope(b, s):
        return jnp.where(s == 0, GLA_LAT_BLOCKS, GLA_STEPS - 1 - s)

    def dir_specs(blk, rope):
        return [pl.BlockSpec((c, GLA_QK_W), lambda b, s: (blk(b, s), 0)),
                pl.BlockSpec((c, GLA_QK_W), lambda b, s: (blk(b, s), 1)),
                pl.BlockSpec((c, GLA_V_W), lambda b, s: (blk(b, s), 0)),
                pl.BlockSpec((c, GLA_LR_PAD), lambda b, s: (blk(b, s), 0)),
                pl.BlockSpec((c, GLA_DK), lambda b, s: (rope(b, s), 0)),
                pl.BlockSpec((c, GLA_DK), lambda b, s: (rope(b, s), 0))]

    const = lambda shape: pl.BlockSpec(shape, lambda b, s: (0, 0))
    in_specs = (dir_specs(fwd_blk, fwd_rope) + dir_specs(bwd_blk, bwd_rope)
                + [const((GLA_LR_PAD, GLA_QK_W)), const((1, GLA_QK_W)),
                   const((GLA_LR_PAD, GLA_QK_W)), const((1, GLA_QK_W))])
    flat = lambda b, s: b * GLA_STEPS + s
    casts = [_cast_specs(w, BATCH * GLA_STEPS, flat) for w in cast_ws]
    args = [pf, pf, pb, lr, cos_t, sin_t, pf, pf, pb, lr, cos_t, sin_t, gwf, gbf, gwb, gbb]
    return pl.pallas_call(
        _hosting_casts(_gla_kernel, len(args), 2, [u for _, _, u in casts], flat, 2),
        grid=(BATCH, GLA_STEPS),
        in_specs=in_specs + [s for s, _, _ in casts],
        out_specs=[pl.BlockSpec((c, GLA_V_W), lambda b, s: (fwd_blk(b, s), 0)),
                   pl.BlockSpec((c, GLA_V_W), lambda b, s: (bwd_blk(b, s), 0))] + [s for s, _, _ in casts],
        out_shape=[jax.ShapeDtypeStruct((N_TOK, GLA_V_W), F32)] * 2 + [o for _, o, _ in casts],
        scratch_shapes=[pltpu.VMEM((2, GLA_HEADS, GLA_DV, GLA_DK), F32)],
        compiler_params=_cparams(("arbitrary", "arbitrary"), 40),
        name="gla_scan",
    )(*args, *cast_ws)


def _tdft_kernel(c_ref, s_ref, zc_ref, zs_ref, *rest):
    o_ref = rest[-1]
    o_ref[...] = (_dot(c_ref[...], zc_ref[...]) - _dot(s_ref[...], zs_ref[...])).astype(o_ref.dtype)


def _time_dft(cmat, smat, zc, zs, t_len, tm, tn, row0, prev=None, cast_ws=()):
    zb0 = row0 // t_len
    ob0 = row0 // tm
    mt = t_len // tm
    in_specs = [pl.BlockSpec((tm, t_len), lambda b, n, m: (m, 0)),
                pl.BlockSpec((tm, t_len), lambda b, n, m: (m, 0)),
                pl.BlockSpec((t_len, tn), lambda b, n, m: (zb0 + b, n)),
                pl.BlockSpec((t_len, tn), lambda b, n, m: (zb0 + b, n))]
    args = [cmat, smat, zc, zs]
    aliases = {}
    if prev is not None:
        in_specs.append(pl.BlockSpec(memory_space=pl.ANY))
        args.append(prev)
        aliases = {4: 0}
    nt = FNET_W // tn
    flat = lambda b, n, m: (b * nt + n) * mt + m
    casts = [_cast_specs(w, BATCH * nt * mt, flat) for w in cast_ws]
    return pl.pallas_call(
        _hosting_casts(_tdft_kernel, len(args), 1, [u for _, _, u in casts], flat, 3),
        grid=(BATCH, nt, mt),
        in_specs=in_specs + [s for s, _, _ in casts],
        out_specs=[pl.BlockSpec((tm, tn), lambda b, n, m: (ob0 + b * mt + m, n))] + [s for s, _, _ in casts],
        out_shape=[jax.ShapeDtypeStruct((N_TOK, FNET_W), BF16)] + [o for _, o, _ in casts],
        input_output_aliases=aliases,
        compiler_params=_cparams(("arbitrary", "arbitrary", "arbitrary"), 48),
        name="fnet_time_dft",
    )(*args, *cast_ws)


FOLD_BLK = 256


def _tdft_fold_kernel(c_ref, s_ref, zc_ref, zs_ref, j1_ref, j2_ref, o_ref, zcf_ref, zsf_ref):
    half, blk = zcf_ref.shape[0], FOLD_BLK
    nblk = 2 * half // blk

    @pl.when(pl.program_id(2) == 0)
    def _():
        for kb in range(half // blk):
            rows = slice(kb * blk, (kb + 1) * blk)
            mirror = slice((nblk - 1 - kb) * blk, (nblk - kb) * blk)
            after = slice((nblk - kb) * blk, (nblk - kb + 1) * blk)
            for z_ref, zf_ref, sign in ((zc_ref, zcf_ref, 1.0), (zs_ref, zsf_ref, -1.0)):
                partner = _dot(j1_ref[...], z_ref[mirror, :])
                if kb > 0:
                    partner = partner + _dot(j2_ref[...], z_ref[after, :])
                zf_ref[rows, :] = (z_ref[rows, :].astype(F32) + sign * partner).astype(BF16)

    tm = o_ref.shape[0]
    j = pl.program_id(2) * tm + lax.broadcasted_iota(jnp.int32, (tm, 1), 0)
    sign = jnp.where(j % 2 == 0, SEQ ** -0.5, -(SEQ ** -0.5))
    mid = zc_ref[half:half + 1, :].astype(F32)
    o = _dot(c_ref[...], zcf_ref[...]) - _dot(s_ref[...], zsf_ref[...]) + sign * mid
    o_ref[...] = o.astype(o_ref.dtype)


def _time_dft_folded(cmat, smat, zc, zs, tm, tn, cast_ws=()):
    half = SEQ // 2
    mt = SEQ // tm
    nt = FNET_W // tn
    i = np.arange(FOLD_BLK)
    j1 = jnp.asarray((i[:, None] + i[None, :] == FOLD_BLK) & (i[:, None] > 0), BF16)
    j2 = jnp.asarray((i[:, None] == 0) & (i[None, :] == 0), BF16)
    args = [cmat, smat, zc, zs, j1, j2]
    in_specs = [pl.BlockSpec((tm, half), lambda b, n, m: (m, 0)),
                pl.BlockSpec((tm, half), lambda b, n, m: (m, 0)),
                pl.BlockSpec((SEQ, tn), lambda b, n, m: (b, n)),
                pl.BlockSpec((SEQ, tn), lambda b, n, m: (b, n)),
                pl.BlockSpec((FOLD_BLK, FOLD_BLK), lambda b, n, m: (0, 0)),
                pl.BlockSpec((FOLD_BLK, FOLD_BLK), lambda b, n, m: (0, 0))]
    flat = lambda b, n, m: (b * nt + n) * mt + m
    casts = [_cast_specs(w, BATCH * nt * mt, flat) for w in cast_ws]
    return pl.pallas_call(
        _hosting_casts(_tdft_fold_kernel, len(args), 1, [u for _, _, u in casts], flat, 3),
        grid=(BATCH, nt, mt),
        in_specs=in_specs + [s for s, _, _ in casts],
        out_specs=[pl.BlockSpec((tm, tn), lambda b, n, m: (b * mt + m, n))] + [s for s, _, _ in casts],
        out_shape=[jax.ShapeDtypeStruct((N_TOK, FNET_W), BF16)] + [o for _, o, _ in casts],
        scratch_shapes=[pltpu.VMEM((half, tn), BF16), pltpu.VMEM((half, tn), BF16)],
        compiler_params=_cparams(("arbitrary", "arbitrary", "arbitrary"), 48),
        name="fnet_time_dft_folded",
    )(*args, *cast_ws)


def _dft_tables(n):
    k = np.arange(n)
    ang = 2.0 * np.pi * ((k[:, None] * k[None, :]) % n) / n
    return np.cos(ang) / np.sqrt(n), np.sin(ang) / np.sqrt(n)


def _dft_table_kernel(ac_ref, as_ref, bc_ref, bs_ref, co_ref, so_ref):
    ac, as_ = ac_ref[0], as_ref[0]
    bc, bs = bc_ref[...], bs_ref[...]
    co_ref[...] = (ac * bc - as_ * bs).astype(BF16)
    so_ref[...] = (ac * bs + as_ * bc).astype(BF16)


def _big_dft_tables():
    r = GRID_W
    k = np.arange(SEQ)
    j = np.arange(r)
    pa = 2.0 * np.pi * ((j[:, None] * r * k[None, :]) % SEQ) / SEQ
    pb = 2.0 * np.pi * ((j[:, None] * k[None, :]) % SEQ) / SEQ
    ac = jnp.asarray(np.cos(pa).reshape(r, 1, SEQ), F32)
    as_ = jnp.asarray(np.sin(pa).reshape(r, 1, SEQ), F32)
    bc = jnp.asarray(np.cos(pb) * SEQ ** -0.5, F32)
    bs = jnp.asarray(np.sin(pb) * SEQ ** -0.5, F32)
    row = pl.BlockSpec((1, 1, SEQ), lambda i: (i, 0, 0))
    full = pl.BlockSpec((r, SEQ), lambda i: (0, 0))
    return pl.pallas_call(
        _dft_table_kernel,
        grid=(r,),
        in_specs=[row, row, full, full],
        out_specs=[pl.BlockSpec((r, SEQ), lambda i: (i, 0))] * 2,
        out_shape=[jax.ShapeDtypeStruct((SEQ, SEQ), BF16)] * 2,
        compiler_params=_cparams(("parallel",), 16),
        name="dft_tables",
    )(ac, as_, bc, bs)


GO_TM = 512
GO_TN = 2048


def _glaout_kernel(of_ref, ob_ref, r_ref, fm_ref, gn_ref, x_ref, gt_ref, w_ref, o_ref, h_ref):
    @pl.when(pl.program_id(1) == 0)
    def _():
        heads = [slice(h * GLA_DV, (h + 1) * GLA_DV) for h in range(GLA_HEADS)]

        def body(chunks):
            inv = [[_inv_rms(of_ref[rows, sv] + ob_ref[rows, sv]) for sv in heads] for rows in chunks]
            for rows, inv_c in zip(chunks, inv):
                for sv, r in zip(heads, inv_c):
                    o = (of_ref[rows, sv] + ob_ref[rows, sv]) * r
                    gate = r_ref[rows, sv]
                    h_ref[rows, sv] = (o * gn_ref[:, sv] * (gate * _sigmoid(gate))).astype(BF16)
                h_ref[rows, GLA_V_W:] = fm_ref[rows, :].astype(BF16)
        _row_groups(of_ref.shape[0], body, group=4)

    o_ref[...] = x_ref[...] + gt_ref[0] * _dot(h_ref[...], w_ref[...])


def _glaout(of, ob, pf, fm, g_norm, xs, mod, w_out):
    tm, tn = GO_TM, GO_TN
    half = lambda col: pl.BlockSpec((tm, GLA_V_W), lambda i, j: (i, col))
    return pl.pallas_call(
        _glaout_kernel,
        grid=(N_TOK // tm, D_MODEL // tn),
        in_specs=[half(0), half(0), half(1), half(0),
                  pl.BlockSpec((1, GLA_V_W), lambda i, j: (0, 0)),
                  pl.BlockSpec((tm, tn), lambda i, j: (i, j)),
                  _mod_spec(5, tm, tn, col=True),
                  pl.BlockSpec((D_MODEL, tn), lambda i, j: (0, j))],
        out_specs=pl.BlockSpec((tm, tn), lambda i, j: (i, j)),
        out_shape=jax.ShapeDtypeStruct((N_TOK, D_MODEL), F32),
        scratch_shapes=[pltpu.VMEM((tm, D_MODEL), BF16)],
        compiler_params=_cparams(("parallel", "arbitrary"), 56),
        name="gla_fnet_out",
    )(of, ob, pf, fm, g_norm.reshape(1, GLA_V_W), xs, mod, w_out)


NA_HROWS = NA_QROWS // 2
NA_WROWS = NA_HROWS + NA_KH
NA_TQ = NA_QROWS * GRID_W
NA_HQ = NA_HROWS * GRID_W
NA_WK = NA_WROWS * GRID_W
NA_KBLK = NA_HROWS * GRID_W
NA_WBLKS = NA_WK // NA_KBLK
NA_MASKED = 2 * NA_KH - 1
NA_ROW_BLOCKS = (SEQ // GRID_W) // NA_QROWS
NA_CASES = (0, 1, NA_ROW_BLOCKS - 1)
NA_PAIRS = 2
NA_LANES = NA_PAIRS * 2 * NA_HEAD_DIM


def _na_window_row0(j, half):
    q0 = NA_QROWS * j + NA_HROWS * half
    return int(np.clip(q0 - NA_KH // 2, 0, SEQ // GRID_W - NA_WROWS))


def _na_bias_slots():
    rows = SEQ // GRID_W
    slots = np.full((len(NA_CASES), 2, NA_HROWS, NA_WROWS), NA_MASKED, np.int32)
    for case, j in enumerate(NA_CASES):
        for half in range(2):
            w0 = _na_window_row0(j, half)
            for i in range(NA_HROWS):
                qr = NA_QROWS * j + NA_HROWS * half + i
                r0 = int(np.clip(qr - NA_KH // 2, 0, rows - NA_KH))
                for l in range(NA_WROWS):
                    kr = w0 + l
                    if r0 <= kr < r0 + NA_KH:
                        slots[case, half, i, l] = kr - qr + NA_KH - 1
    return slots


def _na_kernel(q_ref, *refs):
    nb = NA_WBLKS
    ka, kb, va, vb = refs[0:nb], refs[nb:2 * nb], refs[2 * nb:3 * nb], refs[3 * nb:4 * nb]
    kc_ref, vc_ref, rp_ref, o_ref, bias_ref, td_ref = refs[4 * nb:]
    j = pl.program_id(1)

    @pl.when((j == 0) & (pl.program_id(2) == 0))
    def _():
        qc = lax.broadcasted_iota(jnp.int32, (GRID_W, 2 * GRID_W), 0)
        kc = lax.broadcasted_iota(jnp.int32, (GRID_W, 2 * GRID_W), 1) % GRID_W
        start = jnp.clip(qc - NA_KW // 2, 0, GRID_W - NA_KW)
        window = (kc >= start) & (kc < start + NA_KW)
        for hh in range(2 * NA_PAIRS):
            for slot in range(NA_MASKED):
                td_ref[hh, slot] = jnp.broadcast_to(rp_ref[hh, slot:slot + 1, :], (GRID_W, 2 * GRID_W))
                tile = pltpu.roll(td_ref[hh, slot], 2 * GRID_W - (NA_KW - 1), 1, stride=1, stride_axis=0)
                td_ref[hh, slot] = jnp.where(window, tile, NEG_BIG)
            td_ref[hh, NA_MASKED] = jnp.full((GRID_W, 2 * GRID_W), NEG_BIG, F32)

        slots = _na_bias_slots()
        left = lax.broadcasted_iota(jnp.int32, (GRID_W, 2 * GRID_W), 1) < GRID_W
        for case in range(len(NA_CASES)):
            for half in range(2):
                for hh in range(2 * NA_PAIRS):
                    dst = (case * 2 + half) * 2 * NA_PAIRS + hh
                    for i in range(NA_HROWS):
                        for p in range(NA_WROWS // 2):
                            sl = int(slots[case, half, i, 2 * p])
                            sr = int(slots[case, half, i, 2 * p + 1])
                            tile = td_ref[hh, sl] if sl == sr else jnp.where(left, td_ref[hh, sl], td_ref[hh, sr])
                            bias_ref[dst, i * GRID_W:(i + 1) * GRID_W, p * 2 * GRID_W:(p + 1) * 2 * GRID_W] = tile

    case = jnp.where(j == 0, 0, jnp.where(j == NA_ROW_BLOCKS - 1, 2, 1))
    lane = lax.broadcasted_iota(jnp.int32, (NA_HQ, 2 * NA_HEAD_DIM), 1)
    kv_refs = ((ka, va), (kb, vb))
    items = [(half, pair, hh) for half in range(2) for pair in range(NA_PAIRS) for hh in range(2)]
    pair_lanes = lambda pair: slice(pair * 2 * NA_HEAD_DIM, (pair + 1) * 2 * NA_HEAD_DIM)

    def scores(half, pair, hh):
        lanes = pair_lanes(pair)
        q = q_ref[half * NA_HQ:(half + 1) * NA_HQ, lanes]
        qm = jnp.where((lane // NA_HEAD_DIM) == hh, q, jnp.zeros_like(q))
        k = jnp.concatenate([r[:, lanes] for r in kv_refs[half][0]], axis=0)
        bias = bias_ref[(case * 2 + half) * 2 * NA_PAIRS + 2 * pair + hh]
        return _dot_nt(qm, k) + bias, _dot_nt(qm, kc_ref[:, lanes])

    def softmax(s_loc, s_ctx):
        m = jnp.maximum(jnp.max(s_loc, axis=-1, keepdims=True), jnp.max(s_ctx, axis=-1, keepdims=True))
        p_loc = jnp.exp2(s_loc - m)
        p_ctx = jnp.exp2(s_ctx - m)
        denom = jnp.sum(p_loc, axis=-1, keepdims=True) + jnp.sum(p_ctx, axis=-1, keepdims=True)
        return p_loc.astype(BF16), p_ctx.astype(BF16), denom

    def weighted(half, pair, p_loc, p_ctx, denom):
        lanes = pair_lanes(pair)
        v = jnp.concatenate([r[:, lanes] for r in kv_refs[half][1]], axis=0)
        return (_dot(p_loc, v) + _dot(p_ctx, vc_ref[:, lanes])) / denom

    s, p, o = {}, {}, {}
    for t in range(len(items) + 2):
        if t < len(items):
            s[t] = scores(*items[t])
        if 0 <= t - 1 < len(items):
            p[t - 1] = softmax(*s.pop(t - 1))
        if 0 <= t - 2 < len(items):
            o[t - 2] = weighted(*items[t - 2][:2], *p.pop(t - 2))
    first = (lane // NA_HEAD_DIM) == 0
    for n, (half, pair, hh) in enumerate(items):
        if hh == 0:
            out = jnp.where(first, o[n], o[n + 1])
            o_ref[half * NA_HQ:(half + 1) * NA_HQ, pair_lanes(pair)] = out.astype(o_ref.dtype)


def _na(qkv, bias_rows, cast_ws=()):
    hp_n = NA_HEADS // (2 * NA_PAIRS)
    jn = NA_ROW_BLOCKS
    kb_per_img = SEQ // NA_KBLK
    last_blk = kb_per_img - NA_WBLKS

    def kv_spec(col0, half, t):
        return pl.BlockSpec((NA_KBLK, NA_LANES), lambda hp, j, b: (
            b * kb_per_img + jnp.clip(2 * j + half - 1, 0, last_blk) + t, col0 + hp))

    def kv_specs(col0):
        return [kv_spec(col0, half, t) for half in range(2) for t in range(NA_WBLKS)]

    ctx0 = N_LAT // CTX_LEN
    in_specs = ([pl.BlockSpec((NA_TQ, NA_LANES), lambda hp, j, b: (b * jn + j, hp))]
                + kv_specs(hp_n) + kv_specs(2 * hp_n)
                + [pl.BlockSpec((CTX_LEN, NA_LANES), lambda hp, j, b: (ctx0 + b, hp_n + hp)),
                   pl.BlockSpec((CTX_LEN, NA_LANES), lambda hp, j, b: (ctx0 + b, 2 * hp_n + hp)),
                   pl.BlockSpec((2 * NA_PAIRS, 2 * NA_KH, 2 * GRID_W), lambda hp, j, b: (hp, 0, 0))])
    flat = lambda hp, j, b: (hp * jn + j) * BATCH + b
    casts = [_cast_specs(w, hp_n * jn * BATCH, flat) for w in cast_ws]
    return pl.pallas_call(
        _hosting_casts(_na_kernel, len(in_specs), 1, [u for _, _, u in casts], flat, 3),
        grid=(hp_n, jn, BATCH),
        in_specs=in_specs + [s for s, _, _ in casts],
        out_specs=[pl.BlockSpec((NA_TQ, NA_LANES), lambda hp, j, b: (b * jn + j, hp))] + [s for s, _, _ in casts],
        out_shape=[jax.ShapeDtypeStruct((N_LAT, D_MODEL), BF16)] + [o for _, o, _ in casts],
        scratch_shapes=[pltpu.VMEM((len(NA_CASES) * 4 * NA_PAIRS, NA_HQ, NA_WK), F32),
                        pltpu.VMEM((2 * NA_PAIRS, 2 * NA_KH, GRID_W, 2 * GRID_W), F32)],
        compiler_params=_cparams(("arbitrary", "arbitrary", "arbitrary"), 48),
        name="neighbourhood_attention",
    )(qkv, *([qkv] * (4 * NA_WBLKS)), qkv, qkv, bias_rows, *cast_ws)


def _na_bias_rows(rpb):
    r = rpb.astype(F32) * LOG2_E
    r = jnp.pad(r, ((0, 0), (0, 2 * NA_KH - r.shape[1]), (0, GRID_W - r.shape[2])))
    return jnp.concatenate([r, r], axis=-1)


MR_TM = 1024
MR_TN = 1024


def _mmres_kernel(a_ref, w_ref, x_ref, gt_ref, o_ref):
    o_ref[...] = x_ref[...] + gt_ref[0] * _dot(a_ref[...], w_ref[...])


def _mm_res(a, w, xs, mod, n_rows):
    tm, tn = MR_TM, MR_TN
    return pl.pallas_call(
        _mmres_kernel,
        grid=(n_rows // tm, D_MODEL // tn),
        in_specs=[pl.BlockSpec((tm, D_MODEL), lambda i, j: (i, 0)),
                  pl.BlockSpec((D_MODEL, tn), lambda i, j: (0, j)),
                  pl.BlockSpec((tm, tn), lambda i, j: (i, j)),
                  _mod_spec(5, tm, tn, col=True)],
        out_specs=pl.BlockSpec((tm, tn), lambda i, j: (i, j)),
        out_shape=jax.ShapeDtypeStruct((n_rows, D_MODEL), F32),
        compiler_params=_cparams(("parallel", "arbitrary"), 40),
        name="matmul_gated_residual",
    )(a, w, xs, mod)


def _rope_tables():
    half = GLA_DK // 4
    inv_freq = ROPE_BASE ** (-jnp.arange(half, dtype=F32) / half)
    t = jnp.arange(SEQ)
    ang_r = (t // GRID_W).astype(F32)[:, None] * inv_freq[None, :]
    ang_c = (t % GRID_W).astype(F32)[:, None] * inv_freq[None, :]
    cr, sr, cc, sc = jnp.cos(ang_r), jnp.sin(ang_r), jnp.cos(ang_c), jnp.sin(ang_c)
    cos_t = jnp.concatenate([cr, cc, cr, cc], axis=-1)
    sin_t = jnp.concatenate([-sr, -sc, sr, sc], axis=-1)
    cos_t = jnp.concatenate([cos_t, jnp.ones((GLA_BLOCK, GLA_DK), F32)], axis=0)
    sin_t = jnp.concatenate([sin_t, jnp.zeros((GLA_BLOCK, GLA_DK), F32)], axis=0)
    return cos_t, sin_t


def _pair_layout(w):
    lead = w.shape[:-1]
    quarter = GLA_DK // 4
    w = w.reshape(*lead, GLA_HEADS, 2, 2, quarter)
    return jnp.swapaxes(w, -2, -3).reshape(*lead, GLA_QK_W)


def kernel(x, c, ctx, c_ctx, l0_ada_w, l0_ada_b, l0_norm_ffn1, l0_norm_mix, l0_norm_ffn2, l0_ffn1_w_gu, l0_ffn1_w_down, l0_ffn2_w_gu, l0_ffn2_w_down, l0_w_in, l0_gla_gate_w_fwd, l0_gla_gate_b_fwd, l0_gla_gate_w_bwd, l0_gla_gate_b_bwd, l0_gla_norm, l0_w_out, l1_ada_w, l1_ada_b, l1_norm_ffn1, l1_norm_mix, l1_norm_ffn2, l1_ffn1_w_gu, l1_ffn1_w_down, l1_ffn2_w_gu, l1_ffn2_w_down, l1_w_qkv, l1_rpb, l1_w_out, norm_out):
    bf = lambda w: w.astype(BF16)
    cc = jnp.concatenate([c, c_ctx[None, :], jnp.zeros((8 - BATCH - 1, D_MODEL), F32)], axis=0)

    mod0 = _ada(cc, l0_ada_w, l0_ada_b)
    w_gu, w_down = bf(l0_ffn1_w_gu), bf(l0_ffn1_w_down)
    xs = _ffn(x.reshape(N_LAT, D_MODEL), mod0, 0, l0_norm_ffn1, w_gu, w_down, out_rows=N_TOK)
    xs = _ffn(ctx.reshape(N_CTX, D_MODEL), mod0, 0, l0_norm_ffn1, w_gu, w_down, out_rows=N_TOK,
              tile0=N_LAT // FFN_TM, prev=xs)

    a0 = 2 * GLA_QK_W + GLA_V_W
    a1 = a0 + 2 * GLA_LOWRANK
    w_in = bf(l0_w_in)
    w_main = jnp.concatenate([_pair_layout(w_in[:, :GLA_QK_W]), _pair_layout(w_in[:, GLA_QK_W:2 * GLA_QK_W]),
                              w_in[:, 2 * GLA_QK_W:a0], w_in[:, a1:]], axis=1)
    w_lr = jnp.pad(w_in[:, a0:a1], ((0, 0), (0, GLA_LR_PAD - 2 * GLA_LOWRANK)))
    cc_np, sc_np = _dft_tables(FNET_GROUP_DIM)
    wc = jnp.asarray(np.concatenate([cc_np, sc_np], axis=1), F32).astype(BF16)
    pf, pb, zc, zs, lr = _nmm(xs, mod0, 1, l0_norm_mix, w_main, w_narrow=w_lr, w_chan=wc)

    cos_t, sin_t = _rope_tables()
    gwf = jnp.pad(_pair_layout(l0_gla_gate_w_fwd), ((0, GLA_LR_PAD - GLA_LOWRANK), (0, 0)))
    gwb = jnp.pad(_pair_layout(l0_gla_gate_w_bwd), ((GLA_LOWRANK, GLA_LR_PAD - 2 * GLA_LOWRANK), (0, 0)))
    of, ob, w_gu2, w_down2 = _gla(pf, pb, lr, cos_t, sin_t, gwf, _pair_layout(l0_gla_gate_b_fwd).reshape(1, -1),
                                  gwb, _pair_layout(l0_gla_gate_b_bwd).reshape(1, -1),
                                  cast_ws=(l0_ffn2_w_gu, l0_ffn2_w_down))

    cos_big, sin_big = _big_dft_tables()
    fm, w_gu3, w_down3 = _time_dft_folded(cos_big, sin_big, zc, zs, 512, 512, cast_ws=(l1_ffn1_w_gu, l1_ffn1_w_down))
    ct_np, st_np = _dft_tables(CTX_LEN)
    fm = _time_dft(jnp.asarray(ct_np, F32).astype(BF16), jnp.asarray(st_np, F32).astype(BF16),
                   zc, zs, CTX_LEN, CTX_LEN, 512, N_LAT, prev=fm)[0]

    xs = _glaout(of, ob, pf, fm, l0_gla_norm, xs, mod0, bf(l0_w_out))
    xs = _ffn(xs, mod0, 2, l0_norm_ffn2, w_gu2, w_down2, out_rows=N_TOK)

    mod1 = _ada(cc, l1_ada_w, l1_ada_b)
    xs = _ffn(xs, mod1, 0, l1_norm_ffn1, w_gu3, w_down3, out_rows=N_TOK)
    w_qkv = bf(jnp.concatenate([l1_w_qkv[:, :D_MODEL] * NA_Q_SCALE, l1_w_qkv[:, D_MODEL:]], axis=1))
    qkv, w_gu4, w_down4 = _nmm(xs, mod1, 1, l1_norm_mix, w_qkv, cast_ws=(l1_ffn2_w_gu, l1_ffn2_w_down))
    ao = _na(qkv, _na_bias_rows(l1_rpb))[0]
    xl = _mm_res(ao, bf(l1_w_out), xs, mod1, N_LAT)
    out = _ffn(xl, mod1, 2, l1_norm_ffn2, w_gu4, w_down4, out_rows=N_LAT, final_g=norm_out)
    return out.reshape(BATCH, SEQ, D_MODEL)
```

```python
import functools

import numpy as np
import jax
import jax.numpy as jnp
from jax import lax
from jax.experimental import pallas as pl
from jax.experimental.pallas import tpu as pltpu

D_MODEL = 2048
BATCH = 4
SEQ = 4096
GRID_W = 64
CTX_LEN = 256
N_MOD = 9
D_FF = 5632
RMS_EPS = 1e-6
ROPE_BASE = 10000.0

GLA_HEADS = 4
GLA_DK = 128
GLA_DV = 256
GLA_LOWRANK = 16
GLA_TAU = 16.0
GLA_CHUNK = 64
FNET_GROUPS = 4
FNET_GROUP_DIM = 256
GLA_QK_W = GLA_HEADS * GLA_DK
GLA_V_W = GLA_HEADS * GLA_DV
FNET_W = FNET_GROUPS * FNET_GROUP_DIM

NA_HEADS = 32
NA_HEAD_DIM = 64
NA_KH = 8
NA_KW = 16
NA_QROWS = 8
NEG_BIG = -1e30
LOG2_E = 1.4426950408889634
NA_Q_SCALE = NA_HEAD_DIM ** -0.5 * LOG2_E

N_LAT = BATCH * SEQ
N_CTX = BATCH * CTX_LEN
N_TOK = N_LAT + N_CTX

F32 = jnp.float32
BF16 = jnp.bfloat16
MIB = 1024 * 1024


def _cparams(sem, vmem_mib):
    return pltpu.CompilerParams(dimension_semantics=sem, vmem_limit_bytes=vmem_mib * MIB)


def _dot(a, b):
    return jnp.dot(a, b, preferred_element_type=F32)


def _dot_nt(a, b):
    return lax.dot_general(a, b, (((1,), (1,)), ((), ())), preferred_element_type=F32)


def _dot_tn(a, b):
    return lax.dot_general(a, b, (((0,), (0,)), ((), ())), preferred_element_type=F32)


def _split2(x):
    hi = x.astype(BF16)
    lo = (x - hi.astype(F32)).astype(BF16)
    return hi, lo


def _dot3(a, b):
    ah, al = _split2(a)
    bh, bl = _split2(b)
    return _dot(ah, bh) + (_dot(al, bh) + _dot(ah, bl))


def _dot_exact_lhs(l_bf16, x):
    x1 = x.astype(BF16)
    r1 = x - x1.astype(F32)
    x2 = r1.astype(BF16)
    x3 = (r1 - x2.astype(F32)).astype(BF16)
    return _dot(l_bf16, x1) + (_dot(l_bf16, x2) + _dot(l_bf16, x3))


def _sigmoid(x):
    return 1.0 / (1.0 + jnp.exp(-x))


def _rms(x, g):
    ms = jnp.mean(x * x, axis=-1, keepdims=True)
    return x * lax.rsqrt(ms + RMS_EPS) * g


ROW_CHUNK = 16
ROW_GROUP = 8


def _row_groups(n_rows, body, group=ROW_GROUP):
    span = ROW_CHUNK * group

    def step(t, carry):
        base = t * span
        body([pl.ds(pl.multiple_of(base + c * ROW_CHUNK, ROW_CHUNK), ROW_CHUNK) for c in range(group)])
        return carry
    lax.fori_loop(0, n_rows // span, step, 0)


def _inv_rms(x):
    return lax.rsqrt(jnp.mean(x * x, axis=-1, keepdims=True) + RMS_EPS)


def _norm_mod_rows(x_ref, g_ref, sh_ref, sc_ref, h_ref, gs_ref, zero_ref=None):
    d = x_ref.shape[1]
    gs_ref[0] = jnp.broadcast_to(g_ref[...] * (1.0 + sc_ref[0]), (ROW_CHUNK, d))
    gs_ref[1] = jnp.broadcast_to(sh_ref[0], (ROW_CHUNK, d))

    def body(chunks):
        inv = [_inv_rms(x_ref[rows, :]) for rows in chunks]
        for rows, r in zip(chunks, inv):
            h_ref[rows, :] = (x_ref[rows, :] * r * gs_ref[0] + gs_ref[1]).astype(BF16)
            if zero_ref is not None:
                zero_ref[rows, :] = jnp.zeros((ROW_CHUNK, zero_ref.shape[1]), zero_ref.dtype)
    _row_groups(x_ref.shape[0], body)


def _mod_row(i, tm):
    return jnp.minimum(i // (SEQ // tm), BATCH)


def _mod_spec(j, tm, width=D_MODEL, col=None, tile0=0):
    if col is None:
        return pl.BlockSpec((1, 1, width), lambda i, *_: (_mod_row(i + tile0, tm) * N_MOD + j, 0, 0))
    return pl.BlockSpec((1, 1, width), lambda i, n, *_: (_mod_row(i + tile0, tm) * N_MOD + j, 0, n))


CAST_ROW_ALIGN = 16


def _cast_plan(rows, steps):
    per = pl.cdiv(pl.cdiv(rows, steps), CAST_ROW_ALIGN) * CAST_ROW_ALIGN
    while rows % per:
        per += CAST_ROW_ALIGN
    return per, rows // per


def _cast_specs(w, steps, flat_index):
    per, used = _cast_plan(w.shape[0], steps)
    spec = pl.BlockSpec((per, w.shape[1]), lambda *g: (jnp.minimum(flat_index(*g), used - 1), 0))
    return spec, jax.ShapeDtypeStruct(w.shape, BF16), used


def _hosting_casts(kernel_fn, n_in, n_out, used, flat_index, grid_rank):
    nc = len(used)

    def wrapped(*refs):
        a, b, c = n_in + nc, n_in + nc + n_out, n_in + nc + n_out + nc
        kernel_fn(*refs[:n_in], *refs[a:b], *refs[c:])
        t = flat_index(*[pl.program_id(d) for d in range(grid_rank)])
        for n_used, src_ref, dst_ref in zip(used, refs[n_in:a], refs[b:c]):
            @pl.when(t < n_used)
            def _(src_ref=src_ref, dst_ref=dst_ref):
                dst_ref[...] = src_ref[...].astype(BF16)
    return wrapped


ADA_TN = 1024


def _ada_kernel(c_ref, w_ref, b_ref, o_ref):
    c = c_ref[...]
    s = (c * _sigmoid(c)).astype(BF16)
    o_ref[...] = _dot(s, w_ref[...].astype(BF16)) + b_ref[...]


def _ada(cc, w, b):
    n = w.shape[1]
    out = pl.pallas_call(
        _ada_kernel,
        grid=(n // ADA_TN,),
        in_specs=[pl.BlockSpec((8, D_MODEL), lambda j: (0, 0)),
                  pl.BlockSpec((D_MODEL, ADA_TN), lambda j: (0, j)),
                  pl.BlockSpec((1, ADA_TN), lambda j: (0, j))],
        out_specs=pl.BlockSpec((8, ADA_TN), lambda j: (0, j)),
        out_shape=jax.ShapeDtypeStruct((8, n), F32),
        compiler_params=_cparams(("arbitrary",), 40),
        name="ada_mod",
    )(cc, w, b.reshape(1, n))
    return out.reshape(8 * N_MOD, 1, D_MODEL)


FFN_TM = 1024
FFN_TF = 512
FFN_SLAB = 128


def _ffn_kernel(x_hbm, xn_ref, g_ref, sh_ref, sc_ref, shn_ref, scn_ref, gt_ref, wg_ref, wu_ref, wd_ref, *rest,
                final_norm):
    go_ref = rest[0] if final_norm else None
    o_ref, h0_ref, h1_ref, gs_ref, x_ref, x_sem = rest[-6:]
    i = pl.program_id(0)
    f = pl.program_id(1)
    last = pl.num_programs(1) - 1
    tm, d = x_ref.shape

    def x_copy():
        return pltpu.make_async_copy(x_hbm.at[pl.ds(pl.multiple_of(i * tm, tm), tm), :], x_ref, x_sem)

    @pl.when(f == 0)
    def _():
        x_copy().start()

    @pl.when((i == 0) & (f == 0))
    def _():
        x_copy().wait()
        _norm_mod_rows(x_ref, g_ref, sh_ref, sc_ref, h0_ref, gs_ref)

    def step(h_ref, hn_ref, first):
        gs_ref[0] = jnp.broadcast_to(g_ref[...] * (1.0 + scn_ref[0]), (ROW_CHUNK, d))
        gs_ref[1] = jnp.broadcast_to(shn_ref[0], (ROW_CHUNK, d))
        slab0 = jnp.minimum(f, tm // FFN_SLAB - 1) * FFN_SLAB
        chunks = [pl.ds(k * ROW_CHUNK, ROW_CHUNK) for k in range(FFN_SLAB // ROW_CHUNK)]
        inv = [_inv_rms(xn_ref[rows, :]) for rows in chunks]
        for k, (rows, r) in enumerate(zip(chunks, inv)):
            dst = pl.ds(pl.multiple_of(slab0 + k * ROW_CHUNK, ROW_CHUNK), ROW_CHUNK)
            hn_ref[dst, :] = (xn_ref[rows, :] * r * gs_ref[0] + gs_ref[1]).astype(BF16)

        h = h_ref[...]
        a = _dot(h, wg_ref[...])
        u = _dot(h, wu_ref[...])
        act = (a * _sigmoid(a) * u).astype(BF16)
        if first:
            o_ref[...] = _dot(act, wd_ref[...])
        else:
            o_ref[...] += _dot(act, wd_ref[...])

    for parity, (h_ref, hn_ref) in enumerate(((h0_ref, h1_ref), (h1_ref, h0_ref))):
        for first in (True, False):
            @pl.when((i % 2 == parity) & ((f == 0) == first))
            def _(h_ref=h_ref, hn_ref=hn_ref, first=first):
                step(h_ref, hn_ref, first)

    @pl.when(f == last)
    def _():
        @pl.when(i > 0)
        def _():
            x_copy().wait()

        if not final_norm:
            o_ref[...] = x_ref[...] + (0.5 * gt_ref[0]) * o_ref[...]
        else:
            gs_ref[2] = jnp.broadcast_to(0.5 * gt_ref[0], (ROW_CHUNK, d))
            gs_ref[3] = jnp.broadcast_to(go_ref[...], (ROW_CHUNK, d))

            def body(chunks):
                resid = lambda rows: x_ref[rows, :] + gs_ref[2] * o_ref[rows, :]
                inv = [_inv_rms(resid(rows)) for rows in chunks]
                for rows, r in zip(chunks, inv):
                    o_ref[rows, :] = resid(rows) * r * gs_ref[3]
            _row_groups(tm, body)


def _ffn(x_in, mod, sub, norm_g, w_gu, w_down, *, out_rows, tile0=0, prev=None, final_g=None):
    tm, tf = FFN_TM, FFN_TF
    nf = D_FF // tf
    n_tiles = x_in.shape[0] // tm
    slabs = tm // FFN_SLAB
    assert slabs <= nf
    nxt = lambda i: jnp.minimum(i + 1, n_tiles - 1)
    in_specs = [pl.BlockSpec(memory_space=pl.ANY),
                pl.BlockSpec((FFN_SLAB, D_MODEL), lambda i, f: (nxt(i) * slabs + jnp.minimum(f, slabs - 1), 0)),
                pl.BlockSpec((1, D_MODEL), lambda i, f: (0, 0)),
                _mod_spec(3 * sub, tm, tile0=tile0), _mod_spec(3 * sub + 1, tm, tile0=tile0),
                pl.BlockSpec((1, 1, D_MODEL), lambda i, f: (_mod_row(nxt(i) + tile0, tm) * N_MOD + 3 * sub, 0, 0)),
                pl.BlockSpec((1, 1, D_MODEL), lambda i, f: (_mod_row(nxt(i) + tile0, tm) * N_MOD + 3 * sub + 1, 0, 0)),
                _mod_spec(3 * sub + 2, tm, tile0=tile0),
                pl.BlockSpec((D_MODEL, tf), lambda i, f: (0, f)),
                pl.BlockSpec((D_MODEL, tf), lambda i, f: (0, f + nf)),
                pl.BlockSpec((tf, D_MODEL), lambda i, f: (f, 0))]
    args = [x_in, x_in, norm_g.reshape(1, D_MODEL), mod, mod, mod, mod, mod, w_gu, w_gu, w_down]
    if final_g is not None:
        in_specs.append(pl.BlockSpec((1, D_MODEL), lambda i, f: (0, 0)))
        args.append(final_g.reshape(1, D_MODEL))
    aliases = {}
    if prev is not None:
        aliases = {len(args): 0}
        in_specs.append(pl.BlockSpec(memory_space=pl.ANY))
        args.append(prev)
    return pl.pallas_call(
        functools.partial(_ffn_kernel, final_norm=final_g is not None),
        grid=(n_tiles, nf),
        in_specs=in_specs,
        out_specs=pl.BlockSpec((tm, D_MODEL), lambda i, f: (i + tile0, 0)),
        out_shape=jax.ShapeDtypeStruct((out_rows, D_MODEL), F32),
        scratch_shapes=[pltpu.VMEM((tm, D_MODEL), BF16), pltpu.VMEM((tm, D_MODEL), BF16),
                        pltpu.VMEM((4, ROW_CHUNK, D_MODEL), F32), pltpu.VMEM((tm, D_MODEL), F32),
                        pltpu.SemaphoreType.DMA(())],
        input_output_aliases=aliases,
        compiler_params=_cparams(("arbitrary", "arbitrary"), 56),
        name="ffn_half",
    )(*args)


NMM_TM = 1024
NMM_TN = 1024


def _nmm_kernel(x_ref, g_ref, sh_ref, sc_ref, w_ref, *rest, gla):
    if gla:
        wn_ref, wc_ref, o_ref, ob_ref, zc_ref, zs_ref, on_ref, h_ref, gs_ref = rest
    else:
        o_ref, h_ref, gs_ref = rest
    j = pl.program_id(1)

    @pl.when(j == 0)
    def _():
        _norm_mod_rows(x_ref, g_ref, sh_ref, sc_ref, h_ref, gs_ref)
        if gla:
            on_ref[...] = _dot(h_ref[...], wn_ref[...])

    if not gla:
        o_ref[...] = _dot(h_ref[...], w_ref[...]).astype(o_ref.dtype)
    else:
        @pl.when(j % 2 == 0)
        def _():
            o_ref[...] = _dot(h_ref[...], w_ref[...])

        @pl.when(j == 1)
        def _():
            ob_ref[...] = _dot(h_ref[...], w_ref[...]).astype(BF16)

        @pl.when(j == 3)
        def _():
            f = _dot(h_ref[...], w_ref[...]).astype(BF16)
            for grp in range(FNET_GROUPS):
                cols = slice(grp * FNET_GROUP_DIM, (grp + 1) * FNET_GROUP_DIM)
                z = _dot(f[:, cols], wc_ref[...])
                zc_ref[:, cols] = z[:, :FNET_GROUP_DIM].astype(BF16)
                zs_ref[:, cols] = z[:, FNET_GROUP_DIM:].astype(BF16)


def _nmm(xs, mod, sub, norm_g, w, w_narrow=None, w_chan=None, cast_ws=()):
    tm, tn = NMM_TM, NMM_TN
    n = w.shape[1]
    gla = w_narrow is not None
    in_specs = [pl.BlockSpec((tm, D_MODEL), lambda i, j: (i, 0)),
                pl.BlockSpec((1, D_MODEL), lambda i, j: (0, 0)),
                _mod_spec(3 * sub, tm), _mod_spec(3 * sub + 1, tm),
                pl.BlockSpec((D_MODEL, tn), lambda i, j: (0, j))]
    args = [xs, norm_g.reshape(1, D_MODEL), mod, mod, w]
    if gla:
        nn = w_narrow.shape[1]
        assert n == 4 * tn and tn == GLA_V_W == FNET_W
        in_specs += [pl.BlockSpec((D_MODEL, nn), lambda i, j: (0, 0)),
                     pl.BlockSpec(w_chan.shape, lambda i, j: (0, 0))]
        args += [w_narrow, w_chan]
        row_tile = pl.BlockSpec((tm, tn), lambda i, j: (i, 0))
        out_specs = [pl.BlockSpec((tm, tn), lambda i, j: (i, j // 2)), row_tile, row_tile, row_tile,
                     pl.BlockSpec((tm, nn), lambda i, j: (i, 0))]
        out_shape = [jax.ShapeDtypeStruct((N_TOK, n // 2), F32)] + [jax.ShapeDtypeStruct((N_TOK, tn), BF16)] * 3 + [
            jax.ShapeDtypeStruct((N_TOK, nn), F32)]
    else:
        out_specs = [pl.BlockSpec((tm, tn), lambda i, j: (i, j))]
        out_shape = [jax.ShapeDtypeStruct((N_TOK, n), BF16)]
    nj = n // tn
    flat = lambda i, j: i * nj + j
    casts = [_cast_specs(cw, (N_TOK // tm) * nj, flat) for cw in cast_ws]
    return pl.pallas_call(
        _hosting_casts(functools.partial(_nmm_kernel, gla=gla), len(args), len(out_specs),
                       [u for _, _, u in casts], flat, 2),
        grid=(N_TOK // tm, nj),
        in_specs=in_specs + [s for s, _, _ in casts],
        out_specs=out_specs + [s for s, _, _ in casts],
        out_shape=out_shape + [o for _, o, _ in casts],
        scratch_shapes=[pltpu.VMEM((tm, D_MODEL), BF16), pltpu.VMEM((2, ROW_CHUNK, D_MODEL), F32)],
        compiler_params=_cparams(("arbitrary", "arbitrary"), 56),
        name="norm_mod_matmul",
    )(*args, *cast_ws)


GLA_BLOCK = CTX_LEN
GLA_LAT_BLOCKS = SEQ // GLA_BLOCK
GLA_STEPS = 1 + GLA_LAT_BLOCKS
GLA_LR_PAD = 128
GLA_SPLIT_RANGE = 80.0


def _rope(x, cos, sin_signed):
    return x * cos + pltpu.roll(x, GLA_DK // 2, 1) * sin_signed


def _gla_kernel(qf, kf, vf, lf, cf, sf, qb, kb, vb, lb, cb, sb, gwf, gbf, gwb, gbb,
                of_ref, ob_ref, st_ref, row_ref):
    @pl.when(pl.program_id(1) == 0)
    def _():
        st_ref[...] = jnp.zeros_like(st_ref)

    c = GLA_CHUNK
    n = GLA_BLOCK
    row = lax.broadcasted_iota(jnp.int32, (n, n), 0)
    col = lax.broadcasted_iota(jnp.int32, (n, n), 1)
    same_chunk = (row // c) == (col // c)
    row_c = lax.broadcasted_iota(jnp.int32, (c, c), 0)
    col_c = lax.broadcasted_iota(jnp.int32, (c, c), 1)
    scale = GLA_DK ** -0.5

    dirs = ((qf, kf, vf, lf, cf, sf, gwf, gbf, of_ref), (qb, kb, vb, lb, cb, sb, gwb, gbb, ob_ref))
    n_chunks = n // c
    heads = [(slice(h * GLA_DK, (h + 1) * GLA_DK), slice(h * GLA_DV, (h + 1) * GLA_DV)) for h in range(GLA_HEADS)]
    tris, decays = [], []
    max_decay = jnp.zeros((1, GLA_QK_W), F32)
    for d, (q_ref, k_ref, v_ref, l_ref, c_ref, s_ref, gw_ref, gb_ref, o_ref) in enumerate(dirs):
        causal = (row >= col) if d == 0 else (row <= col)
        tri_b = jnp.where(same_chunk & causal, 1.0, 0.0).astype(BF16)
        tris.append((row_c >= col_c) if d == 0 else (row_c <= col_c))
        x = _dot3(l_ref[...], gw_ref[...]) + gb_ref[...]
        g = (jnp.minimum(x, 0.0) - jnp.log1p(jnp.exp(-jnp.abs(x)))) * (1.0 / GLA_TAU)
        bc_all = _dot_exact_lhs(tri_b, g)
        per_chunk = []
        for ci in (range(n_chunks) if d == 0 else range(n_chunks - 1, -1, -1)):
            rows = slice(ci * c, (ci + 1) * c)
            bc = bc_all[rows, :]
            mid = bc[c // 2:c // 2 + 1, :]
            last = bc[c - 1:c, :] if d == 0 else bc[0:1, :]
            max_decay = jnp.maximum(max_decay, -last)
            per_chunk.append((rows, bc, mid, last))
        decays.append(per_chunk)

    def scores_by_rows(d, qh, kh, bc):
        row_ref[0], row_ref[1] = bc, qh
        s_idx = lax.broadcasted_iota(jnp.int32, (c, 1), 0)
        t_idx = lax.broadcasted_iota(jnp.int32, (c, c), 1)

        def one_row(t, att_t):
            valid = (s_idx <= t) if d == 0 else (s_idx >= t)
            w = jnp.exp(jnp.where(valid, row_ref[0, pl.ds(t, 1), :] - bc, -jnp.inf))
            col = jnp.sum(kh * w * row_ref[1, pl.ds(t, 1), :], axis=-1, keepdims=True)
            return jnp.where(t_idx == t, col, att_t)
        return lax.fori_loop(0, c, one_row, jnp.zeros((c, c), F32))

    def local_part(k, exact):
        out = {}
        for d, (q_ref, k_ref, v_ref, _, c_ref, s_ref, _, _, _) in enumerate(dirs):
            rows, bc, mid, last = decays[d][k]
            e_q, e_kl, e_l = jnp.exp(bc), jnp.exp(last - bc), jnp.exp(last)
            if not exact:
                e_qm, e_km = jnp.exp(bc - mid), jnp.exp(mid - bc)
            cos, sin = c_ref[rows, :], s_ref[rows, :]
            for h, (sk, sv) in enumerate(heads):
                qh = _rope(q_ref[rows, sk] * scale, cos, sin)
                kh = _rope(k_ref[rows, sk], cos, sin)
                vh = v_ref[rows, sv].astype(BF16)
                kv = _dot_tn(vh, (kh * e_kl[:, sk]).astype(BF16))
                if exact:
                    att_t = scores_by_rows(d, qh, kh, bc[:, sk]).astype(BF16)
                    mix = lambda att_t=att_t, vh=vh: _dot_tn(att_t, vh)
                else:
                    att = _dot_nt((qh * e_qm[:, sk]).astype(BF16), (kh * e_km[:, sk]).astype(BF16))
                    att = jnp.where(tris[d], att, 0.0).astype(BF16)
                    mix = lambda att=att, vh=vh: _dot(att, vh)
                out[d, h] = ((qh * e_q[:, sk]).astype(BF16), mix, kv, e_l[:, sk])
        return out

    def scan_block(exact):
        st = {(d, h): st_ref[d, h] for d in range(2) for h in range(GLA_HEADS)}
        local = local_part(0, exact)
        for k in range(n_chunks):
            nxt = local_part(k + 1, exact) if k + 1 < n_chunks else None
            for d in range(2):
                rows = decays[d][k][0]
                for h, (sk, sv) in enumerate(heads):
                    qe, mix, kv, e_l = local[d, h]
                    dirs[d][-1][rows, sv] = _dot_nt(qe, st[d, h].astype(BF16)) + mix()
                    st[d, h] = st[d, h] * e_l + kv
            local = nxt
        for (d, h), val in st.items():
            st_ref[d, h] = val

    in_range = jnp.max(max_decay) <= GLA_SPLIT_RANGE

    @pl.when(in_range)
    def _():
        scan_block(exact=False)

    @pl.when(jnp.logical_not(in_range))
    def _():
        scan_block(exact=True)


def _gla(pf, pb, lr, cos_t, sin_t, gwf, gbf, gwb, gbb, cast_ws):
    c = GLA_BLOCK
    ctx0 = N_LAT // c

    def fwd_blk(b, s):
        return jnp.where(s == 0, ctx0 + b, GLA_LAT_BLOCKS * b + s - 1)

    def bwd_blk(b, s):
        return jnp.where(s == 0, ctx0 + b, GLA_LAT_BLOCKS * b + (GLA_STEPS - 1 - s))

    def fwd_rope(b, s):
        return jnp.where(s == 0, GLA_LAT_BLOCKS, s - 1)

    def bwd_rope(b, s):
        return jnp.where(s == 0, GLA_LAT_BLOCKS, GLA_STEPS - 1 - s)

    def dir_specs(blk, rope):
        return [pl.BlockSpec((c, GLA_QK_W), lambda b, s: (blk(b, s), 0)),
                pl.BlockSpec((c, GLA_QK_W), lambda b, s: (blk(b, s), 1)),
                pl.BlockSpec((c, GLA_V_W), lambda b, s: (blk(b, s), 0)),
                pl.BlockSpec((c, GLA_LR_PAD), lambda b, s: (blk(b, s), 0)),
                pl.BlockSpec((c, GLA_DK), lambda b, s: (rope(b, s), 0)),
                pl.BlockSpec((c, GLA_DK), lambda b, s: (rope(b, s), 0))]

    const = lambda shape: pl.BlockSpec(shape, lambda b, s: (0, 0))
    in_specs = (dir_specs(fwd_blk, fwd_rope) + dir_specs(bwd_blk, bwd_rope)
                + [const((GLA_LR_PAD, GLA_QK_W)), const((1, GLA_QK_W)),
                   const((GLA_LR_PAD, GLA_QK_W)), const((1, GLA_QK_W))])
    flat = lambda b, s: b * GLA_STEPS + s
    casts = [_cast_specs(w, BATCH * GLA_STEPS, flat) for w in cast_ws]
    args = [pf, pf, pb, lr, cos_t, sin_t, pf, pf, pb, lr, cos_t, sin_t, gwf, gbf, gwb, gbb]
    return pl.pallas_call(
        _hosting_casts(_gla_kernel, len(args), 2, [u for _, _, u in casts], flat, 2),
        grid=(BATCH, GLA_STEPS),
        in_specs=in_specs + [s for s, _, _ in casts],
        out_specs=[pl.BlockSpec((c, GLA_V_W), lambda b, s: (fwd_blk(b, s), 0)),
                   pl.BlockSpec((c, GLA_V_W), lambda b, s: (bwd_blk(b, s), 0))] + [s for s, _, _ in casts],
        out_shape=[jax.ShapeDtypeStruct((N_TOK, GLA_V_W), F32)] * 2 + [o for _, o, _ in casts],
        scratch_shapes=[pltpu.VMEM((2, GLA_HEADS, GLA_DV, GLA_DK), F32), pltpu.VMEM((2, GLA_CHUNK, GLA_DK), F32)],
        compiler_params=_cparams(("arbitrary", "arbitrary"), 40),
        name="gla_scan",
    )(*args, *cast_ws)


def _tdft_kernel(c_ref, s_ref, zc_ref, zs_ref, *rest):
    o_ref = rest[-1]
    o_ref[...] = (_dot(c_ref[...], zc_ref[...]) - _dot(s_ref[...], zs_ref[...])).astype(o_ref.dtype)


def _time_dft(cmat, smat, zc, zs, t_len, tm, tn, row0, prev=None, cast_ws=()):
    zb0 = row0 // t_len
    ob0 = row0 // tm
    mt = t_len // tm
    in_specs = [pl.BlockSpec((tm, t_len), lambda b, n, m: (m, 0)),
                pl.BlockSpec((tm, t_len), lambda b, n, m: (m, 0)),
                pl.BlockSpec((t_len, tn), lambda b, n, m: (zb0 + b, n)),
                pl.BlockSpec((t_len, tn), lambda b, n, m: (zb0 + b, n))]
    args = [cmat, smat, zc, zs]
    aliases = {}
    if prev is not None:
        in_specs.append(pl.BlockSpec(memory_space=pl.ANY))
        args.append(prev)
        aliases = {4: 0}
    nt = FNET_W // tn
    flat = lambda b, n, m: (b * nt + n) * mt + m
    casts = [_cast_specs(w, BATCH * nt * mt, flat) for w in cast_ws]
    return pl.pallas_call(
        _hosting_casts(_tdft_kernel, len(args), 1, [u for _, _, u in casts], flat, 3),
        grid=(BATCH, nt, mt),
        in_specs=in_specs + [s for s, _, _ in casts],
        out_specs=[pl.BlockSpec((tm, tn), lambda b, n, m: (ob0 + b * mt + m, n))] + [s for s, _, _ in casts],
        out_shape=[jax.ShapeDtypeStruct((N_TOK, FNET_W), BF16)] + [o for _, o, _ in casts],
        input_output_aliases=aliases,
        compiler_params=_cparams(("arbitrary", "arbitrary", "arbitrary"), 48),
        name="fnet_time_dft",
    )(*args, *cast_ws)


FOLD_BLK = 256


def _tdft_fold_kernel(c_ref, s_ref, zc_ref, zs_ref, j1_ref, j2_ref, o_ref, zcf_ref, zsf_ref):
    half, blk = zcf_ref.shape[0], FOLD_BLK
    nblk = 2 * half // blk

    @pl.when(pl.program_id(2) == 0)
    def _():
        for kb in range(half // blk):
            rows = slice(kb * blk, (kb + 1) * blk)
            mirror = slice((nblk - 1 - kb) * blk, (nblk - kb) * blk)
            after = slice((nblk - kb) * blk, (nblk - kb + 1) * blk)
            for z_ref, zf_ref, sign in ((zc_ref, zcf_ref, 1.0), (zs_ref, zsf_ref, -1.0)):
                partner = _dot(j1_ref[...], z_ref[mirror, :])
                if kb > 0:
                    partner = partner + _dot(j2_ref[...], z_ref[after, :])
                zf_ref[rows, :] = (z_ref[rows, :].astype(F32) + sign * partner).astype(BF16)

    tm = o_ref.shape[0]
    j = pl.program_id(2) * tm + lax.broadcasted_iota(jnp.int32, (tm, 1), 0)
    sign = jnp.where(j % 2 == 0, SEQ ** -0.5, -(SEQ ** -0.5))
    mid = zc_ref[half:half + 1, :].astype(F32)
    o = _dot(c_ref[...], zcf_ref[...]) - _dot(s_ref[...], zsf_ref[...]) + sign * mid
    o_ref[...] = o.astype(o_ref.dtype)


def _time_dft_folded(cmat, smat, zc, zs, tm, tn, cast_ws=()):
    half = SEQ // 2
    mt = SEQ // tm
    nt = FNET_W // tn
    i = np.arange(FOLD_BLK)
    j1 = jnp.asarray((i[:, None] + i[None, :] == FOLD_BLK) & (i[:, None] > 0), BF16)
    j2 = jnp.asarray((i[:, None] == 0) & (i[None, :] == 0), BF16)
    args = [cmat, smat, zc, zs, j1, j2]
    in_specs = [pl.BlockSpec((tm, half), lambda b, n, m: (m, 0)),
                pl.BlockSpec((tm, half), lambda b, n, m: (m, 0)),
                pl.BlockSpec((SEQ, tn), lambda b, n, m: (b, n)),
                pl.BlockSpec((SEQ, tn), lambda b, n, m: (b, n)),
                pl.BlockSpec((FOLD_BLK, FOLD_BLK), lambda b, n, m: (0, 0)),
                pl.BlockSpec((FOLD_BLK, FOLD_BLK), lambda b, n, m: (0, 0))]
    flat = lambda b, n, m: (b * nt + n) * mt + m
    casts = [_cast_specs(w, BATCH * nt * mt, flat) for w in cast_ws]
    return pl.pallas_call(
        _hosting_casts(_tdft_fold_kernel, len(args), 1, [u for _, _, u in casts], flat, 3),
        grid=(BATCH, nt, mt),
        in_specs=in_specs + [s for s, _, _ in casts],
        out_specs=[pl.BlockSpec((tm, tn), lambda b, n, m: (b * mt + m, n))] + [s for s, _, _ in casts],
        out_shape=[jax.ShapeDtypeStruct((N_TOK, FNET_W), BF16)] + [o for _, o, _ in casts],
        scratch_shapes=[pltpu.VMEM((half, tn), BF16), pltpu.VMEM((half, tn), BF16)],
        compiler_params=_cparams(("arbitrary", "arbitrary", "arbitrary"), 48),
        name="fnet_time_dft_folded",
    )(*args, *cast_ws)


def _dft_tables(n):
    k = np.arange(n)
    ang = 2.0 * np.pi * ((k[:, None] * k[None, :]) % n) / n
    return np.cos(ang) / np.sqrt(n), np.sin(ang) / np.sqrt(n)


def _dft_table_kernel(ac_ref, as_ref, bc_ref, bs_ref, co_ref, so_ref):
    ac, as_ = ac_ref[0], as_ref[0]
    bc, bs = bc_ref[...], bs_ref[...]
    co_ref[...] = (ac * bc - as_ * bs).astype(BF16)
    so_ref[...] = (ac * bs + as_ * bc).astype(BF16)


def _big_dft_tables():
    r = GRID_W
    k = np.arange(SEQ)
    j = np.arange(r)
    pa = 2.0 * np.pi * ((j[:, None] * r * k[None, :]) % SEQ) / SEQ
    pb = 2.0 * np.pi * ((j[:, None] * k[None, :]) % SEQ) / SEQ
    ac = jnp.asarray(np.cos(pa).reshape(r, 1, SEQ), F32)
    as_ = jnp.asarray(np.sin(pa).reshape(r, 1, SEQ), F32)
    bc = jnp.asarray(np.cos(pb) * SEQ ** -0.5, F32)
    bs = jnp.asarray(np.sin(pb) * SEQ ** -0.5, F32)
    row = pl.BlockSpec((1, 1, SEQ), lambda i: (i, 0, 0))
    full = pl.BlockSpec((r, SEQ), lambda i: (0, 0))
    return pl.pallas_call(
        _dft_table_kernel,
        grid=(r,),
        in_specs=[row, row, full, full],
        out_specs=[pl.BlockSpec((r, SEQ), lambda i: (i, 0))] * 2,
        out_shape=[jax.ShapeDtypeStruct((SEQ, SEQ), BF16)] * 2,
        compiler_params=_cparams(("parallel",), 16),
        name="dft_tables",
    )(ac, as_, bc, bs)


GO_TM = 512
GO_TN = 2048


def _glaout_kernel(of_ref, ob_ref, r_ref, fm_ref, gn_ref, x_ref, gt_ref, w_ref, o_ref, h_ref):
    @pl.when(pl.program_id(1) == 0)
    def _():
        heads = [slice(h * GLA_DV, (h + 1) * GLA_DV) for h in range(GLA_HEADS)]

        def body(chunks):
            inv = [[_inv_rms(of_ref[rows, sv] + ob_ref[rows, sv]) for sv in heads] for rows in chunks]
            for rows, inv_c in zip(chunks, inv):
                for sv, r in zip(heads, inv_c):
                    o = (of_ref[rows, sv] + ob_ref[rows, sv]) * r
                    gate = r_ref[rows, sv]
                    h_ref[rows, sv] = (o * gn_ref[:, sv] * (gate * _sigmoid(gate))).astype(BF16)
                h_ref[rows, GLA_V_W:] = fm_ref[rows, :].astype(BF16)
        _row_groups(of_ref.shape[0], body, group=4)

    o_ref[...] = x_ref[...] + gt_ref[0] * _dot(h_ref[...], w_ref[...])


def _glaout(of, ob, pf, fm, g_norm, xs, mod, w_out):
    tm, tn = GO_TM, GO_TN
    half = lambda col: pl.BlockSpec((tm, GLA_V_W), lambda i, j: (i, col))
    return pl.pallas_call(
        _glaout_kernel,
        grid=(N_TOK // tm, D_MODEL // tn),
        in_specs=[half(0), half(0), half(1), half(0),
                  pl.BlockSpec((1, GLA_V_W), lambda i, j: (0, 0)),
                  pl.BlockSpec((tm, tn), lambda i, j: (i, j)),
                  _mod_spec(5, tm, tn, col=True),
                  pl.BlockSpec((D_MODEL, tn), lambda i, j: (0, j))],
        out_specs=pl.BlockSpec((tm, tn), lambda i, j: (i, j)),
        out_shape=jax.ShapeDtypeStruct((N_TOK, D_MODEL), F32),
        scratch_shapes=[pltpu.VMEM((tm, D_MODEL), BF16)],
        compiler_params=_cparams(("parallel", "arbitrary"), 56),
        name="gla_fnet_out",
    )(of, ob, pf, fm, g_norm.reshape(1, GLA_V_W), xs, mod, w_out)


NA_HROWS = NA_QROWS // 2
NA_WROWS = NA_HROWS + NA_KH
NA_TQ = NA_QROWS * GRID_W
NA_HQ = NA_HROWS * GRID_W
NA_WK = NA_WROWS * GRID_W
NA_KBLK = NA_HROWS * GRID_W
NA_WBLKS = NA_WK // NA_KBLK
NA_MASKED = 2 * NA_KH - 1
NA_ROW_BLOCKS = (SEQ // GRID_W) // NA_QROWS
NA_CASES = (0, 1, NA_ROW_BLOCKS - 1)
NA_PAIRS = 2
NA_LANES = NA_PAIRS * 2 * NA_HEAD_DIM


def _na_window_row0(j, half):
    q0 = NA_QROWS * j + NA_HROWS * half
    return int(np.clip(q0 - NA_KH // 2, 0, SEQ // GRID_W - NA_WROWS))


def _na_bias_slots():
    rows = SEQ // GRID_W
    slots = np.full((len(NA_CASES), 2, NA_HROWS, NA_WROWS), NA_MASKED, np.int32)
    for case, j in enumerate(NA_CASES):
        for half in range(2):
            w0 = _na_window_row0(j, half)
            for i in range(NA_HROWS):
                qr = NA_QROWS * j + NA_HROWS * half + i
                r0 = int(np.clip(qr - NA_KH // 2, 0, rows - NA_KH))
                for l in range(NA_WROWS):
                    kr = w0 + l
                    if r0 <= kr < r0 + NA_KH:
                        slots[case, half, i, l] = kr - qr + NA_KH - 1
    return slots


def _na_kernel(q_ref, *refs):
    nb = NA_WBLKS
    ka, kb, va, vb = refs[0:nb], refs[nb:2 * nb], refs[2 * nb:3 * nb], refs[3 * nb:4 * nb]
    kc_ref, vc_ref, rp_ref, o_ref, bias_ref, td_ref = refs[4 * nb:]
    j = pl.program_id(1)

    @pl.when((j == 0) & (pl.program_id(2) == 0))
    def _():
        qc = lax.broadcasted_iota(jnp.int32, (GRID_W, 2 * GRID_W), 0)
        kc = lax.broadcasted_iota(jnp.int32, (GRID_W, 2 * GRID_W), 1) % GRID_W
        start = jnp.clip(qc - NA_KW // 2, 0, GRID_W - NA_KW)
        window = (kc >= start) & (kc < start + NA_KW)
        for hh in range(2 * NA_PAIRS):
            for slot in range(NA_MASKED):
                td_ref[hh, slot] = jnp.broadcast_to(rp_ref[hh, slot:slot + 1, :], (GRID_W, 2 * GRID_W))
                tile = pltpu.roll(td_ref[hh, slot], 2 * GRID_W - (NA_KW - 1), 1, stride=1, stride_axis=0)
                td_ref[hh, slot] = jnp.where(window, tile, NEG_BIG)
            td_ref[hh, NA_MASKED] = jnp.full((GRID_W, 2 * GRID_W), NEG_BIG, F32)

        slots = _na_bias_slots()
        left = lax.broadcasted_iota(jnp.int32, (GRID_W, 2 * GRID_W), 1) < GRID_W
        for case in range(len(NA_CASES)):
            for half in range(2):
                for hh in range(2 * NA_PAIRS):
                    dst = (case * 2 + half) * 2 * NA_PAIRS + hh
                    for i in range(NA_HROWS):
                        for p in range(NA_WROWS // 2):
                            sl = int(slots[case, half, i, 2 * p])
                            sr = int(slots[case, half, i, 2 * p + 1])
                            tile = td_ref[hh, sl] if sl == sr else jnp.where(left, td_ref[hh, sl], td_ref[hh, sr])
                            bias_ref[dst, i * GRID_W:(i + 1) * GRID_W, p * 2 * GRID_W:(p + 1) * 2 * GRID_W] = tile

    case = jnp.where(j == 0, 0, jnp.where(j == NA_ROW_BLOCKS - 1, 2, 1))
    lane = lax.broadcasted_iota(jnp.int32, (NA_HQ, 2 * NA_HEAD_DIM), 1)
    kv_refs = ((ka, va), (kb, vb))
    items = [(half, pair, hh) for half in range(2) for pair in range(NA_PAIRS) for hh in range(2)]
    pair_lanes = lambda pair: slice(pair * 2 * NA_HEAD_DIM, (pair + 1) * 2 * NA_HEAD_DIM)

    def scores(half, pair, hh):
        lanes = pair_lanes(pair)
        q = q_ref[half * NA_HQ:(half + 1) * NA_HQ, lanes]
        qm = jnp.where((lane // NA_HEAD_DIM) == hh, q, jnp.zeros_like(q))
        k = jnp.concatenate([r[:, lanes] for r in kv_refs[half][0]], axis=0)
        bias = bias_ref[(case * 2 + half) * 2 * NA_PAIRS + 2 * pair + hh]
        return _dot_nt(qm, k) + bias, _dot_nt(qm, kc_ref[:, lanes])

    def softmax(s_loc, s_ctx):
        m = jnp.maximum(jnp.max(s_loc, axis=-1, keepdims=True), jnp.max(s_ctx, axis=-1, keepdims=True))
        p_loc = jnp.exp2(s_loc - m)
        p_ctx = jnp.exp2(s_ctx - m)
        denom = jnp.sum(p_loc, axis=-1, keepdims=True) + jnp.sum(p_ctx, axis=-1, keepdims=True)
        return p_loc.astype(BF16), p_ctx.astype(BF16), denom

    def weighted(half, pair, p_loc, p_ctx, denom):
        lanes = pair_lanes(pair)
        v = jnp.concatenate([r[:, lanes] for r in kv_refs[half][1]], axis=0)
        return (_dot(p_loc, v) + _dot(p_ctx, vc_ref[:, lanes])) / denom

    s, p, o = {}, {}, {}
    for t in range(len(items) + 2):
        if t < len(items):
            s[t] = scores(*items[t])
        if 0 <= t - 1 < len(items):
            p[t - 1] = softmax(*s.pop(t - 1))
        if 0 <= t - 2 < len(items):
            o[t - 2] = weighted(*items[t - 2][:2], *p.pop(t - 2))
    first = (lane // NA_HEAD_DIM) == 0
    for n, (half, pair, hh) in enumerate(items):
        if hh == 0:
            out = jnp.where(first, o[n], o[n + 1])
            o_ref[half * NA_HQ:(half + 1) * NA_HQ, pair_lanes(pair)] = out.astype(o_ref.dtype)


def _na(qkv, bias_rows, cast_ws=()):
    hp_n = NA_HEADS // (2 * NA_PAIRS)
    jn = NA_ROW_BLOCKS
    kb_per_img = SEQ // NA_KBLK
    last_blk = kb_per_img - NA_WBLKS

    def kv_spec(col0, half, t):
        return pl.BlockSpec((NA_KBLK, NA_LANES), lambda hp, j, b: (
            b * kb_per_img + jnp.clip(2 * j + half - 1, 0, last_blk) + t, col0 + hp))

    def kv_specs(col0):
        return [kv_spec(col0, half, t) for half in range(2) for t in range(NA_WBLKS)]

    ctx0 = N_LAT // CTX_LEN
    in_specs = ([pl.BlockSpec((NA_TQ, NA_LANES), lambda hp, j, b: (b * jn + j, hp))]
                + kv_specs(hp_n) + kv_specs(2 * hp_n)
                + [pl.BlockSpec((CTX_LEN, NA_LANES), lambda hp, j, b: (ctx0 + b, hp_n + hp)),
                   pl.BlockSpec((CTX_LEN, NA_LANES), lambda hp, j, b: (ctx0 + b, 2 * hp_n + hp)),
                   pl.BlockSpec((2 * NA_PAIRS, 2 * NA_KH, 2 * GRID_W), lambda hp, j, b: (hp, 0, 0))])
    flat = lambda hp, j, b: (hp * jn + j) * BATCH + b
    casts = [_cast_specs(w, hp_n * jn * BATCH, flat) for w in cast_ws]
    return pl.pallas_call(
        _hosting_casts(_na_kernel, len(in_specs), 1, [u for _, _, u in casts], flat, 3),
        grid=(hp_n, jn, BATCH),
        in_specs=in_specs + [s for s, _, _ in casts],
        out_specs=[pl.BlockSpec((NA_TQ, NA_LANES), lambda hp, j, b: (b * jn + j, hp))] + [s for s, _, _ in casts],
        out_shape=[jax.ShapeDtypeStruct((N_LAT, D_MODEL), BF16)] + [o for _, o, _ in casts],
        scratch_shapes=[pltpu.VMEM((len(NA_CASES) * 4 * NA_PAIRS, NA_HQ, NA_WK), F32),
                        pltpu.VMEM((2 * NA_PAIRS, 2 * NA_KH, GRID_W, 2 * GRID_W), F32)],
        compiler_params=_cparams(("arbitrary", "arbitrary", "arbitrary"), 48),
        name="neighbourhood_attention",
    )(qkv, *([qkv] * (4 * NA_WBLKS)), qkv, qkv, bias_rows, *cast_ws)


def _na_bias_rows(rpb):
    r = rpb.astype(F32) * LOG2_E
    r = jnp.pad(r, ((0, 0), (0, 2 * NA_KH - r.shape[1]), (0, GRID_W - r.shape[2])))
    return jnp.concatenate([r, r], axis=-1)


MR_TM = 1024
MR_TN = 1024


def _mmres_kernel(a_ref, w_ref, x_ref, gt_ref, o_ref):
    o_ref[...] = x_ref[...] + gt_ref[0] * _dot(a_ref[...], w_ref[...])


def _mm_res(a, w, xs, mod, n_rows):
    tm, tn = MR_TM, MR_TN
    return pl.pallas_call(
        _mmres_kernel,
        grid=(n_rows // tm, D_MODEL // tn),
        in_specs=[pl.BlockSpec((tm, D_MODEL), lambda i, j: (i, 0)),
                  pl.BlockSpec((D_MODEL, tn), lambda i, j: (0, j)),
                  pl.BlockSpec((tm, tn), lambda i, j: (i, j)),
                  _mod_spec(5, tm, tn, col=True)],
        out_specs=pl.BlockSpec((tm, tn), lambda i, j: (i, j)),
        out_shape=jax.ShapeDtypeStruct((n_rows, D_MODEL), F32),
        compiler_params=_cparams(("parallel", "arbitrary"), 40),
        name="matmul_gated_residual",
    )(a, w, xs, mod)


def _rope_tables():
    half = GLA_DK // 4
    inv_freq = ROPE_BASE ** (-jnp.arange(half, dtype=F32) / half)
    t = jnp.arange(SEQ)
    ang_r = (t // GRID_W).astype(F32)[:, None] * inv_freq[None, :]
    ang_c = (t % GRID_W).astype(F32)[:, None] * inv_freq[None, :]
    cr, sr, cc, sc = jnp.cos(ang_r), jnp.sin(ang_r), jnp.cos(ang_c), jnp.sin(ang_c)
    cos_t = jnp.concatenate([cr, cc, cr, cc], axis=-1)
    sin_t = jnp.concatenate([-sr, -sc, sr, sc], axis=-1)
    cos_t = jnp.concatenate([cos_t, jnp.ones((GLA_BLOCK, GLA_DK), F32)], axis=0)
    sin_t = jnp.concatenate([sin_t, jnp.zeros((GLA_BLOCK, GLA_DK), F32)], axis=0)
    return cos_t, sin_t


def _pair_layout(w):
    lead = w.shape[:-1]
    quarter = GLA_DK // 4
    w = w.reshape(*lead, GLA_HEADS, 2, 2, quarter)
    return jnp.swapaxes(w, -2, -3).reshape(*lead, GLA_QK_W)


def kernel(x, c, ctx, c_ctx, l0_ada_w, l0_ada_b, l0_norm_ffn1, l0_norm_mix, l0_norm_ffn2, l0_ffn1_w_gu, l0_ffn1_w_down, l0_ffn2_w_gu, l0_ffn2_w_down, l0_w_in, l0_gla_gate_w_fwd, l0_gla_gate_b_fwd, l0_gla_gate_w_bwd, l0_gla_gate_b_bwd, l0_gla_norm, l0_w_out, l1_ada_w, l1_ada_b, l1_norm_ffn1, l1_norm_mix, l1_norm_ffn2, l1_ffn1_w_gu, l1_ffn1_w_down, l1_ffn2_w_gu, l1_ffn2_w_down, l1_w_qkv, l1_rpb, l1_w_out, norm_out):
    bf = lambda w: w.astype(BF16)
    cc = jnp.concatenate([c, c_ctx[None, :], jnp.zeros((8 - BATCH - 1, D_MODEL), F32)], axis=0)

    mod0 = _ada(cc, l0_ada_w, l0_ada_b)
    w_gu, w_down = bf(l0_ffn1_w_gu), bf(l0_ffn1_w_down)
    xs = _ffn(x.reshape(N_LAT, D_MODEL), mod0, 0, l0_norm_ffn1, w_gu, w_down, out_rows=N_TOK)
    xs = _ffn(ctx.reshape(N_CTX, D_MODEL), mod0, 0, l0_norm_ffn1, w_gu, w_down, out_rows=N_TOK,
              tile0=N_LAT // FFN_TM, prev=xs)

    a0 = 2 * GLA_QK_W + GLA_V_W
    a1 = a0 + 2 * GLA_LOWRANK
    w_in = bf(l0_w_in)
    w_main = jnp.concatenate([_pair_layout(w_in[:, :GLA_QK_W]), _pair_layout(w_in[:, GLA_QK_W:2 * GLA_QK_W]),
                              w_in[:, 2 * GLA_QK_W:a0], w_in[:, a1:]], axis=1)
    w_lr = jnp.pad(w_in[:, a0:a1], ((0, 0), (0, GLA_LR_PAD - 2 * GLA_LOWRANK)))
    cc_np, sc_np = _dft_tables(FNET_GROUP_DIM)
    wc = jnp.asarray(np.concatenate([cc_np, sc_np], axis=1), F32).astype(BF16)
    pf, pb, zc, zs, lr = _nmm(xs, mod0, 1, l0_norm_mix, w_main, w_narrow=w_lr, w_chan=wc)

    cos_t, sin_t = _rope_tables()
    gwf = jnp.pad(_pair_layout(l0_gla_gate_w_fwd), ((0, GLA_LR_PAD - GLA_LOWRANK), (0, 0)))
    gwb = jnp.pad(_pair_layout(l0_gla_gate_w_bwd), ((GLA_LOWRANK, GLA_LR_PAD - 2 * GLA_LOWRANK), (0, 0)))
    of, ob, w_gu2, w_down2 = _gla(pf, pb, lr, cos_t, sin_t, gwf, _pair_layout(l0_gla_gate_b_fwd).reshape(1, -1),
                                  gwb, _pair_layout(l0_gla_gate_b_bwd).reshape(1, -1),
                                  cast_ws=(l0_ffn2_w_gu, l0_ffn2_w_down))

    cos_big, sin_big = _big_dft_tables()
    fm, w_gu3, w_down3 = _time_dft_folded(cos_big, sin_big, zc, zs, 512, 512, cast_ws=(l1_ffn1_w_gu, l1_ffn1_w_down))
    ct_np, st_np = _dft_tables(CTX_LEN)
    fm = _time_dft(jnp.asarray(ct_np, F32).astype(BF16), jnp.asarray(st_np, F32).astype(BF16),
                   zc, zs, CTX_LEN, CTX_LEN, 512, N_LAT, prev=fm)[0]

    xs = _glaout(of, ob, pf, fm, l0_gla_norm, xs, mod0, bf(l0_w_out))
    xs = _ffn(xs, mod0, 2, l0_norm_ffn2, w_gu2, w_down2, out_rows=N_TOK)

    mod1 = _ada(cc, l1_ada_w, l1_ada_b)
    xs = _ffn(xs, mod1, 0, l1_norm_ffn1, w_gu3, w_down3, out_rows=N_TOK)
    w_qkv = bf(jnp.concatenate([l1_w_qkv[:, :D_MODEL] * NA_Q_SCALE, l1_w_qkv[:, D_MODEL:]], axis=1))
    qkv, w_gu4, w_down4 = _nmm(xs, mod1, 1, l1_norm_mix, w_qkv, cast_ws=(l1_ffn2_w_gu, l1_ffn2_w_down))
    ao = _na(qkv, _na_bias_rows(l1_rpb))[0]
    xl = _mm_res(ao, bf(l1_w_out), xs, mod1, N_LAT)
    out = _ffn(xl, mod1, 2, l1_norm_ffn2, w_gu4, w_down4, out_rows=N_LAT, final_g=norm_out)
    return out.reshape(BATCH, SEQ, D_MODEL)
```

```python
import functools

import numpy as np
import jax
import jax.numpy as jnp
from jax import lax
from jax.experimental import pallas as pl
from jax.experimental.pallas import tpu as pltpu

D_MODEL = 2048
BATCH = 4
SEQ = 4096
GRID_W = 64
CTX_LEN = 256
N_MOD = 9
D_FF = 5632
RMS_EPS = 1e-6
ROPE_BASE = 10000.0

GLA_HEADS = 4
GLA_DK = 128
GLA_DV = 256
GLA_LOWRANK = 16
GLA_TAU = 16.0
GLA_CHUNK = 64
FNET_GROUPS = 4
FNET_GROUP_DIM = 256
GLA_QK_W = GLA_HEADS * GLA_DK
GLA_V_W = GLA_HEADS * GLA_DV
FNET_W = FNET_GROUPS * FNET_GROUP_DIM

NA_HEADS = 32
NA_HEAD_DIM = 64
NA_KH = 8
NA_KW = 16
NA_QROWS = 8
NEG_BIG = -1e30
LOG2_E = 1.4426950408889634
NA_Q_SCALE = NA_HEAD_DIM ** -0.5 * LOG2_E

N_LAT = BATCH * SEQ
N_CTX = BATCH * CTX_LEN
N_TOK = N_LAT + N_CTX

F32 = jnp.float32
BF16 = jnp.bfloat16
MIB = 1024 * 1024


def _cparams(sem, vmem_mib):
    return pltpu.CompilerParams(dimension_semantics=sem, vmem_limit_bytes=vmem_mib * MIB)


def _dot(a, b):
    return jnp.dot(a, b, preferred_element_type=F32)


def _dot_nt(a, b):
    return lax.dot_general(a, b, (((1,), (1,)), ((), ())), preferred_element_type=F32)


def _dot_tn(a, b):
    return lax.dot_general(a, b, (((0,), (0,)), ((), ())), preferred_element_type=F32)


def _split2(x):
    hi = x.astype(BF16)
    lo = (x - hi.astype(F32)).astype(BF16)
    return hi, lo


def _dot3(a, b):
    ah, al = _split2(a)
    bh, bl = _split2(b)
    return _dot(ah, bh) + (_dot(al, bh) + _dot(ah, bl))


def _dot_exact_lhs(l_bf16, x):
    x1 = x.astype(BF16)
    r1 = x - x1.astype(F32)
    x2 = r1.astype(BF16)
    x3 = (r1 - x2.astype(F32)).astype(BF16)
    return _dot(l_bf16, x1) + (_dot(l_bf16, x2) + _dot(l_bf16, x3))


def _sigmoid(x):
    return 1.0 / (1.0 + jnp.exp(-x))


def _rms(x, g):
    ms = jnp.mean(x * x, axis=-1, keepdims=True)
    return x * lax.rsqrt(ms + RMS_EPS) * g


ROW_CHUNK = 16
ROW_GROUP = 8


def _row_groups(n_rows, body, group=ROW_GROUP):
    span = ROW_CHUNK * group

    def step(t, carry):
        base = t * span
        body([pl.ds(pl.multiple_of(base + c * ROW_CHUNK, ROW_CHUNK), ROW_CHUNK) for c in range(group)])
        return carry
    lax.fori_loop(0, n_rows // span, step, 0)


def _inv_rms(x):
    return lax.rsqrt(jnp.mean(x * x, axis=-1, keepdims=True) + RMS_EPS)


def _norm_mod_rows(x_ref, g_ref, sh_ref, sc_ref, h_ref, gs_ref, zero_ref=None):
    d = x_ref.shape[1]
    gs_ref[0] = jnp.broadcast_to(g_ref[...] * (1.0 + sc_ref[0]), (ROW_CHUNK, d))
    gs_ref[1] = jnp.broadcast_to(sh_ref[0], (ROW_CHUNK, d))

    def body(chunks):
        inv = [_inv_rms(x_ref[rows, :]) for rows in chunks]
        for rows, r in zip(chunks, inv):
            h_ref[rows, :] = (x_ref[rows, :] * r * gs_ref[0] + gs_ref[1]).astype(BF16)
            if zero_ref is not None:
                zero_ref[rows, :] = jnp.zeros((ROW_CHUNK, zero_ref.shape[1]), zero_ref.dtype)
    _row_groups(x_ref.shape[0], body)


def _mod_row(i, tm):
    return jnp.minimum(i // (SEQ // tm), BATCH)


def _mod_spec(j, tm, width=D_MODEL, col=None, tile0=0):
    if col is None:
        return pl.BlockSpec((1, 1, width), lambda i, *_: (_mod_row(i + tile0, tm) * N_MOD + j, 0, 0))
    return pl.BlockSpec((1, 1, width), lambda i, n, *_: (_mod_row(i + tile0, tm) * N_MOD + j, 0, n))


CAST_ROW_ALIGN = 16


def _cast_plan(rows, steps):
    per = pl.cdiv(pl.cdiv(rows, steps), CAST_ROW_ALIGN) * CAST_ROW_ALIGN
    while rows % per:
        per += CAST_ROW_ALIGN
    return per, rows // per


def _cast_specs(w, steps, flat_index):
    per, used = _cast_plan(w.shape[0], steps)
    spec = pl.BlockSpec((per, w.shape[1]), lambda *g: (jnp.minimum(flat_index(*g), used - 1), 0))
    return spec, jax.ShapeDtypeStruct(w.shape, BF16), used


def _hosting_casts(kernel_fn, n_in, n_out, used, flat_index, grid_rank):
    nc = len(used)

    def wrapped(*refs):
        a, b, c = n_in + nc, n_in + nc + n_out, n_in + nc + n_out + nc
        kernel_fn(*refs[:n_in], *refs[a:b], *refs[c:])
        t = flat_index(*[pl.program_id(d) for d in range(grid_rank)])
        for n_used, src_ref, dst_ref in zip(used, refs[n_in:a], refs[b:c]):
            @pl.when(t < n_used)
            def _(src_ref=src_ref, dst_ref=dst_ref):
                dst_ref[...] = src_ref[...].astype(BF16)
    return wrapped


ADA_TN = 1024


def _ada_kernel(c_ref, w_ref, b_ref, o_ref):
    c = c_ref[...]
    s = (c * _sigmoid(c)).astype(BF16)
    o_ref[...] = _dot(s, w_ref[...].astype(BF16)) + b_ref[...]


def _ada(cc, w, b):
    n = w.shape[1]
    out = pl.pallas_call(
        _ada_kernel,
        grid=(n // ADA_TN,),
        in_specs=[pl.BlockSpec((8, D_MODEL), lambda j: (0, 0)),
                  pl.BlockSpec((D_MODEL, ADA_TN), lambda j: (0, j)),
                  pl.BlockSpec((1, ADA_TN), lambda j: (0, j))],
        out_specs=pl.BlockSpec((8, ADA_TN), lambda j: (0, j)),
        out_shape=jax.ShapeDtypeStruct((8, n), F32),
        compiler_params=_cparams(("arbitrary",), 40),
        name="ada_mod",
    )(cc, w, b.reshape(1, n))
    return out.reshape(8 * N_MOD, 1, D_MODEL)


FFN_TM = 1024
FFN_TF = 512
FFN_SLAB = 128


def _ffn_kernel(x_hbm, xn_ref, g_ref, sh_ref, sc_ref, shn_ref, scn_ref, gt_ref, wg_ref, wu_ref, wd_ref, *rest,
                final_norm):
    go_ref = rest[0] if final_norm else None
    o_ref, h0_ref, h1_ref, gs_ref, x_ref, x_sem = rest[-6:]
    i = pl.program_id(0)
    f = pl.program_id(1)
    last = pl.num_programs(1) - 1
    tm, d = x_ref.shape

    def x_copy():
        return pltpu.make_async_copy(x_hbm.at[pl.ds(pl.multiple_of(i * tm, tm), tm), :], x_ref, x_sem)

    @pl.when(f == 0)
    def _():
        x_copy().start()

    @pl.when((i == 0) & (f == 0))
    def _():
        x_copy().wait()
        _norm_mod_rows(x_ref, g_ref, sh_ref, sc_ref, h0_ref, gs_ref)

    def step(h_ref, hn_ref, first):
        gs_ref[0] = jnp.broadcast_to(g_ref[...] * (1.0 + scn_ref[0]), (ROW_CHUNK, d))
        gs_ref[1] = jnp.broadcast_to(shn_ref[0], (ROW_CHUNK, d))
        slab0 = jnp.minimum(f, tm // FFN_SLAB - 1) * FFN_SLAB
        chunks = [pl.ds(k * ROW_CHUNK, ROW_CHUNK) for k in range(FFN_SLAB // ROW_CHUNK)]
        inv = [_inv_rms(xn_ref[rows, :]) for rows in chunks]
        for k, (rows, r) in enumerate(zip(chunks, inv)):
            dst = pl.ds(pl.multiple_of(slab0 + k * ROW_CHUNK, ROW_CHUNK), ROW_CHUNK)
            hn_ref[dst, :] = (xn_ref[rows, :] * r * gs_ref[0] + gs_ref[1]).astype(BF16)

        h = h_ref[...]
        a = _dot(h, wg_ref[...])
        u = _dot(h, wu_ref[...])
        act = (a * _sigmoid(a) * u).astype(BF16)
        if first:
            o_ref[...] = _dot(act, wd_ref[...])
        else:
            o_ref[...] += _dot(act, wd_ref[...])

    for parity, (h_ref, hn_ref) in enumerate(((h0_ref, h1_ref), (h1_ref, h0_ref))):
        for first in (True, False):
            @pl.when((i % 2 == parity) & ((f == 0) == first))
            def _(h_ref=h_ref, hn_ref=hn_ref, first=first):
                step(h_ref, hn_ref, first)

    @pl.when(f == last)
    def _():
        @pl.when(i > 0)
        def _():
            x_copy().wait()

        if not final_norm:
            o_ref[...] = x_ref[...] + (0.5 * gt_ref[0]) * o_ref[...]
        else:
            gs_ref[2] = jnp.broadcast_to(0.5 * gt_ref[0], (ROW_CHUNK, d))
            gs_ref[3] = jnp.broadcast_to(go_ref[...], (ROW_CHUNK, d))

            def body(chunks):
                resid = lambda rows: x_ref[rows, :] + gs_ref[2] * o_ref[rows, :]
                inv = [_inv_rms(resid(rows)) for rows in chunks]
                for rows, r in zip(chunks, inv):
                    o_ref[rows, :] = resid(rows) * r * gs_ref[3]
            _row_groups(tm, body)


def _ffn(x_in, mod, sub, norm_g, w_gu, w_down, *, out_rows, tile0=0, prev=None, final_g=None):
    tm, tf = FFN_TM, FFN_TF
    nf = D_FF // tf
    n_tiles = x_in.shape[0] // tm
    slabs = tm // FFN_SLAB
    assert slabs <= nf
    nxt = lambda i: jnp.minimum(i + 1, n_tiles - 1)
    in_specs = [pl.BlockSpec(memory_space=pl.ANY),
                pl.BlockSpec((FFN_SLAB, D_MODEL), lambda i, f: (nxt(i) * slabs + jnp.minimum(f, slabs - 1), 0)),
                pl.BlockSpec((1, D_MODEL), lambda i, f: (0, 0)),
                _mod_spec(3 * sub, tm, tile0=tile0), _mod_spec(3 * sub + 1, tm, tile0=tile0),
                pl.BlockSpec((1, 1, D_MODEL), lambda i, f: (_mod_row(nxt(i) + tile0, tm) * N_MOD + 3 * sub, 0, 0)),
                pl.BlockSpec((1, 1, D_MODEL), lambda i, f: (_mod_row(nxt(i) + tile0, tm) * N_MOD + 3 * sub + 1, 0, 0)),
                _mod_spec(3 * sub + 2, tm, tile0=tile0),
                pl.BlockSpec((D_MODEL, tf), lambda i, f: (0, f)),
                pl.BlockSpec((D_MODEL, tf), lambda i, f: (0, f + nf)),
                pl.BlockSpec((tf, D_MODEL), lambda i, f: (f, 0))]
    args = [x_in, x_in, norm_g.reshape(1, D_MODEL), mod, mod, mod, mod, mod, w_gu, w_gu, w_down]
    if final_g is not None:
        in_specs.append(pl.BlockSpec((1, D_MODEL), lambda i, f: (0, 0)))
        args.append(final_g.reshape(1, D_MODEL))
    aliases = {}
    if prev is not None:
        aliases = {len(args): 0}
        in_specs.append(pl.BlockSpec(memory_space=pl.ANY))
        args.append(prev)
    return pl.pallas_call(
        functools.partial(_ffn_kernel, final_norm=final_g is not None),
        grid=(n_tiles, nf),
        in_specs=in_specs,
        out_specs=pl.BlockSpec((tm, D_MODEL), lambda i, f: (i + tile0, 0)),
        out_shape=jax.ShapeDtypeStruct((out_rows, D_MODEL), F32),
        scratch_shapes=[pltpu.VMEM((tm, D_MODEL), BF16), pltpu.VMEM((tm, D_MODEL), BF16),
                        pltpu.VMEM((4, ROW_CHUNK, D_MODEL), F32), pltpu.VMEM((tm, D_MODEL), F32),
                        pltpu.SemaphoreType.DMA(())],
        input_output_aliases=aliases,
        compiler_params=_cparams(("arbitrary", "arbitrary"), 56),
        name="ffn_half",
    )(*args)


NMM_TM = 1024
NMM_TN = 1024


def _nmm_kernel(x_ref, g_ref, sh_ref, sc_ref, w_ref, *rest, gla):
    if gla:
        wn_ref, wc_ref, o_ref, ob_ref, zc_ref, zs_ref, on_ref, h_ref, gs_ref = rest
    else:
        o_ref, h_ref, gs_ref = rest
    j = pl.program_id(1)

    @pl.when(j == 0)
    def _():
        _norm_mod_rows(x_ref, g_ref, sh_ref, sc_ref, h_ref, gs_ref)
        if gla:
            on_ref[...] = _dot(h_ref[...], wn_ref[...])

    if not gla:
        o_ref[...] = _dot(h_ref[...], w_ref[...]).astype(o_ref.dtype)
    else:
        @pl.when(j % 2 == 0)
        def _():
            o_ref[...] = _dot(h_ref[...], w_ref[...])

        @pl.when(j == 1)
        def _():
            ob_ref[...] = _dot(h_ref[...], w_ref[...]).astype(BF16)

        @pl.when(j == 3)
        def _():
            f = _dot(h_ref[...], w_ref[...]).astype(BF16)
            for grp in range(FNET_GROUPS):
                cols = slice(grp * FNET_GROUP_DIM, (grp + 1) * FNET_GROUP_DIM)
                z = _dot(f[:, cols], wc_ref[...])
                zc_ref[:, cols] = z[:, :FNET_GROUP_DIM].astype(BF16)
                zs_ref[:, cols] = z[:, FNET_GROUP_DIM:].astype(BF16)


def _nmm(xs, mod, sub, norm_g, w, w_narrow=None, w_chan=None, cast_ws=()):
    tm, tn = NMM_TM, NMM_TN
    n = w.shape[1]
    gla = w_narrow is not None
    in_specs = [pl.BlockSpec((tm, D_MODEL), lambda i, j: (i, 0)),
                pl.BlockSpec((1, D_MODEL), lambda i, j: (0, 0)),
                _mod_spec(3 * sub, tm), _mod_spec(3 * sub + 1, tm),
                pl.BlockSpec((D_MODEL, tn), lambda i, j: (0, j))]
    args = [xs, norm_g.reshape(1, D_MODEL), mod, mod, w]
    if gla:
        nn = w_narrow.shape[1]
        assert n == 4 * tn and tn == GLA_V_W == FNET_W
        in_specs += [pl.BlockSpec((D_MODEL, nn), lambda i, j: (0, 0)),
                     pl.BlockSpec(w_chan.shape, lambda i, j: (0, 0))]
        args += [w_narrow, w_chan]
        row_tile = pl.BlockSpec((tm, tn), lambda i, j: (i, 0))
        out_specs = [pl.BlockSpec((tm, tn), lambda i, j: (i, j // 2)), row_tile, row_tile, row_tile,
                     pl.BlockSpec((tm, nn), lambda i, j: (i, 0))]
        out_shape = [jax.ShapeDtypeStruct((N_TOK, n // 2), F32)] + [jax.ShapeDtypeStruct((N_TOK, tn), BF16)] * 3 + [
            jax.ShapeDtypeStruct((N_TOK, nn), F32)]
    else:
        out_specs = [pl.BlockSpec((tm, tn), lambda i, j: (i, j))]
        out_shape = [jax.ShapeDtypeStruct((N_TOK, n), BF16)]
    nj = n // tn
    flat = lambda i, j: i * nj + j
    casts = [_cast_specs(cw, (N_TOK // tm) * nj, flat) for cw in cast_ws]
    return pl.pallas_call(
        _hosting_casts(functools.partial(_nmm_kernel, gla=gla), len(args), len(out_specs),
                       [u for _, _, u in casts], flat, 2),
        grid=(N_TOK // tm, nj),
        in_specs=in_specs + [s for s, _, _ in casts],
        out_specs=out_specs + [s for s, _, _ in casts],
        out_shape=out_shape + [o for _, o, _ in casts],
        scratch_shapes=[pltpu.VMEM((tm, D_MODEL), BF16), pltpu.VMEM((2, ROW_CHUNK, D_MODEL), F32)],
        compiler_params=_cparams(("arbitrary", "arbitrary"), 56),
        name="norm_mod_matmul",
    )(*args, *cast_ws)


GLA_BLOCK = CTX_LEN
GLA_LAT_BLOCKS = SEQ // GLA_BLOCK
GLA_STEPS = 1 + GLA_LAT_BLOCKS
GLA_LR_PAD = 128
GLA_SPLIT_RANGE = 80.0


def _rope(x, cos, sin_signed):
    return x * cos + pltpu.roll(x, GLA_DK // 2, 1) * sin_signed


def _gla_kernel(qf, kf, vf, lf, cf, sf, qb, kb, vb, lb, cb, sb, lfn, lbn, gwf, gbf, gwb, gbb,
                of_ref, ob_ref, st_ref, row_ref, bc_ref, md_ref):

    c = GLA_CHUNK
    n = GLA_BLOCK
    row = lax.broadcasted_iota(jnp.int32, (n, n), 0)
    col = lax.broadcasted_iota(jnp.int32, (n, n), 1)
    same_chunk = (row // c) == (col // c)
    row_c = lax.broadcasted_iota(jnp.int32, (c, c), 0)
    col_c = lax.broadcasted_iota(jnp.int32, (c, c), 1)
    scale = GLA_DK ** -0.5

    dirs = ((qf, kf, vf, lf, cf, sf, gwf, gbf, of_ref), (qb, kb, vb, lb, cb, sb, gwb, gbb, ob_ref))
    n_chunks = n // c
    heads = [(slice(h * GLA_DK, (h + 1) * GLA_DK), slice(h * GLA_DV, (h + 1) * GLA_DV)) for h in range(GLA_HEADS)]
    tris = [row_c >= col_c, row_c <= col_c]
    last_row = (c - 1, 0)

    def gate_decays(d, l_ref):
        causal = (row >= col) if d == 0 else (row <= col)
        tri_b = jnp.where(same_chunk & causal, 1.0, 0.0).astype(BF16)
        x = _dot3(l_ref[...], dirs[d][6][...]) + dirs[d][7][...]
        g = (jnp.minimum(x, 0.0) - jnp.log1p(jnp.exp(-jnp.abs(x)))) * (1.0 / GLA_TAU)
        bc_all = _dot_exact_lhs(tri_b, g)
        totals = [-bc_all[ci * c + last_row[d]:ci * c + last_row[d] + 1, :] for ci in range(n_chunks)]
        return bc_all, functools.reduce(jnp.maximum, totals)

    def prepare_decays(l_refs):
        both = [gate_decays(d, l_ref) for d, l_ref in enumerate(l_refs)]

        def store():
            bc_ref[0] = both[0][0]
            bc_ref[1] = both[1][0]
            md_ref[...] = jnp.maximum(both[0][1], both[1][1])
        return store

    @pl.when(pl.program_id(1) == 0)
    def _():
        st_ref[...] = jnp.zeros_like(st_ref)
        prepare_decays((lf, lb))()

    decays = []
    for d in range(2):
        bc_all = bc_ref[d]
        per_chunk = []
        for ci in (range(n_chunks) if d == 0 else range(n_chunks - 1, -1, -1)):
            rows = slice(ci * c, (ci + 1) * c)
            bc = bc_all[rows, :]
            per_chunk.append((rows, bc, bc[c // 2:c // 2 + 1, :], bc[last_row[d]:last_row[d] + 1, :]))
        decays.append(per_chunk)
    max_decay = md_ref[...]

    def scores_by_rows(d, qh, kh, bc):
        row_ref[0], row_ref[1] = bc, qh
        s_idx = lax.broadcasted_iota(jnp.int32, (c, 1), 0)
        t_idx = lax.broadcasted_iota(jnp.int32, (c, c), 1)

        def one_row(t, att_t):
            valid = (s_idx <= t) if d == 0 else (s_idx >= t)
            w = jnp.exp(jnp.where(valid, row_ref[0, pl.ds(t, 1), :] - bc, -jnp.inf))
            col = jnp.sum(kh * w * row_ref[1, pl.ds(t, 1), :], axis=-1, keepdims=True)
            return jnp.where(t_idx == t, col, att_t)
        return lax.fori_loop(0, c, one_row, jnp.zeros((c, c), F32))

    def local_part(k, exact):
        out = {}
        for d, (q_ref, k_ref, v_ref, _, c_ref, s_ref, _, _, _) in enumerate(dirs):
            rows, bc, mid, last = decays[d][k]
            e_q, e_kl, e_l = jnp.exp(bc), jnp.exp(last - bc), jnp.exp(last)
            if not exact:
                e_qm, e_km = jnp.exp(bc - mid), jnp.exp(mid - bc)
            cos, sin = c_ref[rows, :], s_ref[rows, :]
            for h, (sk, sv) in enumerate(heads):
                qh = _rope(q_ref[rows, sk] * scale, cos, sin)
                kh = _rope(k_ref[rows, sk], cos, sin)
                vh = v_ref[rows, sv].astype(BF16)
                kv = _dot_tn(vh, (kh * e_kl[:, sk]).astype(BF16))
                if exact:
                    att_t = scores_by_rows(d, qh, kh, bc[:, sk]).astype(BF16)
                    mix = lambda att_t=att_t, vh=vh: _dot_tn(att_t, vh)
                else:
                    att = _dot_nt((qh * e_qm[:, sk]).astype(BF16), (kh * e_km[:, sk]).astype(BF16))
                    att = jnp.where(tris[d], att, 0.0).astype(BF16)
                    mix = lambda att=att, vh=vh: _dot(att, vh)
                out[d, h] = ((qh * e_q[:, sk]).astype(BF16), mix, kv, e_l[:, sk])
        return out

    def scan_block(exact):
        hand_over = prepare_decays((lfn, lbn))
        st = {(d, h): st_ref[d, h] for d in range(2) for h in range(GLA_HEADS)}
        local = local_part(0, exact)
        for k in range(n_chunks):
            nxt = local_part(k + 1, exact) if k + 1 < n_chunks else None
            for d in range(2):
                rows = decays[d][k][0]
                for h, (sk, sv) in enumerate(heads):
                    qe, mix, kv, e_l = local[d, h]
                    dirs[d][-1][rows, sv] = _dot_nt(qe, st[d, h].astype(BF16)) + mix()
                    st[d, h] = st[d, h] * e_l + kv
            local = nxt
        for (d, h), val in st.items():
            st_ref[d, h] = val
        hand_over()

    in_range = jnp.max(max_decay) <= GLA_SPLIT_RANGE

    @pl.when(in_range)
    def _():
        scan_block(exact=False)

    @pl.when(jnp.logical_not(in_range))
    def _():
        scan_block(exact=True)


def _gla(pf, pb, lr, cos_t, sin_t, gwf, gbf, gwb, gbb, cast_ws):
    c = GLA_BLOCK
    ctx0 = N_LAT // c

    def fwd_blk(b, s):
        return jnp.where(s == 0, ctx0 + b, GLA_LAT_BLOCKS * b + s - 1)

    def bwd_blk(b, s):
        return jnp.where(s == 0, ctx0 + b, GLA_LAT_BLOCKS * b + (GLA_STEPS - 1 - s))

    def fwd_rope(b, s):
        return jnp.where(s == 0, GLA_LAT_BLOCKS, s - 1)

    def bwd_rope(b, s):
        return jnp.where(s == 0, GLA_LAT_BLOCKS, GLA_STEPS - 1 - s)

    def dir_specs(blk, rope):
        return [pl.BlockSpec((c, GLA_QK_W), lambda b, s: (blk(b, s), 0)),
                pl.BlockSpec((c, GLA_QK_W), lambda b, s: (blk(b, s), 1)),
                pl.BlockSpec((c, GLA_V_W), lambda b, s: (blk(b, s), 0)),
                pl.BlockSpec((c, GLA_LR_PAD), lambda b, s: (blk(b, s), 0)),
                pl.BlockSpec((c, GLA_DK), lambda b, s: (rope(b, s), 0)),
                pl.BlockSpec((c, GLA_DK), lambda b, s: (rope(b, s), 0))]

    const = lambda shape: pl.BlockSpec(shape, lambda b, s: (0, 0))
    ahead = lambda s: jnp.minimum(s + 1, GLA_STEPS - 1)
    in_specs = (dir_specs(fwd_blk, fwd_rope) + dir_specs(bwd_blk, bwd_rope)
                + [pl.BlockSpec((c, GLA_LR_PAD), lambda b, s: (fwd_blk(b, ahead(s)), 0)),
                   pl.BlockSpec((c, GLA_LR_PAD), lambda b, s: (bwd_blk(b, ahead(s)), 0))]
                + [const((GLA_LR_PAD, GLA_QK_W)), const((1, GLA_QK_W)),
                   const((GLA_LR_PAD, GLA_QK_W)), const((1, GLA_QK_W))])
    flat = lambda b, s: b * GLA_STEPS + s
    casts = [_cast_specs(w, BATCH * GLA_STEPS, flat) for w in cast_ws]
    args = [pf, pf, pb, lr, cos_t, sin_t, pf, pf, pb, lr, cos_t, sin_t, lr, lr, gwf, gbf, gwb, gbb]
    return pl.pallas_call(
        _hosting_casts(_gla_kernel, len(args), 2, [u for _, _, u in casts], flat, 2),
        grid=(BATCH, GLA_STEPS),
        in_specs=in_specs + [s for s, _, _ in casts],
        out_specs=[pl.BlockSpec((c, GLA_V_W), lambda b, s: (fwd_blk(b, s), 0)),
                   pl.BlockSpec((c, GLA_V_W), lambda b, s: (bwd_blk(b, s), 0))] + [s for s, _, _ in casts],
        out_shape=[jax.ShapeDtypeStruct((N_TOK, GLA_V_W), F32)] * 2 + [o for _, o, _ in casts],
        scratch_shapes=[pltpu.VMEM((2, GLA_HEADS, GLA_DV, GLA_DK), F32), pltpu.VMEM((2, GLA_CHUNK, GLA_DK), F32),
                        pltpu.VMEM((2, c, GLA_QK_W), F32), pltpu.VMEM((1, GLA_QK_W), F32)],
        compiler_params=_cparams(("arbitrary", "arbitrary"), 40),
        name="gla_scan",
    )(*args, *cast_ws)


def _tdft_kernel(c_ref, s_ref, zc_ref, zs_ref, *rest):
    o_ref = rest[-1]
    o_ref[...] = (_dot(c_ref[...], zc_ref[...]) - _dot(s_ref[...], zs_ref[...])).astype(o_ref.dtype)


def _time_dft(cmat, smat, zc, zs, t_len, tm, tn, row0, prev=None, cast_ws=()):
    zb0 = row0 // t_len
    ob0 = row0 // tm
    mt = t_len // tm
    in_specs = [pl.BlockSpec((tm, t_len), lambda b, n, m: (m, 0)),
                pl.BlockSpec((tm, t_len), lambda b, n, m: (m, 0)),
                pl.BlockSpec((t_len, tn), lambda b, n, m: (zb0 + b, n)),
                pl.BlockSpec((t_len, tn), lambda b, n, m: (zb0 + b, n))]
    args = [cmat, smat, zc, zs]
    aliases = {}
    if prev is not None:
        in_specs.append(pl.BlockSpec(memory_space=pl.ANY))
        args.append(prev)
        aliases = {4: 0}
    nt = FNET_W // tn
    flat = lambda b, n, m: (b * nt + n) * mt + m
    casts = [_cast_specs(w, BATCH * nt * mt, flat) for w in cast_ws]
    return pl.pallas_call(
        _hosting_casts(_tdft_kernel, len(args), 1, [u for _, _, u in casts], flat, 3),
        grid=(BATCH, nt, mt),
        in_specs=in_specs + [s for s, _, _ in casts],
        out_specs=[pl.BlockSpec((tm, tn), lambda b, n, m: (ob0 + b * mt + m, n))] + [s for s, _, _ in casts],
        out_shape=[jax.ShapeDtypeStruct((N_TOK, FNET_W), BF16)] + [o for _, o, _ in casts],
        input_output_aliases=aliases,
        compiler_params=_cparams(("arbitrary", "arbitrary", "arbitrary"), 48),
        name="fnet_time_dft",
    )(*args, *cast_ws)


FOLD_BLK = 256


def _tdft_fold_kernel(c_ref, s_ref, zc_ref, zs_ref, j1_ref, j2_ref, o_ref, zcf_ref, zsf_ref):
    half, blk = zcf_ref.shape[0], FOLD_BLK
    nblk = 2 * half // blk

    @pl.when(pl.program_id(2) == 0)
    def _():
        for kb in range(half // blk):
            rows = slice(kb * blk, (kb + 1) * blk)
            mirror = slice((nblk - 1 - kb) * blk, (nblk - kb) * blk)
            after = slice((nblk - kb) * blk, (nblk - kb + 1) * blk)
            for z_ref, zf_ref, sign in ((zc_ref, zcf_ref, 1.0), (zs_ref, zsf_ref, -1.0)):
                partner = _dot(j1_ref[...], z_ref[mirror, :])
                if kb > 0:
                    partner = partner + _dot(j2_ref[...], z_ref[after, :])
                zf_ref[rows, :] = (z_ref[rows, :].astype(F32) + sign * partner).astype(BF16)

    tm = o_ref.shape[0]
    j = pl.program_id(2) * tm + lax.broadcasted_iota(jnp.int32, (tm, 1), 0)
    sign = jnp.where(j % 2 == 0, SEQ ** -0.5, -(SEQ ** -0.5))
    mid = zc_ref[half:half + 1, :].astype(F32)
    o = _dot(c_ref[...], zcf_ref[...]) - _dot(s_ref[...], zsf_ref[...]) + sign * mid
    o_ref[...] = o.astype(o_ref.dtype)


def _time_dft_folded(cmat, smat, zc, zs, tm, tn, cast_ws=()):
    half = SEQ // 2
    mt = SEQ // tm
    nt = FNET_W // tn
    i = np.arange(FOLD_BLK)
    j1 = jnp.asarray((i[:, None] + i[None, :] == FOLD_BLK) & (i[:, None] > 0), BF16)
    j2 = jnp.asarray((i[:, None] == 0) & (i[None, :] == 0), BF16)
    args = [cmat, smat, zc, zs, j1, j2]
    in_specs = [pl.BlockSpec((tm, half), lambda b, n, m: (m, 0)),
                pl.BlockSpec((tm, half), lambda b, n, m: (m, 0)),
                pl.BlockSpec((SEQ, tn), lambda b, n, m: (b, n)),
                pl.BlockSpec((SEQ, tn), lambda b, n, m: (b, n)),
                pl.BlockSpec((FOLD_BLK, FOLD_BLK), lambda b, n, m: (0, 0)),
                pl.BlockSpec((FOLD_BLK, FOLD_BLK), lambda b, n, m: (0, 0))]
    flat = lambda b, n, m: (b * nt + n) * mt + m
    casts = [_cast_specs(w, BATCH * nt * mt, flat) for w in cast_ws]
    return pl.pallas_call(
        _hosting_casts(_tdft_fold_kernel, len(args), 1, [u for _, _, u in casts], flat, 3),
        grid=(BATCH, nt, mt),
        in_specs=in_specs + [s for s, _, _ in casts],
        out_specs=[pl.BlockSpec((tm, tn), lambda b, n, m: (b * mt + m, n))] + [s for s, _, _ in casts],
        out_shape=[jax.ShapeDtypeStruct((N_TOK, FNET_W), BF16)] + [o for _, o, _ in casts],
        scratch_shapes=[pltpu.VMEM((half, tn), BF16), pltpu.VMEM((half, tn), BF16)],
        compiler_params=_cparams(("arbitrary", "arbitrary", "arbitrary"), 48),
        name="fnet_time_dft_folded",
    )(*args, *cast_ws)


def _dft_tables(n):
    k = np.arange(n)
    ang = 2.0 * np.pi * ((k[:, None] * k[None, :]) % n) / n
    return np.cos(ang) / np.sqrt(n), np.sin(ang) / np.sqrt(n)


def _dft_table_kernel(ac_ref, as_ref, bc_ref, bs_ref, co_ref, so_ref):
    ac, as_ = ac_ref[0], as_ref[0]
    bc, bs = bc_ref[...], bs_ref[...]
    co_ref[...] = (ac * bc - as_ * bs).astype(BF16)
    so_ref[...] = (ac * bs + as_ * bc).astype(BF16)


def _big_dft_tables():
    r = GRID_W
    k = np.arange(SEQ)
    j = np.arange(r)
    pa = 2.0 * np.pi * ((j[:, None] * r * k[None, :]) % SEQ) / SEQ
    pb = 2.0 * np.pi * ((j[:, None] * k[None, :]) % SEQ) / SEQ
    ac = jnp.asarray(np.cos(pa).reshape(r, 1, SEQ), F32)
    as_ = jnp.asarray(np.sin(pa).reshape(r, 1, SEQ), F32)
    bc = jnp.asarray(np.cos(pb) * SEQ ** -0.5, F32)
    bs = jnp.asarray(np.sin(pb) * SEQ ** -0.5, F32)
    row = pl.BlockSpec((1, 1, SEQ), lambda i: (i, 0, 0))
    full = pl.BlockSpec((r, SEQ), lambda i: (0, 0))
    return pl.pallas_call(
        _dft_table_kernel,
        grid=(r,),
        in_specs=[row, row, full, full],
        out_specs=[pl.BlockSpec((r, SEQ), lambda i: (i, 0))] * 2,
        out_shape=[jax.ShapeDtypeStruct((SEQ, SEQ), BF16)] * 2,
        compiler_params=_cparams(("parallel",), 16),
        name="dft_tables",
    )(ac, as_, bc, bs)


GO_TM = 512
GO_TN = 2048


def _glaout_kernel(of_ref, ob_ref, r_ref, fm_ref, gn_ref, x_ref, gt_ref, w_ref, o_ref, h_ref):
    @pl.when(pl.program_id(1) == 0)
    def _():
        heads = [slice(h * GLA_DV, (h + 1) * GLA_DV) for h in range(GLA_HEADS)]

        def body(chunks):
            inv = [[_inv_rms(of_ref[rows, sv] + ob_ref[rows, sv]) for sv in heads] for rows in chunks]
            for rows, inv_c in zip(chunks, inv):
                for sv, r in zip(heads, inv_c):
                    o = (of_ref[rows, sv] + ob_ref[rows, sv]) * r
                    gate = r_ref[rows, sv]
                    h_ref[rows, sv] = (o * gn_ref[:, sv] * (gate * _sigmoid(gate))).astype(BF16)
                h_ref[rows, GLA_V_W:] = fm_ref[rows, :].astype(BF16)
        _row_groups(of_ref.shape[0], body, group=4)

    o_ref[...] = x_ref[...] + gt_ref[0] * _dot(h_ref[...], w_ref[...])


def _glaout(of, ob, pf, fm, g_norm, xs, mod, w_out):
    tm, tn = GO_TM, GO_TN
    half = lambda col: pl.BlockSpec((tm, GLA_V_W), lambda i, j: (i, col))
    return pl.pallas_call(
        _glaout_kernel,
        grid=(N_TOK // tm, D_MODEL // tn),
        in_specs=[half(0), half(0), half(1), half(0),
                  pl.BlockSpec((1, GLA_V_W), lambda i, j: (0, 0)),
                  pl.BlockSpec((tm, tn), lambda i, j: (i, j)),
                  _mod_spec(5, tm, tn, col=True),
                  pl.BlockSpec((D_MODEL, tn), lambda i, j: (0, j))],
        out_specs=pl.BlockSpec((tm, tn), lambda i, j: (i, j)),
        out_shape=jax.ShapeDtypeStruct((N_TOK, D_MODEL), F32),
        scratch_shapes=[pltpu.VMEM((tm, D_MODEL), BF16)],
        compiler_params=_cparams(("parallel", "arbitrary"), 56),
        name="gla_fnet_out",
    )(of, ob, pf, fm, g_norm.reshape(1, GLA_V_W), xs, mod, w_out)


NA_HROWS = NA_QROWS // 2
NA_WROWS = NA_HROWS + NA_KH
NA_TQ = NA_QROWS * GRID_W
NA_HQ = NA_HROWS * GRID_W
NA_WK = NA_WROWS * GRID_W
NA_KBLK = NA_HROWS * GRID_W
NA_WBLKS = NA_WK // NA_KBLK
NA_MASKED = 2 * NA_KH - 1
NA_ROW_BLOCKS = (SEQ // GRID_W) // NA_QROWS
NA_CASES = (0, 1, NA_ROW_BLOCKS - 1)
NA_PAIRS = 2
NA_LANES = NA_PAIRS * 2 * NA_HEAD_DIM


def _na_window_row0(j, half):
    q0 = NA_QROWS * j + NA_HROWS * half
    return int(np.clip(q0 - NA_KH // 2, 0, SEQ // GRID_W - NA_WROWS))


def _na_bias_slots():
    rows = SEQ // GRID_W
    slots = np.full((len(NA_CASES), 2, NA_HROWS, NA_WROWS), NA_MASKED, np.int32)
    for case, j in enumerate(NA_CASES):
        for half in range(2):
            w0 = _na_window_row0(j, half)
            for i in range(NA_HROWS):
                qr = NA_QROWS * j + NA_HROWS * half + i
                r0 = int(np.clip(qr - NA_KH // 2, 0, rows - NA_KH))
                for l in range(NA_WROWS):
                    kr = w0 + l
                    if r0 <= kr < r0 + NA_KH:
                        slots[case, half, i, l] = kr - qr + NA_KH - 1
    return slots


def _na_kernel(q_ref, *refs):
    nb = NA_WBLKS
    ka, kb, va, vb = refs[0:nb], refs[nb:2 * nb], refs[2 * nb:3 * nb], refs[3 * nb:4 * nb]
    kc_ref, vc_ref, rp_ref, o_ref, bias_ref, td_ref = refs[4 * nb:]
    j = pl.program_id(1)

    @pl.when((j == 0) & (pl.program_id(2) == 0))
    def _():
        qc = lax.broadcasted_iota(jnp.int32, (GRID_W, 2 * GRID_W), 0)
        kc = lax.broadcasted_iota(jnp.int32, (GRID_W, 2 * GRID_W), 1) % GRID_W
        start = jnp.clip(qc - NA_KW // 2, 0, GRID_W - NA_KW)
        window = (kc >= start) & (kc < start + NA_KW)
        for hh in range(2 * NA_PAIRS):
            for slot in range(NA_MASKED):
                td_ref[hh, slot] = jnp.broadcast_to(rp_ref[hh, slot:slot + 1, :], (GRID_W, 2 * GRID_W))
                tile = pltpu.roll(td_ref[hh, slot], 2 * GRID_W - (NA_KW - 1), 1, stride=1, stride_axis=0)
                td_ref[hh, slot] = jnp.where(window, tile, NEG_BIG)
            td_ref[hh, NA_MASKED] = jnp.full((GRID_W, 2 * GRID_W), NEG_BIG, F32)

        slots = _na_bias_slots()
        left = lax.broadcasted_iota(jnp.int32, (GRID_W, 2 * GRID_W), 1) < GRID_W
        for case in range(len(NA_CASES)):
            for half in range(2):
                for hh in range(2 * NA_PAIRS):
                    dst = (case * 2 + half) * 2 * NA_PAIRS + hh
                    for i in range(NA_HROWS):
                        for p in range(NA_WROWS // 2):
                            sl = int(slots[case, half, i, 2 * p])
                            sr = int(slots[case, half, i, 2 * p + 1])
                            tile = td_ref[hh, sl] if sl == sr else jnp.where(left, td_ref[hh, sl], td_ref[hh, sr])
                            bias_ref[dst, i * GRID_W:(i + 1) * GRID_W, p * 2 * GRID_W:(p + 1) * 2 * GRID_W] = tile

    case = jnp.where(j == 0, 0, jnp.where(j == NA_ROW_BLOCKS - 1, 2, 1))
    lane = lax.broadcasted_iota(jnp.int32, (NA_HQ, 2 * NA_HEAD_DIM), 1)
    kv_refs = ((ka, va), (kb, vb))
    items = [(half, pair, hh) for half in range(2) for pair in range(NA_PAIRS) for hh in range(2)]
    pair_lanes = lambda pair: slice(pair * 2 * NA_HEAD_DIM, (pair + 1) * 2 * NA_HEAD_DIM)

    def scores(half, pair, hh):
        lanes = pair_lanes(pair)
        q = q_ref[half * NA_HQ:(half + 1) * NA_HQ, lanes]
        qm = jnp.where((lane // NA_HEAD_DIM) == hh, q, jnp.zeros_like(q))
        k = jnp.concatenate([r[:, lanes] for r in kv_refs[half][0]], axis=0)
        bias = bias_ref[(case * 2 + half) * 2 * NA_PAIRS + 2 * pair + hh]
        return _dot_nt(qm, k) + bias, _dot_nt(qm, kc_ref[:, lanes])

    def softmax(s_loc, s_ctx):
        m = jnp.maximum(jnp.max(s_loc, axis=-1, keepdims=True), jnp.max(s_ctx, axis=-1, keepdims=True))
        p_loc = jnp.exp2(s_loc - m)
        p_ctx = jnp.exp2(s_ctx - m)
        denom = jnp.sum(p_loc, axis=-1, keepdims=True) + jnp.sum(p_ctx, axis=-1, keepdims=True)
        return p_loc.astype(BF16), p_ctx.astype(BF16), denom

    def weighted(half, pair, p_loc, p_ctx, denom):
        lanes = pair_lanes(pair)
        v = jnp.concatenate([r[:, lanes] for r in kv_refs[half][1]], axis=0)
        return (_dot(p_loc, v) + _dot(p_ctx, vc_ref[:, lanes])) / denom

    s, p, o = {}, {}, {}
    for t in range(len(items) + 2):
        if t < len(items):
            s[t] = scores(*items[t])
        if 0 <= t - 1 < len(items):
            p[t - 1] = softmax(*s.pop(t - 1))
        if 0 <= t - 2 < len(items):
            o[t - 2] = weighted(*items[t - 2][:2], *p.pop(t - 2))
    first = (lane // NA_HEAD_DIM) == 0
    for n, (half, pair, hh) in enumerate(items):
        if hh == 0:
            out = jnp.where(first, o[n], o[n + 1])
            o_ref[half * NA_HQ:(half + 1) * NA_HQ, pair_lanes(pair)] = out.astype(o_ref.dtype)


def _na(qkv, bias_rows, cast_ws=()):
    hp_n = NA_HEADS // (2 * NA_PAIRS)
    jn = NA_ROW_BLOCKS
    kb_per_img = SEQ // NA_KBLK
    last_blk = kb_per_img - NA_WBLKS

    def kv_spec(col0, half, t):
        return pl.BlockSpec((NA_KBLK, NA_LANES), lambda hp, j, b: (
            b * kb_per_img + jnp.clip(2 * j + half - 1, 0, last_blk) + t, col0 + hp))

    def kv_specs(col0):
        return [kv_spec(col0, half, t) for half in range(2) for t in range(NA_WBLKS)]

    ctx0 = N_LAT // CTX_LEN
    in_specs = ([pl.BlockSpec((NA_TQ, NA_LANES), lambda hp, j, b: (b * jn + j, hp))]
                + kv_specs(hp_n) + kv_specs(2 * hp_n)
                + [pl.BlockSpec((CTX_LEN, NA_LANES), lambda hp, j, b: (ctx0 + b, hp_n + hp)),
                   pl.BlockSpec((CTX_LEN, NA_LANES), lambda hp, j, b: (ctx0 + b, 2 * hp_n + hp)),
                   pl.BlockSpec((2 * NA_PAIRS, 2 * NA_KH, 2 * GRID_W), lambda hp, j, b: (hp, 0, 0))])
    flat = lambda hp, j, b: (hp * jn + j) * BATCH + b
    casts = [_cast_specs(w, hp_n * jn * BATCH, flat) for w in cast_ws]
    return pl.pallas_call(
        _hosting_casts(_na_kernel, len(in_specs), 1, [u for _, _, u in casts], flat, 3),
        grid=(hp_n, jn, BATCH),
        in_specs=in_specs + [s for s, _, _ in casts],
        out_specs=[pl.BlockSpec((NA_TQ, NA_LANES), lambda hp, j, b: (b * jn + j, hp))] + [s for s, _, _ in casts],
        out_shape=[jax.ShapeDtypeStruct((N_LAT, D_MODEL), BF16)] + [o for _, o, _ in casts],
        scratch_shapes=[pltpu.VMEM((len(NA_CASES) * 4 * NA_PAIRS, NA_HQ, NA_WK), F32),
                        pltpu.VMEM((2 * NA_PAIRS, 2 * NA_KH, GRID_W, 2 * GRID_W), F32)],
        compiler_params=_cparams(("arbitrary", "arbitrary", "arbitrary"), 48),
        name="neighbourhood_attention",
    )(qkv, *([qkv] * (4 * NA_WBLKS)), qkv, qkv, bias_rows, *cast_ws)


def _na_bias_rows(rpb):
    r = rpb.astype(F32) * LOG2_E
    r = jnp.pad(r, ((0, 0), (0, 2 * NA_KH - r.shape[1]), (0, GRID_W - r.shape[2])))
    return jnp.concatenate([r, r], axis=-1)


MR_TM = 1024
MR_TN = 1024


def _mmres_kernel(a_ref, w_ref, x_ref, gt_ref, o_ref):
    o_ref[...] = x_ref[...] + gt_ref[0] * _dot(a_ref[...], w_ref[...])


def _mm_res(a, w, xs, mod, n_rows):
    tm, tn = MR_TM, MR_TN
    return pl.pallas_call(
        _mmres_kernel,
        grid=(n_rows // tm, D_MODEL // tn),
        in_specs=[pl.BlockSpec((tm, D_MODEL), lambda i, j: (i, 0)),
                  pl.BlockSpec((D_MODEL, tn), lambda i, j: (0, j)),
                  pl.BlockSpec((tm, tn), lambda i, j: (i, j)),
                  _mod_spec(5, tm, tn, col=True)],
        out_specs=pl.BlockSpec((tm, tn), lambda i, j: (i, j)),
        out_shape=jax.ShapeDtypeStruct((n_rows, D_MODEL), F32),
        compiler_params=_cparams(("parallel", "arbitrary"), 40),
        name="matmul_gated_residual",
    )(a, w, xs, mod)


def _rope_tables():
    half = GLA_DK // 4
    inv_freq = ROPE_BASE ** (-jnp.arange(half, dtype=F32) / half)
    t = jnp.arange(SEQ)
    ang_r = (t // GRID_W).astype(F32)[:, None] * inv_freq[None, :]
    ang_c = (t % GRID_W).astype(F32)[:, None] * inv_freq[None, :]
    cr, sr, cc, sc = jnp.cos(ang_r), jnp.sin(ang_r), jnp.cos(ang_c), jnp.sin(ang_c)
    cos_t = jnp.concatenate([cr, cc, cr, cc], axis=-1)
    sin_t = jnp.concatenate([-sr, -sc, sr, sc], axis=-1)
    cos_t = jnp.concatenate([cos_t, jnp.ones((GLA_BLOCK, GLA_DK), F32)], axis=0)
    sin_t = jnp.concatenate([sin_t, jnp.zeros((GLA_BLOCK, GLA_DK), F32)], axis=0)
    return cos_t, sin_t


def _pair_layout(w):
    lead = w.shape[:-1]
    quarter = GLA_DK // 4
    w = w.reshape(*lead, GLA_HEADS, 2, 2, quarter)
    return jnp.swapaxes(w, -2, -3).reshape(*lead, GLA_QK_W)


def kernel(x, c, ctx, c_ctx, l0_ada_w, l0_ada_b, l0_norm_ffn1, l0_norm_mix, l0_norm_ffn2, l0_ffn1_w_gu, l0_ffn1_w_down, l0_ffn2_w_gu, l0_ffn2_w_down, l0_w_in, l0_gla_gate_w_fwd, l0_gla_gate_b_fwd, l0_gla_gate_w_bwd, l0_gla_gate_b_bwd, l0_gla_norm, l0_w_out, l1_ada_w, l1_ada_b, l1_norm_ffn1, l1_norm_mix, l1_norm_ffn2, l1_ffn1_w_gu, l1_ffn1_w_down, l1_ffn2_w_gu, l1_ffn2_w_down, l1_w_qkv, l1_rpb, l1_w_out, norm_out):
    bf = lambda w: w.astype(BF16)
    cc = jnp.concatenate([c, c_ctx[None, :], jnp.zeros((8 - BATCH - 1, D_MODEL), F32)], axis=0)

    mod0 = _ada(cc, l0_ada_w, l0_ada_b)
    w_gu, w_down = bf(l0_ffn1_w_gu), bf(l0_ffn1_w_down)
    xs = _ffn(x.reshape(N_LAT, D_MODEL), mod0, 0, l0_norm_ffn1, w_gu, w_down, out_rows=N_TOK)
    xs = _ffn(ctx.reshape(N_CTX, D_MODEL), mod0, 0, l0_norm_ffn1, w_gu, w_down, out_rows=N_TOK,
              tile0=N_LAT // FFN_TM, prev=xs)

    a0 = 2 * GLA_QK_W + GLA_V_W
    a1 = a0 + 2 * GLA_LOWRANK
    w_in = bf(l0_w_in)
    w_main = jnp.concatenate([_pair_layout(w_in[:, :GLA_QK_W]), _pair_layout(w_in[:, GLA_QK_W:2 * GLA_QK_W]),
                              w_in[:, 2 * GLA_QK_W:a0], w_in[:, a1:]], axis=1)
    w_lr = jnp.pad(w_in[:, a0:a1], ((0, 0), (0, GLA_LR_PAD - 2 * GLA_LOWRANK)))
    cc_np, sc_np = _dft_tables(FNET_GROUP_DIM)
    wc = jnp.asarray(np.concatenate([cc_np, sc_np], axis=1), F32).astype(BF16)
    pf, pb, zc, zs, lr = _nmm(xs, mod0, 1, l0_norm_mix, w_main, w_narrow=w_lr, w_chan=wc)

    cos_t, sin_t = _rope_tables()
    gwf = jnp.pad(_pair_layout(l0_gla_gate_w_fwd), ((0, GLA_LR_PAD - GLA_LOWRANK), (0, 0)))
    gwb = jnp.pad(_pair_layout(l0_gla_gate_w_bwd), ((GLA_LOWRANK, GLA_LR_PAD - 2 * GLA_LOWRANK), (0, 0)))
    of, ob, w_gu2, w_down2 = _gla(pf, pb, lr, cos_t, sin_t, gwf, _pair_layout(l0_gla_gate_b_fwd).reshape(1, -1),
                                  gwb, _pair_layout(l0_gla_gate_b_bwd).reshape(1, -1),
                                  cast_ws=(l0_ffn2_w_gu, l0_ffn2_w_down))

    cos_big, sin_big = _big_dft_tables()
    fm, w_gu3, w_down3 = _time_dft_folded(cos_big, sin_big, zc, zs, 512, 512, cast_ws=(l1_ffn1_w_gu, l1_ffn1_w_down))
    ct_np, st_np = _dft_tables(CTX_LEN)
    fm = _time_dft(jnp.asarray(ct_np, F32).astype(BF16), jnp.asarray(st_np, F32).astype(BF16),
                   zc, zs, CTX_LEN, CTX_LEN, 512, N_LAT, prev=fm)[0]

    xs = _glaout(of, ob, pf, fm, l0_gla_norm, xs, mod0, bf(l0_w_out))
    xs = _ffn(xs, mod0, 2, l0_norm_ffn2, w_gu2, w_down2, out_rows=N_TOK)

    mod1 = _ada(cc, l1_ada_w, l1_ada_b)
    xs = _ffn(xs, mod1, 0, l1_norm_ffn1, w_gu3, w_down3, out_rows=N_TOK)
    w_qkv = bf(jnp.concatenate([l1_w_qkv[:, :D_MODEL] * NA_Q_SCALE, l1_w_qkv[:, D_MODEL:]], axis=1))
    qkv, w_gu4, w_down4 = _nmm(xs, mod1, 1, l1_norm_mix, w_qkv, cast_ws=(l1_ffn2_w_gu, l1_ffn2_w_down))
    ao = _na(qkv, _na_bias_rows(l1_rpb))[0]
    xl = _mm_res(ao, bf(l1_w_out), xs, mod1, N_LAT)
    out = _ffn(xl, mod1, 2, l1_norm_ffn2, w_gu4, w_down4, out_rows=N_LAT, final_g=norm_out)
    return out.reshape(BATCH, SEQ, D_MODEL)
```

```python
import functools

import numpy as np
import jax
import jax.numpy as jnp
from jax import lax
from jax.experimental import pallas as pl
from jax.experimental.pallas import tpu as pltpu

D_MODEL = 2048
BATCH = 4
SEQ = 4096
GRID_W = 64
CTX_LEN = 256
N_MOD = 9
D_FF = 5632
RMS_EPS = 1e-6
ROPE_BASE = 10000.0

GLA_HEADS = 4
GLA_DK = 128
GLA_DV = 256
GLA_LOWRANK = 16
GLA_TAU = 16.0
GLA_CHUNK = 64
FNET_GROUPS = 4
FNET_GROUP_DIM = 256
GLA_QK_W = GLA_HEADS * GLA_DK
GLA_V_W = GLA_HEADS * GLA_DV
FNET_W = FNET_GROUPS * FNET_GROUP_DIM

NA_HEADS = 32
NA_HEAD_DIM = 64
NA_KH = 8
NA_KW = 16
NA_QROWS = 8
NEG_BIG = -1e30
LOG2_E = 1.4426950408889634
NA_Q_SCALE = NA_HEAD_DIM ** -0.5 * LOG2_E

N_LAT = BATCH * SEQ
N_CTX = BATCH * CTX_LEN
N_TOK = N_LAT + N_CTX

F32 = jnp.float32
BF16 = jnp.bfloat16
MIB = 1024 * 1024


def _cparams(sem, vmem_mib):
    return pltpu.CompilerParams(dimension_semantics=sem, vmem_limit_bytes=vmem_mib * MIB)


def _dot(a, b):
    return jnp.dot(a, b, preferred_element_type=F32)


def _dot_nt(a, b):
    return lax.dot_general(a, b, (((1,), (1,)), ((), ())), preferred_element_type=F32)


def _dot_tn(a, b):
    return lax.dot_general(a, b, (((0,), (0,)), ((), ())), preferred_element_type=F32)


def _split2(x):
    hi = x.astype(BF16)
    lo = (x - hi.astype(F32)).astype(BF16)
    return hi, lo


def _dot3(a, b):
    ah, al = _split2(a)
    bh, bl = _split2(b)
    return _dot(ah, bh) + (_dot(al, bh) + _dot(ah, bl))


def _dot_exact_lhs(l_bf16, x):
    x1 = x.astype(BF16)
    r1 = x - x1.astype(F32)
    x2 = r1.astype(BF16)
    x3 = (r1 - x2.astype(F32)).astype(BF16)
    return _dot(l_bf16, x1) + (_dot(l_bf16, x2) + _dot(l_bf16, x3))


def _sigmoid(x):
    return 1.0 / (1.0 + jnp.exp(-x))


ROW_CHUNK = 16
ROW_GROUP = 8


def _row_groups(n_rows, body, group=ROW_GROUP):
    span = ROW_CHUNK * group

    def step(t, carry):
        base = t * span
        body([pl.ds(pl.multiple_of(base + c * ROW_CHUNK, ROW_CHUNK), ROW_CHUNK) for c in range(group)])
        return carry
    lax.fori_loop(0, n_rows // span, step, 0)


def _inv_rms(x):
    return lax.rsqrt(jnp.mean(x * x, axis=-1, keepdims=True) + RMS_EPS)


def _norm_mod_rows(x_ref, g_ref, sh_ref, sc_ref, h_ref, gs_ref):
    d = x_ref.shape[1]
    gs_ref[0] = jnp.broadcast_to(g_ref[...] * (1.0 + sc_ref[0]), (ROW_CHUNK, d))
    gs_ref[1] = jnp.broadcast_to(sh_ref[0], (ROW_CHUNK, d))

    def body(chunks):
        inv = [_inv_rms(x_ref[rows, :]) for rows in chunks]
        for rows, r in zip(chunks, inv):
            h_ref[rows, :] = (x_ref[rows, :] * r * gs_ref[0] + gs_ref[1]).astype(BF16)
    _row_groups(x_ref.shape[0], body)


def _mod_row(i, tm):
    return jnp.minimum(i // (SEQ // tm), BATCH)


def _mod_spec(j, tm, width=D_MODEL, col=None, tile0=0):
    if col is None:
        return pl.BlockSpec((1, 1, width), lambda i, *_: (_mod_row(i + tile0, tm) * N_MOD + j, 0, 0))
    return pl.BlockSpec((1, 1, width), lambda i, n, *_: (_mod_row(i + tile0, tm) * N_MOD + j, 0, n))


CAST_ROW_ALIGN = 16


def _cast_plan(rows, steps):
    per = pl.cdiv(pl.cdiv(rows, steps), CAST_ROW_ALIGN) * CAST_ROW_ALIGN
    while rows % per:
        per += CAST_ROW_ALIGN
    return per, rows // per


def _cast_specs(w, steps, flat_index):
    per, used = _cast_plan(w.shape[0], steps)
    spec = pl.BlockSpec((per, w.shape[1]), lambda *g: (jnp.minimum(flat_index(*g), used - 1), 0))
    return spec, jax.ShapeDtypeStruct(w.shape, BF16), used


def _hosting_casts(kernel_fn, n_in, n_out, used, flat_index, grid_rank):
    nc = len(used)

    def wrapped(*refs):
        a, b, c = n_in + nc, n_in + nc + n_out, n_in + nc + n_out + nc
        kernel_fn(*refs[:n_in], *refs[a:b], *refs[c:])
        t = flat_index(*[pl.program_id(d) for d in range(grid_rank)])
        for n_used, src_ref, dst_ref in zip(used, refs[n_in:a], refs[b:c]):
            @pl.when(t < n_used)
            def _(src_ref=src_ref, dst_ref=dst_ref):
                dst_ref[...] = src_ref[...].astype(BF16)
    return wrapped


ADA_TN = 1024


def _ada_kernel(c_ref, w_ref, b_ref, o_ref):
    c = c_ref[...]
    s = (c * _sigmoid(c)).astype(BF16)
    o_ref[...] = _dot(s, w_ref[...].astype(BF16)) + b_ref[...]


def _ada(cc, w, b):
    n = w.shape[1]
    out = pl.pallas_call(
        _ada_kernel,
        grid=(n // ADA_TN,),
        in_specs=[pl.BlockSpec((8, D_MODEL), lambda j: (0, 0)),
                  pl.BlockSpec((D_MODEL, ADA_TN), lambda j: (0, j)),
                  pl.BlockSpec((1, ADA_TN), lambda j: (0, j))],
        out_specs=pl.BlockSpec((8, ADA_TN), lambda j: (0, j)),
        out_shape=jax.ShapeDtypeStruct((8, n), F32),
        compiler_params=_cparams(("arbitrary",), 40),
        name="ada_mod",
    )(cc, w, b.reshape(1, n))
    return out.reshape(8 * N_MOD, 1, D_MODEL)


FFN_TM = 1024
FFN_TF = 512
FFN_SLAB = 128


def _ffn_kernel(x_hbm, xn_ref, g_ref, sh_ref, sc_ref, shn_ref, scn_ref, gt_ref, wg_ref, wu_ref, wd_ref, *rest,
                final_norm):
    go_ref = rest[0] if final_norm else None
    o_ref, h0_ref, h1_ref, gs_ref, x_ref, x_sem = rest[-6:]
    i = pl.program_id(0)
    f = pl.program_id(1)
    last = pl.num_programs(1) - 1
    tm, d = x_ref.shape

    def x_copy():
        return pltpu.make_async_copy(x_hbm.at[pl.ds(pl.multiple_of(i * tm, tm), tm), :], x_ref, x_sem)

    @pl.when(f == 0)
    def _():
        x_copy().start()

    @pl.when((i == 0) & (f == 0))
    def _():
        x_copy().wait()
        _norm_mod_rows(x_ref, g_ref, sh_ref, sc_ref, h0_ref, gs_ref)

    def step(h_ref, hn_ref, first):
        gs_ref[0] = jnp.broadcast_to(g_ref[...] * (1.0 + scn_ref[0]), (ROW_CHUNK, d))
        gs_ref[1] = jnp.broadcast_to(shn_ref[0], (ROW_CHUNK, d))
        slab0 = jnp.minimum(f, tm // FFN_SLAB - 1) * FFN_SLAB
        chunks = [pl.ds(k * ROW_CHUNK, ROW_CHUNK) for k in range(FFN_SLAB // ROW_CHUNK)]
        inv = [_inv_rms(xn_ref[rows, :]) for rows in chunks]
        for k, (rows, r) in enumerate(zip(chunks, inv)):
            dst = pl.ds(pl.multiple_of(slab0 + k * ROW_CHUNK, ROW_CHUNK), ROW_CHUNK)
            hn_ref[dst, :] = (xn_ref[rows, :] * r * gs_ref[0] + gs_ref[1]).astype(BF16)

        h = h_ref[...]
        a = _dot(h, wg_ref[...])
        u = _dot(h, wu_ref[...])
        act = (a * _sigmoid(a) * u).astype(BF16)
        if first:
            o_ref[...] = _dot(act, wd_ref[...])
        else:
            o_ref[...] += _dot(act, wd_ref[...])

    for parity, (h_ref, hn_ref) in enumerate(((h0_ref, h1_ref), (h1_ref, h0_ref))):
        for first in (True, False):
            @pl.when((i % 2 == parity) & ((f == 0) == first))
            def _(h_ref=h_ref, hn_ref=hn_ref, first=first):
                step(h_ref, hn_ref, first)

    @pl.when(f == last)
    def _():
        @pl.when(i > 0)
        def _():
            x_copy().wait()

        if not final_norm:
            o_ref[...] = x_ref[...] + (0.5 * gt_ref[0]) * o_ref[...]
        else:
            gs_ref[2] = jnp.broadcast_to(0.5 * gt_ref[0], (ROW_CHUNK, d))
            gs_ref[3] = jnp.broadcast_to(go_ref[...], (ROW_CHUNK, d))

            def body(chunks):
                resid = lambda rows: x_ref[rows, :] + gs_ref[2] * o_ref[rows, :]
                inv = [_inv_rms(resid(rows)) for rows in chunks]
                for rows, r in zip(chunks, inv):
                    o_ref[rows, :] = resid(rows) * r * gs_ref[3]
            _row_groups(tm, body)


def _ffn(x_in, mod, sub, norm_g, w_gu, w_down, *, out_rows, tile0=0, prev=None, final_g=None):
    tm, tf = FFN_TM, FFN_TF
    nf = D_FF // tf
    n_tiles = x_in.shape[0] // tm
    slabs = tm // FFN_SLAB
    assert slabs <= nf
    nxt = lambda i: jnp.minimum(i + 1, n_tiles - 1)
    in_specs = [pl.BlockSpec(memory_space=pl.ANY),
                pl.BlockSpec((FFN_SLAB, D_MODEL), lambda i, f: (nxt(i) * slabs + jnp.minimum(f, slabs - 1), 0)),
                pl.BlockSpec((1, D_MODEL), lambda i, f: (0, 0)),
                _mod_spec(3 * sub, tm, tile0=tile0), _mod_spec(3 * sub + 1, tm, tile0=tile0),
                pl.BlockSpec((1, 1, D_MODEL), lambda i, f: (_mod_row(nxt(i) + tile0, tm) * N_MOD + 3 * sub, 0, 0)),
                pl.BlockSpec((1, 1, D_MODEL), lambda i, f: (_mod_row(nxt(i) + tile0, tm) * N_MOD + 3 * sub + 1, 0, 0)),
                _mod_spec(3 * sub + 2, tm, tile0=tile0),
                pl.BlockSpec((D_MODEL, tf), lambda i, f: (0, f)),
                pl.BlockSpec((D_MODEL, tf), lambda i, f: (0, f + nf)),
                pl.BlockSpec((tf, D_MODEL), lambda i, f: (f, 0))]
    args = [x_in, x_in, norm_g.reshape(1, D_MODEL), mod, mod, mod, mod, mod, w_gu, w_gu, w_down]
    if final_g is not None:
        in_specs.append(pl.BlockSpec((1, D_MODEL), lambda i, f: (0, 0)))
        args.append(final_g.reshape(1, D_MODEL))
    aliases = {}
    if prev is not None:
        aliases = {len(args): 0}
        in_specs.append(pl.BlockSpec(memory_space=pl.ANY))
        args.append(prev)
    return pl.pallas_call(
        functools.partial(_ffn_kernel, final_norm=final_g is not None),
        grid=(n_tiles, nf),
        in_specs=in_specs,
        out_specs=pl.BlockSpec((tm, D_MODEL), lambda i, f: (i + tile0, 0)),
        out_shape=jax.ShapeDtypeStruct((out_rows, D_MODEL), F32),
        scratch_shapes=[pltpu.VMEM((tm, D_MODEL), BF16), pltpu.VMEM((tm, D_MODEL), BF16),
                        pltpu.VMEM((4, ROW_CHUNK, D_MODEL), F32), pltpu.VMEM((tm, D_MODEL), F32),
                        pltpu.SemaphoreType.DMA(())],
        input_output_aliases=aliases,
        compiler_params=_cparams(("arbitrary", "arbitrary"), 56),
        name="ffn_half",
    )(*args)


NMM_TM = 1024
NMM_TN = 1024


def _nmm_kernel(x_ref, g_ref, sh_ref, sc_ref, w_ref, *rest, gla):
    if gla:
        wn_ref, wc_ref, o_ref, ob_ref, zc_ref, zs_ref, on_ref, h_ref, gs_ref = rest
    else:
        o_ref, h_ref, gs_ref = rest
    j = pl.program_id(1)

    @pl.when(j == 0)
    def _():
        _norm_mod_rows(x_ref, g_ref, sh_ref, sc_ref, h_ref, gs_ref)
        if gla:
            on_ref[...] = _dot(h_ref[...], wn_ref[...])

    if not gla:
        o_ref[...] = _dot(h_ref[...], w_ref[...]).astype(o_ref.dtype)
    else:
        @pl.when(j % 2 == 0)
        def _():
            o_ref[...] = _dot(h_ref[...], w_ref[...])

        @pl.when(j == 1)
        def _():
            ob_ref[...] = _dot(h_ref[...], w_ref[...]).astype(BF16)

        @pl.when(j == 3)
        def _():
            f = _dot(h_ref[...], w_ref[...]).astype(BF16)
            for grp in range(FNET_GROUPS):
                cols = slice(grp * FNET_GROUP_DIM, (grp + 1) * FNET_GROUP_DIM)
                z = _dot(f[:, cols], wc_ref[...])
                zc_ref[:, cols] = z[:, :FNET_GROUP_DIM].astype(BF16)
                zs_ref[:, cols] = z[:, FNET_GROUP_DIM:].astype(BF16)


def _nmm(xs, mod, sub, norm_g, w, w_narrow=None, w_chan=None, cast_ws=()):
    tm, tn = NMM_TM, NMM_TN
    n = w.shape[1]
    gla = w_narrow is not None
    in_specs = [pl.BlockSpec((tm, D_MODEL), lambda i, j: (i, 0)),
                pl.BlockSpec((1, D_MODEL), lambda i, j: (0, 0)),
                _mod_spec(3 * sub, tm), _mod_spec(3 * sub + 1, tm),
                pl.BlockSpec((D_MODEL, tn), lambda i, j: (0, j))]
    args = [xs, norm_g.reshape(1, D_MODEL), mod, mod, w]
    if gla:
        nn = w_narrow.shape[1]
        assert n == 4 * tn and tn == GLA_V_W == FNET_W
        in_specs += [pl.BlockSpec((D_MODEL, nn), lambda i, j: (0, 0)),
                     pl.BlockSpec(w_chan.shape, lambda i, j: (0, 0))]
        args += [w_narrow, w_chan]
        row_tile = pl.BlockSpec((tm, tn), lambda i, j: (i, 0))
        out_specs = [pl.BlockSpec((tm, tn), lambda i, j: (i, j // 2)), row_tile, row_tile, row_tile,
                     pl.BlockSpec((tm, nn), lambda i, j: (i, 0))]
        out_shape = [jax.ShapeDtypeStruct((N_TOK, n // 2), F32)] + [jax.ShapeDtypeStruct((N_TOK, tn), BF16)] * 3 + [
            jax.ShapeDtypeStruct((N_TOK, nn), F32)]
    else:
        out_specs = [pl.BlockSpec((tm, tn), lambda i, j: (i, j))]
        out_shape = [jax.ShapeDtypeStruct((N_TOK, n), BF16)]
    nj = n // tn
    flat = lambda i, j: i * nj + j
    casts = [_cast_specs(cw, (N_TOK // tm) * nj, flat) for cw in cast_ws]
    return pl.pallas_call(
        _hosting_casts(functools.partial(_nmm_kernel, gla=gla), len(args), len(out_specs),
                       [u for _, _, u in casts], flat, 2),
        grid=(N_TOK // tm, nj),
        in_specs=in_specs + [s for s, _, _ in casts],
        out_specs=out_specs + [s for s, _, _ in casts],
        out_shape=out_shape + [o for _, o, _ in casts],
        scratch_shapes=[pltpu.VMEM((tm, D_MODEL), BF16), pltpu.VMEM((2, ROW_CHUNK, D_MODEL), F32)],
        compiler_params=_cparams(("arbitrary", "arbitrary"), 56),
        name="norm_mod_matmul",
    )(*args, *cast_ws)


GLA_BLOCK = CTX_LEN
GLA_LAT_BLOCKS = SEQ // GLA_BLOCK
GLA_STEPS = 1 + GLA_LAT_BLOCKS
GLA_LR_PAD = 128
GLA_SPLIT_RANGE = 80.0


def _rope(x, cos, sin_signed):
    return x * cos + pltpu.roll(x, GLA_DK // 2, 1) * sin_signed


def _gla_kernel(qf, kf, vf, lf, cf, sf, qb, kb, vb, lb, cb, sb, lfn, lbn, gwf, gbf, gwb, gbb,
                of_ref, ob_ref, st_ref, row_ref, bc_ref, md_ref):

    c = GLA_CHUNK
    n = GLA_BLOCK
    row = lax.broadcasted_iota(jnp.int32, (n, n), 0)
    col = lax.broadcasted_iota(jnp.int32, (n, n), 1)
    same_chunk = (row // c) == (col // c)
    row_c = lax.broadcasted_iota(jnp.int32, (c, c), 0)
    col_c = lax.broadcasted_iota(jnp.int32, (c, c), 1)
    scale = GLA_DK ** -0.5

    dirs = ((qf, kf, vf, lf, cf, sf, gwf, gbf, of_ref), (qb, kb, vb, lb, cb, sb, gwb, gbb, ob_ref))
    n_chunks = n // c
    heads = [(slice(h * GLA_DK, (h + 1) * GLA_DK), slice(h * GLA_DV, (h + 1) * GLA_DV)) for h in range(GLA_HEADS)]
    tris = [row_c >= col_c, row_c <= col_c]
    last_row = (c - 1, 0)

    def gate_decays(d, l_ref):
        causal = (row >= col) if d == 0 else (row <= col)
        tri_b = jnp.where(same_chunk & causal, 1.0, 0.0).astype(BF16)
        x = _dot3(l_ref[...], dirs[d][6][...]) + dirs[d][7][...]
        g = (jnp.minimum(x, 0.0) - jnp.log1p(jnp.exp(-jnp.abs(x)))) * (1.0 / GLA_TAU)
        bc_all = _dot_exact_lhs(tri_b, g)
        totals = [-bc_all[ci * c + last_row[d]:ci * c + last_row[d] + 1, :] for ci in range(n_chunks)]
        return bc_all, functools.reduce(jnp.maximum, totals)

    def prepare_decays(l_refs):
        both = [gate_decays(d, l_ref) for d, l_ref in enumerate(l_refs)]

        def store():
            bc_ref[0] = both[0][0]
            bc_ref[1] = both[1][0]
            md_ref[...] = jnp.maximum(both[0][1], both[1][1])
        return store

    @pl.when(pl.program_id(1) == 0)
    def _():
        st_ref[...] = jnp.zeros_like(st_ref)
        prepare_decays((lf, lb))()

    decays = []
    for d in range(2):
        bc_all = bc_ref[d]
        per_chunk = []
        for ci in (range(n_chunks) if d == 0 else range(n_chunks - 1, -1, -1)):
            rows = slice(ci * c, (ci + 1) * c)
            bc = bc_all[rows, :]
            per_chunk.append((rows, bc, bc[c // 2:c // 2 + 1, :], bc[last_row[d]:last_row[d] + 1, :]))
        decays.append(per_chunk)
    max_decay = md_ref[...]

    def scores_by_rows(d, qh, kh, bc):
        row_ref[0], row_ref[1] = bc, qh
        s_idx = lax.broadcasted_iota(jnp.int32, (c, 1), 0)
        t_idx = lax.broadcasted_iota(jnp.int32, (c, c), 1)

        def one_row(t, att_t):
            valid = (s_idx <= t) if d == 0 else (s_idx >= t)
            w = jnp.exp(jnp.where(valid, row_ref[0, pl.ds(t, 1), :] - bc, -jnp.inf))
            col = jnp.sum(kh * w * row_ref[1, pl.ds(t, 1), :], axis=-1, keepdims=True)
            return jnp.where(t_idx == t, col, att_t)
        return lax.fori_loop(0, c, one_row, jnp.zeros((c, c), F32))

    def local_part(k, exact):
        out = {}
        for d, (q_ref, k_ref, v_ref, _, c_ref, s_ref, _, _, _) in enumerate(dirs):
            rows, bc, mid, last = decays[d][k]
            e_q, e_kl, e_l = jnp.exp(bc), jnp.exp(last - bc), jnp.exp(last)
            if not exact:
                e_qm, e_km = jnp.exp(bc - mid), jnp.exp(mid - bc)
            cos, sin = c_ref[rows, :], s_ref[rows, :]
            for h, (sk, sv) in enumerate(heads):
                qh = _rope(q_ref[rows, sk] * scale, cos, sin)
                kh = _rope(k_ref[rows, sk], cos, sin)
                vh = v_ref[rows, sv].astype(BF16)
                kv = _dot_tn(vh, (kh * e_kl[:, sk]).astype(BF16))
                if exact:
                    att_t = scores_by_rows(d, qh, kh, bc[:, sk]).astype(BF16)
                    mix = lambda att_t=att_t, vh=vh: _dot_tn(att_t, vh)
                else:
                    att = _dot_nt((qh * e_qm[:, sk]).astype(BF16), (kh * e_km[:, sk]).astype(BF16))
                    att = jnp.where(tris[d], att, 0.0).astype(BF16)
                    mix = lambda att=att, vh=vh: _dot(att, vh)
                out[d, h] = ((qh * e_q[:, sk]).astype(BF16), mix, kv, e_l[:, sk])
        return out

    def scan_block(exact):
        hand_over = prepare_decays((lfn, lbn))
        st = {(d, h): st_ref[d, h] for d in range(2) for h in range(GLA_HEADS)}
        local = local_part(0, exact)
        for k in range(n_chunks):
            nxt = local_part(k + 1, exact) if k + 1 < n_chunks else None
            for d in range(2):
                rows = decays[d][k][0]
                for h, (sk, sv) in enumerate(heads):
                    qe, mix, kv, e_l = local[d, h]
                    dirs[d][-1][rows, sv] = _dot_nt(qe, st[d, h].astype(BF16)) + mix()
                    st[d, h] = st[d, h] * e_l + kv
            local = nxt
        for (d, h), val in st.items():
            st_ref[d, h] = val
        hand_over()

    in_range = jnp.max(max_decay) <= GLA_SPLIT_RANGE

    @pl.when(in_range)
    def _():
        scan_block(exact=False)

    @pl.when(jnp.logical_not(in_range))
    def _():
        scan_block(exact=True)


def _gla(pf, pb, lr, cos_t, sin_t, gwf, gbf, gwb, gbb, cast_ws):
    c = GLA_BLOCK
    ctx0 = N_LAT // c

    def fwd_blk(b, s):
        return jnp.where(s == 0, ctx0 + b, GLA_LAT_BLOCKS * b + s - 1)

    def bwd_blk(b, s):
        return jnp.where(s == 0, ctx0 + b, GLA_LAT_BLOCKS * b + (GLA_STEPS - 1 - s))

    def fwd_rope(b, s):
        return jnp.where(s == 0, GLA_LAT_BLOCKS, s - 1)

    def bwd_rope(b, s):
        return jnp.where(s == 0, GLA_LAT_BLOCKS, GLA_STEPS - 1 - s)

    def dir_specs(blk, rope):
        return [pl.BlockSpec((c, GLA_QK_W), lambda b, s: (blk(b, s), 0)),
                pl.BlockSpec((c, GLA_QK_W), lambda b, s: (blk(b, s), 1)),
                pl.BlockSpec((c, GLA_V_W), lambda b, s: (blk(b, s), 0)),
                pl.BlockSpec((c, GLA_LR_PAD), lambda b, s: (blk(b, s), 0)),
                pl.BlockSpec((c, GLA_DK), lambda b, s: (rope(b, s), 0)),
                pl.BlockSpec((c, GLA_DK), lambda b, s: (rope(b, s), 0))]

    const = lambda shape: pl.BlockSpec(shape, lambda b, s: (0, 0))
    ahead = lambda s: jnp.minimum(s + 1, GLA_STEPS - 1)
    in_specs = (dir_specs(fwd_blk, fwd_rope) + dir_specs(bwd_blk, bwd_rope)
                + [pl.BlockSpec((c, GLA_LR_PAD), lambda b, s: (fwd_blk(b, ahead(s)), 0)),
                   pl.BlockSpec((c, GLA_LR_PAD), lambda b, s: (bwd_blk(b, ahead(s)), 0))]
                + [const((GLA_LR_PAD, GLA_QK_W)), const((1, GLA_QK_W)),
                   const((GLA_LR_PAD, GLA_QK_W)), const((1, GLA_QK_W))])
    flat = lambda b, s: b * GLA_STEPS + s
    casts = [_cast_specs(w, BATCH * GLA_STEPS, flat) for w in cast_ws]
    args = [pf, pf, pb, lr, cos_t, sin_t, pf, pf, pb, lr, cos_t, sin_t, lr, lr, gwf, gbf, gwb, gbb]
    return pl.pallas_call(
        _hosting_casts(_gla_kernel, len(args), 2, [u for _, _, u in casts], flat, 2),
        grid=(BATCH, GLA_STEPS),
        in_specs=in_specs + [s for s, _, _ in casts],
        out_specs=[pl.BlockSpec((c, GLA_V_W), lambda b, s: (fwd_blk(b, s), 0)),
                   pl.BlockSpec((c, GLA_V_W), lambda b, s: (bwd_blk(b, s), 0))] + [s for s, _, _ in casts],
        out_shape=[jax.ShapeDtypeStruct((N_TOK, GLA_V_W), F32)] * 2 + [o for _, o, _ in casts],
        scratch_shapes=[pltpu.VMEM((2, GLA_HEADS, GLA_DV, GLA_DK), F32), pltpu.VMEM((2, GLA_CHUNK, GLA_DK), F32),
                        pltpu.VMEM((2, c, GLA_QK_W), F32), pltpu.VMEM((1, GLA_QK_W), F32)],
        compiler_params=_cparams(("arbitrary", "arbitrary"), 40),
        name="gla_scan",
    )(*args, *cast_ws)


def _tdft_kernel(c_ref, s_ref, zc_ref, zs_ref, *rest):
    o_ref = rest[-1]
    o_ref[...] = (_dot(c_ref[...], zc_ref[...]) - _dot(s_ref[...], zs_ref[...])).astype(o_ref.dtype)


def _time_dft(cmat, smat, zc, zs, t_len, tm, tn, row0, prev=None, cast_ws=()):
    zb0 = row0 // t_len
    ob0 = row0 // tm
    mt = t_len // tm
    in_specs = [pl.BlockSpec((tm, t_len), lambda b, n, m: (m, 0)),
                pl.BlockSpec((tm, t_len), lambda b, n, m: (m, 0)),
                pl.BlockSpec((t_len, tn), lambda b, n, m: (zb0 + b, n)),
                pl.BlockSpec((t_len, tn), lambda b, n, m: (zb0 + b, n))]
    args = [cmat, smat, zc, zs]
    aliases = {}
    if prev is not None:
        in_specs.append(pl.BlockSpec(memory_space=pl.ANY))
        args.append(prev)
        aliases = {4: 0}
    nt = FNET_W // tn
    flat = lambda b, n, m: (b * nt + n) * mt + m
    casts = [_cast_specs(w, BATCH * nt * mt, flat) for w in cast_ws]
    return pl.pallas_call(
        _hosting_casts(_tdft_kernel, len(args), 1, [u for _, _, u in casts], flat, 3),
        grid=(BATCH, nt, mt),
        in_specs=in_specs + [s for s, _, _ in casts],
        out_specs=[pl.BlockSpec((tm, tn), lambda b, n, m: (ob0 + b * mt + m, n))] + [s for s, _, _ in casts],
        out_shape=[jax.ShapeDtypeStruct((N_TOK, FNET_W), BF16)] + [o for _, o, _ in casts],
        input_output_aliases=aliases,
        compiler_params=_cparams(("arbitrary", "arbitrary", "arbitrary"), 48),
        name="fnet_time_dft",
    )(*args, *cast_ws)


FOLD_BLK = 256


def _tdft_fold_kernel(c_ref, s_ref, zc_ref, zs_ref, j1_ref, j2_ref, o_ref, zcf_ref, zsf_ref):
    half, blk = zcf_ref.shape[0], FOLD_BLK
    nblk = 2 * half // blk

    @pl.when(pl.program_id(2) == 0)
    def _():
        for kb in range(half // blk):
            rows = slice(kb * blk, (kb + 1) * blk)
            mirror = slice((nblk - 1 - kb) * blk, (nblk - kb) * blk)
            after = slice((nblk - kb) * blk, (nblk - kb + 1) * blk)
            for z_ref, zf_ref, sign in ((zc_ref, zcf_ref, 1.0), (zs_ref, zsf_ref, -1.0)):
                partner = _dot(j1_ref[...], z_ref[mirror, :])
                if kb > 0:
                    partner = partner + _dot(j2_ref[...], z_ref[after, :])
                zf_ref[rows, :] = (z_ref[rows, :].astype(F32) + sign * partner).astype(BF16)

    tm = o_ref.shape[0]
    j = pl.program_id(2) * tm + lax.broadcasted_iota(jnp.int32, (tm, 1), 0)
    sign = jnp.where(j % 2 == 0, SEQ ** -0.5, -(SEQ ** -0.5))
    mid = zc_ref[half:half + 1, :].astype(F32)
    o = _dot(c_ref[...], zcf_ref[...]) - _dot(s_ref[...], zsf_ref[...]) + sign * mid
    o_ref[...] = o.astype(o_ref.dtype)


def _time_dft_folded(cmat, smat, zc, zs, tm, tn, cast_ws=()):
    half = SEQ // 2
    mt = SEQ // tm
    nt = FNET_W // tn
    i = np.arange(FOLD_BLK)
    j1 = jnp.asarray((i[:, None] + i[None, :] == FOLD_BLK) & (i[:, None] > 0), BF16)
    j2 = jnp.asarray((i[:, None] == 0) & (i[None, :] == 0), BF16)
    args = [cmat, smat, zc, zs, j1, j2]
    in_specs = [pl.BlockSpec((tm, half), lambda b, n, m: (m, 0)),
                pl.BlockSpec((tm, half), lambda b, n, m: (m, 0)),
                pl.BlockSpec((SEQ, tn), lambda b, n, m: (b, n)),
                pl.BlockSpec((SEQ, tn), lambda b, n, m: (b, n)),
                pl.BlockSpec((FOLD_BLK, FOLD_BLK), lambda b, n, m: (0, 0)),
                pl.BlockSpec((FOLD_BLK, FOLD_BLK), lambda b, n, m: (0, 0))]
    flat = lambda b, n, m: (b * nt + n) * mt + m
    casts = [_cast_specs(w, BATCH * nt * mt, flat) for w in cast_ws]
    return pl.pallas_call(
        _hosting_casts(_tdft_fold_kernel, len(args), 1, [u for _, _, u in casts], flat, 3),
        grid=(BATCH, nt, mt),
        in_specs=in_specs + [s for s, _, _ in casts],
        out_specs=[pl.BlockSpec((tm, tn), lambda b, n, m: (b * mt + m, n))] + [s for s, _, _ in casts],
        out_shape=[jax.ShapeDtypeStruct((N_TOK, FNET_W), BF16)] + [o for _, o, _ in casts],
        scratch_shapes=[pltpu.VMEM((half, tn), BF16), pltpu.VMEM((half, tn), BF16)],
        compiler_params=_cparams(("arbitrary", "arbitrary", "arbitrary"), 48),
        name="fnet_time_dft_folded",
    )(*args, *cast_ws)


def _dft_tables(n):
    k = np.arange(n)
    ang = 2.0 * np.pi * ((k[:, None] * k[None, :]) % n) / n
    return np.cos(ang) / np.sqrt(n), np.sin(ang) / np.sqrt(n)


def _dft_table_kernel(ac_ref, as_ref, bc_ref, bs_ref, co_ref, so_ref):
    ac, as_ = ac_ref[0], as_ref[0]
    bc, bs = bc_ref[...], bs_ref[...]
    co_ref[...] = (ac * bc - as_ * bs).astype(BF16)
    so_ref[...] = (ac * bs + as_ * bc).astype(BF16)


def _big_dft_tables():
    r = GRID_W
    k = np.arange(SEQ)
    j = np.arange(r)
    pa = 2.0 * np.pi * ((j[:, None] * r * k[None, :]) % SEQ) / SEQ
    pb = 2.0 * np.pi * ((j[:, None] * k[None, :]) % SEQ) / SEQ
    ac = jnp.asarray(np.cos(pa).reshape(r, 1, SEQ), F32)
    as_ = jnp.asarray(np.sin(pa).reshape(r, 1, SEQ), F32)
    bc = jnp.asarray(np.cos(pb) * SEQ ** -0.5, F32)
    bs = jnp.asarray(np.sin(pb) * SEQ ** -0.5, F32)
    row = pl.BlockSpec((1, 1, SEQ), lambda i: (i, 0, 0))
    full = pl.BlockSpec((r, SEQ), lambda i: (0, 0))
    return pl.pallas_call(
        _dft_table_kernel,
        grid=(r,),
        in_specs=[row, row, full, full],
        out_specs=[pl.BlockSpec((r, SEQ), lambda i: (i, 0))] * 2,
        out_shape=[jax.ShapeDtypeStruct((SEQ, SEQ), BF16)] * 2,
        compiler_params=_cparams(("parallel",), 16),
        name="dft_tables",
    )(ac, as_, bc, bs)


GO_TM = 512
GO_TN = 2048


def _glaout_kernel(of_ref, ob_ref, r_ref, fm_ref, gn_ref, x_ref, gt_ref, w_ref, o_ref, h_ref):
    @pl.when(pl.program_id(1) == 0)
    def _():
        heads = [slice(h * GLA_DV, (h + 1) * GLA_DV) for h in range(GLA_HEADS)]

        def body(chunks):
            inv = [[_inv_rms(of_ref[rows, sv] + ob_ref[rows, sv]) for sv in heads] for rows in chunks]
            for rows, inv_c in zip(chunks, inv):
                for sv, r in zip(heads, inv_c):
                    o = (of_ref[rows, sv] + ob_ref[rows, sv]) * r
                    gate = r_ref[rows, sv]
                    h_ref[rows, sv] = (o * gn_ref[:, sv] * (gate * _sigmoid(gate))).astype(BF16)
                h_ref[rows, GLA_V_W:] = fm_ref[rows, :].astype(BF16)
        _row_groups(of_ref.shape[0], body, group=4)

    o_ref[...] = x_ref[...] + gt_ref[0] * _dot(h_ref[...], w_ref[...])


def _glaout(of, ob, pf, fm, g_norm, xs, mod, w_out):
    tm, tn = GO_TM, GO_TN
    half = lambda col: pl.BlockSpec((tm, GLA_V_W), lambda i, j: (i, col))
    return pl.pallas_call(
        _glaout_kernel,
        grid=(N_TOK // tm, D_MODEL // tn),
        in_specs=[half(0), half(0), half(1), half(0),
                  pl.BlockSpec((1, GLA_V_W), lambda i, j: (0, 0)),
                  pl.BlockSpec((tm, tn), lambda i, j: (i, j)),
                  _mod_spec(5, tm, tn, col=True),
                  pl.BlockSpec((D_MODEL, tn), lambda i, j: (0, j))],
        out_specs=pl.BlockSpec((tm, tn), lambda i, j: (i, j)),
        out_shape=jax.ShapeDtypeStruct((N_TOK, D_MODEL), F32),
        scratch_shapes=[pltpu.VMEM((tm, D_MODEL), BF16)],
        compiler_params=_cparams(("parallel", "arbitrary"), 56),
        name="gla_fnet_out",
    )(of, ob, pf, fm, g_norm.reshape(1, GLA_V_W), xs, mod, w_out)


NA_HROWS = NA_QROWS // 2
NA_WROWS = NA_HROWS + NA_KH
NA_TQ = NA_QROWS * GRID_W
NA_HQ = NA_HROWS * GRID_W
NA_WK = NA_WROWS * GRID_W
NA_KBLK = NA_HROWS * GRID_W
NA_WBLKS = NA_WK // NA_KBLK
NA_MASKED = 2 * NA_KH - 1
NA_ROW_BLOCKS = (SEQ // GRID_W) // NA_QROWS
NA_CASES = (0, 1, NA_ROW_BLOCKS - 1)
NA_PAIRS = 2
NA_LANES = NA_PAIRS * 2 * NA_HEAD_DIM


def _na_window_row0(j, half):
    q0 = NA_QROWS * j + NA_HROWS * half
    return int(np.clip(q0 - NA_KH // 2, 0, SEQ // GRID_W - NA_WROWS))


def _na_bias_slots():
    rows = SEQ // GRID_W
    slots = np.full((len(NA_CASES), 2, NA_HROWS, NA_WROWS), NA_MASKED, np.int32)
    for case, j in enumerate(NA_CASES):
        for half in range(2):
            w0 = _na_window_row0(j, half)
            for i in range(NA_HROWS):
                qr = NA_QROWS * j + NA_HROWS * half + i
                r0 = int(np.clip(qr - NA_KH // 2, 0, rows - NA_KH))
                for l in range(NA_WROWS):
                    kr = w0 + l
                    if r0 <= kr < r0 + NA_KH:
                        slots[case, half, i, l] = kr - qr + NA_KH - 1
    return slots


def _na_kernel(q_ref, *refs):
    nb = NA_WBLKS
    ka, kb, va, vb = refs[0:nb], refs[nb:2 * nb], refs[2 * nb:3 * nb], refs[3 * nb:4 * nb]
    kc_ref, vc_ref, rp_ref, o_ref, bias_ref, td_ref = refs[4 * nb:]
    j = pl.program_id(1)

    @pl.when((j == 0) & (pl.program_id(2) == 0))
    def _():
        qc = lax.broadcasted_iota(jnp.int32, (GRID_W, 2 * GRID_W), 0)
        kc = lax.broadcasted_iota(jnp.int32, (GRID_W, 2 * GRID_W), 1) % GRID_W
        start = jnp.clip(qc - NA_KW // 2, 0, GRID_W - NA_KW)
        window = (kc >= start) & (kc < start + NA_KW)
        for hh in range(2 * NA_PAIRS):
            for slot in range(NA_MASKED):
                td_ref[hh, slot] = jnp.broadcast_to(rp_ref[hh, slot:slot + 1, :], (GRID_W, 2 * GRID_W))
                tile = pltpu.roll(td_ref[hh, slot], 2 * GRID_W - (NA_KW - 1), 1, stride=1, stride_axis=0)
                td_ref[hh, slot] = jnp.where(window, tile, NEG_BIG)
            td_ref[hh, NA_MASKED] = jnp.full((GRID_W, 2 * GRID_W), NEG_BIG, F32)

        slots = _na_bias_slots()
        left = lax.broadcasted_iota(jnp.int32, (GRID_W, 2 * GRID_W), 1) < GRID_W
        for case in range(len(NA_CASES)):
            for half in range(2):
                for hh in range(2 * NA_PAIRS):
                    dst = (case * 2 + half) * 2 * NA_PAIRS + hh
                    for i in range(NA_HROWS):
                        for p in range(NA_WROWS // 2):
                            sl = int(slots[case, half, i, 2 * p])
                            sr = int(slots[case, half, i, 2 * p + 1])
                            tile = td_ref[hh, sl] if sl == sr else jnp.where(left, td_ref[hh, sl], td_ref[hh, sr])
                            bias_ref[dst, i * GRID_W:(i + 1) * GRID_W, p * 2 * GRID_W:(p + 1) * 2 * GRID_W] = tile

    case = jnp.where(j == 0, 0, jnp.where(j == NA_ROW_BLOCKS - 1, 2, 1))
    lane = lax.broadcasted_iota(jnp.int32, (NA_HQ, 2 * NA_HEAD_DIM), 1)
    kv_refs = ((ka, va), (kb, vb))
    items = [(half, pair, hh) for half in range(2) for pair in range(NA_PAIRS) for hh in range(2)]
    pair_lanes = lambda pair: slice(pair * 2 * NA_HEAD_DIM, (pair + 1) * 2 * NA_HEAD_DIM)

    def scores(half, pair, hh):
        lanes = pair_lanes(pair)
        q = q_ref[half * NA_HQ:(half + 1) * NA_HQ, lanes]
        qm = jnp.where((lane // NA_HEAD_DIM) == hh, q, jnp.zeros_like(q))
        k = jnp.concatenate([r[:, lanes] for r in kv_refs[half][0]], axis=0)
        bias = bias_ref[(case * 2 + half) * 2 * NA_PAIRS + 2 * pair + hh]
        return _dot_nt(qm, k) + bias, _dot_nt(qm, kc_ref[:, lanes])

    def softmax(s_loc, s_ctx):
        m = jnp.maximum(jnp.max(s_loc, axis=-1, keepdims=True), jnp.max(s_ctx, axis=-1, keepdims=True))
        p_loc = jnp.exp2(s_loc - m)
        p_ctx = jnp.exp2(s_ctx - m)
        denom = jnp.sum(p_loc, axis=-1, keepdims=True) + jnp.sum(p_ctx, axis=-1, keepdims=True)
        return p_loc.astype(BF16), p_ctx.astype(BF16), denom

    def weighted(half, pair, p_loc, p_ctx, denom):
        lanes = pair_lanes(pair)
        v = jnp.concatenate([r[:, lanes] for r in kv_refs[half][1]], axis=0)
        return (_dot(p_loc, v) + _dot(p_ctx, vc_ref[:, lanes])) / denom

    s, p, o = {}, {}, {}
    for t in range(len(items) + 2):
        if t < len(items):
            s[t] = scores(*items[t])
        if 0 <= t - 1 < len(items):
            p[t - 1] = softmax(*s.pop(t - 1))
        if 0 <= t - 2 < len(items):
            o[t - 2] = weighted(*items[t - 2][:2], *p.pop(t - 2))
    first = (lane // NA_HEAD_DIM) == 0
    for n, (half, pair, hh) in enumerate(items):
        if hh == 0:
            out = jnp.where(first, o[n], o[n + 1])
            o_ref[half * NA_HQ:(half + 1) * NA_HQ, pair_lanes(pair)] = out.astype(o_ref.dtype)


def _na(qkv, bias_rows, cast_ws=()):
    hp_n = NA_HEADS // (2 * NA_PAIRS)
    jn = NA_ROW_BLOCKS
    kb_per_img = SEQ // NA_KBLK
    last_blk = kb_per_img - NA_WBLKS

    def kv_spec(col0, half, t):
        return pl.BlockSpec((NA_KBLK, NA_LANES), lambda hp, j, b: (
            b * kb_per_img + jnp.clip(2 * j + half - 1, 0, last_blk) + t, col0 + hp))

    def kv_specs(col0):
        return [kv_spec(col0, half, t) for half in range(2) for t in range(NA_WBLKS)]

    ctx0 = N_LAT // CTX_LEN
    in_specs = ([pl.BlockSpec((NA_TQ, NA_LANES), lambda hp, j, b: (b * jn + j, hp))]
                + kv_specs(hp_n) + kv_specs(2 * hp_n)
                + [pl.BlockSpec((CTX_LEN, NA_LANES), lambda hp, j, b: (ctx0 + b, hp_n + hp)),
                   pl.BlockSpec((CTX_LEN, NA_LANES), lambda hp, j, b: (ctx0 + b, 2 * hp_n + hp)),
                   pl.BlockSpec((2 * NA_PAIRS, 2 * NA_KH, 2 * GRID_W), lambda hp, j, b: (hp, 0, 0))])
    flat = lambda hp, j, b: (hp * jn + j) * BATCH + b
    casts = [_cast_specs(w, hp_n * jn * BATCH, flat) for w in cast_ws]
    return pl.pallas_call(
        _hosting_casts(_na_kernel, len(in_specs), 1, [u for _, _, u in casts], flat, 3),
        grid=(hp_n, jn, BATCH),
        in_specs=in_specs + [s for s, _, _ in casts],
        out_specs=[pl.BlockSpec((NA_TQ, NA_LANES), lambda hp, j, b: (b * jn + j, hp))] + [s for s, _, _ in casts],
        out_shape=[jax.ShapeDtypeStruct((N_LAT, D_MODEL), BF16)] + [o for _, o, _ in casts],
        scratch_shapes=[pltpu.VMEM((len(NA_CASES) * 4 * NA_PAIRS, NA_HQ, NA_WK), F32),
                        pltpu.VMEM((2 * NA_PAIRS, 2 * NA_KH, GRID_W, 2 * GRID_W), F32)],
        compiler_params=_cparams(("arbitrary", "arbitrary", "arbitrary"), 48),
        name="neighbourhood_attention",
    )(qkv, *([qkv] * (4 * NA_WBLKS)), qkv, qkv, bias_rows, *cast_ws)


def _na_bias_rows(rpb):
    r = rpb.astype(F32) * LOG2_E
    r = jnp.pad(r, ((0, 0), (0, 2 * NA_KH - r.shape[1]), (0, GRID_W - r.shape[2])))
    return jnp.concatenate([r, r], axis=-1)


MR_TM = 1024
MR_TN = 1024


def _mmres_kernel(a_ref, w_ref, x_ref, gt_ref, o_ref):
    o_ref[...] = x_ref[...] + gt_ref[0] * _dot(a_ref[...], w_ref[...])


def _mm_res(a, w, xs, mod, n_rows):
    tm, tn = MR_TM, MR_TN
    return pl.pallas_call(
        _mmres_kernel,
        grid=(n_rows // tm, D_MODEL // tn),
        in_specs=[pl.BlockSpec((tm, D_MODEL), lambda i, j: (i, 0)),
                  pl.BlockSpec((D_MODEL, tn), lambda i, j: (0, j)),
                  pl.BlockSpec((tm, tn), lambda i, j: (i, j)),
                  _mod_spec(5, tm, tn, col=True)],
        out_specs=pl.BlockSpec((tm, tn), lambda i, j: (i, j)),
        out_shape=jax.ShapeDtypeStruct((n_rows, D_MODEL), F32),
        compiler_params=_cparams(("parallel", "arbitrary"), 40),
        name="matmul_gated_residual",
    )(a, w, xs, mod)


def _rope_tables():
    half = GLA_DK // 4
    inv_freq = ROPE_BASE ** (-jnp.arange(half, dtype=F32) / half)
    t = jnp.arange(SEQ)
    ang_r = (t // GRID_W).astype(F32)[:, None] * inv_freq[None, :]
    ang_c = (t % GRID_W).astype(F32)[:, None] * inv_freq[None, :]
    cr, sr, cc, sc = jnp.cos(ang_r), jnp.sin(ang_r), jnp.cos(ang_c), jnp.sin(ang_c)
    cos_t = jnp.concatenate([cr, cc, cr, cc], axis=-1)
    sin_t = jnp.concatenate([-sr, -sc, sr, sc], axis=-1)
    cos_t = jnp.concatenate([cos_t, jnp.ones((GLA_BLOCK, GLA_DK), F32)], axis=0)
    sin_t = jnp.concatenate([sin_t, jnp.zeros((GLA_BLOCK, GLA_DK), F32)], axis=0)
    return cos_t, sin_t


def _pair_layout(w):
    lead = w.shape[:-1]
    quarter = GLA_DK // 4
    w = w.reshape(*lead, GLA_HEADS, 2, 2, quarter)
    return jnp.swapaxes(w, -2, -3).reshape(*lead, GLA_QK_W)


def kernel(x, c, ctx, c_ctx, l0_ada_w, l0_ada_b, l0_norm_ffn1, l0_norm_mix, l0_norm_ffn2, l0_ffn1_w_gu, l0_ffn1_w_down, l0_ffn2_w_gu, l0_ffn2_w_down, l0_w_in, l0_gla_gate_w_fwd, l0_gla_gate_b_fwd, l0_gla_gate_w_bwd, l0_gla_gate_b_bwd, l0_gla_norm, l0_w_out, l1_ada_w, l1_ada_b, l1_norm_ffn1, l1_norm_mix, l1_norm_ffn2, l1_ffn1_w_gu, l1_ffn1_w_down, l1_ffn2_w_gu, l1_ffn2_w_down, l1_w_qkv, l1_rpb, l1_w_out, norm_out):
    bf = lambda w: w.astype(BF16)
    cc = jnp.concatenate([c, c_ctx[None, :], jnp.zeros((8 - BATCH - 1, D_MODEL), F32)], axis=0)

    mod0 = _ada(cc, l0_ada_w, l0_ada_b)
    w_gu, w_down = bf(l0_ffn1_w_gu), bf(l0_ffn1_w_down)
    xs = _ffn(x.reshape(N_LAT, D_MODEL), mod0, 0, l0_norm_ffn1, w_gu, w_down, out_rows=N_TOK)
    xs = _ffn(ctx.reshape(N_CTX, D_MODEL), mod0, 0, l0_norm_ffn1, w_gu, w_down, out_rows=N_TOK,
              tile0=N_LAT // FFN_TM, prev=xs)

    a0 = 2 * GLA_QK_W + GLA_V_W
    a1 = a0 + 2 * GLA_LOWRANK
    w_in = bf(l0_w_in)
    w_main = jnp.concatenate([_pair_layout(w_in[:, :GLA_QK_W]), _pair_layout(w_in[:, GLA_QK_W:2 * GLA_QK_W]),
                              w_in[:, 2 * GLA_QK_W:a0], w_in[:, a1:]], axis=1)
    w_lr = jnp.pad(w_in[:, a0:a1], ((0, 0), (0, GLA_LR_PAD - 2 * GLA_LOWRANK)))
    cc_np, sc_np = _dft_tables(FNET_GROUP_DIM)
    wc = jnp.asarray(np.concatenate([cc_np, sc_np], axis=1), F32).astype(BF16)
    pf, pb, zc, zs, lr = _nmm(xs, mod0, 1, l0_norm_mix, w_main, w_narrow=w_lr, w_chan=wc)

    cos_t, sin_t = _rope_tables()
    gwf = jnp.pad(_pair_layout(l0_gla_gate_w_fwd), ((0, GLA_LR_PAD - GLA_LOWRANK), (0, 0)))
    gwb = jnp.pad(_pair_layout(l0_gla_gate_w_bwd), ((GLA_LOWRANK, GLA_LR_PAD - 2 * GLA_LOWRANK), (0, 0)))
    of, ob, w_gu2, w_down2 = _gla(pf, pb, lr, cos_t, sin_t, gwf, _pair_layout(l0_gla_gate_b_fwd).reshape(1, -1),
                                  gwb, _pair_layout(l0_gla_gate_b_bwd).reshape(1, -1),
                                  cast_ws=(l0_ffn2_w_gu, l0_ffn2_w_down))

    cos_big, sin_big = _big_dft_tables()
    fm, w_gu3, w_down3 = _time_dft_folded(cos_big, sin_big, zc, zs, 512, 512, cast_ws=(l1_ffn1_w_gu, l1_ffn1_w_down))
    ct_np, st_np = _dft_tables(CTX_LEN)
    fm = _time_dft(jnp.asarray(ct_np, F32).astype(BF16), jnp.asarray(st_np, F32).astype(BF16),
                   zc, zs, CTX_LEN, CTX_LEN, 512, N_LAT, prev=fm)[0]

    xs = _glaout(of, ob, pf, fm, l0_gla_norm, xs, mod0, bf(l0_w_out))
    xs = _ffn(xs, mod0, 2, l0_norm_ffn2, w_gu2, w_down2, out_rows=N_TOK)

    mod1 = _ada(cc, l1_ada_w, l1_ada_b)
    xs = _ffn(xs, mod1, 0, l1_norm_ffn1, w_gu3, w_down3, out_rows=N_TOK)
    w_qkv = bf(jnp.concatenate([l1_w_qkv[:, :D_MODEL] * NA_Q_SCALE, l1_w_qkv[:, D_MODEL:]], axis=1))
    qkv, w_gu4, w_down4 = _nmm(xs, mod1, 1, l1_norm_mix, w_qkv, cast_ws=(l1_ffn2_w_gu, l1_ffn2_w_down))
    ao = _na(qkv, _na_bias_rows(l1_rpb))[0]
    xl = _mm_res(ao, bf(l1_w_out), xs, mod1, N_LAT)
    out = _ffn(xl, mod1, 2, l1_norm_ffn2, w_gu4, w_down4, out_rows=N_LAT, final_g=norm_out)
    return out.reshape(BATCH, SEQ, D_MODEL)
```

```python
import functools

import numpy as np
import jax
import jax.numpy as jnp
from jax import lax
from jax.experimental import pallas as pl
from jax.experimental.pallas import tpu as pltpu

D_MODEL = 2048
BATCH = 4
SEQ = 4096
GRID_W = 64
CTX_LEN = 256
N_MOD = 9
D_FF = 5632
RMS_EPS = 1e-6
ROPE_BASE = 10000.0

GLA_HEADS = 4
GLA_DK = 128
GLA_DV = 256
GLA_LOWRANK = 16
GLA_TAU = 16.0
GLA_CHUNK = 64
FNET_GROUPS = 4
FNET_GROUP_DIM = 256
GLA_QK_W = GLA_HEADS * GLA_DK
GLA_V_W = GLA_HEADS * GLA_DV
FNET_W = FNET_GROUPS * FNET_GROUP_DIM

NA_HEADS = 32
NA_HEAD_DIM = 64
NA_KH = 8
NA_KW = 16
NA_QROWS = 8
NEG_BIG = -1e30
LOG2_E = 1.4426950408889634
NA_Q_SCALE = NA_HEAD_DIM ** -0.5 * LOG2_E

N_LAT = BATCH * SEQ
N_CTX = BATCH * CTX_LEN
N_TOK = N_LAT + N_CTX

F32 = jnp.float32
BF16 = jnp.bfloat16
MIB = 1024 * 1024


def _cparams(sem, vmem_mib):
    return pltpu.CompilerParams(dimension_semantics=sem, vmem_limit_bytes=vmem_mib * MIB)


def _dot(a, b):
    return jnp.dot(a, b, preferred_element_type=F32)


def _dot_nt(a, b):
    return lax.dot_general(a, b, (((1,), (1,)), ((), ())), preferred_element_type=F32)


def _dot_tn(a, b):
    return lax.dot_general(a, b, (((0,), (0,)), ((), ())), preferred_element_type=F32)


def _split2(x):
    hi = x.astype(BF16)
    lo = (x - hi.astype(F32)).astype(BF16)
    return hi, lo


def _dot3(a, b):
    ah, al = _split2(a)
    bh, bl = _split2(b)
    return _dot(ah, bh) + (_dot(al, bh) + _dot(ah, bl))


def _dot_exact_lhs(l_bf16, x):
    x1 = x.astype(BF16)
    r1 = x - x1.astype(F32)
    x2 = r1.astype(BF16)
    x3 = (r1 - x2.astype(F32)).astype(BF16)
    return _dot(l_bf16, x1) + (_dot(l_bf16, x2) + _dot(l_bf16, x3))


def _sigmoid(x):
    return 1.0 / (1.0 + jnp.exp(-x))


ROW_CHUNK = 16
ROW_GROUP = 8


def _row_groups(n_rows, body, group=ROW_GROUP):
    span = ROW_CHUNK * group

    def step(t, carry):
        base = t * span
        body([pl.ds(pl.multiple_of(base + c * ROW_CHUNK, ROW_CHUNK), ROW_CHUNK) for c in range(group)])
        return carry
    lax.fori_loop(0, n_rows // span, step, 0)


def _inv_rms(x):
    return lax.rsqrt(jnp.mean(x * x, axis=-1, keepdims=True) + RMS_EPS)


def _norm_mod_rows(x_ref, g_ref, sh_ref, sc_ref, h_ref, gs_ref):
    d = x_ref.shape[1]
    gs_ref[0] = jnp.broadcast_to(g_ref[...] * (1.0 + sc_ref[0]), (ROW_CHUNK, d))
    gs_ref[1] = jnp.broadcast_to(sh_ref[0], (ROW_CHUNK, d))

    def body(chunks):
        inv = [_inv_rms(x_ref[rows, :]) for rows in chunks]
        for rows, r in zip(chunks, inv):
            h_ref[rows, :] = (x_ref[rows, :] * r * gs_ref[0] + gs_ref[1]).astype(BF16)
    _row_groups(x_ref.shape[0], body)


def _mod_row(i, tm):
    return jnp.minimum(i // (SEQ // tm), BATCH)


def _mod_spec(j, tm, width=D_MODEL, col=None, tile0=0):
    if col is None:
        return pl.BlockSpec((1, 1, width), lambda i, *_: (_mod_row(i + tile0, tm) * N_MOD + j, 0, 0))
    return pl.BlockSpec((1, 1, width), lambda i, n, *_: (_mod_row(i + tile0, tm) * N_MOD + j, 0, n))


CAST_ROW_ALIGN = 16


def _cast_plan(rows, steps):
    per = pl.cdiv(pl.cdiv(rows, steps), CAST_ROW_ALIGN) * CAST_ROW_ALIGN
    while rows % per:
        per += CAST_ROW_ALIGN
    return per, rows // per


def _cast_specs(w, steps, flat_index):
    per, used = _cast_plan(w.shape[0], steps)
    spec = pl.BlockSpec((per, w.shape[1]), lambda *g: (jnp.minimum(flat_index(*g), used - 1), 0))
    return spec, jax.ShapeDtypeStruct(w.shape, BF16), used


def _hosting_casts(kernel_fn, n_in, n_out, used, flat_index, grid_rank):
    nc = len(used)

    def wrapped(*refs):
        a, b, c = n_in + nc, n_in + nc + n_out, n_in + nc + n_out + nc
        kernel_fn(*refs[:n_in], *refs[a:b], *refs[c:])
        t = flat_index(*[pl.program_id(d) for d in range(grid_rank)])
        for n_used, src_ref, dst_ref in zip(used, refs[n_in:a], refs[b:c]):
            @pl.when(t < n_used)
            def _(src_ref=src_ref, dst_ref=dst_ref):
                dst_ref[...] = src_ref[...].astype(BF16)
    return wrapped


ADA_TN = 1024


def _ada_kernel(c_ref, w_ref, b_ref, o_ref):
    c = c_ref[...]
    s = (c * _sigmoid(c)).astype(BF16)
    o_ref[...] = _dot(s, w_ref[...].astype(BF16)) + b_ref[...]


def _ada(cc, w, b):
    n = w.shape[1]
    out = pl.pallas_call(
        _ada_kernel,
        grid=(n // ADA_TN,),
        in_specs=[pl.BlockSpec((8, D_MODEL), lambda j: (0, 0)),
                  pl.BlockSpec((D_MODEL, ADA_TN), lambda j: (0, j)),
                  pl.BlockSpec((1, ADA_TN), lambda j: (0, j))],
        out_specs=pl.BlockSpec((8, ADA_TN), lambda j: (0, j)),
        out_shape=jax.ShapeDtypeStruct((8, n), F32),
        compiler_params=_cparams(("arbitrary",), 40),
        name="ada_mod",
    )(cc, w, b.reshape(1, n))
    return out.reshape(8 * N_MOD, 1, D_MODEL)


FFN_TM = 1024
FFN_TF = 512
FFN_SLAB = 128


def _ffn_kernel(x_hbm, xn_ref, g_ref, sh_ref, sc_ref, shn_ref, scn_ref, gt_ref, wg_ref, wu_ref, wd_ref, *rest,
                final_norm):
    go_ref = rest[0] if final_norm else None
    o_ref, h0_ref, h1_ref, gs_ref, x_ref, x_sem = rest[-6:]
    i = pl.program_id(0)
    f = pl.program_id(1)
    last = pl.num_programs(1) - 1
    tm, d = x_ref.shape

    def x_copy():
        return pltpu.make_async_copy(x_hbm.at[pl.ds(pl.multiple_of(i * tm, tm), tm), :], x_ref, x_sem)

    @pl.when(f == 0)
    def _():
        x_copy().start()

    @pl.when((i == 0) & (f == 0))
    def _():
        x_copy().wait()
        _norm_mod_rows(x_ref, g_ref, sh_ref, sc_ref, h0_ref, gs_ref)

    def step(h_ref, hn_ref, first):
        gs_ref[0] = jnp.broadcast_to(g_ref[...] * (1.0 + scn_ref[0]), (ROW_CHUNK, d))
        gs_ref[1] = jnp.broadcast_to(shn_ref[0], (ROW_CHUNK, d))
        slab0 = jnp.minimum(f, tm // FFN_SLAB - 1) * FFN_SLAB
        chunks = [pl.ds(k * ROW_CHUNK, ROW_CHUNK) for k in range(FFN_SLAB // ROW_CHUNK)]
        inv = [_inv_rms(xn_ref[rows, :]) for rows in chunks]
        for k, (rows, r) in enumerate(zip(chunks, inv)):
            dst = pl.ds(pl.multiple_of(slab0 + k * ROW_CHUNK, ROW_CHUNK), ROW_CHUNK)
            hn_ref[dst, :] = (xn_ref[rows, :] * r * gs_ref[0] + gs_ref[1]).astype(BF16)

        h = h_ref[...]
        a = _dot(h, wg_ref[...])
        u = _dot(h, wu_ref[...])
        act = (a * _sigmoid(a) * u).astype(BF16)
        if first:
            o_ref[...] = _dot(act, wd_ref[...])
        else:
            o_ref[...] += _dot(act, wd_ref[...])

    for parity, (h_ref, hn_ref) in enumerate(((h0_ref, h1_ref), (h1_ref, h0_ref))):
        for first in (True, False):
            @pl.when((i % 2 == parity) & ((f == 0) == first))
            def _(h_ref=h_ref, hn_ref=hn_ref, first=first):
                step(h_ref, hn_ref, first)

    @pl.when(f == last)
    def _():
        @pl.when(i > 0)
        def _():
            x_copy().wait()

        if not final_norm:
            o_ref[...] = x_ref[...] + (0.5 * gt_ref[0]) * o_ref[...]
        else:
            gs_ref[2] = jnp.broadcast_to(0.5 * gt_ref[0], (ROW_CHUNK, d))
            gs_ref[3] = jnp.broadcast_to(go_ref[...], (ROW_CHUNK, d))

            def body(chunks):
                resid = lambda rows: x_ref[rows, :] + gs_ref[2] * o_ref[rows, :]
                inv = [_inv_rms(resid(rows)) for rows in chunks]
                for rows, r in zip(chunks, inv):
                    o_ref[rows, :] = resid(rows) * r * gs_ref[3]
            _row_groups(tm, body)


def _ffn(x_in, mod, sub, norm_g, w_gu, w_down, *, out_rows, tile0=0, prev=None, final_g=None, cast_ws=()):
    tm, tf = FFN_TM, FFN_TF
    nf = D_FF // tf
    n_tiles = x_in.shape[0] // tm
    slabs = tm // FFN_SLAB
    assert slabs <= nf
    nxt = lambda i: jnp.minimum(i + 1, n_tiles - 1)
    in_specs = [pl.BlockSpec(memory_space=pl.ANY),
                pl.BlockSpec((FFN_SLAB, D_MODEL), lambda i, f: (nxt(i) * slabs + jnp.minimum(f, slabs - 1), 0)),
                pl.BlockSpec((1, D_MODEL), lambda i, f: (0, 0)),
                _mod_spec(3 * sub, tm, tile0=tile0), _mod_spec(3 * sub + 1, tm, tile0=tile0),
                pl.BlockSpec((1, 1, D_MODEL), lambda i, f: (_mod_row(nxt(i) + tile0, tm) * N_MOD + 3 * sub, 0, 0)),
                pl.BlockSpec((1, 1, D_MODEL), lambda i, f: (_mod_row(nxt(i) + tile0, tm) * N_MOD + 3 * sub + 1, 0, 0)),
                _mod_spec(3 * sub + 2, tm, tile0=tile0),
                pl.BlockSpec((D_MODEL, tf), lambda i, f: (0, f)),
                pl.BlockSpec((D_MODEL, tf), lambda i, f: (0, f + nf)),
                pl.BlockSpec((tf, D_MODEL), lambda i, f: (f, 0))]
    args = [x_in, x_in, norm_g.reshape(1, D_MODEL), mod, mod, mod, mod, mod, w_gu, w_gu, w_down]
    if final_g is not None:
        in_specs.append(pl.BlockSpec((1, D_MODEL), lambda i, f: (0, 0)))
        args.append(final_g.reshape(1, D_MODEL))
    aliases = {}
    if prev is not None:
        aliases = {len(args): 0}
        in_specs.append(pl.BlockSpec(memory_space=pl.ANY))
        args.append(prev)
    flat = lambda i, f: i * nf + f
    casts = [_cast_specs(w, n_tiles * nf, flat) for w in cast_ws]
    out = pl.pallas_call(
        _hosting_casts(functools.partial(_ffn_kernel, final_norm=final_g is not None), len(args), 1,
                       [u for _, _, u in casts], flat, 2),
        grid=(n_tiles, nf),
        in_specs=in_specs + [s for s, _, _ in casts],
        out_specs=[pl.BlockSpec((tm, D_MODEL), lambda i, f: (i + tile0, 0))] + [s for s, _, _ in casts],
        out_shape=[jax.ShapeDtypeStruct((out_rows, D_MODEL), F32)] + [o for _, o, _ in casts],
        scratch_shapes=[pltpu.VMEM((tm, D_MODEL), BF16), pltpu.VMEM((tm, D_MODEL), BF16),
                        pltpu.VMEM((4 if final_g is not None else 2, ROW_CHUNK, D_MODEL), F32),
                        pltpu.VMEM((tm, D_MODEL), F32), pltpu.SemaphoreType.DMA(())],
        input_output_aliases=aliases,
        compiler_params=_cparams(("arbitrary", "arbitrary"), 56),
        name="ffn_half",
    )(*args, *cast_ws)
    return out if cast_ws else out[0]


NMM_TM = 1024
NMM_TN = 1024


def _nmm_kernel(x_ref, g_ref, sh_ref, sc_ref, w_ref, *rest, gla):
    if gla:
        wn_ref, wc_ref, o_ref, ob_ref, zc_ref, zs_ref, on_ref, h_ref, gs_ref = rest
    else:
        o_ref, h_ref, gs_ref = rest
    j = pl.program_id(1)

    @pl.when(j == 0)
    def _():
        _norm_mod_rows(x_ref, g_ref, sh_ref, sc_ref, h_ref, gs_ref)
        if gla:
            on_ref[...] = _dot(h_ref[...], wn_ref[...])

    if not gla:
        o_ref[...] = _dot(h_ref[...], w_ref[...]).astype(o_ref.dtype)
    else:
        @pl.when(j % 2 == 0)
        def _():
            o_ref[...] = _dot(h_ref[...], w_ref[...])

        @pl.when(j == 1)
        def _():
            ob_ref[...] = _dot(h_ref[...], w_ref[...]).astype(BF16)

        @pl.when(j == 3)
        def _():
            f = _dot(h_ref[...], w_ref[...]).astype(BF16)
            for grp in range(FNET_GROUPS):
                cols = slice(grp * FNET_GROUP_DIM, (grp + 1) * FNET_GROUP_DIM)
                z = _dot(f[:, cols], wc_ref[...])
                zc_ref[:, cols] = z[:, :FNET_GROUP_DIM].astype(BF16)
                zs_ref[:, cols] = z[:, FNET_GROUP_DIM:].astype(BF16)


def _nmm(xs, mod, sub, norm_g, w, w_narrow=None, w_chan=None, cast_ws=()):
    tm, tn = NMM_TM, NMM_TN
    n = w.shape[1]
    gla = w_narrow is not None
    in_specs = [pl.BlockSpec((tm, D_MODEL), lambda i, j: (i, 0)),
                pl.BlockSpec((1, D_MODEL), lambda i, j: (0, 0)),
                _mod_spec(3 * sub, tm), _mod_spec(3 * sub + 1, tm),
                pl.BlockSpec((D_MODEL, tn), lambda i, j: (0, j))]
    args = [xs, norm_g.reshape(1, D_MODEL), mod, mod, w]
    if gla:
        nn = w_narrow.shape[1]
        assert n == 4 * tn and tn == GLA_V_W == FNET_W
        in_specs += [pl.BlockSpec((D_MODEL, nn), lambda i, j: (0, 0)),
                     pl.BlockSpec(w_chan.shape, lambda i, j: (0, 0))]
        args += [w_narrow, w_chan]
        row_tile = pl.BlockSpec((tm, tn), lambda i, j: (i, 0))
        out_specs = [pl.BlockSpec((tm, tn), lambda i, j: (i, j // 2)), row_tile, row_tile, row_tile,
                     pl.BlockSpec((tm, nn), lambda i, j: (i, 0))]
        out_shape = [jax.ShapeDtypeStruct((N_TOK, n // 2), F32)] + [jax.ShapeDtypeStruct((N_TOK, tn), BF16)] * 3 + [
            jax.ShapeDtypeStruct((N_TOK, nn), F32)]
    else:
        out_specs = [pl.BlockSpec((tm, tn), lambda i, j: (i, j))]
        out_shape = [jax.ShapeDtypeStruct((N_TOK, n), BF16)]
    nj = n // tn
    flat = lambda i, j: i * nj + j
    casts = [_cast_specs(cw, (N_TOK // tm) * nj, flat) for cw in cast_ws]
    return pl.pallas_call(
        _hosting_casts(functools.partial(_nmm_kernel, gla=gla), len(args), len(out_specs),
                       [u for _, _, u in casts], flat, 2),
        grid=(N_TOK // tm, nj),
        in_specs=in_specs + [s for s, _, _ in casts],
        out_specs=out_specs + [s for s, _, _ in casts],
        out_shape=out_shape + [o for _, o, _ in casts],
        scratch_shapes=[pltpu.VMEM((tm, D_MODEL), BF16), pltpu.VMEM((2, ROW_CHUNK, D_MODEL), F32)],
        compiler_params=_cparams(("arbitrary", "arbitrary"), 56),
        name="norm_mod_matmul",
    )(*args, *cast_ws)


GLA_BLOCK = CTX_LEN
GLA_LAT_BLOCKS = SEQ // GLA_BLOCK
GLA_STEPS = 1 + GLA_LAT_BLOCKS
GLA_LR_PAD = 128
GLA_SPLIT_RANGE = 80.0


def _rope(x, cos, sin_signed):
    return x * cos + pltpu.roll(x, GLA_DK // 2, 1) * sin_signed


def _gla_kernel(qf, kf, vf, lf, cf, sf, qb, kb, vb, lb, cb, sb, lfn, lbn, gwf, gbf, gwb, gbb,
                of_ref, ob_ref, st_ref, row_ref, bc_ref, md_ref):

    c = GLA_CHUNK
    n = GLA_BLOCK
    row = lax.broadcasted_iota(jnp.int32, (n, n), 0)
    col = lax.broadcasted_iota(jnp.int32, (n, n), 1)
    same_chunk = (row // c) == (col // c)
    row_c = lax.broadcasted_iota(jnp.int32, (c, c), 0)
    col_c = lax.broadcasted_iota(jnp.int32, (c, c), 1)
    scale = GLA_DK ** -0.5

    dirs = ((qf, kf, vf, lf, cf, sf, gwf, gbf, of_ref), (qb, kb, vb, lb, cb, sb, gwb, gbb, ob_ref))
    n_chunks = n // c
    heads = [(slice(h * GLA_DK, (h + 1) * GLA_DK), slice(h * GLA_DV, (h + 1) * GLA_DV)) for h in range(GLA_HEADS)]
    tris = [row_c >= col_c, row_c <= col_c]
    last_row = (c - 1, 0)

    def gate_decays(d, l_ref):
        causal = (row >= col) if d == 0 else (row <= col)
        tri_b = jnp.where(same_chunk & causal, 1.0, 0.0).astype(BF16)
        x = _dot3(l_ref[...], dirs[d][6][...]) + dirs[d][7][...]
        g = (jnp.minimum(x, 0.0) - jnp.log1p(jnp.exp(-jnp.abs(x)))) * (1.0 / GLA_TAU)
        bc_all = _dot_exact_lhs(tri_b, g)
        totals = [-bc_all[ci * c + last_row[d]:ci * c + last_row[d] + 1, :] for ci in range(n_chunks)]
        return bc_all, functools.reduce(jnp.maximum, totals)

    def prepare_decays(l_refs):
        both = [gate_decays(d, l_ref) for d, l_ref in enumerate(l_refs)]

        def store():
            bc_ref[0] = both[0][0]
            bc_ref[1] = both[1][0]
            md_ref[...] = jnp.maximum(both[0][1], both[1][1])
        return store

    @pl.when(pl.program_id(1) == 0)
    def _():
        st_ref[...] = jnp.zeros_like(st_ref)
        prepare_decays((lf, lb))()

    decays = []
    for d in range(2):
        bc_all = bc_ref[d]
        per_chunk = []
        for ci in (range(n_chunks) if d == 0 else range(n_chunks - 1, -1, -1)):
            rows = slice(ci * c, (ci + 1) * c)
            bc = bc_all[rows, :]
            per_chunk.append((rows, bc, bc[c // 2:c // 2 + 1, :], bc[last_row[d]:last_row[d] + 1, :]))
        decays.append(per_chunk)
    max_decay = md_ref[...]

    def scores_by_rows(d, qh, kh, bc):
        row_ref[0], row_ref[1] = bc, qh
        s_idx = lax.broadcasted_iota(jnp.int32, (c, 1), 0)
        t_idx = lax.broadcasted_iota(jnp.int32, (c, c), 1)

        def one_row(t, att_t):
            valid = (s_idx <= t) if d == 0 else (s_idx >= t)
            w = jnp.exp(jnp.where(valid, row_ref[0, pl.ds(t, 1), :] - bc, -jnp.inf))
            col = jnp.sum(kh * w * row_ref[1, pl.ds(t, 1), :], axis=-1, keepdims=True)
            return jnp.where(t_idx == t, col, att_t)
        return lax.fori_loop(0, c, one_row, jnp.zeros((c, c), F32))

    def local_part(k, exact):
        out = {}
        for d, (q_ref, k_ref, v_ref, _, c_ref, s_ref, _, _, _) in enumerate(dirs):
            rows, bc, mid, last = decays[d][k]
            e_q, e_kl, e_l = jnp.exp(bc), jnp.exp(last - bc), jnp.exp(last)
            if not exact:
                e_qm, e_km = jnp.exp(bc - mid), jnp.exp(mid - bc)
            cos, sin = c_ref[rows, :], s_ref[rows, :]
            for h, (sk, sv) in enumerate(heads):
                qh = _rope(q_ref[rows, sk] * scale, cos, sin)
                kh = _rope(k_ref[rows, sk], cos, sin)
                vh = v_ref[rows, sv].astype(BF16)
                kv = _dot_tn(vh, (kh * e_kl[:, sk]).astype(BF16))
                if exact:
                    att_t = scores_by_rows(d, qh, kh, bc[:, sk]).astype(BF16)
                    mix = lambda att_t=att_t, vh=vh: _dot_tn(att_t, vh)
                else:
                    att = _dot_nt((qh * e_qm[:, sk]).astype(BF16), (kh * e_km[:, sk]).astype(BF16))
                    att = jnp.where(tris[d], att, 0.0).astype(BF16)
                    mix = lambda att=att, vh=vh: _dot(att, vh)
                out[d, h] = ((qh * e_q[:, sk]).astype(BF16), mix, kv, e_l[:, sk])
        return out

    def scan_block(exact):
        hand_over = prepare_decays((lfn, lbn))
        st = {(d, h): st_ref[d, h] for d in range(2) for h in range(GLA_HEADS)}
        local = local_part(0, exact)
        for k in range(n_chunks):
            nxt = local_part(k + 1, exact) if k + 1 < n_chunks else None
            for d in range(2):
                rows = decays[d][k][0]
                for h, (sk, sv) in enumerate(heads):
                    qe, mix, kv, e_l = local[d, h]
                    dirs[d][-1][rows, sv] = _dot_nt(qe, st[d, h].astype(BF16)) + mix()
                    st[d, h] = st[d, h] * e_l + kv
            local = nxt
        for (d, h), val in st.items():
            st_ref[d, h] = val
        hand_over()

    in_range = jnp.max(max_decay) <= GLA_SPLIT_RANGE

    @pl.when(in_range)
    def _():
        scan_block(exact=False)

    @pl.when(jnp.logical_not(in_range))
    def _():
        scan_block(exact=True)


def _gla(pf, pb, lr, cos_t, sin_t, gwf, gbf, gwb, gbb, cast_ws):
    c = GLA_BLOCK
    ctx0 = N_LAT // c

    def fwd_blk(b, s):
        return jnp.where(s == 0, ctx0 + b, GLA_LAT_BLOCKS * b + s - 1)

    def bwd_blk(b, s):
        return jnp.where(s == 0, ctx0 + b, GLA_LAT_BLOCKS * b + (GLA_STEPS - 1 - s))

    def fwd_rope(b, s):
        return jnp.where(s == 0, GLA_LAT_BLOCKS, s - 1)

    def bwd_rope(b, s):
        return jnp.where(s == 0, GLA_LAT_BLOCKS, GLA_STEPS - 1 - s)

    def dir_specs(blk, rope):
        return [pl.BlockSpec((c, GLA_QK_W), lambda b, s: (blk(b, s), 0)),
                pl.BlockSpec((c, GLA_QK_W), lambda b, s: (blk(b, s), 1)),
                pl.BlockSpec((c, GLA_V_W), lambda b, s: (blk(b, s), 0)),
                pl.BlockSpec((c, GLA_LR_PAD), lambda b, s: (blk(b, s), 0)),
                pl.BlockSpec((c, GLA_DK), lambda b, s: (rope(b, s), 0)),
                pl.BlockSpec((c, GLA_DK), lambda b, s: (rope(b, s), 0))]

    const = lambda shape: pl.BlockSpec(shape, lambda b, s: (0, 0))
    ahead = lambda s: jnp.minimum(s + 1, GLA_STEPS - 1)
    in_specs = (dir_specs(fwd_blk, fwd_rope) + dir_specs(bwd_blk, bwd_rope)
                + [pl.BlockSpec((c, GLA_LR_PAD), lambda b, s: (fwd_blk(b, ahead(s)), 0)),
                   pl.BlockSpec((c, GLA_LR_PAD), lambda b, s: (bwd_blk(b, ahead(s)), 0))]
                + [const((GLA_LR_PAD, GLA_QK_W)), const((1, GLA_QK_W)),
                   const((GLA_LR_PAD, GLA_QK_W)), const((1, GLA_QK_W))])
    flat = lambda b, s: b * GLA_STEPS + s
    casts = [_cast_specs(w, BATCH * GLA_STEPS, flat) for w in cast_ws]
    args = [pf, pf, pb, lr, cos_t, sin_t, pf, pf, pb, lr, cos_t, sin_t, lr, lr, gwf, gbf, gwb, gbb]
    return pl.pallas_call(
        _hosting_casts(_gla_kernel, len(args), 2, [u for _, _, u in casts], flat, 2),
        grid=(BATCH, GLA_STEPS),
        in_specs=in_specs + [s for s, _, _ in casts],
        out_specs=[pl.BlockSpec((c, GLA_V_W), lambda b, s: (fwd_blk(b, s), 0)),
                   pl.BlockSpec((c, GLA_V_W), lambda b, s: (bwd_blk(b, s), 0))] + [s for s, _, _ in casts],
        out_shape=[jax.ShapeDtypeStruct((N_TOK, GLA_V_W), F32)] * 2 + [o for _, o, _ in casts],
        scratch_shapes=[pltpu.VMEM((2, GLA_HEADS, GLA_DV, GLA_DK), F32), pltpu.VMEM((2, GLA_CHUNK, GLA_DK), F32),
                        pltpu.VMEM((2, c, GLA_QK_W), F32), pltpu.VMEM((1, GLA_QK_W), F32)],
        compiler_params=_cparams(("arbitrary", "arbitrary"), 40),
        name="gla_scan",
    )(*args, *cast_ws)


def _tdft_kernel(c_ref, s_ref, zc_ref, zs_ref, *rest):
    o_ref = rest[-1]
    o_ref[...] = (_dot(c_ref[...], zc_ref[...]) - _dot(s_ref[...], zs_ref[...])).astype(o_ref.dtype)


def _time_dft(cmat, smat, zc, zs, t_len, tm, tn, row0, prev=None, cast_ws=()):
    zb0 = row0 // t_len
    ob0 = row0 // tm
    mt = t_len // tm
    in_specs = [pl.BlockSpec((tm, t_len), lambda b, n, m: (m, 0)),
                pl.BlockSpec((tm, t_len), lambda b, n, m: (m, 0)),
                pl.BlockSpec((t_len, tn), lambda b, n, m: (zb0 + b, n)),
                pl.BlockSpec((t_len, tn), lambda b, n, m: (zb0 + b, n))]
    args = [cmat, smat, zc, zs]
    aliases = {}
    if prev is not None:
        in_specs.append(pl.BlockSpec(memory_space=pl.ANY))
        args.append(prev)
        aliases = {4: 0}
    nt = FNET_W // tn
    flat = lambda b, n, m: (b * nt + n) * mt + m
    casts = [_cast_specs(w, BATCH * nt * mt, flat) for w in cast_ws]
    return pl.pallas_call(
        _hosting_casts(_tdft_kernel, len(args), 1, [u for _, _, u in casts], flat, 3),
        grid=(BATCH, nt, mt),
        in_specs=in_specs + [s for s, _, _ in casts],
        out_specs=[pl.BlockSpec((tm, tn), lambda b, n, m: (ob0 + b * mt + m, n))] + [s for s, _, _ in casts],
        out_shape=[jax.ShapeDtypeStruct((N_TOK, FNET_W), BF16)] + [o for _, o, _ in casts],
        input_output_aliases=aliases,
        compiler_params=_cparams(("arbitrary", "arbitrary", "arbitrary"), 48),
        name="fnet_time_dft",
    )(*args, *cast_ws)


FOLD_BLK = 256


def _tdft_fold_kernel(c_ref, s_ref, zc_ref, zs_ref, j1_ref, j2_ref, o_ref, zcf_ref, zsf_ref):
    half, blk = zcf_ref.shape[0], FOLD_BLK
    nblk = 2 * half // blk

    @pl.when(pl.program_id(2) == 0)
    def _():
        for kb in range(half // blk):
            rows = slice(kb * blk, (kb + 1) * blk)
            mirror = slice((nblk - 1 - kb) * blk, (nblk - kb) * blk)
            after = slice((nblk - kb) * blk, (nblk - kb + 1) * blk)
            for z_ref, zf_ref, sign in ((zc_ref, zcf_ref, 1.0), (zs_ref, zsf_ref, -1.0)):
                partner = _dot(j1_ref[...], z_ref[mirror, :])
                if kb > 0:
                    partner = partner + _dot(j2_ref[...], z_ref[after, :])
                zf_ref[rows, :] = (z_ref[rows, :].astype(F32) + sign * partner).astype(BF16)

    tm = o_ref.shape[0]
    j = pl.program_id(2) * tm + lax.broadcasted_iota(jnp.int32, (tm, 1), 0)
    sign = jnp.where(j % 2 == 0, SEQ ** -0.5, -(SEQ ** -0.5))
    mid = zc_ref[half:half + 1, :].astype(F32)
    o = _dot(c_ref[...], zcf_ref[...]) - _dot(s_ref[...], zsf_ref[...]) + sign * mid
    o_ref[...] = o.astype(o_ref.dtype)


def _time_dft_folded(cmat, smat, zc, zs, tm, tn, cast_ws=()):
    half = SEQ // 2
    mt = SEQ // tm
    nt = FNET_W // tn
    i = np.arange(FOLD_BLK)
    j1 = jnp.asarray((i[:, None] + i[None, :] == FOLD_BLK) & (i[:, None] > 0), BF16)
    j2 = jnp.asarray((i[:, None] == 0) & (i[None, :] == 0), BF16)
    args = [cmat, smat, zc, zs, j1, j2]
    in_specs = [pl.BlockSpec((tm, half), lambda b, n, m: (m, 0)),
                pl.BlockSpec((tm, half), lambda b, n, m: (m, 0)),
                pl.BlockSpec((SEQ, tn), lambda b, n, m: (b, n)),
                pl.BlockSpec((SEQ, tn), lambda b, n, m: (b, n)),
                pl.BlockSpec((FOLD_BLK, FOLD_BLK), lambda b, n, m: (0, 0)),
                pl.BlockSpec((FOLD_BLK, FOLD_BLK), lambda b, n, m: (0, 0))]
    flat = lambda b, n, m: (b * nt + n) * mt + m
    casts = [_cast_specs(w, BATCH * nt * mt, flat) for w in cast_ws]
    return pl.pallas_call(
        _hosting_casts(_tdft_fold_kernel, len(args), 1, [u for _, _, u in casts], flat, 3),
        grid=(BATCH, nt, mt),
        in_specs=in_specs + [s for s, _, _ in casts],
        out_specs=[pl.BlockSpec((tm, tn), lambda b, n, m: (b * mt + m, n))] + [s for s, _, _ in casts],
        out_shape=[jax.ShapeDtypeStruct((N_TOK, FNET_W), BF16)] + [o for _, o, _ in casts],
        scratch_shapes=[pltpu.VMEM((half, tn), BF16), pltpu.VMEM((half, tn), BF16)],
        compiler_params=_cparams(("arbitrary", "arbitrary", "arbitrary"), 48),
        name="fnet_time_dft_folded",
    )(*args, *cast_ws)


def _dft_tables(n):
    k = np.arange(n)
    ang = 2.0 * np.pi * ((k[:, None] * k[None, :]) % n) / n
    return np.cos(ang) / np.sqrt(n), np.sin(ang) / np.sqrt(n)


def _dft_table_kernel(ac_ref, as_ref, bc_ref, bs_ref, co_ref, so_ref):
    ac, as_ = ac_ref[0], as_ref[0]
    bc, bs = bc_ref[...], bs_ref[...]
    co_ref[...] = (ac * bc - as_ * bs).astype(BF16)
    so_ref[...] = (ac * bs + as_ * bc).astype(BF16)


def _big_dft_tables():
    r = GRID_W
    k = np.arange(SEQ)
    j = np.arange(r)
    pa = 2.0 * np.pi * ((j[:, None] * r * k[None, :]) % SEQ) / SEQ
    pb = 2.0 * np.pi * ((j[:, None] * k[None, :]) % SEQ) / SEQ
    ac = jnp.asarray(np.cos(pa).reshape(r, 1, SEQ), F32)
    as_ = jnp.asarray(np.sin(pa).reshape(r, 1, SEQ), F32)
    bc = jnp.asarray(np.cos(pb) * SEQ ** -0.5, F32)
    bs = jnp.asarray(np.sin(pb) * SEQ ** -0.5, F32)
    row = pl.BlockSpec((1, 1, SEQ), lambda i: (i, 0, 0))
    full = pl.BlockSpec((r, SEQ), lambda i: (0, 0))
    return pl.pallas_call(
        _dft_table_kernel,
        grid=(r,),
        in_specs=[row, row, full, full],
        out_specs=[pl.BlockSpec((r, SEQ), lambda i: (i, 0))] * 2,
        out_shape=[jax.ShapeDtypeStruct((SEQ, SEQ), BF16)] * 2,
        compiler_params=_cparams(("parallel",), 16),
        name="dft_tables",
    )(ac, as_, bc, bs)


GO_TM = 512
GO_TN = 2048


def _glaout_kernel(of_ref, ob_ref, r_ref, fm_ref, gn_ref, x_ref, gt_ref, w_ref, o_ref, h_ref):
    @pl.when(pl.program_id(1) == 0)
    def _():
        heads = [slice(h * GLA_DV, (h + 1) * GLA_DV) for h in range(GLA_HEADS)]

        def body(chunks):
            inv = [[_inv_rms(of_ref[rows, sv] + ob_ref[rows, sv]) for sv in heads] for rows in chunks]
            for rows, inv_c in zip(chunks, inv):
                for sv, r in zip(heads, inv_c):
                    o = (of_ref[rows, sv] + ob_ref[rows, sv]) * r
                    gate = r_ref[rows, sv]
                    h_ref[rows, sv] = (o * gn_ref[:, sv] * (gate * _sigmoid(gate))).astype(BF16)
                h_ref[rows, GLA_V_W:] = fm_ref[rows, :].astype(BF16)
        _row_groups(of_ref.shape[0], body, group=4)

    o_ref[...] = x_ref[...] + gt_ref[0] * _dot(h_ref[...], w_ref[...])


def _glaout(of, ob, pf, fm, g_norm, xs, mod, w_out):
    tm, tn = GO_TM, GO_TN
    half = lambda col: pl.BlockSpec((tm, GLA_V_W), lambda i, j: (i, col))
    return pl.pallas_call(
        _glaout_kernel,
        grid=(N_TOK // tm, D_MODEL // tn),
        in_specs=[half(0), half(0), half(1), half(0),
                  pl.BlockSpec((1, GLA_V_W), lambda i, j: (0, 0)),
                  pl.BlockSpec((tm, tn), lambda i, j: (i, j)),
                  _mod_spec(5, tm, tn, col=True),
                  pl.BlockSpec((D_MODEL, tn), lambda i, j: (0, j))],
        out_specs=pl.BlockSpec((tm, tn), lambda i, j: (i, j)),
        out_shape=jax.ShapeDtypeStruct((N_TOK, D_MODEL), F32),
        scratch_shapes=[pltpu.VMEM((tm, D_MODEL), BF16)],
        compiler_params=_cparams(("parallel", "arbitrary"), 56),
        name="gla_fnet_out",
    )(of, ob, pf, fm, g_norm.reshape(1, GLA_V_W), xs, mod, w_out)


NA_HROWS = NA_QROWS // 2
NA_WROWS = NA_HROWS + NA_KH
NA_TQ = NA_QROWS * GRID_W
NA_HQ = NA_HROWS * GRID_W
NA_WK = NA_WROWS * GRID_W
NA_KBLK = NA_HROWS * GRID_W
NA_WBLKS = NA_WK // NA_KBLK
NA_MASKED = 2 * NA_KH - 1
NA_ROW_BLOCKS = (SEQ // GRID_W) // NA_QROWS
NA_CASES = (0, 1, NA_ROW_BLOCKS - 1)
NA_PAIRS = 2
NA_LANES = NA_PAIRS * 2 * NA_HEAD_DIM


def _na_window_row0(j, half):
    q0 = NA_QROWS * j + NA_HROWS * half
    return int(np.clip(q0 - NA_KH // 2, 0, SEQ // GRID_W - NA_WROWS))


def _na_bias_slots():
    rows = SEQ // GRID_W
    slots = np.full((len(NA_CASES), 2, NA_HROWS, NA_WROWS), NA_MASKED, np.int32)
    for case, j in enumerate(NA_CASES):
        for half in range(2):
            w0 = _na_window_row0(j, half)
            for i in range(NA_HROWS):
                qr = NA_QROWS * j + NA_HROWS * half + i
                r0 = int(np.clip(qr - NA_KH // 2, 0, rows - NA_KH))
                for l in range(NA_WROWS):
                    kr = w0 + l
                    if r0 <= kr < r0 + NA_KH:
                        slots[case, half, i, l] = kr - qr + NA_KH - 1
    return slots


def _na_kernel(q_ref, *refs):
    nb = NA_WBLKS
    ka, kb, va, vb = refs[0:nb], refs[nb:2 * nb], refs[2 * nb:3 * nb], refs[3 * nb:4 * nb]
    kc_ref, vc_ref, rp_ref, o_ref, bias_ref, td_ref = refs[4 * nb:]
    j = pl.program_id(1)

    @pl.when((j == 0) & (pl.program_id(2) == 0))
    def _():
        qc = lax.broadcasted_iota(jnp.int32, (GRID_W, 2 * GRID_W), 0)
        kc = lax.broadcasted_iota(jnp.int32, (GRID_W, 2 * GRID_W), 1) % GRID_W
        start = jnp.clip(qc - NA_KW // 2, 0, GRID_W - NA_KW)
        window = (kc >= start) & (kc < start + NA_KW)
        for hh in range(2 * NA_PAIRS):
            for slot in range(NA_MASKED):
                td_ref[hh, slot] = jnp.broadcast_to(rp_ref[hh, slot:slot + 1, :], (GRID_W, 2 * GRID_W))
                tile = pltpu.roll(td_ref[hh, slot], 2 * GRID_W - (NA_KW - 1), 1, stride=1, stride_axis=0)
                td_ref[hh, slot] = jnp.where(window, tile, NEG_BIG)
            td_ref[hh, NA_MASKED] = jnp.full((GRID_W, 2 * GRID_W), NEG_BIG, F32)

        slots = _na_bias_slots()
        left = lax.broadcasted_iota(jnp.int32, (GRID_W, 2 * GRID_W), 1) < GRID_W
        for case in range(len(NA_CASES)):
            for half in range(2):
                for hh in range(2 * NA_PAIRS):
                    dst = (case * 2 + half) * 2 * NA_PAIRS + hh
                    for i in range(NA_HROWS):
                        for p in range(NA_WROWS // 2):
                            sl = int(slots[case, half, i, 2 * p])
                            sr = int(slots[case, half, i, 2 * p + 1])
                            tile = td_ref[hh, sl] if sl == sr else jnp.where(left, td_ref[hh, sl], td_ref[hh, sr])
                            bias_ref[dst, i * GRID_W:(i + 1) * GRID_W, p * 2 * GRID_W:(p + 1) * 2 * GRID_W] = tile

    case = jnp.where(j == 0, 0, jnp.where(j == NA_ROW_BLOCKS - 1, 2, 1))
    lane = lax.broadcasted_iota(jnp.int32, (NA_HQ, 2 * NA_HEAD_DIM), 1)
    kv_refs = ((ka, va), (kb, vb))
    items = [(half, pair, hh) for half in range(2) for pair in range(NA_PAIRS) for hh in range(2)]
    pair_lanes = lambda pair: slice(pair * 2 * NA_HEAD_DIM, (pair + 1) * 2 * NA_HEAD_DIM)

    def scores(half, pair, hh):
        lanes = pair_lanes(pair)
        q = q_ref[half * NA_HQ:(half + 1) * NA_HQ, lanes]
        qm = jnp.where((lane // NA_HEAD_DIM) == hh, q, jnp.zeros_like(q))
        k = jnp.concatenate([r[:, lanes] for r in kv_refs[half][0]], axis=0)
        bias = bias_ref[(case * 2 + half) * 2 * NA_PAIRS + 2 * pair + hh]
        return _dot_nt(qm, k) + bias, _dot_nt(qm, kc_ref[:, lanes])

    def softmax(s_loc, s_ctx):
        m = jnp.maximum(jnp.max(s_loc, axis=-1, keepdims=True), jnp.max(s_ctx, axis=-1, keepdims=True))
        p_loc = jnp.exp2(s_loc - m)
        p_ctx = jnp.exp2(s_ctx - m)
        denom = jnp.sum(p_loc, axis=-1, keepdims=True) + jnp.sum(p_ctx, axis=-1, keepdims=True)
        return p_loc.astype(BF16), p_ctx.astype(BF16), denom

    def weighted(half, pair, p_loc, p_ctx, denom):
        lanes = pair_lanes(pair)
        v = jnp.concatenate([r[:, lanes] for r in kv_refs[half][1]], axis=0)
        return (_dot(p_loc, v) + _dot(p_ctx, vc_ref[:, lanes])) / denom

    s, p, o = {}, {}, {}
    for t in range(len(items) + 2):
        if t < len(items):
            s[t] = scores(*items[t])
        if 0 <= t - 1 < len(items):
            p[t - 1] = softmax(*s.pop(t - 1))
        if 0 <= t - 2 < len(items):
            o[t - 2] = weighted(*items[t - 2][:2], *p.pop(t - 2))
    first = (lane // NA_HEAD_DIM) == 0
    for n, (half, pair, hh) in enumerate(items):
        if hh == 0:
            out = jnp.where(first, o[n], o[n + 1])
            o_ref[half * NA_HQ:(half + 1) * NA_HQ, pair_lanes(pair)] = out.astype(o_ref.dtype)


def _na(qkv, bias_rows, cast_ws=()):
    hp_n = NA_HEADS // (2 * NA_PAIRS)
    jn = NA_ROW_BLOCKS
    kb_per_img = SEQ // NA_KBLK
    last_blk = kb_per_img - NA_WBLKS

    def kv_spec(col0, half, t):
        return pl.BlockSpec((NA_KBLK, NA_LANES), lambda hp, j, b: (
            b * kb_per_img + jnp.clip(2 * j + half - 1, 0, last_blk) + t, col0 + hp))

    def kv_specs(col0):
        return [kv_spec(col0, half, t) for half in range(2) for t in range(NA_WBLKS)]

    ctx0 = N_LAT // CTX_LEN
    in_specs = ([pl.BlockSpec((NA_TQ, NA_LANES), lambda hp, j, b: (b * jn + j, hp))]
                + kv_specs(hp_n) + kv_specs(2 * hp_n)
                + [pl.BlockSpec((CTX_LEN, NA_LANES), lambda hp, j, b: (ctx0 + b, hp_n + hp)),
                   pl.BlockSpec((CTX_LEN, NA_LANES), lambda hp, j, b: (ctx0 + b, 2 * hp_n + hp)),
                   pl.BlockSpec((2 * NA_PAIRS, 2 * NA_KH, 2 * GRID_W), lambda hp, j, b: (hp, 0, 0))])
    flat = lambda hp, j, b: (hp * jn + j) * BATCH + b
    casts = [_cast_specs(w, hp_n * jn * BATCH, flat) for w in cast_ws]
    return pl.pallas_call(
        _hosting_casts(_na_kernel, len(in_specs), 1, [u for _, _, u in casts], flat, 3),
        grid=(hp_n, jn, BATCH),
        in_specs=in_specs + [s for s, _, _ in casts],
        out_specs=[pl.BlockSpec((NA_TQ, NA_LANES), lambda hp, j, b: (b * jn + j, hp))] + [s for s, _, _ in casts],
        out_shape=[jax.ShapeDtypeStruct((N_LAT, D_MODEL), BF16)] + [o for _, o, _ in casts],
        scratch_shapes=[pltpu.VMEM((len(NA_CASES) * 4 * NA_PAIRS, NA_HQ, NA_WK), F32),
                        pltpu.VMEM((2 * NA_PAIRS, 2 * NA_KH, GRID_W, 2 * GRID_W), F32)],
        compiler_params=_cparams(("arbitrary", "arbitrary", "arbitrary"), 48),
        name="neighbourhood_attention",
    )(qkv, *([qkv] * (4 * NA_WBLKS)), qkv, qkv, bias_rows, *cast_ws)


def _na_bias_rows(rpb):
    r = rpb.astype(F32) * LOG2_E
    r = jnp.pad(r, ((0, 0), (0, 2 * NA_KH - r.shape[1]), (0, GRID_W - r.shape[2])))
    return jnp.concatenate([r, r], axis=-1)


MR_TM = 1024
MR_TN = 1024


def _mmres_kernel(a_ref, w_ref, x_ref, gt_ref, o_ref):
    o_ref[...] = x_ref[...] + gt_ref[0] * _dot(a_ref[...], w_ref[...])


def _mm_res(a, w, xs, mod, n_rows):
    tm, tn = MR_TM, MR_TN
    return pl.pallas_call(
        _mmres_kernel,
        grid=(n_rows // tm, D_MODEL // tn),
        in_specs=[pl.BlockSpec((tm, D_MODEL), lambda i, j: (i, 0)),
                  pl.BlockSpec((D_MODEL, tn), lambda i, j: (0, j)),
                  pl.BlockSpec((tm, tn), lambda i, j: (i, j)),
                  _mod_spec(5, tm, tn, col=True)],
        out_specs=pl.BlockSpec((tm, tn), lambda i, j: (i, j)),
        out_shape=jax.ShapeDtypeStruct((n_rows, D_MODEL), F32),
        compiler_params=_cparams(("parallel", "arbitrary"), 40),
        name="matmul_gated_residual",
    )(a, w, xs, mod)


def _rope_tables():
    half = GLA_DK // 4
    inv_freq = ROPE_BASE ** (-jnp.arange(half, dtype=F32) / half)
    t = jnp.arange(SEQ)
    ang_r = (t // GRID_W).astype(F32)[:, None] * inv_freq[None, :]
    ang_c = (t % GRID_W).astype(F32)[:, None] * inv_freq[None, :]
    cr, sr, cc, sc = jnp.cos(ang_r), jnp.sin(ang_r), jnp.cos(ang_c), jnp.sin(ang_c)
    cos_t = jnp.concatenate([cr, cc, cr, cc], axis=-1)
    sin_t = jnp.concatenate([-sr, -sc, sr, sc], axis=-1)
    cos_t = jnp.concatenate([cos_t, jnp.ones((GLA_BLOCK, GLA_DK), F32)], axis=0)
    sin_t = jnp.concatenate([sin_t, jnp.zeros((GLA_BLOCK, GLA_DK), F32)], axis=0)
    return cos_t, sin_t


def _pair_layout(w):
    lead = w.shape[:-1]
    quarter = GLA_DK // 4
    w = w.reshape(*lead, GLA_HEADS, 2, 2, quarter)
    return jnp.swapaxes(w, -2, -3).reshape(*lead, GLA_QK_W)


def kernel(x, c, ctx, c_ctx, l0_ada_w, l0_ada_b, l0_norm_ffn1, l0_norm_mix, l0_norm_ffn2, l0_ffn1_w_gu, l0_ffn1_w_down, l0_ffn2_w_gu, l0_ffn2_w_down, l0_w_in, l0_gla_gate_w_fwd, l0_gla_gate_b_fwd, l0_gla_gate_w_bwd, l0_gla_gate_b_bwd, l0_gla_norm, l0_w_out, l1_ada_w, l1_ada_b, l1_norm_ffn1, l1_norm_mix, l1_norm_ffn2, l1_ffn1_w_gu, l1_ffn1_w_down, l1_ffn2_w_gu, l1_ffn2_w_down, l1_w_qkv, l1_rpb, l1_w_out, norm_out):
    bf = lambda w: w.astype(BF16)
    cc = jnp.concatenate([c, c_ctx[None, :], jnp.zeros((8 - BATCH - 1, D_MODEL), F32)], axis=0)

    mod0 = _ada(cc, l0_ada_w, l0_ada_b)
    w_gu, w_down = bf(l0_ffn1_w_gu), bf(l0_ffn1_w_down)
    xs, w_gu2, w_down2 = _ffn(x.reshape(N_LAT, D_MODEL), mod0, 0, l0_norm_ffn1, w_gu, w_down, out_rows=N_TOK,
                              cast_ws=(l0_ffn2_w_gu, l0_ffn2_w_down))
    xs = _ffn(ctx.reshape(N_CTX, D_MODEL), mod0, 0, l0_norm_ffn1, w_gu, w_down, out_rows=N_TOK,
              tile0=N_LAT // FFN_TM, prev=xs)

    a0 = 2 * GLA_QK_W + GLA_V_W
    a1 = a0 + 2 * GLA_LOWRANK
    w_in = bf(l0_w_in)
    w_main = jnp.concatenate([_pair_layout(w_in[:, :GLA_QK_W]), _pair_layout(w_in[:, GLA_QK_W:2 * GLA_QK_W]),
                              w_in[:, 2 * GLA_QK_W:a0], w_in[:, a1:]], axis=1)
    w_lr = jnp.pad(w_in[:, a0:a1], ((0, 0), (0, GLA_LR_PAD - 2 * GLA_LOWRANK)))
    cc_np, sc_np = _dft_tables(FNET_GROUP_DIM)
    wc = jnp.asarray(np.concatenate([cc_np, sc_np], axis=1), F32).astype(BF16)
    pf, pb, zc, zs, lr = _nmm(xs, mod0, 1, l0_norm_mix, w_main, w_narrow=w_lr, w_chan=wc)

    cos_t, sin_t = _rope_tables()
    gwf = jnp.pad(_pair_layout(l0_gla_gate_w_fwd), ((0, GLA_LR_PAD - GLA_LOWRANK), (0, 0)))
    gwb = jnp.pad(_pair_layout(l0_gla_gate_w_bwd), ((GLA_LOWRANK, GLA_LR_PAD - 2 * GLA_LOWRANK), (0, 0)))
    of, ob = _gla(pf, pb, lr, cos_t, sin_t, gwf, _pair_layout(l0_gla_gate_b_fwd).reshape(1, -1),
                  gwb, _pair_layout(l0_gla_gate_b_bwd).reshape(1, -1), cast_ws=())

    cos_big, sin_big = _big_dft_tables()
    fm = _time_dft_folded(cos_big, sin_big, zc, zs, 512, 512)[0]
    ct_np, st_np = _dft_tables(CTX_LEN)
    fm = _time_dft(jnp.asarray(ct_np, F32).astype(BF16), jnp.asarray(st_np, F32).astype(BF16),
                   zc, zs, CTX_LEN, CTX_LEN, 512, N_LAT, prev=fm)[0]

    xs = _glaout(of, ob, pf, fm, l0_gla_norm, xs, mod0, bf(l0_w_out))
    xs, w_gu3, w_down3 = _ffn(xs, mod0, 2, l0_norm_ffn2, w_gu2, w_down2, out_rows=N_TOK,
                              cast_ws=(l1_ffn1_w_gu, l1_ffn1_w_down))

    mod1 = _ada(cc, l1_ada_w, l1_ada_b)
    xs, w_gu4, w_down4 = _ffn(xs, mod1, 0, l1_norm_ffn1, w_gu3, w_down3, out_rows=N_TOK,
                              cast_ws=(l1_ffn2_w_gu, l1_ffn2_w_down))
    w_qkv = bf(jnp.concatenate([l1_w_qkv[:, :D_MODEL] * NA_Q_SCALE, l1_w_qkv[:, D_MODEL:]], axis=1))
    qkv = _nmm(xs, mod1, 1, l1_norm_mix, w_qkv)[0]
    ao = _na(qkv, _na_bias_rows(l1_rpb))[0]
    xl = _mm_res(ao, bf(l1_w_out), xs, mod1, N_LAT)
    out = _ffn(xl, mod1, 2, l1_norm_ffn2, w_gu4, w_down4, out_rows=N_LAT, final_g=norm_out)
    return out.reshape(BATCH, SEQ, D_MODEL)
```

```python
import functools

import numpy as np
import jax
import jax.numpy as jnp
from jax import lax
from jax.experimental import pallas as pl
from jax.experimental.pallas import tpu as pltpu

D_MODEL = 2048
BATCH = 4
SEQ = 4096
GRID_W = 64
CTX_LEN = 256
N_MOD = 9
D_FF = 5632
RMS_EPS = 1e-6
ROPE_BASE = 10000.0

GLA_HEADS = 4
GLA_DK = 128
GLA_DV = 256
GLA_LOWRANK = 16
GLA_TAU = 16.0
GLA_CHUNK = 64
FNET_GROUPS = 4
FNET_GROUP_DIM = 256
GLA_QK_W = GLA_HEADS * GLA_DK
GLA_V_W = GLA_HEADS * GLA_DV
FNET_W = FNET_GROUPS * FNET_GROUP_DIM

NA_HEADS = 32
NA_HEAD_DIM = 64
NA_KH = 8
NA_KW = 16
NA_QROWS = 8
NEG_BIG = -1e30
LOG2_E = 1.4426950408889634
NA_Q_SCALE = NA_HEAD_DIM ** -0.5 * LOG2_E

N_LAT = BATCH * SEQ
N_CTX = BATCH * CTX_LEN
N_TOK = N_LAT + N_CTX

F32 = jnp.float32
BF16 = jnp.bfloat16
MIB = 1024 * 1024


def _cparams(sem, vmem_mib):
    return pltpu.CompilerParams(dimension_semantics=sem, vmem_limit_bytes=vmem_mib * MIB)


def _dot(a, b):
    return jnp.dot(a, b, preferred_element_type=F32)


def _dot_nt(a, b):
    return lax.dot_general(a, b, (((1,), (1,)), ((), ())), preferred_element_type=F32)


def _dot_tn(a, b):
    return lax.dot_general(a, b, (((0,), (0,)), ((), ())), preferred_element_type=F32)


def _split2(x):
    hi = x.astype(BF16)
    lo = (x - hi.astype(F32)).astype(BF16)
    return hi, lo


def _dot3(a, b):
    ah, al = _split2(a)
    bh, bl = _split2(b)
    return _dot(ah, bh) + (_dot(al, bh) + _dot(ah, bl))


def _dot_exact_lhs(l_bf16, x):
    x1 = x.astype(BF16)
    r1 = x - x1.astype(F32)
    x2 = r1.astype(BF16)
    x3 = (r1 - x2.astype(F32)).astype(BF16)
    return _dot(l_bf16, x1) + (_dot(l_bf16, x2) + _dot(l_bf16, x3))


def _sigmoid(x):
    return 1.0 / (1.0 + jnp.exp(-x))


ROW_CHUNK = 16
ROW_GROUP = 8


def _row_groups(n_rows, body, group=ROW_GROUP):
    span = ROW_CHUNK * group

    def step(t, carry):
        base = t * span
        body([pl.ds(pl.multiple_of(base + c * ROW_CHUNK, ROW_CHUNK), ROW_CHUNK) for c in range(group)])
        return carry
    lax.fori_loop(0, n_rows // span, step, 0)


def _inv_rms(x):
    return lax.rsqrt(jnp.mean(x * x, axis=-1, keepdims=True) + RMS_EPS)


def _norm_mod_rows(x_ref, g_ref, sh_ref, sc_ref, h_ref, gs_ref):
    d = x_ref.shape[1]
    gs_ref[0] = jnp.broadcast_to(g_ref[...] * (1.0 + sc_ref[0]), (ROW_CHUNK, d))
    gs_ref[1] = jnp.broadcast_to(sh_ref[0], (ROW_CHUNK, d))

    def body(chunks):
        inv = [_inv_rms(x_ref[rows, :]) for rows in chunks]
        for rows, r in zip(chunks, inv):
            h_ref[rows, :] = (x_ref[rows, :] * r * gs_ref[0] + gs_ref[1]).astype(BF16)
    _row_groups(x_ref.shape[0], body)


def _mod_row(i, tm):
    return jnp.minimum(i // (SEQ // tm), BATCH)


def _mod_spec(j, tm, width=D_MODEL, col=None, tile0=0):
    if col is None:
        return pl.BlockSpec((1, 1, width), lambda i, *_: (_mod_row(i + tile0, tm) * N_MOD + j, 0, 0))
    return pl.BlockSpec((1, 1, width), lambda i, n, *_: (_mod_row(i + tile0, tm) * N_MOD + j, 0, n))


CAST_ROW_ALIGN = 16


def _cast_plan(rows, steps):
    per = pl.cdiv(pl.cdiv(rows, steps), CAST_ROW_ALIGN) * CAST_ROW_ALIGN
    while rows % per:
        per += CAST_ROW_ALIGN
    return per, rows // per


def _cast_specs(w, steps, flat_index):
    per, used = _cast_plan(w.shape[0], steps)
    spec = pl.BlockSpec((per, w.shape[1]), lambda *g: (jnp.minimum(flat_index(*g), used - 1), 0))
    return spec, jax.ShapeDtypeStruct(w.shape, BF16), used


def _hosting_casts(kernel_fn, n_in, n_out, used, flat_index, grid_rank):
    nc = len(used)

    def wrapped(*refs):
        a, b, c = n_in + nc, n_in + nc + n_out, n_in + nc + n_out + nc
        kernel_fn(*refs[:n_in], *refs[a:b], *refs[c:])
        t = flat_index(*[pl.program_id(d) for d in range(grid_rank)])
        for n_used, src_ref, dst_ref in zip(used, refs[n_in:a], refs[b:c]):
            @pl.when(t < n_used)
            def _(src_ref=src_ref, dst_ref=dst_ref):
                dst_ref[...] = src_ref[...].astype(BF16)
    return wrapped


ADA_TN = 1024


def _ada_kernel(c_ref, w_ref, b_ref, o_ref):
    c = c_ref[...]
    s = (c * _sigmoid(c)).astype(BF16)
    o_ref[...] = _dot(s, w_ref[...].astype(BF16)) + b_ref[...]


def _ada(cc, w, b):
    n = w.shape[1]
    out = pl.pallas_call(
        _ada_kernel,
        grid=(n // ADA_TN,),
        in_specs=[pl.BlockSpec((8, D_MODEL), lambda j: (0, 0)),
                  pl.BlockSpec((D_MODEL, ADA_TN), lambda j: (0, j)),
                  pl.BlockSpec((1, ADA_TN), lambda j: (0, j))],
        out_specs=pl.BlockSpec((8, ADA_TN), lambda j: (0, j)),
        out_shape=jax.ShapeDtypeStruct((8, n), F32),
        compiler_params=_cparams(("arbitrary",), 40),
        name="ada_mod",
    )(cc, w, b.reshape(1, n))
    return out.reshape(8 * N_MOD, 1, D_MODEL)


FFN_TM = 1024
FFN_TF = 512
FFN_SLAB = 128


def _ffn_kernel(x_hbm, xn_ref, g_ref, sh_ref, sc_ref, shn_ref, scn_ref, gt_ref, wg_ref, wu_ref, wd_ref, *rest,
                final_norm, n_cast):
    go_ref = rest[0] if final_norm else None
    n_out = 1 + n_cast
    h0_ref, h1_ref, gs_ref, x_ref, x_sem = rest[-5:]
    o_ref, cast_out = rest[-5 - n_out], rest[-5 - n_out + 1:-5]
    cast_in = rest[-5 - n_out - n_cast:-5 - n_out]
    i = pl.program_id(0)
    f = pl.program_id(1)
    last = pl.num_programs(1) - 1
    tm, d = x_ref.shape

    def x_copy():
        return pltpu.make_async_copy(x_hbm.at[pl.ds(pl.multiple_of(i * tm, tm), tm), :], x_ref, x_sem)

    @pl.when(f == 0)
    def _():
        x_copy().start()

    @pl.when((i == 0) & (f == 0))
    def _():
        x_copy().wait()
        _norm_mod_rows(x_ref, g_ref, sh_ref, sc_ref, h0_ref, gs_ref)

    def step(h_ref, hn_ref, first):
        gs_ref[0] = jnp.broadcast_to(g_ref[...] * (1.0 + scn_ref[0]), (ROW_CHUNK, d))
        gs_ref[1] = jnp.broadcast_to(shn_ref[0], (ROW_CHUNK, d))
        slab0 = jnp.minimum(f, tm // FFN_SLAB - 1) * FFN_SLAB
        chunks = [pl.ds(k * ROW_CHUNK, ROW_CHUNK) for k in range(FFN_SLAB // ROW_CHUNK)]
        inv = [_inv_rms(xn_ref[rows, :]) for rows in chunks]
        for k, (rows, r) in enumerate(zip(chunks, inv)):
            dst = pl.ds(pl.multiple_of(slab0 + k * ROW_CHUNK, ROW_CHUNK), ROW_CHUNK)
            hn_ref[dst, :] = (xn_ref[rows, :] * r * gs_ref[0] + gs_ref[1]).astype(BF16)

        for src_ref, dst_ref in zip(cast_in, cast_out):
            dst_ref[...] = src_ref[...].astype(BF16)

        h = h_ref[...]
        a = _dot(h, wg_ref[...])
        u = _dot(h, wu_ref[...])
        act = (a * _sigmoid(a) * u).astype(BF16)
        if first:
            o_ref[...] = _dot(act, wd_ref[...])
        else:
            o_ref[...] += _dot(act, wd_ref[...])

    for parity, (h_ref, hn_ref) in enumerate(((h0_ref, h1_ref), (h1_ref, h0_ref))):
        for first in (True, False):
            @pl.when((i % 2 == parity) & ((f == 0) == first))
            def _(h_ref=h_ref, hn_ref=hn_ref, first=first):
                step(h_ref, hn_ref, first)

    @pl.when(f == last)
    def _():
        @pl.when(i > 0)
        def _():
            x_copy().wait()

        if not final_norm:
            o_ref[...] = x_ref[...] + (0.5 * gt_ref[0]) * o_ref[...]
        else:
            gs_ref[2] = jnp.broadcast_to(0.5 * gt_ref[0], (ROW_CHUNK, d))
            gs_ref[3] = jnp.broadcast_to(go_ref[...], (ROW_CHUNK, d))

            def body(chunks):
                resid = lambda rows: x_ref[rows, :] + gs_ref[2] * o_ref[rows, :]
                inv = [_inv_rms(resid(rows)) for rows in chunks]
                for rows, r in zip(chunks, inv):
                    o_ref[rows, :] = resid(rows) * r * gs_ref[3]
            _row_groups(tm, body)


def _ffn(x_in, mod, sub, norm_g, w_gu, w_down, *, out_rows, tile0=0, prev=None, final_g=None, cast_ws=()):
    tm, tf = FFN_TM, FFN_TF
    nf = D_FF // tf
    n_tiles = x_in.shape[0] // tm
    slabs = tm // FFN_SLAB
    assert slabs <= nf
    nxt = lambda i: jnp.minimum(i + 1, n_tiles - 1)
    in_specs = [pl.BlockSpec(memory_space=pl.ANY),
                pl.BlockSpec((FFN_SLAB, D_MODEL), lambda i, f: (nxt(i) * slabs + jnp.minimum(f, slabs - 1), 0)),
                pl.BlockSpec((1, D_MODEL), lambda i, f: (0, 0)),
                _mod_spec(3 * sub, tm, tile0=tile0), _mod_spec(3 * sub + 1, tm, tile0=tile0),
                pl.BlockSpec((1, 1, D_MODEL), lambda i, f: (_mod_row(nxt(i) + tile0, tm) * N_MOD + 3 * sub, 0, 0)),
                pl.BlockSpec((1, 1, D_MODEL), lambda i, f: (_mod_row(nxt(i) + tile0, tm) * N_MOD + 3 * sub + 1, 0, 0)),
                _mod_spec(3 * sub + 2, tm, tile0=tile0),
                pl.BlockSpec((D_MODEL, tf), lambda i, f: (0, f)),
                pl.BlockSpec((D_MODEL, tf), lambda i, f: (0, f + nf)),
                pl.BlockSpec((tf, D_MODEL), lambda i, f: (f, 0))]
    args = [x_in, x_in, norm_g.reshape(1, D_MODEL), mod, mod, mod, mod, mod, w_gu, w_gu, w_down]
    if final_g is not None:
        in_specs.append(pl.BlockSpec((1, D_MODEL), lambda i, f: (0, 0)))
        args.append(final_g.reshape(1, D_MODEL))
    aliases = {}
    if prev is not None:
        aliases = {len(args): 0}
        in_specs.append(pl.BlockSpec(memory_space=pl.ANY))
        args.append(prev)
    flat = lambda i, f: i * nf + f
    casts = [_cast_specs(w, n_tiles * nf, flat) for w in cast_ws]
    out = pl.pallas_call(
        functools.partial(_ffn_kernel, final_norm=final_g is not None, n_cast=len(casts)),
        grid=(n_tiles, nf),
        in_specs=in_specs + [s for s, _, _ in casts],
        out_specs=[pl.BlockSpec((tm, D_MODEL), lambda i, f: (i + tile0, 0))] + [s for s, _, _ in casts],
        out_shape=[jax.ShapeDtypeStruct((out_rows, D_MODEL), F32)] + [o for _, o, _ in casts],
        scratch_shapes=[pltpu.VMEM((tm, D_MODEL), BF16), pltpu.VMEM((tm, D_MODEL), BF16),
                        pltpu.VMEM((4 if final_g is not None else 2, ROW_CHUNK, D_MODEL), F32),
                        pltpu.VMEM((tm, D_MODEL), F32), pltpu.SemaphoreType.DMA(())],
        input_output_aliases=aliases,
        compiler_params=_cparams(("arbitrary", "arbitrary"), 56),
        name="ffn_half",
    )(*args, *cast_ws)
    return out if cast_ws else out[0]


NMM_TM = 1024
NMM_TN = 1024


def _nmm_kernel(x_ref, g_ref, sh_ref, sc_ref, w_ref, *rest, gla):
    if gla:
        wn_ref, wc_ref, o_ref, ob_ref, zc_ref, zs_ref, on_ref, h_ref, gs_ref = rest
    else:
        o_ref, h_ref, gs_ref = rest
    j = pl.program_id(1)

    @pl.when(j == 0)
    def _():
        _norm_mod_rows(x_ref, g_ref, sh_ref, sc_ref, h_ref, gs_ref)
        if gla:
            on_ref[...] = _dot(h_ref[...], wn_ref[...])

    if not gla:
        o_ref[...] = _dot(h_ref[...], w_ref[...]).astype(o_ref.dtype)
    else:
        @pl.when(j % 2 == 0)
        def _():
            o_ref[...] = _dot(h_ref[...], w_ref[...])

        @pl.when(j == 1)
        def _():
            ob_ref[...] = _dot(h_ref[...], w_ref[...]).astype(BF16)

        @pl.when(j == 3)
        def _():
            f = _dot(h_ref[...], w_ref[...]).astype(BF16)
            for grp in range(FNET_GROUPS):
                cols = slice(grp * FNET_GROUP_DIM, (grp + 1) * FNET_GROUP_DIM)
                z = _dot(f[:, cols], wc_ref[...])
                zc_ref[:, cols] = z[:, :FNET_GROUP_DIM].astype(BF16)
                zs_ref[:, cols] = z[:, FNET_GROUP_DIM:].astype(BF16)


def _nmm(xs, mod, sub, norm_g, w, w_narrow=None, w_chan=None, cast_ws=()):
    tm, tn = NMM_TM, NMM_TN
    n = w.shape[1]
    gla = w_narrow is not None
    in_specs = [pl.BlockSpec((tm, D_MODEL), lambda i, j: (i, 0)),
                pl.BlockSpec((1, D_MODEL), lambda i, j: (0, 0)),
                _mod_spec(3 * sub, tm), _mod_spec(3 * sub + 1, tm),
                pl.BlockSpec((D_MODEL, tn), lambda i, j: (0, j))]
    args = [xs, norm_g.reshape(1, D_MODEL), mod, mod, w]
    if gla:
        nn = w_narrow.shape[1]
        assert n == 4 * tn and tn == GLA_V_W == FNET_W
        in_specs += [pl.BlockSpec((D_MODEL, nn), lambda i, j: (0, 0)),
                     pl.BlockSpec(w_chan.shape, lambda i, j: (0, 0))]
        args += [w_narrow, w_chan]
        row_tile = pl.BlockSpec((tm, tn), lambda i, j: (i, 0))
        out_specs = [pl.BlockSpec((tm, tn), lambda i, j: (i, j // 2)), row_tile, row_tile, row_tile,
                     pl.BlockSpec((tm, nn), lambda i, j: (i, 0))]
        out_shape = [jax.ShapeDtypeStruct((N_TOK, n // 2), F32)] + [jax.ShapeDtypeStruct((N_TOK, tn), BF16)] * 3 + [
            jax.ShapeDtypeStruct((N_TOK, nn), F32)]
    else:
        out_specs = [pl.BlockSpec((tm, tn), lambda i, j: (i, j))]
        out_shape = [jax.ShapeDtypeStruct((N_TOK, n), BF16)]
    nj = n // tn
    flat = lambda i, j: i * nj + j
    casts = [_cast_specs(cw, (N_TOK // tm) * nj, flat) for cw in cast_ws]
    return pl.pallas_call(
        _hosting_casts(functools.partial(_nmm_kernel, gla=gla), len(args), len(out_specs),
                       [u for _, _, u in casts], flat, 2),
        grid=(N_TOK // tm, nj),
        in_specs=in_specs + [s for s, _, _ in casts],
        out_specs=out_specs + [s for s, _, _ in casts],
        out_shape=out_shape + [o for _, o, _ in casts],
        scratch_shapes=[pltpu.VMEM((tm, D_MODEL), BF16), pltpu.VMEM((2, ROW_CHUNK, D_MODEL), F32)],
        compiler_params=_cparams(("arbitrary", "arbitrary"), 56),
        name="norm_mod_matmul",
    )(*args, *cast_ws)


GLA_BLOCK = CTX_LEN
GLA_LAT_BLOCKS = SEQ // GLA_BLOCK
GLA_STEPS = 1 + GLA_LAT_BLOCKS
GLA_LR_PAD = 128
GLA_SPLIT_RANGE = 80.0


def _rope(x, cos, sin_signed):
    return x * cos + pltpu.roll(x, GLA_DK // 2, 1) * sin_signed


def _gla_kernel(qf, kf, vf, lf, cf, sf, qb, kb, vb, lb, cb, sb, lfn, lbn, gwf, gbf, gwb, gbb,
                of_ref, ob_ref, st_ref, row_ref, bc_ref, md_ref):

    c = GLA_CHUNK
    n = GLA_BLOCK
    row = lax.broadcasted_iota(jnp.int32, (n, n), 0)
    col = lax.broadcasted_iota(jnp.int32, (n, n), 1)
    same_chunk = (row // c) == (col // c)
    row_c = lax.broadcasted_iota(jnp.int32, (c, c), 0)
    col_c = lax.broadcasted_iota(jnp.int32, (c, c), 1)
    scale = GLA_DK ** -0.5

    dirs = ((qf, kf, vf, lf, cf, sf, gwf, gbf, of_ref), (qb, kb, vb, lb, cb, sb, gwb, gbb, ob_ref))
    n_chunks = n // c
    heads = [(slice(h * GLA_DK, (h + 1) * GLA_DK), slice(h * GLA_DV, (h + 1) * GLA_DV)) for h in range(GLA_HEADS)]
    tris = [row_c >= col_c, row_c <= col_c]
    last_row = (c - 1, 0)

    def gate_decays(d, l_ref):
        causal = (row >= col) if d == 0 else (row <= col)
        tri_b = jnp.where(same_chunk & causal, 1.0, 0.0).astype(BF16)
        x = _dot3(l_ref[...], dirs[d][6][...]) + dirs[d][7][...]
        g = (jnp.minimum(x, 0.0) - jnp.log1p(jnp.exp(-jnp.abs(x)))) * (1.0 / GLA_TAU)
        bc_all = _dot_exact_lhs(tri_b, g)
        totals = [-bc_all[ci * c + last_row[d]:ci * c + last_row[d] + 1, :] for ci in range(n_chunks)]
        return bc_all, functools.reduce(jnp.maximum, totals)

    def prepare_decays(l_refs):
        both = [gate_decays(d, l_ref) for d, l_ref in enumerate(l_refs)]

        def store():
            bc_ref[0] = both[0][0]
            bc_ref[1] = both[1][0]
            md_ref[...] = jnp.maximum(both[0][1], both[1][1])
        return store

    @pl.when(pl.program_id(1) == 0)
    def _():
        st_ref[...] = jnp.zeros_like(st_ref)
        prepare_decays((lf, lb))()

    decays = []
    for d in range(2):
        bc_all = bc_ref[d]
        per_chunk = []
        for ci in (range(n_chunks) if d == 0 else range(n_chunks - 1, -1, -1)):
            rows = slice(ci * c, (ci + 1) * c)
            bc = bc_all[rows, :]
            per_chunk.append((rows, bc, bc[c // 2:c // 2 + 1, :], bc[last_row[d]:last_row[d] + 1, :]))
        decays.append(per_chunk)
    max_decay = md_ref[...]

    def scores_by_rows(d, qh, kh, bc):
        row_ref[0], row_ref[1] = bc, qh
        s_idx = lax.broadcasted_iota(jnp.int32, (c, 1), 0)
        t_idx = lax.broadcasted_iota(jnp.int32, (c, c), 1)

        def one_row(t, att_t):
            valid = (s_idx <= t) if d == 0 else (s_idx >= t)
            w = jnp.exp(jnp.where(valid, row_ref[0, pl.ds(t, 1), :] - bc, -jnp.inf))
            col = jnp.sum(kh * w * row_ref[1, pl.ds(t, 1), :], axis=-1, keepdims=True)
            return jnp.where(t_idx == t, col, att_t)
        return lax.fori_loop(0, c, one_row, jnp.zeros((c, c), F32))

    def local_part(k, exact):
        out = {}
        for d, (q_ref, k_ref, v_ref, _, c_ref, s_ref, _, _, _) in enumerate(dirs):
            rows, bc, mid, last = decays[d][k]
            e_q, e_kl, e_l = jnp.exp(bc), jnp.exp(last - bc), jnp.exp(last)
            if not exact:
                e_qm, e_km = jnp.exp(bc - mid), jnp.exp(mid - bc)
            cos, sin = c_ref[rows, :], s_ref[rows, :]
            for h, (sk, sv) in enumerate(heads):
                qh = _rope(q_ref[rows, sk] * scale, cos, sin)
                kh = _rope(k_ref[rows, sk], cos, sin)
                vh = v_ref[rows, sv].astype(BF16)
                kv = _dot_tn(vh, (kh * e_kl[:, sk]).astype(BF16))
                if exact:
                    att_t = scores_by_rows(d, qh, kh, bc[:, sk]).astype(BF16)
                    mix = lambda att_t=att_t, vh=vh: _dot_tn(att_t, vh)
                else:
                    att = _dot_nt((qh * e_qm[:, sk]).astype(BF16), (kh * e_km[:, sk]).astype(BF16))
                    att = jnp.where(tris[d], att, 0.0).astype(BF16)
                    mix = lambda att=att, vh=vh: _dot(att, vh)
                out[d, h] = ((qh * e_q[:, sk]).astype(BF16), mix, kv, e_l[:, sk])
        return out

    def scan_block(exact):
        hand_over = prepare_decays((lfn, lbn))
        st = {(d, h): st_ref[d, h] for d in range(2) for h in range(GLA_HEADS)}
        local = local_part(0, exact)
        for k in range(n_chunks):
            nxt = local_part(k + 1, exact) if k + 1 < n_chunks else None
            for d in range(2):
                rows = decays[d][k][0]
                for h, (sk, sv) in enumerate(heads):
                    qe, mix, kv, e_l = local[d, h]
                    dirs[d][-1][rows, sv] = _dot_nt(qe, st[d, h].astype(BF16)) + mix()
                    st[d, h] = st[d, h] * e_l + kv
            local = nxt
        for (d, h), val in st.items():
            st_ref[d, h] = val
        hand_over()

    in_range = jnp.max(max_decay) <= GLA_SPLIT_RANGE

    @pl.when(in_range)
    def _():
        scan_block(exact=False)

    @pl.when(jnp.logical_not(in_range))
    def _():
        scan_block(exact=True)


def _gla(pf, pb, lr, cos_t, sin_t, gwf, gbf, gwb, gbb, cast_ws):
    c = GLA_BLOCK
    ctx0 = N_LAT // c

    def fwd_blk(b, s):
        return jnp.where(s == 0, ctx0 + b, GLA_LAT_BLOCKS * b + s - 1)

    def bwd_blk(b, s):
        return jnp.where(s == 0, ctx0 + b, GLA_LAT_BLOCKS * b + (GLA_STEPS - 1 - s))

    def fwd_rope(b, s):
        return jnp.where(s == 0, GLA_LAT_BLOCKS, s - 1)

    def bwd_rope(b, s):
        return jnp.where(s == 0, GLA_LAT_BLOCKS, GLA_STEPS - 1 - s)

    def dir_specs(blk, rope):
        return [pl.BlockSpec((c, GLA_QK_W), lambda b, s: (blk(b, s), 0)),
                pl.BlockSpec((c, GLA_QK_W), lambda b, s: (blk(b, s), 1)),
                pl.BlockSpec((c, GLA_V_W), lambda b, s: (blk(b, s), 0)),
                pl.BlockSpec((c, GLA_LR_PAD), lambda b, s: (blk(b, s), 0)),
                pl.BlockSpec((c, GLA_DK), lambda b, s: (rope(b, s), 0)),
                pl.BlockSpec((c, GLA_DK), lambda b, s: (rope(b, s), 0))]

    const = lambda shape: pl.BlockSpec(shape, lambda b, s: (0, 0))
    ahead = lambda s: jnp.minimum(s + 1, GLA_STEPS - 1)
    in_specs = (dir_specs(fwd_blk, fwd_rope) + dir_specs(bwd_blk, bwd_rope)
                + [pl.BlockSpec((c, GLA_LR_PAD), lambda b, s: (fwd_blk(b, ahead(s)), 0)),
                   pl.BlockSpec((c, GLA_LR_PAD), lambda b, s: (bwd_blk(b, ahead(s)), 0))]
                + [const((GLA_LR_PAD, GLA_QK_W)), const((1, GLA_QK_W)),
                   const((GLA_LR_PAD, GLA_QK_W)), const((1, GLA_QK_W))])
    flat = lambda b, s: b * GLA_STEPS + s
    casts = [_cast_specs(w, BATCH * GLA_STEPS, flat) for w in cast_ws]
    args = [pf, pf, pb, lr, cos_t, sin_t, pf, pf, pb, lr, cos_t, sin_t, lr, lr, gwf, gbf, gwb, gbb]
    return pl.pallas_call(
        _hosting_casts(_gla_kernel, len(args), 2, [u for _, _, u in casts], flat, 2),
        grid=(BATCH, GLA_STEPS),
        in_specs=in_specs + [s for s, _, _ in casts],
        out_specs=[pl.BlockSpec((c, GLA_V_W), lambda b, s: (fwd_blk(b, s), 0)),
                   pl.BlockSpec((c, GLA_V_W), lambda b, s: (bwd_blk(b, s), 0))] + [s for s, _, _ in casts],
        out_shape=[jax.ShapeDtypeStruct((N_TOK, GLA_V_W), F32)] * 2 + [o for _, o, _ in casts],
        scratch_shapes=[pltpu.VMEM((2, GLA_HEADS, GLA_DV, GLA_DK), F32), pltpu.VMEM((2, GLA_CHUNK, GLA_DK), F32),
                        pltpu.VMEM((2, c, GLA_QK_W), F32), pltpu.VMEM((1, GLA_QK_W), F32)],
        compiler_params=_cparams(("arbitrary", "arbitrary"), 40),
        name="gla_scan",
    )(*args, *cast_ws)


def _tdft_kernel(c_ref, s_ref, zc_ref, zs_ref, *rest):
    o_ref = rest[-1]
    o_ref[...] = (_dot(c_ref[...], zc_ref[...]) - _dot(s_ref[...], zs_ref[...])).astype(o_ref.dtype)


def _time_dft(cmat, smat, zc, zs, t_len, tm, tn, row0, prev=None, cast_ws=()):
    zb0 = row0 // t_len
    ob0 = row0 // tm
    mt = t_len // tm
    in_specs = [pl.BlockSpec((tm, t_len), lambda b, n, m: (m, 0)),
                pl.BlockSpec((tm, t_len), lambda b, n, m: (m, 0)),
                pl.BlockSpec((t_len, tn), lambda b, n, m: (zb0 + b, n)),
                pl.BlockSpec((t_len, tn), lambda b, n, m: (zb0 + b, n))]
    args = [cmat, smat, zc, zs]
    aliases = {}
    if prev is not None:
        in_specs.append(pl.BlockSpec(memory_space=pl.ANY))
        args.append(prev)
        aliases = {4: 0}
    nt = FNET_W // tn
    flat = lambda b, n, m: (b * nt + n) * mt + m
    casts = [_cast_specs(w, BATCH * nt * mt, flat) for w in cast_ws]
    return pl.pallas_call(
        _hosting_casts(_tdft_kernel, len(args), 1, [u for _, _, u in casts], flat, 3),
        grid=(BATCH, nt, mt),
        in_specs=in_specs + [s for s, _, _ in casts],
        out_specs=[pl.BlockSpec((tm, tn), lambda b, n, m: (ob0 + b * mt + m, n))] + [s for s, _, _ in casts],
        out_shape=[jax.ShapeDtypeStruct((N_TOK, FNET_W), BF16)] + [o for _, o, _ in casts],
        input_output_aliases=aliases,
        compiler_params=_cparams(("arbitrary", "arbitrary", "arbitrary"), 48),
        name="fnet_time_dft",
    )(*args, *cast_ws)


FOLD_BLK = 256


def _tdft_fold_kernel(c_ref, s_ref, zc_ref, zs_ref, j1_ref, j2_ref, o_ref, zcf_ref, zsf_ref):
    half, blk = zcf_ref.shape[0], FOLD_BLK
    nblk = 2 * half // blk

    @pl.when(pl.program_id(2) == 0)
    def _():
        for kb in range(half // blk):
            rows = slice(kb * blk, (kb + 1) * blk)
            mirror = slice((nblk - 1 - kb) * blk, (nblk - kb) * blk)
            after = slice((nblk - kb) * blk, (nblk - kb + 1) * blk)
            for z_ref, zf_ref, sign in ((zc_ref, zcf_ref, 1.0), (zs_ref, zsf_ref, -1.0)):
                partner = _dot(j1_ref[...], z_ref[mirror, :])
                if kb > 0:
                    partner = partner + _dot(j2_ref[...], z_ref[after, :])
                zf_ref[rows, :] = (z_ref[rows, :].astype(F32) + sign * partner).astype(BF16)

    tm = o_ref.shape[0]
    j = pl.program_id(2) * tm + lax.broadcasted_iota(jnp.int32, (tm, 1), 0)
    sign = jnp.where(j % 2 == 0, SEQ ** -0.5, -(SEQ ** -0.5))
    mid = zc_ref[half:half + 1, :].astype(F32)
    o = _dot(c_ref[...], zcf_ref[...]) - _dot(s_ref[...], zsf_ref[...]) + sign * mid
    o_ref[...] = o.astype(o_ref.dtype)


def _time_dft_folded(cmat, smat, zc, zs, tm, tn, cast_ws=()):
    half = SEQ // 2
    mt = SEQ // tm
    nt = FNET_W // tn
    i = np.arange(FOLD_BLK)
    j1 = jnp.asarray((i[:, None] + i[None, :] == FOLD_BLK) & (i[:, None] > 0), BF16)
    j2 = jnp.asarray((i[:, None] == 0) & (i[None, :] == 0), BF16)
    args = [cmat, smat, zc, zs, j1, j2]
    in_specs = [pl.BlockSpec((tm, half), lambda b, n, m: (m, 0)),
                pl.BlockSpec((tm, half), lambda b, n, m: (m, 0)),
                pl.BlockSpec((SEQ, tn), lambda b, n, m: (b, n)),
                pl.BlockSpec((SEQ, tn), lambda b, n, m: (b, n)),
                pl.BlockSpec((FOLD_BLK, FOLD_BLK), lambda b, n, m: (0, 0)),
                pl.BlockSpec((FOLD_BLK, FOLD_BLK), lambda b, n, m: (0, 0))]
    flat = lambda b, n, m: (b * nt + n) * mt + m
    casts = [_cast_specs(w, BATCH * nt * mt, flat) for w in cast_ws]
    return pl.pallas_call(
        _hosting_casts(_tdft_fold_kernel, len(args), 1, [u for _, _, u in casts], flat, 3),
        grid=(BATCH, nt, mt),
        in_specs=in_specs + [s for s, _, _ in casts],
        out_specs=[pl.BlockSpec((tm, tn), lambda b, n, m: (b * mt + m, n))] + [s for s, _, _ in casts],
        out_shape=[jax.ShapeDtypeStruct((N_TOK, FNET_W), BF16)] + [o for _, o, _ in casts],
        scratch_shapes=[pltpu.VMEM((half, tn), BF16), pltpu.VMEM((half, tn), BF16)],
        compiler_params=_cparams(("arbitrary", "arbitrary", "arbitrary"), 48),
        name="fnet_time_dft_folded",
    )(*args, *cast_ws)


def _dft_tables(n):
    k = np.arange(n)
    ang = 2.0 * np.pi * ((k[:, None] * k[None, :]) % n) / n
    return np.cos(ang) / np.sqrt(n), np.sin(ang) / np.sqrt(n)


def _dft_table_kernel(ac_ref, as_ref, bc_ref, bs_ref, co_ref, so_ref):
    ac, as_ = ac_ref[0], as_ref[0]
    bc, bs = bc_ref[...], bs_ref[...]
    co_ref[...] = (ac * bc - as_ * bs).astype(BF16)
    so_ref[...] = (ac * bs + as_ * bc).astype(BF16)


def _big_dft_tables():
    r = GRID_W
    k = np.arange(SEQ)
    j = np.arange(r)
    pa = 2.0 * np.pi * ((j[:, None] * r * k[None, :]) % SEQ) / SEQ
    pb = 2.0 * np.pi * ((j[:, None] * k[None, :]) % SEQ) / SEQ
    ac = jnp.asarray(np.cos(pa).reshape(r, 1, SEQ), F32)
    as_ = jnp.asarray(np.sin(pa).reshape(r, 1, SEQ), F32)
    bc = jnp.asarray(np.cos(pb) * SEQ ** -0.5, F32)
    bs = jnp.asarray(np.sin(pb) * SEQ ** -0.5, F32)
    row = pl.BlockSpec((1, 1, SEQ), lambda i: (i, 0, 0))
    full = pl.BlockSpec((r, SEQ), lambda i: (0, 0))
    return pl.pallas_call(
        _dft_table_kernel,
        grid=(r,),
        in_specs=[row, row, full, full],
        out_specs=[pl.BlockSpec((r, SEQ), lambda i: (i, 0))] * 2,
        out_shape=[jax.ShapeDtypeStruct((SEQ, SEQ), BF16)] * 2,
        compiler_params=_cparams(("parallel",), 16),
        name="dft_tables",
    )(ac, as_, bc, bs)


GO_TM = 512
GO_TN = 2048


def _glaout_kernel(of_ref, ob_ref, r_ref, fm_ref, gn_ref, x_ref, gt_ref, w_ref, o_ref, h_ref):
    @pl.when(pl.program_id(1) == 0)
    def _():
        heads = [slice(h * GLA_DV, (h + 1) * GLA_DV) for h in range(GLA_HEADS)]

        def body(chunks):
            inv = [[_inv_rms(of_ref[rows, sv] + ob_ref[rows, sv]) for sv in heads] for rows in chunks]
            for rows, inv_c in zip(chunks, inv):
                for sv, r in zip(heads, inv_c):
                    o = (of_ref[rows, sv] + ob_ref[rows, sv]) * r
                    gate = r_ref[rows, sv]
                    h_ref[rows, sv] = (o * gn_ref[:, sv] * (gate * _sigmoid(gate))).astype(BF16)
                h_ref[rows, GLA_V_W:] = fm_ref[rows, :].astype(BF16)
        _row_groups(of_ref.shape[0], body, group=4)

    o_ref[...] = x_ref[...] + gt_ref[0] * _dot(h_ref[...], w_ref[...])


def _glaout(of, ob, pf, fm, g_norm, xs, mod, w_out):
    tm, tn = GO_TM, GO_TN
    half = lambda col: pl.BlockSpec((tm, GLA_V_W), lambda i, j: (i, col))
    return pl.pallas_call(
        _glaout_kernel,
        grid=(N_TOK // tm, D_MODEL // tn),
        in_specs=[half(0), half(0), half(1), half(0),
                  pl.BlockSpec((1, GLA_V_W), lambda i, j: (0, 0)),
                  pl.BlockSpec((tm, tn), lambda i, j: (i, j)),
                  _mod_spec(5, tm, tn, col=True),
                  pl.BlockSpec((D_MODEL, tn), lambda i, j: (0, j))],
        out_specs=pl.BlockSpec((tm, tn), lambda i, j: (i, j)),
        out_shape=jax.ShapeDtypeStruct((N_TOK, D_MODEL), F32),
        scratch_shapes=[pltpu.VMEM((tm, D_MODEL), BF16)],
        compiler_params=_cparams(("parallel", "arbitrary"), 56),
        name="gla_fnet_out",
    )(of, ob, pf, fm, g_norm.reshape(1, GLA_V_W), xs, mod, w_out)


NA_HROWS = NA_QROWS // 2
NA_WROWS = NA_HROWS + NA_KH
NA_TQ = NA_QROWS * GRID_W
NA_HQ = NA_HROWS * GRID_W
NA_WK = NA_WROWS * GRID_W
NA_KBLK = NA_HROWS * GRID_W
NA_WBLKS = NA_WK // NA_KBLK
NA_MASKED = 2 * NA_KH - 1
NA_ROW_BLOCKS = (SEQ // GRID_W) // NA_QROWS
NA_CASES = (0, 1, NA_ROW_BLOCKS - 1)
NA_PAIRS = 2
NA_LANES = NA_PAIRS * 2 * NA_HEAD_DIM


def _na_window_row0(j, half):
    q0 = NA_QROWS * j + NA_HROWS * half
    return int(np.clip(q0 - NA_KH // 2, 0, SEQ // GRID_W - NA_WROWS))


def _na_bias_slots():
    rows = SEQ // GRID_W
    slots = np.full((len(NA_CASES), 2, NA_HROWS, NA_WROWS), NA_MASKED, np.int32)
    for case, j in enumerate(NA_CASES):
        for half in range(2):
            w0 = _na_window_row0(j, half)
            for i in range(NA_HROWS):
                qr = NA_QROWS * j + NA_HROWS * half + i
                r0 = int(np.clip(qr - NA_KH // 2, 0, rows - NA_KH))
                for l in range(NA_WROWS):
                    kr = w0 + l
                    if r0 <= kr < r0 + NA_KH:
                        slots[case, half, i, l] = kr - qr + NA_KH - 1
    return slots


def _na_kernel(q_ref, *refs):
    nb = NA_WBLKS
    ka, kb, va, vb = refs[0:nb], refs[nb:2 * nb], refs[2 * nb:3 * nb], refs[3 * nb:4 * nb]
    kc_ref, vc_ref, rp_ref, o_ref, bias_ref, td_ref = refs[4 * nb:]
    j = pl.program_id(1)

    @pl.when((j == 0) & (pl.program_id(2) == 0))
    def _():
        qc = lax.broadcasted_iota(jnp.int32, (GRID_W, 2 * GRID_W), 0)
        kc = lax.broadcasted_iota(jnp.int32, (GRID_W, 2 * GRID_W), 1) % GRID_W
        start = jnp.clip(qc - NA_KW // 2, 0, GRID_W - NA_KW)
        window = (kc >= start) & (kc < start + NA_KW)
        for hh in range(2 * NA_PAIRS):
            for slot in range(NA_MASKED):
                td_ref[hh, slot] = jnp.broadcast_to(rp_ref[hh, slot:slot + 1, :], (GRID_W, 2 * GRID_W))
                tile = pltpu.roll(td_ref[hh, slot], 2 * GRID_W - (NA_KW - 1), 1, stride=1, stride_axis=0)
                td_ref[hh, slot] = jnp.where(window, tile, NEG_BIG)
            td_ref[hh, NA_MASKED] = jnp.full((GRID_W, 2 * GRID_W), NEG_BIG, F32)

        slots = _na_bias_slots()
        left = lax.broadcasted_iota(jnp.int32, (GRID_W, 2 * GRID_W), 1) < GRID_W
        for case in range(len(NA_CASES)):
            for half in range(2):
                for hh in range(2 * NA_PAIRS):
                    dst = (case * 2 + half) * 2 * NA_PAIRS + hh
                    for i in range(NA_HROWS):
                        for p in range(NA_WROWS // 2):
                            sl = int(slots[case, half, i, 2 * p])
                            sr = int(slots[case, half, i, 2 * p + 1])
                            tile = td_ref[hh, sl] if sl == sr else jnp.where(left, td_ref[hh, sl], td_ref[hh, sr])
                            bias_ref[dst, i * GRID_W:(i + 1) * GRID_W, p * 2 * GRID_W:(p + 1) * 2 * GRID_W] = tile

    case = jnp.where(j == 0, 0, jnp.where(j == NA_ROW_BLOCKS - 1, 2, 1))
    lane = lax.broadcasted_iota(jnp.int32, (NA_HQ, 2 * NA_HEAD_DIM), 1)
    kv_refs = ((ka, va), (kb, vb))
    items = [(half, pair, hh) for half in range(2) for pair in range(NA_PAIRS) for hh in range(2)]
    pair_lanes = lambda pair: slice(pair * 2 * NA_HEAD_DIM, (pair + 1) * 2 * NA_HEAD_DIM)

    def scores(half, pair, hh):
        lanes = pair_lanes(pair)
        q = q_ref[half * NA_HQ:(half + 1) * NA_HQ, lanes]
        qm = jnp.where((lane // NA_HEAD_DIM) == hh, q, jnp.zeros_like(q))
        k = jnp.concatenate([r[:, lanes] for r in kv_refs[half][0]], axis=0)
        bias = bias_ref[(case * 2 + half) * 2 * NA_PAIRS + 2 * pair + hh]
        return _dot_nt(qm, k) + bias, _dot_nt(qm, kc_ref[:, lanes])

    def softmax(s_loc, s_ctx):
        m = jnp.maximum(jnp.max(s_loc, axis=-1, keepdims=True), jnp.max(s_ctx, axis=-1, keepdims=True))
        p_loc = jnp.exp2(s_loc - m)
        p_ctx = jnp.exp2(s_ctx - m)
        denom = jnp.sum(p_loc, axis=-1, keepdims=True) + jnp.sum(p_ctx, axis=-1, keepdims=True)
        return p_loc.astype(BF16), p_ctx.astype(BF16), denom

    def weighted(half, pair, p_loc, p_ctx, denom):
        lanes = pair_lanes(pair)
        v = jnp.concatenate([r[:, lanes] for r in kv_refs[half][1]], axis=0)
        return (_dot(p_loc, v) + _dot(p_ctx, vc_ref[:, lanes])) / denom

    s, p, o = {}, {}, {}
    for t in range(len(items) + 2):
        if t < len(items):
            s[t] = scores(*items[t])
        if 0 <= t - 1 < len(items):
            p[t - 1] = softmax(*s.pop(t - 1))
        if 0 <= t - 2 < len(items):
            o[t - 2] = weighted(*items[t - 2][:2], *p.pop(t - 2))
    first = (lane // NA_HEAD_DIM) == 0
    for n, (half, pair, hh) in enumerate(items):
        if hh == 0:
            out = jnp.where(first, o[n], o[n + 1])
            o_ref[half * NA_HQ:(half + 1) * NA_HQ, pair_lanes(pair)] = out.astype(o_ref.dtype)


def _na(qkv, bias_rows, cast_ws=()):
    hp_n = NA_HEADS // (2 * NA_PAIRS)
    jn = NA_ROW_BLOCKS
    kb_per_img = SEQ // NA_KBLK
    last_blk = kb_per_img - NA_WBLKS

    def kv_spec(col0, half, t):
        return pl.BlockSpec((NA_KBLK, NA_LANES), lambda hp, j, b: (
            b * kb_per_img + jnp.clip(2 * j + half - 1, 0, last_blk) + t, col0 + hp))

    def kv_specs(col0):
        return [kv_spec(col0, half, t) for half in range(2) for t in range(NA_WBLKS)]

    ctx0 = N_LAT // CTX_LEN
    in_specs = ([pl.BlockSpec((NA_TQ, NA_LANES), lambda hp, j, b: (b * jn + j, hp))]
                + kv_specs(hp_n) + kv_specs(2 * hp_n)
                + [pl.BlockSpec((CTX_LEN, NA_LANES), lambda hp, j, b: (ctx0 + b, hp_n + hp)),
                   pl.BlockSpec((CTX_LEN, NA_LANES), lambda hp, j, b: (ctx0 + b, 2 * hp_n + hp)),
                   pl.BlockSpec((2 * NA_PAIRS, 2 * NA_KH, 2 * GRID_W), lambda hp, j, b: (hp, 0, 0))])
    flat = lambda hp, j, b: (hp * jn + j) * BATCH + b
    casts = [_cast_specs(w, hp_n * jn * BATCH, flat) for w in cast_ws]
    return pl.pallas_call(
        _hosting_casts(_na_kernel, len(in_specs), 1, [u for _, _, u in casts], flat, 3),
        grid=(hp_n, jn, BATCH),
        in_specs=in_specs + [s for s, _, _ in casts],
        out_specs=[pl.BlockSpec((NA_TQ, NA_LANES), lambda hp, j, b: (b * jn + j, hp))] + [s for s, _, _ in casts],
        out_shape=[jax.ShapeDtypeStruct((N_LAT, D_MODEL), BF16)] + [o for _, o, _ in casts],
        scratch_shapes=[pltpu.VMEM((len(NA_CASES) * 4 * NA_PAIRS, NA_HQ, NA_WK), F32),
                        pltpu.VMEM((2 * NA_PAIRS, 2 * NA_KH, GRID_W, 2 * GRID_W), F32)],
        compiler_params=_cparams(("arbitrary", "arbitrary", "arbitrary"), 48),
        name="neighbourhood_attention",
    )(qkv, *([qkv] * (4 * NA_WBLKS)), qkv, qkv, bias_rows, *cast_ws)


def _na_bias_rows(rpb):
    r = rpb.astype(F32) * LOG2_E
    r = jnp.pad(r, ((0, 0), (0, 2 * NA_KH - r.shape[1]), (0, GRID_W - r.shape[2])))
    return jnp.concatenate([r, r], axis=-1)


MR_TM = 1024
MR_TN = 1024


def _mmres_kernel(a_ref, w_ref, x_ref, gt_ref, o_ref):
    o_ref[...] = x_ref[...] + gt_ref[0] * _dot(a_ref[...], w_ref[...])


def _mm_res(a, w, xs, mod, n_rows):
    tm, tn = MR_TM, MR_TN
    return pl.pallas_call(
        _mmres_kernel,
        grid=(n_rows // tm, D_MODEL // tn),
        in_specs=[pl.BlockSpec((tm, D_MODEL), lambda i, j: (i, 0)),
                  pl.BlockSpec((D_MODEL, tn), lambda i, j: (0, j)),
                  pl.BlockSpec((tm, tn), lambda i, j: (i, j)),
                  _mod_spec(5, tm, tn, col=True)],
        out_specs=pl.BlockSpec((tm, tn), lambda i, j: (i, j)),
        out_shape=jax.ShapeDtypeStruct((n_rows, D_MODEL), F32),
        compiler_params=_cparams(("parallel", "arbitrary"), 40),
        name="matmul_gated_residual",
    )(a, w, xs, mod)


def _rope_tables():
    half = GLA_DK // 4
    inv_freq = ROPE_BASE ** (-jnp.arange(half, dtype=F32) / half)
    t = jnp.arange(SEQ)
    ang_r = (t // GRID_W).astype(F32)[:, None] * inv_freq[None, :]
    ang_c = (t % GRID_W).astype(F32)[:, None] * inv_freq[None, :]
    cr, sr, cc, sc = jnp.cos(ang_r), jnp.sin(ang_r), jnp.cos(ang_c), jnp.sin(ang_c)
    cos_t = jnp.concatenate([cr, cc, cr, cc], axis=-1)
    sin_t = jnp.concatenate([-sr, -sc, sr, sc], axis=-1)
    cos_t = jnp.concatenate([cos_t, jnp.ones((GLA_BLOCK, GLA_DK), F32)], axis=0)
    sin_t = jnp.concatenate([sin_t, jnp.zeros((GLA_BLOCK, GLA_DK), F32)], axis=0)
    return cos_t, sin_t


def _pair_layout(w):
    lead = w.shape[:-1]
    quarter = GLA_DK // 4
    w = w.reshape(*lead, GLA_HEADS, 2, 2, quarter)
    return jnp.swapaxes(w, -2, -3).reshape(*lead, GLA_QK_W)


def kernel(x, c, ctx, c_ctx, l0_ada_w, l0_ada_b, l0_norm_ffn1, l0_norm_mix, l0_norm_ffn2, l0_ffn1_w_gu, l0_ffn1_w_down, l0_ffn2_w_gu, l0_ffn2_w_down, l0_w_in, l0_gla_gate_w_fwd, l0_gla_gate_b_fwd, l0_gla_gate_w_bwd, l0_gla_gate_b_bwd, l0_gla_norm, l0_w_out, l1_ada_w, l1_ada_b, l1_norm_ffn1, l1_norm_mix, l1_norm_ffn2, l1_ffn1_w_gu, l1_ffn1_w_down, l1_ffn2_w_gu, l1_ffn2_w_down, l1_w_qkv, l1_rpb, l1_w_out, norm_out):
    bf = lambda w: w.astype(BF16)
    cc = jnp.concatenate([c, c_ctx[None, :], jnp.zeros((8 - BATCH - 1, D_MODEL), F32)], axis=0)

    mod0 = _ada(cc, l0_ada_w, l0_ada_b)
    w_gu, w_down = bf(l0_ffn1_w_gu), bf(l0_ffn1_w_down)
    xs, w_gu2, w_down2 = _ffn(x.reshape(N_LAT, D_MODEL), mod0, 0, l0_norm_ffn1, w_gu, w_down, out_rows=N_TOK,
                              cast_ws=(l0_ffn2_w_gu, l0_ffn2_w_down))
    xs = _ffn(ctx.reshape(N_CTX, D_MODEL), mod0, 0, l0_norm_ffn1, w_gu, w_down, out_rows=N_TOK,
              tile0=N_LAT // FFN_TM, prev=xs)

    a0 = 2 * GLA_QK_W + GLA_V_W
    a1 = a0 + 2 * GLA_LOWRANK
    w_in = bf(l0_w_in)
    w_main = jnp.concatenate([_pair_layout(w_in[:, :GLA_QK_W]), _pair_layout(w_in[:, GLA_QK_W:2 * GLA_QK_W]),
                              w_in[:, 2 * GLA_QK_W:a0], w_in[:, a1:]], axis=1)
    w_lr = jnp.pad(w_in[:, a0:a1], ((0, 0), (0, GLA_LR_PAD - 2 * GLA_LOWRANK)))
    cc_np, sc_np = _dft_tables(FNET_GROUP_DIM)
    wc = jnp.asarray(np.concatenate([cc_np, sc_np], axis=1), F32).astype(BF16)
    pf, pb, zc, zs, lr = _nmm(xs, mod0, 1, l0_norm_mix, w_main, w_narrow=w_lr, w_chan=wc)

    cos_t, sin_t = _rope_tables()
    gwf = jnp.pad(_pair_layout(l0_gla_gate_w_fwd), ((0, GLA_LR_PAD - GLA_LOWRANK), (0, 0)))
    gwb = jnp.pad(_pair_layout(l0_gla_gate_w_bwd), ((GLA_LOWRANK, GLA_LR_PAD - 2 * GLA_LOWRANK), (0, 0)))
    of, ob = _gla(pf, pb, lr, cos_t, sin_t, gwf, _pair_layout(l0_gla_gate_b_fwd).reshape(1, -1),
                  gwb, _pair_layout(l0_gla_gate_b_bwd).reshape(1, -1), cast_ws=())

    cos_big, sin_big = _big_dft_tables()
    fm = _time_dft_folded(cos_big, sin_big, zc, zs, 512, 512)[0]
    ct_np, st_np = _dft_tables(CTX_LEN)
    fm = _time_dft(jnp.asarray(ct_np, F32).astype(BF16), jnp.asarray(st_np, F32).astype(BF16),
                   zc, zs, CTX_LEN, CTX_LEN, 512, N_LAT, prev=fm)[0]

    xs = _glaout(of, ob, pf, fm, l0_gla_norm, xs, mod0, bf(l0_w_out))
    xs, w_gu3, w_down3 = _ffn(xs, mod0, 2, l0_norm_ffn2, w_gu2, w_down2, out_rows=N_TOK,
                              cast_ws=(l1_ffn1_w_gu, l1_ffn1_w_down))

    mod1 = _ada(cc, l1_ada_w, l1_ada_b)
    xs, w_gu4, w_down4 = _ffn(xs, mod1, 0, l1_norm_ffn1, w_gu3, w_down3, out_rows=N_TOK,
                              cast_ws=(l1_ffn2_w_gu, l1_ffn2_w_down))
    w_qkv = bf(jnp.concatenate([l1_w_qkv[:, :D_MODEL] * NA_Q_SCALE, l1_w_qkv[:, D_MODEL:]], axis=1))
    qkv = _nmm(xs, mod1, 1, l1_norm_mix, w_qkv)[0]
    ao = _na(qkv, _na_bias_rows(l1_rpb))[0]
    xl = _mm_res(ao, bf(l1_w_out), xs, mod1, N_LAT)
    out = _ffn(xl, mod1, 2, l1_norm_ffn2, w_gu4, w_down4, out_rows=N_LAT, final_g=norm_out)
    return out.reshape(BATCH, SEQ, D_MODEL)
```
